```python
import math
import jax, jax.numpy as jnp
from jax import lax
import numpy as np


D_MODEL = 2048
BATCH = 32
SEQ = 256
DEPTH = 2
DEC_BATCH = 8
DEC_SEQ = 1024
PAST_LEN = 256

GRID_W = 64
EPS = 1e-6
MIX_W = D_MODEL
POOL_W = MIX_W // 4
POOL_GROUPS = 4
POOL_GC = POOL_W // POOL_GROUPS
POOL_WINDOWS = (2, 4, 8, 16)
HY_W = MIX_W // 4
HY_ORDER = 2
POS_BANDS = 8
POS_FEAT = 1 + 2 * POS_BANDS
FILT_HID = 64
HY_DECAY_MIN = 3.07
HY_DECAY_MAX = 15.35
MLA_W = MIX_W - POOL_W - HY_W
MLA_HEADS = 8
V_DIM = MLA_W // MLA_HEADS
NOPE_DIM = 128
ROPE_DIM = 64
Q_RANK = D_MODEL // 4
KV_RANK = D_MODEL // 8
ROPE_THETA = 10000.0
Q_BLOCK = 128
IN_COLS = POOL_W + 3 * HY_W + Q_RANK + KV_RANK + ROPE_DIM
IN_SPLITS = (POOL_W, POOL_W + 3 * HY_W, POOL_W + 3 * HY_W + Q_RANK, POOL_W + 3 * HY_W + Q_RANK + KV_RANK)
N_EXPERTS = 16
EXPERT_FF = D_MODEL // 2
CAPACITY_FACTOR = 2

kernel_name = 'hybrid_pool_hyena_mla_ecmoe_diffusion_step'


def rmsnorm(x, g):
    x32 = x.astype(jnp.float32)
    y = x32 * lax.rsqrt(jnp.mean(x32 * x32, axis=-1, keepdims=True) + EPS)
    return (y * g.astype(jnp.float32)).astype(x.dtype)


def modulate(x, shift, scale):
    return x * (1 + scale) + shift


def pool_mix(u, pool_w, pool_scale):
    b, n, _ = u.shape
    csum = jnp.cumsum(u.astype(jnp.float32), axis=1)
    csum = jnp.concatenate([jnp.zeros((b, 1, POOL_W), jnp.float32), csum], axis=1)
    t = np.arange(n)
    means = []
    for g, w in enumerate(POOL_WINDOWS):
        lo = np.clip(t - w // 2, 0, n)
        hi = np.clip(t - w // 2 + w, 0, n)
        cs = csum[:, :, g * POOL_GC:(g + 1) * POOL_GC]
        cnt = jnp.asarray((hi - lo).astype(np.float32))[None, :, None]
        means.append((cs[:, hi] - cs[:, lo]) / cnt)
    pooled = jnp.concatenate(means, axis=-1).astype(u.dtype) - u
    y = jnp.einsum('blgc,gcd->blgd', pooled.reshape(b, n, POOL_GROUPS, POOL_GC), pool_w)
    return y.reshape(b, n, POOL_W) * pool_scale


def hyena_filters(n, w1, b1, w2, b2, w3, freq, log_decay):
    t = np.linspace(0.0, 1.0, n, dtype=np.float32)[:, None]
    bands = np.arange(1, POS_BANDS + 1, dtype=np.float32)[None, :]
    feats = np.concatenate([t, np.sin(2 * np.pi * t * bands), np.cos(2 * np.pi * t * bands)], axis=1).astype(np.float32)
    z = jnp.asarray(feats).astype(w1.dtype)
    z = jnp.sin(freq * (z @ w1 + b1))
    z = jnp.sin(freq * (z @ w2 + b2))
    h = (z @ w3).astype(jnp.float32)
    h = h * jnp.exp(-jnp.exp(log_decay.astype(jnp.float32)) * jnp.asarray(t))
    h = h.reshape(n, HY_ORDER, 2, HY_W)
    h = h.at[0, :, 1].set(0.0)
    return h / (jnp.sum(jnp.abs(h), axis=(0, 2), keepdims=True) + EPS)


def long_conv(u, h_fwd, h_bwd, skip):
    n = u.shape[1]
    k = jnp.concatenate([h_fwd, jnp.zeros((1, HY_W), h_fwd.dtype), h_bwd[:0:-1]], axis=0)
    uf = jnp.fft.rfft(u.astype(jnp.float32), n=2 * n, axis=1)
    kf = jnp.fft.rfft(k, n=2 * n, axis=0)
    y = jnp.fft.irfft(uf * kf[None], n=2 * n, axis=1)[:, :n]
    return (y + u.astype(jnp.float32) * skip.astype(jnp.float32)).astype(u.dtype)


def short_conv3(u, w, b):
    n = u.shape[1]
    up = jnp.pad(u, ((0, 0), (1, 1), (0, 0)))
    return up[:, :n] * w[0] + up[:, 1:n + 1] * w[1] + up[:, 2:] * w[2] + b


def hyena_mix(u, short_w, short_b, filt, skip):
    uc = short_conv3(u, short_w, short_b)
    v, x1, x2 = jnp.split(uc, 3, axis=-1)
    z = x1 * long_conv(v, filt[:, 0, 0], filt[:, 0, 1], skip[0])
    return x2 * long_conv(z, filt[:, 1, 0], filt[:, 1, 1], skip[1])


def grid_angles(n):
    n_rows = n // GRID_W
    row = np.repeat(np.arange(n_rows), GRID_W).astype(np.float32)
    col = np.tile(np.arange(GRID_W), n_rows).astype(np.float32)
    nf = ROPE_DIM // 4
    inv = (1.0 / ROPE_THETA ** (np.arange(nf, dtype=np.float32) / nf)).astype(np.float32)
    return jnp.asarray(row[:, None] * inv[None]), jnp.asarray(col[:, None] * inv[None])


def rope_half(x, ang):
    nf = x.shape[-1] // 2
    x1, x2 = x[..., :nf], x[..., nf:]
    c = jnp.cos(ang).astype(x.dtype)
    s = jnp.sin(ang).astype(x.dtype)
    return jnp.concatenate([x1 * c - x2 * s, x1 * s + x2 * c], axis=-1)


def axial_rope(x, ang_row, ang_col):
    half = ROPE_DIM // 2
    return jnp.concatenate([rope_half(x[..., :half], ang_row), rope_half(x[..., half:], ang_col)], axis=-1)


def mla_attend(q_nope, q_rope, k_nope, k_rope, v):
    b, n, h, _ = q_nope.shape
    nblk = n // Q_BLOCK
    scale = 1.0 / math.sqrt(NOPE_DIM + ROPE_DIM)

    def blocks(a):
        return jnp.moveaxis(a.reshape(b, nblk, Q_BLOCK, *a.shape[2:]), 1, 0)

    def one(qs):
        qn, qr = qs
        s = jnp.einsum('bqhd,bkhd->bhqk', qn, k_nope) + jnp.einsum('bqhr,bkr->bhqk', qr, k_rope)
        p = jax.nn.softmax(s.astype(jnp.float32) * scale, axis=-1).astype(v.dtype)
        return jnp.einsum('bhqk,bkhd->bqhd', p, v)

    o = lax.map(one, (blocks(q_nope), blocks(q_rope)))
    return jnp.moveaxis(o, 0, 1).reshape(b, n, h * V_DIM)


def mla_mix(q_lat, kv_lat, k_r, q_norm, kv_norm, w_uq, w_ukv, ctx_ckv, ctx_kr, angles):
    b, n, _ = q_lat.shape
    q = (rmsnorm(q_lat, q_norm) @ w_uq).reshape(b, n, MLA_HEADS, NOPE_DIM + ROPE_DIM)
    q_nope, q_rope = q[..., :NOPE_DIM], q[..., NOPE_DIM:]
    ckv = rmsnorm(kv_lat, kv_norm)
    if angles is None:
        ckv_keys, kr_keys = ckv, k_r
    else:
        ang_row, ang_col = angles
        q_rope = axial_rope(q_rope, ang_row[:, None], ang_col[:, None])
        ckv_keys = jnp.concatenate([ctx_ckv, ckv], axis=1)
        kr_keys = jnp.concatenate([ctx_kr, axial_rope(k_r, ang_row, ang_col)], axis=1)
    nk = ckv_keys.shape[1]
    kv = (ckv_keys @ w_ukv).reshape(b, nk, MLA_HEADS, NOPE_DIM + V_DIM)
    out = mla_attend(q_nope, q_rope, kv[..., :NOPE_DIM], kr_keys, kv[..., NOPE_DIM:])
    return out, ckv, k_r


def ec_moe(h, router_w, w_gate, w_up, w_down):
    b, n, _ = h.shape
    cap = CAPACITY_FACTOR * n // N_EXPERTS
    aff = jax.nn.softmax(jnp.einsum('bld,de->ble', h, router_w).astype(jnp.float32), axis=-1)
    gate, idx = lax.top_k(jnp.swapaxes(aff, 1, 2), cap)
    bidx = jnp.arange(b)[:, None, None]
    xs = h[bidx, idx]
    hid = jax.nn.silu(jnp.einsum('becd,edf->becf', xs, w_gate)) * jnp.einsum('becd,edf->becf', xs, w_up)
    out = jnp.einsum('becf,efd->becd', hid, w_down) * gate[..., None].astype(h.dtype)
    return jnp.zeros_like(h).at[bidx, idx].add(out)


def trunk_layer(x, mod, lp, ctx_ckv, ctx_kr, angles):
    shift1, scale1, gate1, shift2, scale2, gate2 = jnp.split(mod, 6, axis=-1)
    h = modulate(rmsnorm(x, lp['norm1_g']), shift1, scale1)
    proj = h @ lp['w_in']
    u_pool, u_hy, q_lat, kv_lat, k_r = jnp.split(proj, IN_SPLITS, axis=-1)
    n = x.shape[1]
    filt = hyena_filters(n, lp['hy_ffn_w1'], lp['hy_ffn_b1'], lp['hy_ffn_w2'], lp['hy_ffn_b2'],
                         lp['hy_ffn_w3'], lp['hy_freq'], lp['hy_log_decay'])
    y_pool = pool_mix(u_pool, lp['pool_w'], lp['pool_scale'])
    y_hy = hyena_mix(u_hy, lp['hy_short_w'], lp['hy_short_b'], filt, lp['hy_skip'])
    y_mla, ckv, kr = mla_mix(q_lat, kv_lat, k_r, lp['mla_q_norm'], lp['mla_kv_norm'],
                             lp['mla_w_uq'], lp['mla_w_ukv'], ctx_ckv, ctx_kr, angles)
    mix = jnp.concatenate([y_pool, y_hy, y_mla], axis=-1) @ lp['w_out']
    x = x + gate1 * mix
    h2 = modulate(rmsnorm(x, lp['norm2_g']), shift2, scale2)
    x = x + gate2 * ec_moe(h2, lp['router_w'], lp['exp_w_gate'], lp['exp_w_up'], lp['exp_w_down'])
    return x, ckv, kr


def setup_inputs(seed: int = 0) -> dict:
    key = jax.random.key(seed)
    ks = jax.random.split(key, 40)

    def nrm(k, shape, scale):
        return jax.random.normal(k, shape, jnp.float32) * scale

    return {
        'x_prompt': nrm(ks[0], (BATCH, SEQ, D_MODEL), 1.0),
        'x_sample': nrm(ks[1], (DEC_BATCH, DEC_SEQ, D_MODEL), 1.0),
        'cache_ckv': nrm(ks[2], (DEC_BATCH, DEPTH, PAST_LEN, KV_RANK), 1.0),
        'cache_krope': nrm(ks[3], (DEC_BATCH, DEPTH, PAST_LEN, ROPE_DIM), 1.0),
        'c': nrm(ks[4], (DEC_BATCH, D_MODEL), 1.0),
        'c_ctx': nrm(ks[5], (D_MODEL,), 1.0),
        'ada_w': nrm(ks[6], (DEPTH, D_MODEL, 6 * D_MODEL), D_MODEL ** -0.5),
        'ada_b': nrm(ks[7], (DEPTH, 6 * D_MODEL), 0.02),
        'norm1_g': 1.0 + nrm(ks[8], (DEPTH, D_MODEL), 0.05),
        'norm2_g': 1.0 + nrm(ks[9], (DEPTH, D_MODEL), 0.05),
        'w_in': nrm(ks[10], (DEPTH, D_MODEL, IN_COLS), D_MODEL ** -0.5),
        'pool_w': nrm(ks[11], (DEPTH, POOL_GROUPS, POOL_GC, POOL_GC), POOL_GC ** -0.5),
        'pool_scale': 1.0 + nrm(ks[12], (DEPTH, POOL_W), 0.05),
        'hy_short_w': nrm(ks[13], (DEPTH, 3, 3 * HY_W), 3 ** -0.5),
        'hy_short_b': nrm(ks[14], (DEPTH, 3 * HY_W), 0.02),
        'hy_ffn_w1': nrm(ks[15], (DEPTH, POS_FEAT, FILT_HID), POS_FEAT ** -0.5),
        'hy_ffn_b1': nrm(ks[16], (DEPTH, FILT_HID), 0.02),
        'hy_ffn_w2': nrm(ks[17], (DEPTH, FILT_HID, FILT_HID), FILT_HID ** -0.5),
        'hy_ffn_b2': nrm(ks[18], (DEPTH, FILT_HID), 0.02),
        'hy_ffn_w3': nrm(ks[19], (DEPTH, FILT_HID, HY_ORDER * 2 * HY_W), FILT_HID ** -0.5),
        'hy_freq': 1.0 + nrm(ks[20], (DEPTH, FILT_HID), 0.1),
        'hy_log_decay': jax.random.uniform(ks[21], (DEPTH, HY_ORDER * 2 * HY_W), jnp.float32,
                                           minval=math.log(HY_DECAY_MIN), maxval=math.log(HY_DECAY_MAX)),
        'hy_skip': nrm(ks[22], (DEPTH, HY_ORDER, HY_W), 0.5),
        'mla_q_norm': 1.0 + nrm(ks[23], (DEPTH, Q_RANK), 0.05),
        'mla_kv_norm': 1.0 + nrm(ks[24], (DEPTH, KV_RANK), 0.05),
        'mla_w_uq': nrm(ks[25], (DEPTH, Q_RANK, MLA_HEADS * (NOPE_DIM + ROPE_DIM)), Q_RANK ** -0.5),
        'mla_w_ukv': nrm(ks[26], (DEPTH, KV_RANK, MLA_HEADS * (NOPE_DIM + V_DIM)), KV_RANK ** -0.5),
        'w_out': nrm(ks[27], (DEPTH, MIX_W, D_MODEL), MIX_W ** -0.5),
        'router_w': nrm(ks[28], (DEPTH, D_MODEL, N_EXPERTS), D_MODEL ** -0.5),
        'exp_w_gate': nrm(ks[29], (DEPTH, N_EXPERTS, D_MODEL, EXPERT_FF), D_MODEL ** -0.5),
        'exp_w_up': nrm(ks[30], (DEPTH, N_EXPERTS, D_MODEL, EXPERT_FF), D_MODEL ** -0.5),
        'exp_w_down': nrm(ks[31], (DEPTH, N_EXPERTS, EXPERT_FF, D_MODEL), EXPERT_FF ** -0.5),
        'final_norm_g': 1.0 + nrm(ks[32], (D_MODEL,), 0.05),
    }


def reference(x_prompt, x_sample, cache_ckv, cache_krope, c, c_ctx, ada_w, ada_b, norm1_g, norm2_g,
              w_in, pool_w, pool_scale, hy_short_w, hy_short_b, hy_ffn_w1, hy_ffn_b1, hy_ffn_w2,
              hy_ffn_b2, hy_ffn_w3, hy_freq, hy_log_decay, hy_skip, mla_q_norm, mla_kv_norm,
              mla_w_uq, mla_w_ukv, w_out, router_w, exp_w_gate, exp_w_up, exp_w_down, final_norm_g):
    layers = []
    for l in range(DEPTH):
        layers.append(dict(
            norm1_g=norm1_g[l], norm2_g=norm2_g[l], w_in=w_in[l], pool_w=pool_w[l],
            pool_scale=pool_scale[l], hy_short_w=hy_short_w[l], hy_short_b=hy_short_b[l],
            hy_ffn_w1=hy_ffn_w1[l], hy_ffn_b1=hy_ffn_b1[l], hy_ffn_w2=hy_ffn_w2[l],
            hy_ffn_b2=hy_ffn_b2[l], hy_ffn_w3=hy_ffn_w3[l], hy_freq=hy_freq[l],
            hy_log_decay=hy_log_decay[l], hy_skip=hy_skip[l], mla_q_norm=mla_q_norm[l],
            mla_kv_norm=mla_kv_norm[l], mla_w_uq=mla_w_uq[l], mla_w_ukv=mla_w_ukv[l],
            w_out=w_out[l], router_w=router_w[l], exp_w_gate=exp_w_gate[l],
            exp_w_up=exp_w_up[l], exp_w_down=exp_w_down[l]))

    xp = x_prompt
    ckv_list = []
    kr_list = []
    for l in range(DEPTH):
        mod_ctx = (jnp.einsum('d,dk->k', jax.nn.silu(c_ctx), ada_w[l]) + ada_b[l])[None, None, :]
        xp, ckv, kr = trunk_layer(xp, mod_ctx, layers[l], None, None, None)
        ckv_list.append(ckv)
        kr_list.append(kr)
    y_prompt = rmsnorm(xp, final_norm_g)
    new_ckv = jnp.stack(ckv_list, axis=1)
    new_krope = jnp.stack(kr_list, axis=1)

    xs = x_sample
    angles = grid_angles(x_sample.shape[1])
    for l in range(DEPTH):
        mod = (jnp.einsum('bd,dk->bk', jax.nn.silu(c), ada_w[l]) + ada_b[l])[:, None, :]
        xs, _, _ = trunk_layer(xs, mod, layers[l], cache_ckv[:, l], cache_krope[:, l], angles)
    y_sample = rmsnorm(xs, final_norm_g)

    return (y_prompt, y_sample, new_ckv, new_krope)
```

```python
import functools
import math

import numpy as np
import jax
import jax.numpy as jnp
from jax import lax
from jax.experimental import pallas as pl
from jax.experimental.pallas import tpu as pltpu

F32 = jnp.float32
BF16 = jnp.bfloat16

D = 2048
DEPTH = 2
B_CTX, L_CTX = 32, 256
B_S, L_S = 8, 1024
T_CTX = B_CTX * L_CTX
T = T_CTX + B_S * L_S
GRP = 1024
NG = T // GRP
NG_CTX = T_CTX // GRP
SEQ_PER_GRP = GRP // L_CTX
EPS = 1e-6
GRID_W = 64

POOL_W = 512
POOL_GC = 128
POOL_WINDOWS = (2, 4, 8, 16)
HY_W = 512
POS_BANDS = 8
FILT_HID = 64
H = 8
NOPE = 128
ROPE = 64
VD = 128
Q_RANK = 512
KV_RANK = 256
ROPE_THETA = 10000.0
IN_COLS = 2880
IN_PAD = 3072
E = 16
FF = 1024
CAP_CTX = 2 * L_CTX // E
CAP_S = 2 * L_S // E
SLOTS = GRP * 2 // E
ROWS_E = NG * SLOTS
LANES = 128
NCH = 512
ATT_SCALE = 1.0 / math.sqrt(NOPE + ROPE)
MB = 1024 * 1024


def _cp(sem, vmem_mb=48):
    return pltpu.CompilerParams(dimension_semantics=sem, vmem_limit_bytes=vmem_mb * MB)


def _rms(x, g):
    return x * lax.rsqrt(jnp.mean(x * x, axis=-1, keepdims=True) + EPS) * g


def _dot(a, b):
    return jnp.dot(a, b, preferred_element_type=F32)


def _dot_nt(a, b):
    return lax.dot_general(a, b, (((1,), (1,)), ((), ())), preferred_element_type=F32)


def _mod_kernel(c_ref, w_ref, b_ref, o_ref):
    c = c_ref[...]
    a = (c * jax.nn.sigmoid(c)).astype(BF16)
    o_ref[...] = _dot(a, w_ref[...].astype(BF16)) + b_ref[...]


def _adaln_mod(c16, ada_w, ada_b):
    tn = 1024
    return pl.pallas_call(
        _mod_kernel,
        grid=(DEPTH, 6 * D // tn),
        in_specs=[
            pl.BlockSpec((NG, D), lambda l, j: (0, 0)),
            pl.BlockSpec((None, D, tn), lambda l, j: (l, 0, j)),
            pl.BlockSpec((None, 1, tn), lambda l, j: (l, 0, j)),
        ],
        out_specs=pl.BlockSpec((None, NG, tn), lambda l, j: (l, 0, j)),
        out_shape=jax.ShapeDtypeStruct((DEPTH, NG, 6 * D), F32),
        compiler_params=_cp(("parallel", "parallel")),
        name="adaln_mod",
    )(c16, ada_w, ada_b.reshape(DEPTH, 1, 6 * D))


def _inproj_kernel(x_ref, g_ref, shift_ref, scale_ref, w_ref, o_ref, h_ref):
    @pl.when(pl.program_id(1) == 0)
    def _():
        y = _rms(x_ref[...], g_ref[...])
        h_ref[...] = (y * (1.0 + scale_ref[...]) + shift_ref[...]).astype(BF16)

    o_ref[...] = _dot(h_ref[...], w_ref[...])


def _inproj(x, g1, mod3, w):
    tn = 512
    return pl.pallas_call(
        _inproj_kernel,
        grid=(NG, IN_PAD // tn),
        in_specs=[
            pl.BlockSpec((GRP, D), lambda i, j: (i, 0)),
            pl.BlockSpec((1, D), lambda i, j: (0, 0)),
            pl.BlockSpec((None, 1, D), lambda i, j: (6 * i, 0, 0)),
            pl.BlockSpec((None, 1, D), lambda i, j: (6 * i + 1, 0, 0)),
            pl.BlockSpec((D, tn), lambda i, j: (0, j)),
        ],
        out_specs=pl.BlockSpec((GRP, tn), lambda i, j: (i, j)),
        out_shape=jax.ShapeDtypeStruct((T, IN_PAD), F32),
        scratch_shapes=[pltpu.VMEM((GRP, D), BF16)],
        compiler_params=_cp(("parallel", "arbitrary")),
        name="in_proj",
    )(x, g1, mod3, mod3, w)


def _pool_kernel(u_ref, bc_ref, bs_ref, ic_ref, is_ref, pw_ref, ps_ref, o_ref):
    g = pl.program_id(0)

    def seq(r0, n, band_ref, inv_ref):
        for k in range(len(POOL_WINDOWS)):
            cs = slice(k * POOL_GC, (k + 1) * POOL_GC)
            u = u_ref[r0:r0 + n, cs]
            wsum = _dot(band_ref[k], u.astype(BF16))
            pooled = wsum * inv_ref[:, cs] - u
            y = _dot(pooled.astype(BF16), pw_ref[k]) * ps_ref[:, cs]
            o_ref[r0:r0 + n, cs] = y.astype(o_ref.dtype)

    @pl.when(g < NG_CTX)
    def _():
        for s in range(SEQ_PER_GRP):
            seq(s * L_CTX, L_CTX, bc_ref, ic_ref)

    @pl.when(g >= NG_CTX)
    def _():
        seq(0, L_S, bs_ref, is_ref)


def _pool(proj, consts, pw, ps):
    nw = len(POOL_WINDOWS)
    return pl.pallas_call(
        _pool_kernel,
        grid=(NG,),
        in_specs=[
            pl.BlockSpec((GRP, POOL_W), lambda g: (g, 0)),
            pl.BlockSpec((nw, L_CTX, L_CTX), lambda g: (0, 0, 0)),
            pl.BlockSpec((nw, L_S, L_S), lambda g: (0, 0, 0)),
            pl.BlockSpec((L_CTX, POOL_W), lambda g: (0, 0)),
            pl.BlockSpec((L_S, POOL_W), lambda g: (0, 0)),
            pl.BlockSpec((nw, POOL_GC, POOL_GC), lambda g: (0, 0, 0)),
            pl.BlockSpec((1, POOL_W), lambda g: (0, 0)),
        ],
        out_specs=pl.BlockSpec((GRP, POOL_W), lambda g: (g, 0)),
        out_shape=jax.ShapeDtypeStruct((T, POOL_W), BF16),
        compiler_params=_cp(("parallel",)),
        name="pool_mix",
    )(proj, consts["band_c"], consts["band_s"], consts["inv_c"], consts["inv_s"], pw, ps)


def _hyena_kernel(v_ref, x1_ref, x2_ref, swv_ref, sw1_ref, sw2_ref, sbv_ref, sb1_ref, sb2_ref,
                  skip_ref, wc_ref, wtc_ref, ws_ref, wts_ref, ktc_ref, kts_ref, o_ref):
    g = pl.program_id(1)

    def sconv(u, sw_ref, sb_ref, n):
        row = lax.broadcasted_iota(jnp.int32, u.shape, 0)
        prev = jnp.where(row == 0, 0.0, pltpu.roll(u, 1, 0))
        nxt = jnp.where(row == n - 1, 0.0, pltpu.roll(u, n - 1, 0))
        return prev * sw_ref[0:1, :] + u * sw_ref[1:2, :] + nxt * sw_ref[2:3, :] + sb_ref[...]

    def lconv(u, o, n, w_ref, wt_ref, kt_ref):
        spec = _dot(w_ref[...], u.astype(BF16))
        pr, pi = spec[:n], spec[n:]
        ka, kb, ka2 = kt_ref[o, 0], kt_ref[o, 1], kt_ref[o, 2]
        yr = pr * ka - pi * kb
        yi = pr * kb + pi * ka2
        prod = jnp.concatenate([yr, yi], axis=0).astype(BF16)
        return _dot(wt_ref[...], prod) + u * skip_ref[o:o + 1, :]

    def seq(r0, n, w_ref, wt_ref, kt_ref):
        v = sconv(v_ref[r0:r0 + n, :], swv_ref, sbv_ref, n)
        x1 = sconv(x1_ref[r0:r0 + n, :], sw1_ref, sb1_ref, n)
        x2 = sconv(x2_ref[r0:r0 + n, :], sw2_ref, sb2_ref, n)
        z = x1 * lconv(v, 0, n, w_ref, wt_ref, kt_ref)
        y = x2 * lconv(z, 1, n, w_ref, wt_ref, kt_ref)
        o_ref[r0:r0 + n, :] = y.astype(o_ref.dtype)

    @pl.when(g < NG_CTX)
    def _():
        for s in range(SEQ_PER_GRP):
            seq(s * L_CTX, L_CTX, wc_ref, wtc_ref, ktc_ref)

    @pl.when(g >= NG_CTX)
    def _():
        seq(0, L_S, ws_ref, wts_ref, kts_ref)


def _hyena(proj, sw, sb, skip, consts, kt_c, kt_s):
    tc = 256
    nc = HY_W // tc
    c0 = POOL_W // tc

    def part(p):
        return pl.BlockSpec((GRP, tc), lambda c, g: (g, c0 + p * nc + c))

    def swpart(p):
        return pl.BlockSpec((3, tc), lambda c, g: (0, p * nc + c))

    def sbpart(p):
        return pl.BlockSpec((1, tc), lambda c, g: (0, p * nc + c))

    return pl.pallas_call(
        _hyena_kernel,
        grid=(nc, NG),
        in_specs=[
            part(0), part(1), part(2),
            swpart(0), swpart(1), swpart(2),
            sbpart(0), sbpart(1), sbpart(2),
            pl.BlockSpec((2, tc), lambda c, g: (0, c)),
            pl.BlockSpec((2 * L_CTX, L_CTX), lambda c, g: (0, 0)),
            pl.BlockSpec((L_CTX, 2 * L_CTX), lambda c, g: (0, 0)),
            pl.BlockSpec((2 * L_S, L_S), lambda c, g: (0, 0)),
            pl.BlockSpec((L_S, 2 * L_S), lambda c, g: (0, 0)),
            pl.BlockSpec((2, 3, L_CTX, tc), lambda c, g: (0, 0, 0, c)),
            pl.BlockSpec((2, 3, L_S, tc), lambda c, g: (0, 0, 0, c)),
        ],
        out_specs=pl.BlockSpec((GRP, tc), lambda c, g: (g, c)),
        out_shape=jax.ShapeDtypeStruct((T, HY_W), BF16),
        compiler_params=_cp(("parallel", "parallel"), 56),
        name="hyena_mix",
    )(proj, proj, proj, sw, sw, sw, sb, sb, sb, skip,
      consts["dft_c"], consts["dftt_c"], consts["dft_s"], consts["dftt_s"], kt_c, kt_s)


def _qproj_kernel(x_ref, g_ref, w_ref, cs_ref, o_ref):
    xn = _rms(x_ref[...], g_ref[...]).astype(BF16)
    acc = _dot(xn, w_ref[...]) * ATT_SCALE
    nq = H * NOPE
    nr = H * ROPE
    o_ref[:, :nq] = acc[:, :nq].astype(o_ref.dtype)
    rot = acc[:, nq:nq + nr] * cs_ref[0] + acc[:, nq + nr:] * cs_ref[1]
    o_ref[:, nq:] = rot.astype(o_ref.dtype)


def _qproj(proj, qg, wq, cs_q):
    tm = 512
    per = GRP // tm
    cb = (POOL_W + 3 * HY_W) // Q_RANK
    return pl.pallas_call(
        _qproj_kernel,
        grid=(T // tm,),
        in_specs=[
            pl.BlockSpec((tm, Q_RANK), lambda i: (i, cb)),
            pl.BlockSpec((1, Q_RANK), lambda i: (0, 0)),
            pl.BlockSpec((Q_RANK, H * (NOPE + 2 * ROPE)), lambda i: (0, 0)),
            pl.BlockSpec((None, 2, tm, H * ROPE), lambda i: (jnp.where(i >= NG_CTX * per, 1, 0), 0, i % per, 0)),
        ],
        out_specs=pl.BlockSpec((tm, H * (NOPE + ROPE)), lambda i: (i, 0)),
        out_shape=jax.ShapeDtypeStruct((T, H * (NOPE + ROPE)), BF16),
        compiler_params=_cp(("parallel",)),
        name="q_proj",
    )(proj, qg, wq, cs_q)


def _kvprep_kernel(x_ref, g_ref, cs_ref, w_ref, ckv_ref, krr_ref, kv_ref):
    ckv = _rms(x_ref[:, :KV_RANK], g_ref[...])
    ckv_ref[...] = ckv
    kr = x_ref[:, KV_RANK:KV_RANK + LANES]
    krs = x_ref[:, KV_RANK + LANES:]
    krr_ref[...] = (kr * cs_ref[0] + krs * cs_ref[1]).astype(krr_ref.dtype)
    kv_ref[...] = _dot(ckv.astype(BF16), w_ref[...]).astype(kv_ref.dtype)


def _kvprep(proj, kvg, cs_k, wkv):
    tm = 1024
    cb = (POOL_W + 3 * HY_W + Q_RANK) // 512
    return pl.pallas_call(
        _kvprep_kernel,
        grid=(T // tm,),
        in_specs=[
            pl.BlockSpec((tm, 512), lambda i: (i, cb)),
            pl.BlockSpec((1, KV_RANK), lambda i: (0, 0)),
            pl.BlockSpec((None, 2, tm, LANES), lambda i: (jnp.where(i >= NG_CTX, 1, 0), 0, 0, 0)),
            pl.BlockSpec((KV_RANK, H * (NOPE + VD)), lambda i: (0, 0)),
        ],
        out_specs=[
            pl.BlockSpec((tm, KV_RANK), lambda i: (i, 0)),
            pl.BlockSpec((tm, LANES), lambda i: (i, 0)),
            pl.BlockSpec((tm, H * (NOPE + VD)), lambda i: (i, 0)),
        ],
        out_shape=[
            jax.ShapeDtypeStruct((T, KV_RANK), F32),
            jax.ShapeDtypeStruct((T, LANES), BF16),
            jax.ShapeDtypeStruct((T, H * (NOPE + VD)), BF16),
        ],
        compiler_params=_cp(("parallel",)),
        name="kv_prep",
    )(proj, kvg, cs_k, wkv)


def _cachekv_kernel(x_ref, w_ref, o_ref):
    o_ref[...] = _dot(x_ref[...].astype(BF16), w_ref[...]).astype(o_ref.dtype)


def _cachekv(cache_ckv, layer, wkv):
    return pl.pallas_call(
        _cachekv_kernel,
        grid=(B_S,),
        in_specs=[
            pl.BlockSpec((None, None, L_CTX, KV_RANK), lambda b: (b, layer, 0, 0)),
            pl.BlockSpec((KV_RANK, H * (NOPE + VD)), lambda b: (0, 0)),
        ],
        out_specs=pl.BlockSpec((L_CTX, H * (NOPE + VD)), lambda b: (b, 0)),
        out_shape=jax.ShapeDtypeStruct((B_S * L_CTX, H * (NOPE + VD)), BF16),
        compiler_params=_cp(("parallel",)),
        name="cache_kv",
    )(cache_ckv, wkv)


ATT_TQ = 256
NK_S = L_S + L_CTX


def _attn_kernel(q_ref, kv_ref, krr_ref, kvc_ref, krc_ref, o_ref, kcat_ref):
    g = pl.program_id(0)
    lane = lax.broadcasted_iota(jnp.int32, (1, LANES), 1)
    hk = NOPE + LANES
    vo = H * NOPE

    krr = krr_ref[...]
    for h in range(H):
        kcat_ref[0:GRP, h * hk:h * hk + NOPE] = kv_ref[:, h * NOPE:(h + 1) * NOPE]
        kcat_ref[0:GRP, h * hk + NOPE:(h + 1) * hk] = krr

    def qcat(rows, h):
        qn = q_ref[rows, h * NOPE:(h + 1) * NOPE]
        pair = q_ref[rows, vo + (h // 2) * LANES:vo + (h // 2 + 1) * LANES].astype(F32)
        keep = (lane < ROPE) if h % 2 == 0 else (lane >= ROPE)
        return jnp.concatenate([qn, jnp.where(keep, pair, 0.0).astype(BF16)], axis=1)

    def probs(sc):
        m = jnp.max(sc, axis=-1, keepdims=True)
        p = jnp.exp(sc - m)
        return p.astype(BF16), 1.0 / jnp.sum(p, axis=-1, keepdims=True)

    @pl.when(g < NG_CTX)
    def _():
        def body(s, carry):
            rows = pl.ds(pl.multiple_of(s * L_CTX, L_CTX), L_CTX)
            for h in range(H):
                p, rl = probs(_dot_nt(qcat(rows, h), kcat_ref[rows, h * hk:(h + 1) * hk]))
                o = _dot(p, kv_ref[rows, vo + h * VD:vo + (h + 1) * VD]) * rl
                o_ref[rows, h * VD:(h + 1) * VD] = o.astype(o_ref.dtype)
            return carry

        lax.fori_loop(0, SEQ_PER_GRP, body, 0)

    @pl.when(g >= NG_CTX)
    def _():
        krc = krc_ref[...]
        for h in range(H):
            kcat_ref[GRP:NK_S, h * hk:h * hk + NOPE] = kvc_ref[:, h * NOPE:(h + 1) * NOPE]
            kcat_ref[GRP:NK_S, h * hk + NOPE:(h + 1) * hk] = krc

        def body(t, carry):
            rows = pl.ds(pl.multiple_of(t * ATT_TQ, ATT_TQ), ATT_TQ)
            for h in range(H):
                p, rl = probs(_dot_nt(qcat(rows, h), kcat_ref[:, h * hk:(h + 1) * hk]))
                o = _dot(p[:, :GRP], kv_ref[:, vo + h * VD:vo + (h + 1) * VD])
                o = o + _dot(p[:, GRP:], kvc_ref[:, vo + h * VD:vo + (h + 1) * VD])
                o_ref[rows, h * VD:(h + 1) * VD] = (o * rl).astype(o_ref.dtype)
            return carry

        lax.fori_loop(0, L_S // ATT_TQ, body, 0)


def _attention(q, kv, krr, kvc, krc):
    def cache_blk(g):
        return jnp.maximum(g - NG_CTX, 0)

    return pl.pallas_call(
        _attn_kernel,
        grid=(NG,),
        in_specs=[
            pl.BlockSpec((GRP, H * (NOPE + ROPE)), lambda g: (g, 0)),
            pl.BlockSpec((GRP, H * (NOPE + VD)), lambda g: (g, 0)),
            pl.BlockSpec((GRP, LANES), lambda g: (g, 0)),
            pl.BlockSpec((L_CTX, H * (NOPE + VD)), lambda g: (cache_blk(g), 0)),
            pl.BlockSpec((L_CTX, LANES), lambda g: (cache_blk(g), 0)),
        ],
        out_specs=pl.BlockSpec((GRP, H * VD), lambda g: (g, 0)),
        out_shape=jax.ShapeDtypeStruct((T, H * VD), BF16),
        scratch_shapes=[pltpu.VMEM((NK_S, H * (NOPE + LANES)), BF16)],
        compiler_params=_cp(("parallel",)),
        name="mla_attention",
    )(q, kv, krr, kvc, krc)


def _wout_kernel(yp_ref, yh_ref, ym_ref, wp_ref, wh_ref, wm_ref, x_ref, gate_ref, o_ref):
    acc = _dot(yp_ref[...], wp_ref[...]) + _dot(yh_ref[...], wh_ref[...]) + _dot(ym_ref[...], wm_ref[...])
    o_ref[...] = x_ref[...] + gate_ref[...] * acc


def _wout(yp, yh, ym, w, x, mod3):
    tn = 512
    return pl.pallas_call(
        _wout_kernel,
        grid=(NG, D // tn),
        in_specs=[
            pl.BlockSpec((GRP, POOL_W), lambda i, j: (i, 0)),
            pl.BlockSpec((GRP, HY_W), lambda i, j: (i, 0)),
            pl.BlockSpec((GRP, H * VD), lambda i, j: (i, 0)),
            pl.BlockSpec((POOL_W, tn), lambda i, j: (0, j)),
            pl.BlockSpec((HY_W, tn), lambda i, j: (1, j)),
            pl.BlockSpec((H * VD, tn), lambda i, j: (1, j)),
            pl.BlockSpec((GRP, tn), lambda i, j: (i, j)),
            pl.BlockSpec((None, 1, tn), lambda i, j: (6 * i + 2, 0, j)),
        ],
        out_specs=pl.BlockSpec((GRP, tn), lambda i, j: (i, j)),
        out_shape=jax.ShapeDtypeStruct((T, D), F32),
        compiler_params=_cp(("parallel", "parallel")),
        name="out_proj",
    )(yp, yh, ym, w, w, w, x, mod3)


def _router_kernel(x_ref, g_ref, shift_ref, scale_ref, rw_ref, h_ref, aff_ref):
    y = _rms(x_ref[...], g_ref[...])
    hb = (y * (1.0 + scale_ref[...]) + shift_ref[...]).astype(BF16)
    h_ref[...] = hb
    logits = _dot(hb, rw_ref[...])
    lane = lax.broadcasted_iota(jnp.int32, logits.shape, 1)
    logits = jnp.where(lane < E, logits, -jnp.inf)
    ex = jnp.exp(logits - jnp.max(logits, axis=-1, keepdims=True))
    aff_ref[...] = ex / jnp.sum(ex, axis=-1, keepdims=True)


def _router(x, g2, mod3, rw):
    tm = 512
    per = GRP // tm
    return pl.pallas_call(
        _router_kernel,
        grid=(T // tm,),
        in_specs=[
            pl.BlockSpec((tm, D), lambda i: (i, 0)),
            pl.BlockSpec((1, D), lambda i: (0, 0)),
            pl.BlockSpec((None, 1, D), lambda i: (6 * (i // per) + 3, 0, 0)),
            pl.BlockSpec((None, 1, D), lambda i: (6 * (i // per) + 4, 0, 0)),
            pl.BlockSpec((D, LANES), lambda i: (0, 0)),
        ],
        out_specs=[
            pl.BlockSpec((tm, D), lambda i: (i, 0)),
            pl.BlockSpec((tm, LANES), lambda i: (i, 0)),
        ],
        out_shape=[
            jax.ShapeDtypeStruct((T, D), BF16),
            jax.ShapeDtypeStruct((T, LANES), F32),
        ],
        compiler_params=_cp(("parallel",)),
        name="moe_router",
    )(x, g2, mod3, mod3, rw)


RANK_CH = 256


def _rank_kernel(aff_ref, afft_ref, rank_ref):
    g = pl.program_id(0)

    def seq(r0, n):
        lane_i = lax.broadcasted_iota(jnp.int32, (RANK_CH, n), 1)
        sub_i = lax.broadcasted_iota(jnp.int32, (RANK_CH, n), 0)
        earlier = [jnp.where(sub_i + c * RANK_CH < lane_i, 1.0, 0.0) for c in range(n // RANK_CH)]
        for e in range(E):
            row = afft_ref[e:e + 1, r0:r0 + n]
            acc = jnp.zeros((1, n), F32)
            for c in range(n // RANK_CH):
                col = aff_ref[r0 + c * RANK_CH:r0 + (c + 1) * RANK_CH, e:e + 1]
                beats = jnp.where(col > row, 1.0, jnp.where(col == row, earlier[c], 0.0))
                acc = acc + jnp.sum(beats, axis=0, keepdims=True)
            rank_ref[e:e + 1, r0:r0 + n] = acc.astype(jnp.int32)

    @pl.when(g < NG_CTX)
    def _():
        for s in range(SEQ_PER_GRP):
            seq(s * L_CTX, L_CTX)

    @pl.when(g >= NG_CTX)
    def _():
        seq(0, L_S)


def _rank(aff, afft):
    return pl.pallas_call(
        _rank_kernel,
        grid=(NG,),
        in_specs=[
            pl.BlockSpec((GRP, LANES), lambda g: (g, 0)),
            pl.BlockSpec((E, GRP), lambda g: (0, g)),
        ],
        out_specs=pl.BlockSpec((E, GRP), lambda g: (0, g)),
        out_shape=jax.ShapeDtypeStruct((E, T), jnp.int32),
        compiler_params=_cp(("parallel",)),
        name="moe_rank",
    )(aff, afft)


def _gather_kernel(rank_ref, afft_ref, h_ref, xs_ref, gs_ref, sel_ref):
    g = pl.program_id(0)

    def build(s, r0, n, cap, slot0):
        slot_i = lax.broadcasted_iota(jnp.int32, (cap, n), 0)
        for e in range(E):
            hit = slot_i == rank_ref[e:e + 1, r0:r0 + n]
            sel_ref[(s * E + e) * cap:(s * E + e + 1) * cap, 0:n] = jnp.where(hit, 1.0, 0.0).astype(BF16)
            gs_ref[e, slot0:slot0 + cap, :] = jnp.sum(
                jnp.where(hit, afft_ref[e:e + 1, r0:r0 + n], 0.0), axis=1, keepdims=True)

    def move(s, r0, n, cap, slot0):
        res = _dot(sel_ref[s * E * cap:(s + 1) * E * cap, 0:n], h_ref[r0:r0 + n, :])
        for e in range(E):
            xs_ref[e, slot0:slot0 + cap, :] = res[e * cap:(e + 1) * cap].astype(xs_ref.dtype)

    def both(fn):
        @pl.when(g < NG_CTX)
        def _():
            for s in range(SEQ_PER_GRP):
                fn(s, s * L_CTX, L_CTX, CAP_CTX, s * CAP_CTX)

        @pl.when(g >= NG_CTX)
        def _():
            fn(0, 0, L_S, CAP_S, 0)

    @pl.when(pl.program_id(1) == 0)
    def _():
        both(build)

    both(move)


def _gather(rank, afft, h2):
    return pl.pallas_call(
        _gather_kernel,
        grid=(NG, D // NCH),
        in_specs=[
            pl.BlockSpec((E, GRP), lambda g, j: (0, g)),
            pl.BlockSpec((E, GRP), lambda g, j: (0, g)),
            pl.BlockSpec((GRP, NCH), lambda g, j: (g, j)),
        ],
        out_specs=[
            pl.BlockSpec((E, SLOTS, NCH), lambda g, j: (0, g, j)),
            pl.BlockSpec((E, SLOTS, 1), lambda g, j: (0, g, 0)),
        ],
        out_shape=[
            jax.ShapeDtypeStruct((E, ROWS_E, D), BF16),
            jax.ShapeDtypeStruct((E, ROWS_E, 1), F32),
        ],
        scratch_shapes=[pltpu.VMEM((E * CAP_S, L_S), BF16)],
        compiler_params=_cp(("parallel", "arbitrary")),
        name="moe_gather",
    )(rank, afft, h2)


FFN_TF = 256
FFN_TM = 1024


def _ffn_kernel(x_ref, wg_ref, wu_ref, wd_ref, gs_ref, o_ref, acc_ref):
    f = pl.program_id(2)

    @pl.when(f == 0)
    def _():
        acc_ref[...] = jnp.zeros_like(acc_ref)

    x = x_ref[...]
    a = _dot(x, wg_ref[...].astype(BF16))
    b = _dot(x, wu_ref[...].astype(BF16))
    hid = (a * jax.nn.sigmoid(a) * b).astype(BF16)
    acc_ref[...] += _dot(hid, wd_ref[...].astype(BF16))

    @pl.when(f == pl.num_programs(2) - 1)
    def _():
        o_ref[...] = (acc_ref[...] * gs_ref[...]).astype(o_ref.dtype)


def _ffn(xs, gs, w_gate, w_up, w_down, layer):
    return pl.pallas_call(
        _ffn_kernel,
        grid=(E, ROWS_E // FFN_TM, FF // FFN_TF),
        in_specs=[
            pl.BlockSpec((None, FFN_TM, D), lambda e, m, f: (e, m, 0)),
            pl.BlockSpec((None, None, D, FFN_TF), lambda e, m, f: (layer, e, 0, f)),
            pl.BlockSpec((None, None, D, FFN_TF), lambda e, m, f: (layer, e, 0, f)),
            pl.BlockSpec((None, None, FFN_TF, D), lambda e, m, f: (layer, e, f, 0)),
            pl.BlockSpec((None, FFN_TM, 1), lambda e, m, f: (e, m, 0)),
        ],
        out_specs=pl.BlockSpec((None, FFN_TM, D), lambda e, m, f: (e, m, 0)),
        out_shape=jax.ShapeDtypeStruct((E, ROWS_E, D), BF16),
        scratch_shapes=[pltpu.VMEM((FFN_TM, D), F32)],
        compiler_params=_cp(("parallel", "parallel", "arbitrary"), 56),
        name="moe_ffn",
    )(xs, w_gate, w_up, w_down, gs)


def _combine_kernel(rt_ref, ys_ref, x_ref, gate_ref, exc_ref, exs_ref, o_ref, st_ref):
    g = pl.program_id(0)

    def build(r0, n, cap, slot0, ex_ref):
        r = jnp.minimum(rt_ref[r0:r0 + n, :], cap).astype(F32).astype(BF16)
        want = (lax.broadcasted_iota(jnp.int32, (1, NCH), 1) & (cap - 1)).astype(F32)
        for c in range(E * cap // NCH):
            cs = slice(c * NCH, (c + 1) * NCH)
            st_ref[r0:r0 + n, cs] = jnp.where(_dot(r, ex_ref[:, cs]) == want, 1.0, 0.0).astype(BF16)

    def move(r0, n, cap, slot0, ex_ref):
        ys = ys_ref[:, slot0:slot0 + cap, :].reshape(E * cap, NCH)
        moe = _dot(st_ref[r0:r0 + n, 0:E * cap], ys)
        o_ref[r0:r0 + n, :] = x_ref[r0:r0 + n, :] + gate_ref[...] * moe

    def both(fn):
        @pl.when(g < NG_CTX)
        def _():
            for s in range(SEQ_PER_GRP):
                fn(s * L_CTX, L_CTX, CAP_CTX, s * CAP_CTX, exc_ref)

        @pl.when(g >= NG_CTX)
        def _():
            fn(0, L_S, CAP_S, 0, exs_ref)

    @pl.when(pl.program_id(1) == 0)
    def _():
        both(build)

    both(move)


def _combine(rank_t, ys, x, mod3, consts):
    return pl.pallas_call(
        _combine_kernel,
        grid=(NG, D // NCH),
        in_specs=[
            pl.BlockSpec((GRP, LANES), lambda g, j: (g, 0)),
            pl.BlockSpec((E, SLOTS, NCH), lambda g, j: (0, g, j)),
            pl.BlockSpec((GRP, NCH), lambda g, j: (g, j)),
            pl.BlockSpec((None, 1, NCH), lambda g, j: (6 * g + 5, 0, j)),
            pl.BlockSpec((LANES, E * CAP_CTX), lambda g, j: (0, 0)),
            pl.BlockSpec((LANES, E * CAP_S), lambda g, j: (0, 0)),
        ],
        out_specs=pl.BlockSpec((GRP, NCH), lambda g, j: (g, j)),
        out_shape=jax.ShapeDtypeStruct((T, D), F32),
        scratch_shapes=[pltpu.VMEM((L_S, E * CAP_S), BF16)],
        compiler_params=_cp(("parallel", "arbitrary")),
        name="moe_combine",
    )(rank_t, ys, x, mod3, consts["ex_c"], consts["ex_s"])


def _fnorm_kernel(x_ref, g_ref, o_ref):
    o_ref[...] = _rms(x_ref[...], g_ref[...])


def _final_norm(x, g, blk0, nblk):
    tm = 512
    return pl.pallas_call(
        _fnorm_kernel,
        grid=(nblk,),
        in_specs=[
            pl.BlockSpec((tm, D), lambda i: (blk0 + i, 0)),
            pl.BlockSpec((1, D), lambda i: (0, 0)),
        ],
        out_specs=pl.BlockSpec((tm, D), lambda i: (i, 0)),
        out_shape=jax.ShapeDtypeStruct((nblk * tm, D), F32),
        compiler_params=_cp(("parallel",)),
        name="final_norm",
    )(x, g)


def _np_constants():
    c = {}
    for tag, n in (("c", L_CTX), ("s", L_S)):
        t = np.arange(n)
        band = np.zeros((len(POOL_WINDOWS), n, n), np.float32)
        inv = np.zeros((n, POOL_W), np.float32)
        for k, w in enumerate(POOL_WINDOWS):
            lo = np.clip(t - w // 2, 0, n)
            hi = np.clip(t - w // 2 + w, 0, n)
            band[k] = (t[None, :] >= lo[:, None]) & (t[None, :] < hi[:, None])
            inv[:, k * POOL_GC:(k + 1) * POOL_GC] = (1.0 / (hi - lo).astype(np.float64))[:, None]
        c["band_" + tag] = band
        c["inv_" + tag] = inv
        kk = np.arange(n, dtype=np.float64)[:, None]
        tt = np.arange(n, dtype=np.float64)[None, :]
        ang = np.pi * kk * tt / n
        dft = np.concatenate([np.cos(ang), -np.sin(ang)], axis=0)
        dft[n] = (-1.0) ** np.arange(n)
        c["dft_" + tag] = dft.astype(np.float32)
        c["dftt_" + tag] = np.ascontiguousarray(dft.T).astype(np.float32)
    n_rows = L_S // GRID_W
    row = np.repeat(np.arange(n_rows), GRID_W).astype(np.float32)
    col = np.tile(np.arange(GRID_W), n_rows).astype(np.float32)
    nf = ROPE // 4
    inv_f = (1.0 / ROPE_THETA ** (np.arange(nf, dtype=np.float32) / nf)).astype(np.float32)
    a_row = (row[:, None] * inv_f[None]).astype(np.float32).astype(np.float64)
    a_col = (col[:, None] * inv_f[None]).astype(np.float32).astype(np.float64)
    cos64 = np.concatenate([np.cos(a_row), np.cos(a_row), np.cos(a_col), np.cos(a_col)], axis=1)
    sin64 = np.concatenate([np.sin(a_row), np.sin(a_row), np.sin(a_col), np.sin(a_col)], axis=1)

    def table(reps):
        ident = np.stack([np.ones((L_S, ROPE * reps)), np.zeros((L_S, ROPE * reps))])
        rot = np.stack([np.tile(cos64, (1, reps)), np.tile(sin64, (1, reps))])
        return np.stack([ident, rot]).astype(np.float32)

    c["cs_q"] = table(H)
    c["cs_k"] = table(LANES // ROPE)
    for tag, cap in (("c", CAP_CTX), ("s", CAP_S)):
        ex = np.zeros((LANES, E * cap), np.float32)
        ex[np.arange(E * cap) // cap, np.arange(E * cap)] = 1.0
        c["ex_" + tag] = ex
    return c


def _constants():
    c = {k: jnp.asarray(v) for k, v in _np_constants().items()}
    for k in ("band_c", "band_s", "dft_c", "dftt_c", "dft_s", "dftt_s", "ex_c", "ex_s"):
        c[k] = c[k].astype(BF16)
    return c


def _rope_swap(w):
    q = ROPE // 4
    return jnp.concatenate([-w[:, q:2 * q], w[:, :q], -w[:, 3 * q:], w[:, 2 * q:3 * q]], axis=1)


def _hyena_tables(n, w1, b1, w2, b2, w3, freq, log_decay):
    t = np.linspace(0.0, 1.0, n, dtype=np.float32)[:, None]
    bands = np.arange(1, POS_BANDS + 1, dtype=np.float32)[None, :]
    feats = np.concatenate([t, np.sin(2 * np.pi * t * bands), np.cos(2 * np.pi * t * bands)], axis=1).astype(np.float32)
    z = jnp.asarray(feats)
    z = jnp.sin(freq * (z @ w1 + b1))
    z = jnp.sin(freq * (z @ w2 + b2))
    hf = (z @ w3).astype(F32)
    hf = hf * jnp.exp(-jnp.exp(log_decay.astype(F32)) * jnp.asarray(t))
    hf = hf.reshape(n, 2, 2, HY_W)
    hf = hf.at[0, :, 1].set(0.0)
    hf = hf / (jnp.sum(jnp.abs(hf), axis=(0, 2), keepdims=True) + EPS)
    tabs = []
    for o in range(2):
        k = jnp.concatenate([hf[:, o, 0], jnp.zeros((1, HY_W), F32), hf[:0:-1, o, 1]], axis=0)
        kf = jnp.fft.rfft(k, n=2 * n, axis=0)
        kr, ki = jnp.real(kf), jnp.imag(kf)
        sc = jnp.full((n, 1), 1.0 / n, F32).at[0, 0].set(0.5 / n)
        ka = kr[:n] * sc
        kb = (ki[:n] * sc).at[0].set(0.0)
        ka2 = ka.at[0].set(kr[n] * (0.5 / n))
        tabs.append(jnp.stack([ka, kb, ka2]))
    return jnp.stack(tabs).astype(F32)


def _prep_layer(l, w_in, pool_w, pool_scale, hy_short_w, hy_short_b, hy_skip, mla_q_norm, mla_kv_norm,
                mla_w_uq, mla_w_ukv, w_out, router_w, norm1_g, norm2_g):
    wi = w_in[l]
    kr_cols = wi[:, IN_COLS - ROPE:]
    kr_swap = _rope_swap(kr_cols)
    wi = jnp.concatenate([wi, kr_cols, kr_swap, kr_swap], axis=1).astype(BF16)
    wq = mla_w_uq[l].reshape(Q_RANK, H, NOPE + ROPE)
    wq_rope = wq[:, :, NOPE:]
    wq_swap = jnp.stack([_rope_swap(wq_rope[:, h]) for h in range(H)], axis=1)
    wq = jnp.concatenate([wq[:, :, :NOPE].reshape(Q_RANK, -1), wq_rope.reshape(Q_RANK, -1),
                          wq_swap.reshape(Q_RANK, -1)], axis=1).astype(BF16)
    wkv = mla_w_ukv[l].reshape(KV_RANK, H, NOPE + VD)
    wkv = jnp.concatenate([wkv[:, :, :NOPE].reshape(KV_RANK, -1), wkv[:, :, NOPE:].reshape(KV_RANK, -1)],
                          axis=1).astype(BF16)
    rw = jnp.pad(router_w[l], ((0, 0), (0, LANES - E))).astype(BF16)
    return dict(
        w_in=wi, wq=wq, wkv=wkv, w_out=w_out[l].astype(BF16), rw=rw,
        pool_w=pool_w[l].astype(BF16), pool_scale=pool_scale[l].reshape(1, POOL_W),
        sw=hy_short_w[l], sb=hy_short_b[l].reshape(1, 3 * HY_W), skip=hy_skip[l],
        qg=mla_q_norm[l].reshape(1, Q_RANK), kvg=mla_kv_norm[l].reshape(1, KV_RANK),
        g1=norm1_g[l].reshape(1, D), g2=norm2_g[l].reshape(1, D))


def _mixers(x, lp, mod3, consts, kt_c, kt_s, cache_ckv, krc, layer):
    proj = _inproj(x, lp["g1"], mod3, lp["w_in"])
    y_pool = _pool(proj, consts, lp["pool_w"], lp["pool_scale"])
    y_hy = _hyena(proj, lp["sw"], lp["sb"], lp["skip"], consts, kt_c, kt_s)
    q = _qproj(proj, lp["qg"], lp["wq"], consts["cs_q"])
    ckv, krr, kv = _kvprep(proj, lp["kvg"], consts["cs_k"], lp["wkv"])
    kvc = _cachekv(cache_ckv, layer, lp["wkv"])
    y_mla = _attention(q, kv, krr, kvc, krc)
    x = _wout(y_pool, y_hy, y_mla, lp["w_out"], x, mod3)
    return x, proj, ckv


def _moe(x, lp, mod3, consts, exp_w_gate, exp_w_up, exp_w_down, layer):
    h2, aff = _router(x, lp["g2"], mod3, lp["rw"])
    afft = jnp.swapaxes(aff[:, :E], 0, 1)
    rank = _rank(aff, afft)
    xs, gs = _gather(rank, afft, h2)
    ys = _ffn(xs, gs, exp_w_gate, exp_w_up, exp_w_down, layer)
    rank_t = jnp.pad(jnp.swapaxes(rank, 0, 1), ((0, 0), (0, LANES - E)))
    return _combine(rank_t, ys, x, mod3, consts)


def kernel(x_prompt, x_sample, cache_ckv, cache_krope, c, c_ctx, ada_w, ada_b, norm1_g, norm2_g, w_in, pool_w, pool_scale, hy_short_w, hy_short_b, hy_ffn_w1, hy_ffn_b1, hy_ffn_w2, hy_ffn_b2, hy_ffn_w3, hy_freq, hy_log_decay, hy_skip, mla_q_norm, mla_kv_norm, mla_w_uq, mla_w_ukv, w_out, router_w, exp_w_gate, exp_w_up, exp_w_down, final_norm_g):
    consts = _constants()
    x = jnp.concatenate([x_prompt.reshape(T_CTX, D), x_sample.reshape(T - T_CTX, D)], axis=0)
    c16 = jnp.concatenate([jnp.broadcast_to(c_ctx[None], (NG_CTX, D)), c], axis=0)
    mod = _adaln_mod(c16, ada_w, ada_b).reshape(DEPTH, NG * 6, 1, D)

    ckv_list, kr_list = [], []
    for l in range(DEPTH):
        lp = _prep_layer(l, w_in, pool_w, pool_scale, hy_short_w, hy_short_b, hy_skip, mla_q_norm,
                         mla_kv_norm, mla_w_uq, mla_w_ukv, w_out, router_w, norm1_g, norm2_g)
        filt = (hy_ffn_w1[l], hy_ffn_b1[l], hy_ffn_w2[l], hy_ffn_b2[l], hy_ffn_w3[l], hy_freq[l], hy_log_decay[l])
        kt_c = _hyena_tables(L_CTX, *filt)
        kt_s = _hyena_tables(L_S, *filt)
        krc = cache_krope[:, l].reshape(B_S * L_CTX, ROPE)
        krc = jnp.concatenate([krc, krc], axis=1).astype(BF16)
        x, proj, ckv = _mixers(x, lp, mod[l], consts, kt_c, kt_s, cache_ckv, krc, l)
        x = _moe(x, lp, mod[l], consts, exp_w_gate, exp_w_up, exp_w_down, l)
        ckv_list.append(ckv[:T_CTX].reshape(B_CTX, L_CTX, KV_RANK))
        kr_list.append(proj[:T_CTX, IN_COLS - ROPE:IN_COLS].reshape(B_CTX, L_CTX, ROPE))

    fg = final_norm_g.reshape(1, D)
    y_prompt = _final_norm(x, fg, 0, T_CTX // 512).reshape(B_CTX, L_CTX, D)
    y_sample = _final_norm(x, fg, T_CTX // 512, (T - T_CTX) // 512).reshape(B_S, L_S, D)
    return (y_prompt, y_sample, jnp.stack(ckv_list, axis=1), jnp.stack(kr_list, axis=1))
```

```python
import functools
import math

import numpy as np
import jax
import jax.numpy as jnp
from jax import lax
from jax.experimental import pallas as pl
from jax.experimental.pallas import tpu as pltpu

F32 = jnp.float32
BF16 = jnp.bfloat16

D = 2048
DEPTH = 2
B_CTX, L_CTX = 32, 256
B_S, L_S = 8, 1024
T_CTX = B_CTX * L_CTX
T = T_CTX + B_S * L_S
GRP = 1024
NG = T // GRP
NG_CTX = T_CTX // GRP
SEQ_PER_GRP = GRP // L_CTX
EPS = 1e-6
GRID_W = 64

POOL_W = 512
POOL_GC = 128
POOL_WINDOWS = (2, 4, 8, 16)
HY_W = 512
POS_BANDS = 8
FILT_HID = 64
H = 8
NOPE = 128
ROPE = 64
VD = 128
Q_RANK = 512
KV_RANK = 256
ROPE_THETA = 10000.0
IN_COLS = 2880
IN_PAD = 3072
E = 16
FF = 1024
CAP_CTX = 2 * L_CTX // E
CAP_S = 2 * L_S // E
SLOTS = GRP * 2 // E
ROWS_E = NG * SLOTS
LANES = 128
NCH = 512
ATT_SCALE = 1.0 / math.sqrt(NOPE + ROPE)
MB = 1024 * 1024


def _cp(sem, vmem_mb=48):
    return pltpu.CompilerParams(dimension_semantics=sem, vmem_limit_bytes=vmem_mb * MB)


def _rms(x, g):
    return x * lax.rsqrt(jnp.mean(x * x, axis=-1, keepdims=True) + EPS) * g


def _dot(a, b):
    return jnp.dot(a, b, preferred_element_type=F32)


def _dot_nt(a, b):
    return lax.dot_general(a, b, (((1,), (1,)), ((), ())), preferred_element_type=F32)


def _mod_kernel(c_ref, w_ref, b_ref, o_ref):
    c = c_ref[...]
    a = (c * jax.nn.sigmoid(c)).astype(BF16)
    o_ref[...] = _dot(a, w_ref[...].astype(BF16)) + b_ref[...]


def _adaln_mod(c16, ada_w, ada_b):
    tn = 1024
    return pl.pallas_call(
        _mod_kernel,
        grid=(DEPTH, 6 * D // tn),
        in_specs=[
            pl.BlockSpec((NG, D), lambda l, j: (0, 0)),
            pl.BlockSpec((None, D, tn), lambda l, j: (l, 0, j)),
            pl.BlockSpec((None, 1, tn), lambda l, j: (l, 0, j)),
        ],
        out_specs=pl.BlockSpec((None, NG, tn), lambda l, j: (l, 0, j)),
        out_shape=jax.ShapeDtypeStruct((DEPTH, NG, 6 * D), F32),
        compiler_params=_cp(("parallel", "parallel")),
        name="adaln_mod",
    )(c16, ada_w, ada_b.reshape(DEPTH, 1, 6 * D))


def _src_specs(x, width, col):
    if isinstance(x, tuple):
        return list(x), [
            pl.BlockSpec((GRP, width), lambda i, j: (jnp.minimum(i, NG_CTX - 1), col(j))),
            pl.BlockSpec((GRP, width), lambda i, j: (jnp.maximum(i - NG_CTX, 0), col(j))),
        ]
    return [x], [pl.BlockSpec((GRP, width), lambda i, j: (i, col(j)))]


def _for_src(x_refs, fn):
    if len(x_refs) == 1:
        fn(x_refs[0])
        return
    is_ctx = pl.program_id(0) < NG_CTX
    pl.when(is_ctx)(lambda: fn(x_refs[0]))
    pl.when(jnp.logical_not(is_ctx))(lambda: fn(x_refs[1]))


def _inproj_kernel(*refs):
    *x_refs, g_ref, shift_ref, scale_ref, w_ref, o_ref, h_ref = refs

    def norm(x_ref):
        y = _rms(x_ref[...], g_ref[...])
        h_ref[...] = (y * (1.0 + scale_ref[...]) + shift_ref[...]).astype(BF16)

    @pl.when(pl.program_id(1) == 0)
    def _():
        _for_src(x_refs, norm)

    o_ref[...] = _dot(h_ref[...], w_ref[...])


def _inproj(x, g1, mod3, w):
    tn = 512
    xs, x_specs = _src_specs(x, D, lambda j: 0)
    return pl.pallas_call(
        _inproj_kernel,
        grid=(NG, IN_PAD // tn),
        in_specs=x_specs + [
            pl.BlockSpec((1, D), lambda i, j: (0, 0)),
            pl.BlockSpec((None, 1, D), lambda i, j: (6 * i, 0, 0)),
            pl.BlockSpec((None, 1, D), lambda i, j: (6 * i + 1, 0, 0)),
            pl.BlockSpec((D, tn), lambda i, j: (0, j)),
        ],
        out_specs=pl.BlockSpec((GRP, tn), lambda i, j: (i, j)),
        out_shape=jax.ShapeDtypeStruct((T, IN_PAD), F32),
        scratch_shapes=[pltpu.VMEM((GRP, D), BF16)],
        compiler_params=_cp(("parallel", "arbitrary"), 56),
        name="in_proj",
    )(*xs, g1, mod3, mod3, w)


def _pool_kernel(u_ref, bc_ref, bs_ref, ic_ref, is_ref, pw_ref, ps_ref, o_ref):
    g = pl.program_id(0)

    def seq(r0, n, band_ref, inv_ref):
        for k in range(len(POOL_WINDOWS)):
            cs = slice(k * POOL_GC, (k + 1) * POOL_GC)
            u = u_ref[r0:r0 + n, cs]
            wsum = _dot(band_ref[k], u.astype(BF16))
            pooled = wsum * inv_ref[:, cs] - u
            y = _dot(pooled.astype(BF16), pw_ref[k]) * ps_ref[:, cs]
            o_ref[r0:r0 + n, cs] = y.astype(o_ref.dtype)

    @pl.when(g < NG_CTX)
    def _():
        for s in range(SEQ_PER_GRP):
            seq(s * L_CTX, L_CTX, bc_ref, ic_ref)

    @pl.when(g >= NG_CTX)
    def _():
        seq(0, L_S, bs_ref, is_ref)


def _pool(proj, consts, pw, ps):
    nw = len(POOL_WINDOWS)
    return pl.pallas_call(
        _pool_kernel,
        grid=(NG,),
        in_specs=[
            pl.BlockSpec((GRP, POOL_W), lambda g: (g, 0)),
            pl.BlockSpec((nw, L_CTX, L_CTX), lambda g: (0, 0, 0)),
            pl.BlockSpec((nw, L_S, L_S), lambda g: (0, 0, 0)),
            pl.BlockSpec((L_CTX, POOL_W), lambda g: (0, 0)),
            pl.BlockSpec((L_S, POOL_W), lambda g: (0, 0)),
            pl.BlockSpec((nw, POOL_GC, POOL_GC), lambda g: (0, 0, 0)),
            pl.BlockSpec((1, POOL_W), lambda g: (0, 0)),
        ],
        out_specs=pl.BlockSpec((GRP, POOL_W), lambda g: (g, 0)),
        out_shape=jax.ShapeDtypeStruct((T, POOL_W), BF16),
        compiler_params=_cp(("parallel",)),
        name="pool_mix",
    )(proj, consts["band_c"], consts["band_s"], consts["inv_c"], consts["inv_s"], pw, ps)


def _hyena_kernel(v_ref, x1_ref, x2_ref, swv_ref, sw1_ref, sw2_ref, sbv_ref, sb1_ref, sb2_ref,
                  skip_ref, wc_ref, wtc_ref, ws_ref, wts_ref, ktc_ref, kts_ref, o_ref):
    g = pl.program_id(1)

    def sconv(u, sw_ref, sb_ref, n):
        row = lax.broadcasted_iota(jnp.int32, u.shape, 0)
        prev = jnp.where(row == 0, 0.0, pltpu.roll(u, 1, 0))
        nxt = jnp.where(row == n - 1, 0.0, pltpu.roll(u, n - 1, 0))
        return prev * sw_ref[0:1, :] + u * sw_ref[1:2, :] + nxt * sw_ref[2:3, :] + sb_ref[...]

    def lconv(u, o, n, w_ref, wt_ref, kt_ref):
        spec = _dot(w_ref[...], u.astype(BF16))
        pr, pi = spec[:n], spec[n:]
        ka, kb, ka2 = kt_ref[o, 0], kt_ref[o, 1], kt_ref[o, 2]
        yr = pr * ka - pi * kb
        yi = pr * kb + pi * ka2
        prod = jnp.concatenate([yr, yi], axis=0).astype(BF16)
        return _dot(wt_ref[...], prod) + u * skip_ref[o:o + 1, :]

    def seq(r0, n, w_ref, wt_ref, kt_ref):
        v = sconv(v_ref[r0:r0 + n, :], swv_ref, sbv_ref, n)
        x1 = sconv(x1_ref[r0:r0 + n, :], sw1_ref, sb1_ref, n)
        x2 = sconv(x2_ref[r0:r0 + n, :], sw2_ref, sb2_ref, n)
        z = x1 * lconv(v, 0, n, w_ref, wt_ref, kt_ref)
        y = x2 * lconv(z, 1, n, w_ref, wt_ref, kt_ref)
        o_ref[r0:r0 + n, :] = y.astype(o_ref.dtype)

    @pl.when(g < NG_CTX)
    def _():
        for s in range(SEQ_PER_GRP):
            seq(s * L_CTX, L_CTX, wc_ref, wtc_ref, ktc_ref)

    @pl.when(g >= NG_CTX)
    def _():
        seq(0, L_S, ws_ref, wts_ref, kts_ref)


def _hyena(proj, sw, sb, skip, consts, kt_c, kt_s, layer):
    tc = 256
    nc = HY_W // tc
    c0 = POOL_W // tc

    def part(p):
        return pl.BlockSpec((GRP, tc), lambda c, g: (g, c0 + p * nc + c))

    def swpart(p):
        return pl.BlockSpec((3, tc), lambda c, g: (0, p * nc + c))

    def sbpart(p):
        return pl.BlockSpec((1, tc), lambda c, g: (0, p * nc + c))

    return pl.pallas_call(
        _hyena_kernel,
        grid=(nc, NG),
        in_specs=[
            part(0), part(1), part(2),
            swpart(0), swpart(1), swpart(2),
            sbpart(0), sbpart(1), sbpart(2),
            pl.BlockSpec((2, tc), lambda c, g: (0, c)),
            pl.BlockSpec((2 * L_CTX, L_CTX), lambda c, g: (0, 0)),
            pl.BlockSpec((L_CTX, 2 * L_CTX), lambda c, g: (0, 0)),
            pl.BlockSpec((2 * L_S, L_S), lambda c, g: (0, 0)),
            pl.BlockSpec((L_S, 2 * L_S), lambda c, g: (0, 0)),
            pl.BlockSpec((None, 2, 3, L_CTX, tc), lambda c, g: (layer, 0, 0, 0, c)),
            pl.BlockSpec((None, 2, 3, L_S, tc), lambda c, g: (layer, 0, 0, 0, c)),
        ],
        out_specs=pl.BlockSpec((GRP, tc), lambda c, g: (g, c)),
        out_shape=jax.ShapeDtypeStruct((T, HY_W), BF16),
        compiler_params=_cp(("parallel", "parallel"), 56),
        name="hyena_mix",
    )(proj, proj, proj, sw, sw, sw, sb, sb, sb, skip,
      consts["dft_c"], consts["dftt_c"], consts["dft_s"], consts["dftt_s"], kt_c, kt_s)


def _qproj_kernel(x_ref, g_ref, w_ref, cs_ref, o_ref):
    xn = _rms(x_ref[...], g_ref[...]).astype(BF16)
    acc = _dot(xn, w_ref[...]) * ATT_SCALE
    nq = H * NOPE
    nr = H * ROPE
    o_ref[:, :nq] = acc[:, :nq].astype(o_ref.dtype)
    rot = acc[:, nq:nq + nr] * cs_ref[0] + acc[:, nq + nr:] * cs_ref[1]
    o_ref[:, nq:] = rot.astype(o_ref.dtype)


def _qproj(proj, qg, wq, cs_q):
    tm = 512
    per = GRP // tm
    cb = (POOL_W + 3 * HY_W) // Q_RANK
    return pl.pallas_call(
        _qproj_kernel,
        grid=(T // tm,),
        in_specs=[
            pl.BlockSpec((tm, Q_RANK), lambda i: (i, cb)),
            pl.BlockSpec((1, Q_RANK), lambda i: (0, 0)),
            pl.BlockSpec((Q_RANK, H * (NOPE + 2 * ROPE)), lambda i: (0, 0)),
            pl.BlockSpec((None, 2, tm, H * ROPE), lambda i: (jnp.where(i >= NG_CTX * per, 1, 0), 0, i % per, 0)),
        ],
        out_specs=pl.BlockSpec((tm, H * (NOPE + ROPE)), lambda i: (i, 0)),
        out_shape=jax.ShapeDtypeStruct((T, H * (NOPE + ROPE)), BF16),
        compiler_params=_cp(("parallel",)),
        name="q_proj",
    )(proj, qg, wq, cs_q)


def _kvprep_kernel(x_ref, g_ref, cs_ref, w_ref, ckv_ref, krr_ref, kv_ref):
    ckv = _rms(x_ref[:, :KV_RANK], g_ref[...])
    ckv_ref[...] = ckv
    kr = x_ref[:, KV_RANK:KV_RANK + LANES]
    krs = x_ref[:, KV_RANK + LANES:]
    krr_ref[...] = (kr * cs_ref[0] + krs * cs_ref[1]).astype(krr_ref.dtype)
    kv_ref[...] = _dot(ckv.astype(BF16), w_ref[...]).astype(kv_ref.dtype)


def _kvprep(proj, kvg, cs_k, wkv):
    tm = 1024
    cb = (POOL_W + 3 * HY_W + Q_RANK) // 512
    return pl.pallas_call(
        _kvprep_kernel,
        grid=(T // tm,),
        in_specs=[
            pl.BlockSpec((tm, 512), lambda i: (i, cb)),
            pl.BlockSpec((1, KV_RANK), lambda i: (0, 0)),
            pl.BlockSpec((None, 2, tm, LANES), lambda i: (jnp.where(i >= NG_CTX, 1, 0), 0, 0, 0)),
            pl.BlockSpec((KV_RANK, H * (NOPE + VD)), lambda i: (0, 0)),
        ],
        out_specs=[
            pl.BlockSpec((tm, KV_RANK), lambda i: (i, 0)),
            pl.BlockSpec((tm, LANES), lambda i: (i, 0)),
            pl.BlockSpec((tm, H * (NOPE + VD)), lambda i: (i, 0)),
        ],
        out_shape=[
            jax.ShapeDtypeStruct((T, KV_RANK), F32),
            jax.ShapeDtypeStruct((T, LANES), BF16),
            jax.ShapeDtypeStruct((T, H * (NOPE + VD)), BF16),
        ],
        compiler_params=_cp(("parallel",)),
        name="kv_prep",
    )(proj, kvg, cs_k, wkv)


def _cachekv_kernel(x_ref, w_ref, o_ref):
    o_ref[...] = _dot(x_ref[...].astype(BF16), w_ref[...]).astype(o_ref.dtype)


def _cachekv(cache_ckv, layer, wkv):
    return pl.pallas_call(
        _cachekv_kernel,
        grid=(B_S,),
        in_specs=[
            pl.BlockSpec((None, None, L_CTX, KV_RANK), lambda b: (b, layer, 0, 0)),
            pl.BlockSpec((KV_RANK, H * (NOPE + VD)), lambda b: (0, 0)),
        ],
        out_specs=pl.BlockSpec((L_CTX, H * (NOPE + VD)), lambda b: (b, 0)),
        out_shape=jax.ShapeDtypeStruct((B_S * L_CTX, H * (NOPE + VD)), BF16),
        compiler_params=_cp(("parallel",)),
        name="cache_kv",
    )(cache_ckv, wkv)


ATT_TQ = 256
NK_S = L_S + L_CTX


def _attn_kernel(q_ref, kv_ref, krr_ref, kvc_ref, krc_ref, o_ref, kcat_ref):
    g = pl.program_id(0)
    lane = lax.broadcasted_iota(jnp.int32, (1, LANES), 1)
    hk = NOPE + LANES
    vo = H * NOPE

    krr = krr_ref[...]
    for h in range(H):
        kcat_ref[0:GRP, h * hk:h * hk + NOPE] = kv_ref[:, h * NOPE:(h + 1) * NOPE]
        kcat_ref[0:GRP, h * hk + NOPE:(h + 1) * hk] = krr

    def qcat(rows, h):
        qn = q_ref[rows, h * NOPE:(h + 1) * NOPE]
        pair = q_ref[rows, vo + (h // 2) * LANES:vo + (h // 2 + 1) * LANES].astype(F32)
        keep = (lane < ROPE) if h % 2 == 0 else (lane >= ROPE)
        return jnp.concatenate([qn, jnp.where(keep, pair, 0.0).astype(BF16)], axis=1)

    def probs(sc):
        m = jnp.max(sc, axis=-1, keepdims=True)
        p = jnp.exp(sc - m)
        return p.astype(BF16), 1.0 / jnp.sum(p, axis=-1, keepdims=True)

    @pl.when(g < NG_CTX)
    def _():
        def body(s, carry):
            rows = pl.ds(pl.multiple_of(s * L_CTX, L_CTX), L_CTX)
            for h in range(H):
                p, rl = probs(_dot_nt(qcat(rows, h), kcat_ref[rows, h * hk:(h + 1) * hk]))
                o = _dot(p, kv_ref[rows, vo + h * VD:vo + (h + 1) * VD]) * rl
                o_ref[rows, h * VD:(h + 1) * VD] = o.astype(o_ref.dtype)
            return carry

        lax.fori_loop(0, SEQ_PER_GRP, body, 0)

    @pl.when(g >= NG_CTX)
    def _():
        krc = krc_ref[...]
        for h in range(H):
            kcat_ref[GRP:NK_S, h * hk:h * hk + NOPE] = kvc_ref[:, h * NOPE:(h + 1) * NOPE]
            kcat_ref[GRP:NK_S, h * hk + NOPE:(h + 1) * hk] = krc

        def body(t, carry):
            rows = pl.ds(pl.multiple_of(t * ATT_TQ, ATT_TQ), ATT_TQ)
            for h in range(H):
                p, rl = probs(_dot_nt(qcat(rows, h), kcat_ref[:, h * hk:(h + 1) * hk]))
                o = _dot(p[:, :GRP], kv_ref[:, vo + h * VD:vo + (h + 1) * VD])
                o = o + _dot(p[:, GRP:], kvc_ref[:, vo + h * VD:vo + (h + 1) * VD])
                o_ref[rows, h * VD:(h + 1) * VD] = (o * rl).astype(o_ref.dtype)
            return carry

        lax.fori_loop(0, L_S // ATT_TQ, body, 0)


def _attention(q, kv, krr, kvc, krc):
    def cache_blk(g):
        return jnp.maximum(g - NG_CTX, 0)

    return pl.pallas_call(
        _attn_kernel,
        grid=(NG,),
        in_specs=[
            pl.BlockSpec((GRP, H * (NOPE + ROPE)), lambda g: (g, 0)),
            pl.BlockSpec((GRP, H * (NOPE + VD)), lambda g: (g, 0)),
            pl.BlockSpec((GRP, LANES), lambda g: (g, 0)),
            pl.BlockSpec((L_CTX, H * (NOPE + VD)), lambda g: (cache_blk(g), 0)),
            pl.BlockSpec((L_CTX, LANES), lambda g: (cache_blk(g), 0)),
        ],
        out_specs=pl.BlockSpec((GRP, H * VD), lambda g: (g, 0)),
        out_shape=jax.ShapeDtypeStruct((T, H * VD), BF16),
        scratch_shapes=[pltpu.VMEM((NK_S, H * (NOPE + LANES)), BF16)],
        compiler_params=_cp(("parallel",)),
        name="mla_attention",
    )(q, kv, krr, kvc, krc)


def _wout_kernel(yp_ref, yh_ref, ym_ref, wp_ref, wh_ref, wm_ref, gate_ref, *refs):
    *x_refs, o_ref = refs
    acc = _dot(yp_ref[...], wp_ref[...]) + _dot(yh_ref[...], wh_ref[...]) + _dot(ym_ref[...], wm_ref[...])

    def residual(x_ref):
        o_ref[...] = x_ref[...] + gate_ref[...] * acc

    _for_src(x_refs, residual)


def _wout(yp, yh, ym, w, x, mod3):
    tn = 512
    xs, x_specs = _src_specs(x, tn, lambda j: j)
    return pl.pallas_call(
        _wout_kernel,
        grid=(NG, D // tn),
        in_specs=[
            pl.BlockSpec((GRP, POOL_W), lambda i, j: (i, 0)),
            pl.BlockSpec((GRP, HY_W), lambda i, j: (i, 0)),
            pl.BlockSpec((GRP, H * VD), lambda i, j: (i, 0)),
            pl.BlockSpec((POOL_W, tn), lambda i, j: (0, j)),
            pl.BlockSpec((HY_W, tn), lambda i, j: (1, j)),
            pl.BlockSpec((H * VD, tn), lambda i, j: (1, j)),
            pl.BlockSpec((None, 1, tn), lambda i, j: (6 * i + 2, 0, j)),
        ] + x_specs,
        out_specs=pl.BlockSpec((GRP, tn), lambda i, j: (i, j)),
        out_shape=jax.ShapeDtypeStruct((T, D), F32),
        compiler_params=_cp(("parallel", "parallel")),
        name="out_proj",
    )(yp, yh, ym, w, w, w, mod3, *xs)


def _router_kernel(x_ref, g_ref, shift_ref, scale_ref, rw_ref, h_ref, aff_ref):
    y = _rms(x_ref[...], g_ref[...])
    hb = (y * (1.0 + scale_ref[...]) + shift_ref[...]).astype(BF16)
    h_ref[...] = hb
    logits = _dot(hb, rw_ref[...])
    lane = lax.broadcasted_iota(jnp.int32, logits.shape, 1)
    logits = jnp.where(lane < E, logits, -jnp.inf)
    ex = jnp.exp(logits - jnp.max(logits, axis=-1, keepdims=True))
    aff_ref[...] = ex / jnp.sum(ex, axis=-1, keepdims=True)


def _router(x, g2, mod3, rw):
    tm = 512
    per = GRP // tm
    return pl.pallas_call(
        _router_kernel,
        grid=(T // tm,),
        in_specs=[
            pl.BlockSpec((tm, D), lambda i: (i, 0)),
            pl.BlockSpec((1, D), lambda i: (0, 0)),
            pl.BlockSpec((None, 1, D), lambda i: (6 * (i // per) + 3, 0, 0)),
            pl.BlockSpec((None, 1, D), lambda i: (6 * (i // per) + 4, 0, 0)),
            pl.BlockSpec((D, LANES), lambda i: (0, 0)),
        ],
        out_specs=[
            pl.BlockSpec((tm, D), lambda i: (i, 0)),
            pl.BlockSpec((tm, LANES), lambda i: (i, 0)),
        ],
        out_shape=[
            jax.ShapeDtypeStruct((T, D), BF16),
            jax.ShapeDtypeStruct((T, LANES), F32),
        ],
        compiler_params=_cp(("parallel",)),
        name="moe_router",
    )(x, g2, mod3, mod3, rw)


RANK_CH = 256


def _rank_kernel(aff_ref, afft_ref, rank_ref):
    g = pl.program_id(0)

    def seq(r0, n):
        lane_i = lax.broadcasted_iota(jnp.int32, (RANK_CH, n), 1)
        sub_i = lax.broadcasted_iota(jnp.int32, (RANK_CH, n), 0)
        earlier = [jnp.where(sub_i + c * RANK_CH < lane_i, 1.0, 0.0) for c in range(n // RANK_CH)]
        for e in range(E):
            row = afft_ref[e:e + 1, r0:r0 + n]
            acc = jnp.zeros((1, n), F32)
            for c in range(n // RANK_CH):
                col = aff_ref[r0 + c * RANK_CH:r0 + (c + 1) * RANK_CH, e:e + 1]
                beats = jnp.where(col > row, 1.0, jnp.where(col == row, earlier[c], 0.0))
                acc = acc + jnp.sum(beats, axis=0, keepdims=True)
            rank_ref[e:e + 1, r0:r0 + n] = acc.astype(jnp.int32)

    @pl.when(g < NG_CTX)
    def _():
        for s in range(SEQ_PER_GRP):
            seq(s * L_CTX, L_CTX)

    @pl.when(g >= NG_CTX)
    def _():
        seq(0, L_S)


def _rank(aff, afft):
    return pl.pallas_call(
        _rank_kernel,
        grid=(NG,),
        in_specs=[
            pl.BlockSpec((GRP, LANES), lambda g: (g, 0)),
            pl.BlockSpec((E, GRP), lambda g: (0, g)),
        ],
        out_specs=pl.BlockSpec((E, GRP), lambda g: (0, g)),
        out_shape=jax.ShapeDtypeStruct((E, T), jnp.int32),
        compiler_params=_cp(("parallel",)),
        name="moe_rank",
    )(aff, afft)


def _gather_kernel(rank_ref, afft_ref, h_ref, xs_ref, gs_ref, sel_ref):
    g = pl.program_id(0)

    def build(s, r0, n, cap, slot0):
        slot_i = lax.broadcasted_iota(jnp.int32, (cap, n), 0)
        for e in range(E):
            hit = slot_i == rank_ref[e:e + 1, r0:r0 + n]
            sel_ref[(s * E + e) * cap:(s * E + e + 1) * cap, 0:n] = jnp.where(hit, 1.0, 0.0).astype(BF16)
            gs_ref[e, slot0:slot0 + cap, :] = jnp.sum(
                jnp.where(hit, afft_ref[e:e + 1, r0:r0 + n], 0.0), axis=1, keepdims=True)

    def move(s, r0, n, cap, slot0):
        res = _dot(sel_ref[s * E * cap:(s + 1) * E * cap, 0:n], h_ref[r0:r0 + n, :])
        for e in range(E):
            xs_ref[e, slot0:slot0 + cap, :] = res[e * cap:(e + 1) * cap].astype(xs_ref.dtype)

    def both(fn):
        @pl.when(g < NG_CTX)
        def _():
            for s in range(SEQ_PER_GRP):
                fn(s, s * L_CTX, L_CTX, CAP_CTX, s * CAP_CTX)

        @pl.when(g >= NG_CTX)
        def _():
            fn(0, 0, L_S, CAP_S, 0)

    @pl.when(pl.program_id(1) == 0)
    def _():
        both(build)

    both(move)


def _gather(rank, afft, h2):
    return pl.pallas_call(
        _gather_kernel,
        grid=(NG, D // NCH),
        in_specs=[
            pl.BlockSpec((E, GRP), lambda g, j: (0, g)),
            pl.BlockSpec((E, GRP), lambda g, j: (0, g)),
            pl.BlockSpec((GRP, NCH), lambda g, j: (g, j)),
        ],
        out_specs=[
            pl.BlockSpec((E, SLOTS, NCH), lambda g, j: (0, g, j)),
            pl.BlockSpec((E, SLOTS, 1), lambda g, j: (0, g, 0)),
        ],
        out_shape=[
            jax.ShapeDtypeStruct((E, ROWS_E, D), BF16),
            jax.ShapeDtypeStruct((E, ROWS_E, 1), F32),
        ],
        scratch_shapes=[pltpu.VMEM((E * CAP_S, L_S), BF16)],
        compiler_params=_cp(("parallel", "arbitrary")),
        name="moe_gather",
    )(rank, afft, h2)


FFN_TF = 256
FFN_TM = 1024


def _ffn_kernel(x_ref, wg_ref, wu_ref, wd_ref, gs_ref, o_ref, acc_ref):
    f = pl.program_id(2)

    @pl.when(f == 0)
    def _():
        acc_ref[...] = jnp.zeros_like(acc_ref)

    x = x_ref[...]
    a = _dot(x, wg_ref[...].astype(BF16))
    b = _dot(x, wu_ref[...].astype(BF16))
    hid = (a * jax.nn.sigmoid(a) * b).astype(BF16)
    acc_ref[...] += _dot(hid, wd_ref[...].astype(BF16))

    @pl.when(f == pl.num_programs(2) - 1)
    def _():
        o_ref[...] = (acc_ref[...] * gs_ref[...]).astype(o_ref.dtype)


def _ffn(xs, gs, w_gate, w_up, w_down, layer):
    return pl.pallas_call(
        _ffn_kernel,
        grid=(E, ROWS_E // FFN_TM, FF // FFN_TF),
        in_specs=[
            pl.BlockSpec((None, FFN_TM, D), lambda e, m, f: (e, m, 0)),
            pl.BlockSpec((None, None, D, FFN_TF), lambda e, m, f: (layer, e, 0, f)),
            pl.BlockSpec((None, None, D, FFN_TF), lambda e, m, f: (layer, e, 0, f)),
            pl.BlockSpec((None, None, FFN_TF, D), lambda e, m, f: (layer, e, f, 0)),
            pl.BlockSpec((None, FFN_TM, 1), lambda e, m, f: (e, m, 0)),
        ],
        out_specs=pl.BlockSpec((None, FFN_TM, D), lambda e, m, f: (e, m, 0)),
        out_shape=jax.ShapeDtypeStruct((E, ROWS_E, D), BF16),
        scratch_shapes=[pltpu.VMEM((FFN_TM, D), F32)],
        compiler_params=_cp(("parallel", "parallel", "arbitrary"), 56),
        name="moe_ffn",
    )(xs, w_gate, w_up, w_down, gs)


def _combine_kernel(rt_ref, ys_ref, x_ref, gate_ref, exc_ref, exs_ref, o_ref, st_ref):
    g = pl.program_id(0)

    def build(r0, n, cap, slot0, ex_ref):
        r = jnp.minimum(rt_ref[r0:r0 + n, :], cap).astype(F32).astype(BF16)
        want = (lax.broadcasted_iota(jnp.int32, (1, NCH), 1) & (cap - 1)).astype(F32)
        for c in range(E * cap // NCH):
            cs = slice(c * NCH, (c + 1) * NCH)
            st_ref[r0:r0 + n, cs] = jnp.where(_dot(r, ex_ref[:, cs]) == want, 1.0, 0.0).astype(BF16)

    def move(r0, n, cap, slot0, ex_ref):
        ys = ys_ref[:, slot0:slot0 + cap, :].reshape(E * cap, NCH)
        moe = _dot(st_ref[r0:r0 + n, 0:E * cap], ys)
        o_ref[r0:r0 + n, :] = x_ref[r0:r0 + n, :] + gate_ref[...] * moe

    def both(fn):
        @pl.when(g < NG_CTX)
        def _():
            for s in range(SEQ_PER_GRP):
                fn(s * L_CTX, L_CTX, CAP_CTX, s * CAP_CTX, exc_ref)

        @pl.when(g >= NG_CTX)
        def _():
            fn(0, L_S, CAP_S, 0, exs_ref)

    @pl.when(pl.program_id(1) == 0)
    def _():
        both(build)

    both(move)


def _combine(rank_t, ys, x, mod3, consts):
    return pl.pallas_call(
        _combine_kernel,
        grid=(NG, D // NCH),
        in_specs=[
            pl.BlockSpec((GRP, LANES), lambda g, j: (g, 0)),
            pl.BlockSpec((E, SLOTS, NCH), lambda g, j: (0, g, j)),
            pl.BlockSpec((GRP, NCH), lambda g, j: (g, j)),
            pl.BlockSpec((None, 1, NCH), lambda g, j: (6 * g + 5, 0, j)),
            pl.BlockSpec((LANES, E * CAP_CTX), lambda g, j: (0, 0)),
            pl.BlockSpec((LANES, E * CAP_S), lambda g, j: (0, 0)),
        ],
        out_specs=pl.BlockSpec((GRP, NCH), lambda g, j: (g, j)),
        out_shape=jax.ShapeDtypeStruct((T, D), F32),
        scratch_shapes=[pltpu.VMEM((L_S, E * CAP_S), BF16)],
        compiler_params=_cp(("parallel", "arbitrary")),
        name="moe_combine",
    )(rank_t, ys, x, mod3, consts["ex_c"], consts["ex_s"])


def _fnorm_kernel(x_ref, g_ref, o_ref):
    o_ref[...] = _rms(x_ref[...], g_ref[...])


def _final_norm(x, g, blk0, nblk):
    tm = 512
    return pl.pallas_call(
        _fnorm_kernel,
        grid=(nblk,),
        in_specs=[
            pl.BlockSpec((tm, D), lambda i: (blk0 + i, 0)),
            pl.BlockSpec((1, D), lambda i: (0, 0)),
        ],
        out_specs=pl.BlockSpec((tm, D), lambda i: (i, 0)),
        out_shape=jax.ShapeDtypeStruct((nblk * tm, D), F32),
        compiler_params=_cp(("parallel",)),
        name="final_norm",
    )(x, g)


def _np_constants():
    c = {}
    for tag, n in (("c", L_CTX), ("s", L_S)):
        t = np.arange(n)
        band = np.zeros((len(POOL_WINDOWS), n, n), np.float32)
        inv = np.zeros((n, POOL_W), np.float32)
        for k, w in enumerate(POOL_WINDOWS):
            lo = np.clip(t - w // 2, 0, n)
            hi = np.clip(t - w // 2 + w, 0, n)
            band[k] = (t[None, :] >= lo[:, None]) & (t[None, :] < hi[:, None])
            inv[:, k * POOL_GC:(k + 1) * POOL_GC] = (1.0 / (hi - lo).astype(np.float64))[:, None]
        c["band_" + tag] = band
        c["inv_" + tag] = inv
        kk = np.arange(n, dtype=np.float64)[:, None]
        tt = np.arange(n, dtype=np.float64)[None, :]
        ang = np.pi * kk * tt / n
        dft = np.concatenate([np.cos(ang), -np.sin(ang)], axis=0)
        dft[n] = (-1.0) ** np.arange(n)
        c["dft_" + tag] = dft.astype(np.float32)
        c["dftt_" + tag] = np.ascontiguousarray(dft.T).astype(np.float32)
        tl = np.linspace(0.0, 1.0, n, dtype=np.float32)[:, None]
        bands = np.arange(1, POS_BANDS + 1, dtype=np.float32)[None, :]
        feats = np.concatenate([tl, np.sin(2 * np.pi * tl * bands), np.cos(2 * np.pi * tl * bands)], axis=1)
        c["feat_" + tag] = np.pad(feats.astype(np.float32), ((0, 0), (0, LANES - feats.shape[1])))
        c["t_" + tag] = tl
    n_rows = L_S // GRID_W
    row = np.repeat(np.arange(n_rows), GRID_W).astype(np.float32)
    col = np.tile(np.arange(GRID_W), n_rows).astype(np.float32)
    nf = ROPE // 4
    inv_f = (1.0 / ROPE_THETA ** (np.arange(nf, dtype=np.float32) / nf)).astype(np.float32)
    a_row = (row[:, None] * inv_f[None]).astype(np.float32).astype(np.float64)
    a_col = (col[:, None] * inv_f[None]).astype(np.float32).astype(np.float64)
    cos64 = np.concatenate([np.cos(a_row), np.cos(a_row), np.cos(a_col), np.cos(a_col)], axis=1)
    sin64 = np.concatenate([np.sin(a_row), np.sin(a_row), np.sin(a_col), np.sin(a_col)], axis=1)

    def table(reps):
        ident = np.stack([np.ones((L_S, ROPE * reps)), np.zeros((L_S, ROPE * reps))])
        rot = np.stack([np.tile(cos64, (1, reps)), np.tile(sin64, (1, reps))])
        return np.stack([ident, rot]).astype(np.float32)

    c["cs_q"] = table(H)
    c["cs_k"] = table(LANES // ROPE)
    for tag, cap in (("c", CAP_CTX), ("s", CAP_S)):
        ex = np.zeros((LANES, E * cap), np.float32)
        ex[np.arange(E * cap) // cap, np.arange(E * cap)] = 1.0
        c["ex_" + tag] = ex
    return c


def _constants():
    c = {k: jnp.asarray(v) for k, v in _np_constants().items()}
    for k in ("band_c", "band_s", "dft_c", "dftt_c", "dft_s", "dftt_s", "ex_c", "ex_s"):
        c[k] = c[k].astype(BF16)
    return c


def _rope_swap(w):
    q = ROPE // 4
    return jnp.concatenate([-w[:, q:2 * q], w[:, :q], -w[:, 3 * q:], w[:, 2 * q:3 * q]], axis=1)


def _dot_hi(a, b):
    return jnp.dot(a, b, preferred_element_type=F32, precision=lax.Precision.HIGHEST)


def _filter_kernel(feat_ref, t_ref, w1_ref, b1_ref, w2_ref, b2_ref, fr_ref, w3f_ref, w3b_ref,
                   ldf_ref, ldb_ref, dft_ref, o_ref, z_ref):
    n = feat_ref.shape[0]

    @pl.when((pl.program_id(1) == 0) & (pl.program_id(2) == 0))
    def _():
        fr = fr_ref[...]
        z = jnp.sin(fr * (_dot_hi(feat_ref[...], w1_ref[...]) + b1_ref[...]))
        z_ref[...] = jnp.sin(fr * (_dot_hi(z, w2_ref[...]) + b2_ref[...]))

    z = z_ref[...]
    t = t_ref[...]
    first = lax.broadcasted_iota(jnp.int32, (n, 1), 0) == 0
    hf = _dot_hi(z, w3f_ref[...]) * jnp.exp(-jnp.exp(ldf_ref[...]) * t)
    hb = _dot_hi(z, w3b_ref[...]) * jnp.exp(-jnp.exp(ldb_ref[...]) * t)
    hb = jnp.where(first, 0.0, hb)
    norm = jnp.sum(jnp.abs(hf), axis=0, keepdims=True) + jnp.sum(jnp.abs(hb), axis=0, keepdims=True) + EPS
    pf = _dot(dft_ref[...], (hf / norm).astype(BF16))
    pb = _dot(dft_ref[...], (hb / norm).astype(BF16))
    sc = jnp.where(first, 0.5 / n, 1.0 / n)
    ka = (pf[:n] + pb[:n]) * sc
    o_ref[0] = ka
    o_ref[1] = jnp.where(first, 0.0, (pf[n:] - pb[n:]) * sc)
    o_ref[2] = jnp.where(first, (pf[n:] + pb[n:]) * sc, ka)


def _hyena_tables(n, feats, tcol, dft, fw):
    tc = 256
    nc = HY_W // tc

    def lay(shape):
        return pl.BlockSpec((None,) + shape, lambda l, o, c: (l, 0, 0))

    def w3(back):
        return pl.BlockSpec((None, LANES, tc), lambda l, o, c: (l, 0, (2 * o + back) * nc + c))

    def ld(back):
        return pl.BlockSpec((None, 1, tc), lambda l, o, c: (l, 0, (2 * o + back) * nc + c))

    return pl.pallas_call(
        _filter_kernel,
        grid=(DEPTH, 2, nc),
        in_specs=[
            pl.BlockSpec((n, LANES), lambda l, o, c: (0, 0)),
            pl.BlockSpec((n, 1), lambda l, o, c: (0, 0)),
            lay((LANES, LANES)), lay((1, LANES)), lay((LANES, LANES)), lay((1, LANES)), lay((1, LANES)),
            w3(0), w3(1), ld(0), ld(1),
            pl.BlockSpec((2 * n, n), lambda l, o, c: (0, 0)),
        ],
        out_specs=pl.BlockSpec((None, None, 3, n, tc), lambda l, o, c: (l, o, 0, 0, c)),
        out_shape=jax.ShapeDtypeStruct((DEPTH, 2, 3, n, HY_W), F32),
        scratch_shapes=[pltpu.VMEM((n, LANES), F32)],
        compiler_params=_cp(("arbitrary", "arbitrary", "arbitrary")),
        name="hyena_filter",
    )(feats, tcol, fw["w1"], fw["b1"], fw["w2"], fw["b2"], fw["freq"], fw["w3"], fw["w3"],
      fw["ld"], fw["ld"], dft)


def _filter_weights(w1, b1, w2, b2, w3, freq, log_decay):
    ph = LANES - FILT_HID
    return dict(
        w1=jnp.pad(w1, ((0, 0), (0, LANES - w1.shape[1]), (0, ph))),
        b1=jnp.pad(b1, ((0, 0), (0, ph))).reshape(DEPTH, 1, LANES),
        w2=jnp.pad(w2, ((0, 0), (0, ph), (0, ph))),
        b2=jnp.pad(b2, ((0, 0), (0, ph))).reshape(DEPTH, 1, LANES),
        freq=jnp.pad(freq, ((0, 0), (0, ph))).reshape(DEPTH, 1, LANES),
        w3=jnp.pad(w3, ((0, 0), (0, ph), (0, 0))),
        ld=log_decay.reshape(DEPTH, 1, 4 * HY_W))


def _prep_layer(l, w_in, pool_w, pool_scale, hy_short_w, hy_short_b, hy_skip, mla_q_norm, mla_kv_norm,
                mla_w_uq, mla_w_ukv, w_out, router_w, norm1_g, norm2_g):
    wi = w_in[l]
    kr_cols = wi[:, IN_COLS - ROPE:]
    kr_swap = _rope_swap(kr_cols)
    wi = jnp.concatenate([wi, kr_cols, kr_swap, kr_swap], axis=1).astype(BF16)
    wq = mla_w_uq[l].reshape(Q_RANK, H, NOPE + ROPE)
    wq_rope = wq[:, :, NOPE:]
    wq_swap = jnp.stack([_rope_swap(wq_rope[:, h]) for h in range(H)], axis=1)
    wq = jnp.concatenate([wq[:, :, :NOPE].reshape(Q_RANK, -1), wq_rope.reshape(Q_RANK, -1),
                          wq_swap.reshape(Q_RANK, -1)], axis=1).astype(BF16)
    wkv = mla_w_ukv[l].reshape(KV_RANK, H, NOPE + VD)
    wkv = jnp.concatenate([wkv[:, :, :NOPE].reshape(KV_RANK, -1), wkv[:, :, NOPE:].reshape(KV_RANK, -1)],
                          axis=1).astype(BF16)
    rw = jnp.pad(router_w[l], ((0, 0), (0, LANES - E))).astype(BF16)
    return dict(
        w_in=wi, wq=wq, wkv=wkv, w_out=w_out[l].astype(BF16), rw=rw,
        pool_w=pool_w[l].astype(BF16), pool_scale=pool_scale[l].reshape(1, POOL_W),
        sw=hy_short_w[l], sb=hy_short_b[l].reshape(1, 3 * HY_W), skip=hy_skip[l],
        qg=mla_q_norm[l].reshape(1, Q_RANK), kvg=mla_kv_norm[l].reshape(1, KV_RANK),
        g1=norm1_g[l].reshape(1, D), g2=norm2_g[l].reshape(1, D))


def _mixers(x, lp, mod3, consts, kt_c, kt_s, cache_ckv, krc, layer):
    proj = _inproj(x, lp["g1"], mod3, lp["w_in"])
    y_pool = _pool(proj, consts, lp["pool_w"], lp["pool_scale"])
    y_hy = _hyena(proj, lp["sw"], lp["sb"], lp["skip"], consts, kt_c, kt_s, layer)
    q = _qproj(proj, lp["qg"], lp["wq"], consts["cs_q"])
    ckv, krr, kv = _kvprep(proj, lp["kvg"], consts["cs_k"], lp["wkv"])
    kvc = _cachekv(cache_ckv, layer, lp["wkv"])
    y_mla = _attention(q, kv, krr, kvc, krc)
    x = _wout(y_pool, y_hy, y_mla, lp["w_out"], x, mod3)
    return x, proj, ckv


def _moe(x, lp, mod3, consts, exp_w_gate, exp_w_up, exp_w_down, layer):
    h2, aff = _router(x, lp["g2"], mod3, lp["rw"])
    afft = jnp.swapaxes(aff[:, :E], 0, 1)
    rank = _rank(aff, afft)
    xs, gs = _gather(rank, afft, h2)
    ys = _ffn(xs, gs, exp_w_gate, exp_w_up, exp_w_down, layer)
    rank_t = jnp.pad(jnp.swapaxes(rank, 0, 1), ((0, 0), (0, LANES - E)))
    return _combine(rank_t, ys, x, mod3, consts)


def kernel(x_prompt, x_sample, cache_ckv, cache_krope, c, c_ctx, ada_w, ada_b, norm1_g, norm2_g, w_in, pool_w, pool_scale, hy_short_w, hy_short_b, hy_ffn_w1, hy_ffn_b1, hy_ffn_w2, hy_ffn_b2, hy_ffn_w3, hy_freq, hy_log_decay, hy_skip, mla_q_norm, mla_kv_norm, mla_w_uq, mla_w_ukv, w_out, router_w, exp_w_gate, exp_w_up, exp_w_down, final_norm_g):
    consts = _constants()
    x = (x_prompt.reshape(T_CTX, D), x_sample.reshape(T - T_CTX, D))
    c16 = jnp.concatenate([jnp.broadcast_to(c_ctx[None], (NG_CTX, D)), c], axis=0)
    mod = _adaln_mod(c16, ada_w, ada_b).reshape(DEPTH, NG * 6, 1, D)

    fw = _filter_weights(hy_ffn_w1, hy_ffn_b1, hy_ffn_w2, hy_ffn_b2, hy_ffn_w3, hy_freq, hy_log_decay)
    kt_c = _hyena_tables(L_CTX, consts["feat_c"], consts["t_c"], consts["dft_c"], fw)
    kt_s = _hyena_tables(L_S, consts["feat_s"], consts["t_s"], consts["dft_s"], fw)

    ckv_list, kr_list = [], []
    for l in range(DEPTH):
        lp = _prep_layer(l, w_in, pool_w, pool_scale, hy_short_w, hy_short_b, hy_skip, mla_q_norm,
                         mla_kv_norm, mla_w_uq, mla_w_ukv, w_out, router_w, norm1_g, norm2_g)
        krc = cache_krope[:, l].reshape(B_S * L_CTX, ROPE)
        krc = jnp.concatenate([krc, krc], axis=1).astype(BF16)
        x, proj, ckv = _mixers(x, lp, mod[l], consts, kt_c, kt_s, cache_ckv, krc, l)
        x = _moe(x, lp, mod[l], consts, exp_w_gate, exp_w_up, exp_w_down, l)
        ckv_list.append(ckv[:T_CTX].reshape(B_CTX, L_CTX, KV_RANK))
        kr_list.append(proj[:T_CTX, IN_COLS - ROPE:IN_COLS].reshape(B_CTX, L_CTX, ROPE))

    fg = final_norm_g.reshape(1, D)
    y_prompt = _final_norm(x, fg, 0, T_CTX // 512).reshape(B_CTX, L_CTX, D)
    y_sample = _final_norm(x, fg, T_CTX // 512, (T - T_CTX) // 512).reshape(B_S, L_S, D)
    return (y_prompt, y_sample, jnp.stack(ckv_list, axis=1), jnp.stack(kr_list, axis=1))
```

```python
import functools
import math

import numpy as np
import jax
import jax.numpy as jnp
from jax import lax
from jax.experimental import pallas as pl
from jax.experimental.pallas import tpu as pltpu

F32 = jnp.float32
BF16 = jnp.bfloat16

D = 2048
DEPTH = 2
B_CTX, L_CTX = 32, 256
B_S, L_S = 8, 1024
T_CTX = B_CTX * L_CTX
T = T_CTX + B_S * L_S
GRP = 1024
NG = T // GRP
NG_CTX = T_CTX // GRP
SEQ_PER_GRP = GRP // L_CTX
EPS = 1e-6
GRID_W = 64

POOL_W = 512
POOL_GC = 128
POOL_WINDOWS = (2, 4, 8, 16)
HY_W = 512
POS_BANDS = 8
FILT_HID = 64
H = 8
NOPE = 128
ROPE = 64
VD = 128
Q_RANK = 512
KV_RANK = 256
ROPE_THETA = 10000.0
IN_COLS = 2880
IN_PAD = 3072
E = 16
FF = 1024
CAP_CTX = 2 * L_CTX // E
CAP_S = 2 * L_S // E
SLOTS = GRP * 2 // E
ROWS_E = NG * SLOTS
LANES = 128
NCH = 512
ATT_SCALE = 1.0 / math.sqrt(NOPE + ROPE)
MB = 1024 * 1024


def _cp(sem, vmem_mb=48):
    return pltpu.CompilerParams(dimension_semantics=sem, vmem_limit_bytes=vmem_mb * MB)


def _rms(x, g):
    return x * lax.rsqrt(jnp.mean(x * x, axis=-1, keepdims=True) + EPS) * g


def _dot(a, b):
    return jnp.dot(a, b, preferred_element_type=F32)


def _dot_nt(a, b):
    return lax.dot_general(a, b, (((1,), (1,)), ((), ())), preferred_element_type=F32)


def _mod_kernel(c_ref, w_ref, b_ref, o_ref):
    c = c_ref[...]
    a = (c * jax.nn.sigmoid(c)).astype(BF16)
    o_ref[...] = _dot(a, w_ref[...].astype(BF16)) + b_ref[...]


def _adaln_mod(c16, ada_w, ada_b):
    tn = 1024
    return pl.pallas_call(
        _mod_kernel,
        grid=(DEPTH, 6 * D // tn),
        in_specs=[
            pl.BlockSpec((NG, D), lambda l, j: (0, 0)),
            pl.BlockSpec((None, D, tn), lambda l, j: (l, 0, j)),
            pl.BlockSpec((None, 1, tn), lambda l, j: (l, 0, j)),
        ],
        out_specs=pl.BlockSpec((None, NG, tn), lambda l, j: (l, 0, j)),
        out_shape=jax.ShapeDtypeStruct((DEPTH, NG, 6 * D), F32),
        compiler_params=_cp(("parallel", "parallel")),
        name="adaln_mod",
    )(c16, ada_w, ada_b.reshape(DEPTH, 1, 6 * D))


def _src_specs(x, tm):
    n_ctx = T_CTX // tm
    if isinstance(x, tuple):
        return list(x), [
            pl.BlockSpec((tm, D), lambda i: (jnp.minimum(i, n_ctx - 1), 0)),
            pl.BlockSpec((tm, D), lambda i: (jnp.maximum(i - n_ctx, 0), 0)),
        ]
    return [x], [pl.BlockSpec((tm, D), lambda i: (i, 0))]


def _for_src(x_refs, tm, fn):
    if len(x_refs) == 1:
        fn(x_refs[0])
        return
    is_ctx = pl.program_id(0) < T_CTX // tm
    pl.when(is_ctx)(lambda: fn(x_refs[0]))
    pl.when(jnp.logical_not(is_ctx))(lambda: fn(x_refs[1]))


def _resident(shape, layer):
    nd = len(shape)
    return pl.BlockSpec((None,) + shape, lambda i: (layer,) + (0,) * nd, pipeline_mode=pl.Buffered(1))


def _mod_row(k, per):
    return pl.BlockSpec((None, 1, D), lambda i: (6 * (i // per) + k, 0, 0))


N_U = POOL_W + 3 * HY_W
N_QR = H * ROPE


def _inproj_kernel(*refs, tm):
    (*x_refs, g_ref, shift_ref, scale_ref, wm_ref, wx_ref, qg_ref, wq_ref, csq_ref, kvg_ref, csk_ref, wkv_ref,
     u_ref, q_ref, ckv_ref, kr_ref, krr_ref, kv_ref, h_ref) = refs

    def norm(x_ref):
        y = _rms(x_ref[...], g_ref[...])
        h_ref[...] = (y * (1.0 + scale_ref[...]) + shift_ref[...]).astype(BF16)

    _for_src(x_refs, tm, norm)
    h = h_ref[...]
    for c in range(N_U // NCH):
        cs = slice(c * NCH, (c + 1) * NCH)
        u_ref[:, cs] = _dot(h, wm_ref[:, cs]).astype(u_ref.dtype)

    qn = _rms(_dot(h, wm_ref[:, N_U:]), qg_ref[...]).astype(BF16)
    nq = H * NOPE
    for c in range(nq // NCH):
        cs = slice(c * NCH, (c + 1) * NCH)
        q_ref[:, cs] = (_dot(qn, wq_ref[:, cs]) * ATT_SCALE).astype(q_ref.dtype)
    rot = _dot(qn, wq_ref[:, nq:nq + N_QR]) * csq_ref[0] + _dot(qn, wq_ref[:, nq + N_QR:]) * csq_ref[1]
    q_ref[:, nq:] = (rot * ATT_SCALE).astype(q_ref.dtype)

    kx = _dot(h, wx_ref[...])
    ckv = _rms(kx[:, :KV_RANK], kvg_ref[...])
    ckv_ref[...] = ckv
    kr = kx[:, KV_RANK:KV_RANK + LANES]
    kr_ref[...] = kr
    krr_ref[...] = (kr * csk_ref[0] + kx[:, KV_RANK + LANES:] * csk_ref[1]).astype(krr_ref.dtype)
    kv_ref[...] = _dot(ckv.astype(BF16), wkv_ref[...]).astype(kv_ref.dtype)


def _inproj(x, mod3, wts, consts, layer):
    tm = 512
    per = GRP // tm
    xs, x_specs = _src_specs(x, tm)

    def kind(i):
        return jnp.where(i >= T_CTX // tm, 1, 0)

    def rows(width):
        return pl.BlockSpec((tm, width), lambda i: (i, 0))

    nkv = H * (NOPE + VD)
    return pl.pallas_call(
        functools.partial(_inproj_kernel, tm=tm),
        grid=(T // tm,),
        in_specs=x_specs + [
            _resident((1, D), layer), _mod_row(0, per), _mod_row(1, per),
            _resident((D, N_U + Q_RANK), layer),
            _resident((D, 4 * LANES), layer),
            _resident((1, Q_RANK), layer),
            _resident((Q_RANK, H * (NOPE + 2 * ROPE)), layer),
            pl.BlockSpec((None, 2, tm, N_QR), lambda i: (kind(i), 0, i % per, 0)),
            _resident((1, KV_RANK), layer),
            pl.BlockSpec((None, 2, tm, LANES), lambda i: (kind(i), 0, i % per, 0)),
            _resident((KV_RANK, nkv), layer),
        ],
        out_specs=[rows(N_U), rows(H * (NOPE + ROPE)), rows(KV_RANK), rows(LANES), rows(LANES), rows(nkv)],
        out_shape=[
            jax.ShapeDtypeStruct((T, N_U), BF16),
            jax.ShapeDtypeStruct((T, H * (NOPE + ROPE)), BF16),
            jax.ShapeDtypeStruct((T, KV_RANK), F32),
            jax.ShapeDtypeStruct((T, LANES), F32),
            jax.ShapeDtypeStruct((T, LANES), BF16),
            jax.ShapeDtypeStruct((T, nkv), BF16),
        ],
        scratch_shapes=[pltpu.VMEM((tm, D), BF16)],
        compiler_params=_cp(("parallel",), 58),
        name="in_proj",
    )(*xs, wts["g1"], mod3, mod3, wts["w_main"], wts["w_x"], wts["qg"], wts["wq"], consts["cs_q"],
      wts["kvg"], consts["cs_k"], wts["wkv"])


def _pool_kernel(u_ref, bc_ref, bs_ref, ic_ref, is_ref, pw_ref, ps_ref, o_ref):
    g = pl.program_id(0)

    def seq(r0, n, band_ref, inv_ref):
        for k in range(len(POOL_WINDOWS)):
            cs = slice(k * POOL_GC, (k + 1) * POOL_GC)
            u = u_ref[r0:r0 + n, cs]
            wsum = _dot(band_ref[k], u)
            pooled = wsum * inv_ref[:, cs] - u.astype(F32)
            y = _dot(pooled.astype(BF16), pw_ref[k]) * ps_ref[:, cs]
            o_ref[r0:r0 + n, cs] = y.astype(o_ref.dtype)

    @pl.when(g < NG_CTX)
    def _():
        for s in range(SEQ_PER_GRP):
            seq(s * L_CTX, L_CTX, bc_ref, ic_ref)

    @pl.when(g >= NG_CTX)
    def _():
        seq(0, L_S, bs_ref, is_ref)


def _pool(proj, consts, pw, ps):
    nw = len(POOL_WINDOWS)
    return pl.pallas_call(
        _pool_kernel,
        grid=(NG,),
        in_specs=[
            pl.BlockSpec((GRP, POOL_W), lambda g: (g, 0)),
            pl.BlockSpec((nw, L_CTX, L_CTX), lambda g: (0, 0, 0)),
            pl.BlockSpec((nw, L_S, L_S), lambda g: (0, 0, 0)),
            pl.BlockSpec((L_CTX, POOL_W), lambda g: (0, 0)),
            pl.BlockSpec((L_S, POOL_W), lambda g: (0, 0)),
            pl.BlockSpec((nw, POOL_GC, POOL_GC), lambda g: (0, 0, 0)),
            pl.BlockSpec((1, POOL_W), lambda g: (0, 0)),
        ],
        out_specs=pl.BlockSpec((GRP, POOL_W), lambda g: (g, 0)),
        out_shape=jax.ShapeDtypeStruct((T, POOL_W), BF16),
        compiler_params=_cp(("parallel",)),
        name="pool_mix",
    )(proj, consts["band_c"], consts["band_s"], consts["inv_c"], consts["inv_s"], pw, ps)


HY_HALF = HY_W // 2


def _hyena_kernel(v_ref, x1_ref, x2_ref, swv_ref, sw1_ref, sw2_ref, sbv_ref, sb1_ref, sb2_ref,
                  skip_ref, wc_ref, wtc_ref, ws_ref, wts_ref, ktc_ref, kts_ref, o_ref):
    g = pl.program_id(0)

    def sconv(u_ref, sw_ref, sb_ref, r0, n, cs):
        u = u_ref[r0:r0 + n, cs].astype(F32)
        row = lax.broadcasted_iota(jnp.int32, u.shape, 0)
        prev = jnp.where(row == 0, 0.0, pltpu.roll(u, 1, 0))
        nxt = jnp.where(row == n - 1, 0.0, pltpu.roll(u, n - 1, 0))
        return prev * sw_ref[0:1, cs] + u * sw_ref[1:2, cs] + nxt * sw_ref[2:3, cs] + sb_ref[:, cs]

    def lconv(u, o, n, cs, w_ref, wt_ref, kt_ref):
        spec = _dot(w_ref[...], u.astype(BF16))
        pr, pi = spec[:n], spec[n:]
        ka, kb, ka2 = kt_ref[o, 0, :, cs], kt_ref[o, 1, :, cs], kt_ref[o, 2, :, cs]
        yr = pr * ka - pi * kb
        yi = pr * kb + pi * ka2
        prod = jnp.concatenate([yr, yi], axis=0).astype(BF16)
        return _dot(wt_ref[...], prod) + u * skip_ref[o:o + 1, cs]

    def seq(r0, n, cs, w_ref, wt_ref, kt_ref):
        v = sconv(v_ref, swv_ref, sbv_ref, r0, n, cs)
        x1 = sconv(x1_ref, sw1_ref, sb1_ref, r0, n, cs)
        x2 = sconv(x2_ref, sw2_ref, sb2_ref, r0, n, cs)
        z = x1 * lconv(v, 0, n, cs, w_ref, wt_ref, kt_ref)
        y = x2 * lconv(z, 1, n, cs, w_ref, wt_ref, kt_ref)
        o_ref[r0:r0 + n, cs] = y.astype(o_ref.dtype)

    halves = [slice(0, HY_HALF), slice(HY_HALF, HY_W)]

    @pl.when(g < NG_CTX)
    def _():
        for s in range(SEQ_PER_GRP):
            for cs in halves:
                seq(s * L_CTX, L_CTX, cs, wc_ref, wtc_ref, ktc_ref)

    @pl.when(g >= NG_CTX)
    def _():
        for cs in halves:
            seq(0, L_S, cs, ws_ref, wts_ref, kts_ref)


def _hyena(u, sw, sb, skip, consts, kt_c, kt_s, layer):
    c0 = POOL_W // HY_W

    def part(p):
        return pl.BlockSpec((GRP, HY_W), lambda g: (g, c0 + p))

    def swpart(p):
        return pl.BlockSpec((3, HY_W), lambda g: (0, p))

    def sbpart(p):
        return pl.BlockSpec((1, HY_W), lambda g: (0, p))

    def tables(n):
        return pl.BlockSpec((None, 2, 3, n, HY_W), lambda g: (layer, 0, 0, 0, 0), pipeline_mode=pl.Buffered(1))

    return pl.pallas_call(
        _hyena_kernel,
        grid=(NG,),
        in_specs=[
            part(0), part(1), part(2),
            swpart(0), swpart(1), swpart(2),
            sbpart(0), sbpart(1), sbpart(2),
            pl.BlockSpec((2, HY_W), lambda g: (0, 0)),
            pl.BlockSpec((2 * L_CTX, L_CTX), lambda g: (0, 0)),
            pl.BlockSpec((L_CTX, 2 * L_CTX), lambda g: (0, 0)),
            pl.BlockSpec((2 * L_S, L_S), lambda g: (0, 0)),
            pl.BlockSpec((L_S, 2 * L_S), lambda g: (0, 0)),
            tables(L_CTX), tables(L_S),
        ],
        out_specs=pl.BlockSpec((GRP, HY_W), lambda g: (g, 0)),
        out_shape=jax.ShapeDtypeStruct((T, HY_W), BF16),
        compiler_params=_cp(("parallel",), 56),
        name="hyena_mix",
    )(u, u, u, sw, sw, sw, sb, sb, sb, skip,
      consts["dft_c"], consts["dftt_c"], consts["dft_s"], consts["dftt_s"], kt_c, kt_s)


def _cachekv_kernel(x_ref, w_ref, o_ref):
    o_ref[...] = _dot(x_ref[...].astype(BF16), w_ref[...]).astype(o_ref.dtype)


def _cachekv(cache_ckv, layer, wkv):
    return pl.pallas_call(
        _cachekv_kernel,
        grid=(B_S,),
        in_specs=[
            pl.BlockSpec((None, None, L_CTX, KV_RANK), lambda b: (b, layer, 0, 0)),
            pl.BlockSpec((None, KV_RANK, H * (NOPE + VD)), lambda b: (layer, 0, 0)),
        ],
        out_specs=pl.BlockSpec((L_CTX, H * (NOPE + VD)), lambda b: (b, 0)),
        out_shape=jax.ShapeDtypeStruct((B_S * L_CTX, H * (NOPE + VD)), BF16),
        compiler_params=_cp(("parallel",)),
        name="cache_kv",
    )(cache_ckv, wkv)


ATT_TQ = 512
NK_S = L_S + L_CTX


def _attn_kernel(q_ref, kv_ref, krr_ref, kvc_ref, krc_ref, o_ref, kcat_ref):
    g = pl.program_id(0)
    lane = lax.broadcasted_iota(jnp.int32, (1, LANES), 1)
    hk = NOPE + LANES
    vo = H * NOPE

    krr = krr_ref[...]
    for h in range(H):
        kcat_ref[0:GRP, h * hk:h * hk + NOPE] = kv_ref[:, h * NOPE:(h + 1) * NOPE]
        kcat_ref[0:GRP, h * hk + NOPE:(h + 1) * hk] = krr

    def qcat(rows, h):
        qn = q_ref[rows, h * NOPE:(h + 1) * NOPE]
        pair = q_ref[rows, vo + (h // 2) * LANES:vo + (h // 2 + 1) * LANES].astype(F32)
        keep = (lane < ROPE) if h % 2 == 0 else (lane >= ROPE)
        return jnp.concatenate([qn, jnp.where(keep, pair, 0.0).astype(BF16)], axis=1)

    def probs(sc):
        m = jnp.max(sc, axis=-1, keepdims=True)
        p = jnp.exp(sc - m)
        return p.astype(BF16), 1.0 / jnp.sum(p, axis=-1, keepdims=True)

    @pl.when(g < NG_CTX)
    def _():
        def body(s, carry):
            rows = pl.ds(pl.multiple_of(s * L_CTX, L_CTX), L_CTX)
            for h in range(H):
                p, rl = probs(_dot_nt(qcat(rows, h), kcat_ref[rows, h * hk:(h + 1) * hk]))
                o = _dot(p, kv_ref[rows, vo + h * VD:vo + (h + 1) * VD]) * rl
                o_ref[rows, h * VD:(h + 1) * VD] = o.astype(o_ref.dtype)
            return carry

        lax.fori_loop(0, SEQ_PER_GRP, body, 0)

    @pl.when(g >= NG_CTX)
    def _():
        krc = krc_ref[...]
        for h in range(H):
            kcat_ref[GRP:NK_S, h * hk:h * hk + NOPE] = kvc_ref[:, h * NOPE:(h + 1) * NOPE]
            kcat_ref[GRP:NK_S, h * hk + NOPE:(h + 1) * hk] = krc

        def body(t, carry):
            rows = pl.ds(pl.multiple_of(t * ATT_TQ, ATT_TQ), ATT_TQ)
            for h in range(H):
                p, rl = probs(_dot_nt(qcat(rows, h), kcat_ref[:, h * hk:(h + 1) * hk]))
                o = _dot(p[:, :GRP], kv_ref[:, vo + h * VD:vo + (h + 1) * VD])
                o = o + _dot(p[:, GRP:], kvc_ref[:, vo + h * VD:vo + (h + 1) * VD])
                o_ref[rows, h * VD:(h + 1) * VD] = (o * rl).astype(o_ref.dtype)
            return carry

        lax.fori_loop(0, L_S // ATT_TQ, body, 0)


def _attention(q, kv, krr, kvc, krc):
    def cache_blk(g):
        return jnp.maximum(g - NG_CTX, 0)

    return pl.pallas_call(
        _attn_kernel,
        grid=(NG,),
        in_specs=[
            pl.BlockSpec((GRP, H * (NOPE + ROPE)), lambda g: (g, 0)),
            pl.BlockSpec((GRP, H * (NOPE + VD)), lambda g: (g, 0)),
            pl.BlockSpec((GRP, LANES), lambda g: (g, 0)),
            pl.BlockSpec((L_CTX, H * (NOPE + VD)), lambda g: (cache_blk(g), 0)),
            pl.BlockSpec((L_CTX, LANES), lambda g: (cache_blk(g), 0)),
        ],
        out_specs=pl.BlockSpec((GRP, H * VD), lambda g: (g, 0)),
        out_shape=jax.ShapeDtypeStruct((T, H * VD), BF16),
        scratch_shapes=[pltpu.VMEM((NK_S, H * (NOPE + LANES)), BF16)],
        compiler_params=_cp(("parallel",)),
        name="mla_attention",
    )(q, kv, krr, kvc, krc)


def _wout_kernel(yp_ref, yh_ref, ym_ref, w_ref, gate_ref, g2_ref, shift_ref, scale_ref, rw_ref, *refs, tm):
    *x_refs, o_ref, h_ref, aff_ref = refs
    y = jnp.concatenate([yp_ref[...], yh_ref[...], ym_ref[...]], axis=1)
    for c in range(D // NCH):
        cs = slice(c * NCH, (c + 1) * NCH)
        o_ref[:, cs] = gate_ref[:, cs] * _dot(y, w_ref[:, cs])

    def residual(x_ref):
        o_ref[...] = o_ref[...] + x_ref[...]

    _for_src(x_refs, tm, residual)

    hb = (_rms(o_ref[...], g2_ref[...]) * (1.0 + scale_ref[...]) + shift_ref[...]).astype(BF16)
    h_ref[...] = hb
    logits = _dot(hb, rw_ref[...])
    lane = lax.broadcasted_iota(jnp.int32, logits.shape, 1)
    logits = jnp.where(lane < E, logits, -jnp.inf)
    ex = jnp.exp(logits - jnp.max(logits, axis=-1, keepdims=True))
    aff_ref[...] = ex / jnp.sum(ex, axis=-1, keepdims=True)


def _wout(yp, yh, ym, x, mod3, wts, layer):
    tm = 512
    per = GRP // tm
    xs, x_specs = _src_specs(x, tm)

    def rows(width):
        return pl.BlockSpec((tm, width), lambda i: (i, 0))

    return pl.pallas_call(
        functools.partial(_wout_kernel, tm=tm),
        grid=(T // tm,),
        in_specs=[
            rows(POOL_W), rows(HY_W), rows(H * VD),
            _resident((D, D), layer),
            _mod_row(2, per),
            _resident((1, D), layer), _mod_row(3, per), _mod_row(4, per),
            _resident((D, LANES), layer),
        ] + x_specs,
        out_specs=[rows(D), rows(D), rows(LANES)],
        out_shape=[
            jax.ShapeDtypeStruct((T, D), F32),
            jax.ShapeDtypeStruct((T, D), BF16),
            jax.ShapeDtypeStruct((T, LANES), F32),
        ],
        compiler_params=_cp(("parallel",), 56),
        name="out_proj",
    )(yp, yh, ym, wts["w_out"], mod3, wts["g2"], mod3, mod3, wts["rw"], *xs)


RANK_CH = 256


def _rank_kernel(aff_ref, afft_ref, tri_ref, rank_ref, cnt_ref):
    g = pl.program_id(0)

    def seq(r0, n, cap):
        capf = float(cap)
        for e in range(E):
            row = afft_ref[e:e + 1, r0:r0 + n]
            acc = jnp.zeros((1, n), F32)
            for c in range(n // RANK_CH):
                col = aff_ref[r0 + c * RANK_CH:r0 + (c + 1) * RANK_CH, e:e + 1]
                acc = acc + jnp.sum(jnp.where(col >= row, 1.0, 0.0), axis=0, keepdims=True)
            cnt_ref[e:e + 1, 0:n] = acc
        a = afft_ref[:, r0:r0 + n]
        cut = jnp.max(jnp.where(cnt_ref[:, 0:n] >= capf, a, -1.0), axis=1, keepdims=True)
        above = a > cut
        tied = a == cut
        above_f = jnp.where(above, 1.0, 0.0)
        n_above = jnp.sum(above_f, axis=1, keepdims=True)
        marks = jnp.concatenate([above_f, jnp.where(tied, 1.0, 0.0)], axis=0).astype(BF16)
        before = _dot(marks, tri_ref[0:n, 0:n])
        tie_slot = n_above + before[E:]
        slot = jnp.where(above, before[:E], jnp.where(tied, jnp.where(tie_slot < capf, tie_slot, n), n))
        rank_ref[:, r0:r0 + n] = slot.astype(jnp.int32)

    @pl.when(g < NG_CTX)
    def _():
        for s in range(SEQ_PER_GRP):
            seq(s * L_CTX, L_CTX, CAP_CTX)

    @pl.when(g >= NG_CTX)
    def _():
        seq(0, L_S, CAP_S)


def _rank(aff, afft, tri):
    return pl.pallas_call(
        _rank_kernel,
        grid=(NG,),
        in_specs=[
            pl.BlockSpec((GRP, LANES), lambda g: (g, 0)),
            pl.BlockSpec((E, GRP), lambda g: (0, g)),
            pl.BlockSpec((L_S, L_S), lambda g: (0, 0)),
        ],
        out_specs=pl.BlockSpec((E, GRP), lambda g: (0, g)),
        out_shape=jax.ShapeDtypeStruct((E, T), jnp.int32),
        scratch_shapes=[pltpu.VMEM((E, GRP), F32)],
        compiler_params=_cp(("parallel",)),
        name="moe_rank",
    )(aff, afft, tri)


def _gather_kernel(rank_ref, afft_ref, h_ref, xs_ref, gs_ref, sel_ref):
    g = pl.program_id(0)

    def build(s, r0, n, cap, slot0):
        slot_i = lax.broadcasted_iota(jnp.int32, (cap, n), 0)
        for e in range(E):
            hit = slot_i == rank_ref[e:e + 1, r0:r0 + n]
            sel_ref[(s * E + e) * cap:(s * E + e + 1) * cap, 0:n] = jnp.where(hit, 1.0, 0.0).astype(BF16)
            gs_ref[e, slot0:slot0 + cap, :] = jnp.sum(
                jnp.where(hit, afft_ref[e:e + 1, r0:r0 + n], 0.0), axis=1, keepdims=True)

    def move(s, r0, n, cap, slot0):
        res = _dot(sel_ref[s * E * cap:(s + 1) * E * cap, 0:n], h_ref[r0:r0 + n, :])
        for e in range(E):
            xs_ref[e, slot0:slot0 + cap, :] = res[e * cap:(e + 1) * cap].astype(xs_ref.dtype)

    def both(fn):
        @pl.when(g < NG_CTX)
        def _():
            for s in range(SEQ_PER_GRP):
                fn(s, s * L_CTX, L_CTX, CAP_CTX, s * CAP_CTX)

        @pl.when(g >= NG_CTX)
        def _():
            fn(0, 0, L_S, CAP_S, 0)

    @pl.when(pl.program_id(1) == 0)
    def _():
        both(build)

    both(move)


def _gather(rank, afft, h2):
    return pl.pallas_call(
        _gather_kernel,
        grid=(NG, D // NCH),
        in_specs=[
            pl.BlockSpec((E, GRP), lambda g, j: (0, g)),
            pl.BlockSpec((E, GRP), lambda g, j: (0, g)),
            pl.BlockSpec((GRP, NCH), lambda g, j: (g, j)),
        ],
        out_specs=[
            pl.BlockSpec((E, SLOTS, NCH), lambda g, j: (0, g, j)),
            pl.BlockSpec((E, SLOTS, 1), lambda g, j: (0, g, 0)),
        ],
        out_shape=[
            jax.ShapeDtypeStruct((E, ROWS_E, D), BF16),
            jax.ShapeDtypeStruct((E, ROWS_E, 1), F32),
        ],
        scratch_shapes=[pltpu.VMEM((E * CAP_S, L_S), BF16)],
        compiler_params=_cp(("parallel", "arbitrary")),
        name="moe_gather",
    )(rank, afft, h2)


FFN_TF = 256
FFN_TM = 1024


def _ffn_kernel(x_ref, wg_ref, wu_ref, wd_ref, gs_ref, o_ref, acc_ref):
    f = pl.program_id(2)

    @pl.when(f == 0)
    def _():
        acc_ref[...] = jnp.zeros_like(acc_ref)

    x = x_ref[...]
    a = _dot(x, wg_ref[...].astype(BF16))
    b = _dot(x, wu_ref[...].astype(BF16))
    hid = (a * jax.nn.sigmoid(a) * b).astype(BF16)
    acc_ref[...] += _dot(hid, wd_ref[...].astype(BF16))

    @pl.when(f == pl.num_programs(2) - 1)
    def _():
        o_ref[...] = (acc_ref[...] * gs_ref[...]).astype(o_ref.dtype)


def _ffn(xs, gs, w_gate, w_up, w_down, layer):
    return pl.pallas_call(
        _ffn_kernel,
        grid=(E, ROWS_E // FFN_TM, FF // FFN_TF),
        in_specs=[
            pl.BlockSpec((None, FFN_TM, D), lambda e, m, f: (e, m, 0)),
            pl.BlockSpec((None, None, D, FFN_TF), lambda e, m, f: (layer, e, 0, f)),
            pl.BlockSpec((None, None, D, FFN_TF), lambda e, m, f: (layer, e, 0, f)),
            pl.BlockSpec((None, None, FFN_TF, D), lambda e, m, f: (layer, e, f, 0)),
            pl.BlockSpec((None, FFN_TM, 1), lambda e, m, f: (e, m, 0)),
        ],
        out_specs=pl.BlockSpec((None, FFN_TM, D), lambda e, m, f: (e, m, 0)),
        out_shape=jax.ShapeDtypeStruct((E, ROWS_E, D), BF16),
        scratch_shapes=[pltpu.VMEM((FFN_TM, D), F32)],
        compiler_params=_cp(("parallel", "parallel", "arbitrary"), 56),
        name="moe_ffn",
    )(xs, w_gate, w_up, w_down, gs)


def _combine_kernel(rt_ref, ys_ref, x_ref, gate_ref, exc_ref, exs_ref, o_ref, st_ref):
    g = pl.program_id(0)

    def build(r0, n, cap, slot0, ex_ref):
        r = jnp.minimum(rt_ref[r0:r0 + n, :], cap).astype(F32).astype(BF16)
        want = (lax.broadcasted_iota(jnp.int32, (1, NCH), 1) & (cap - 1)).astype(F32)
        for c in range(E * cap // NCH):
            cs = slice(c * NCH, (c + 1) * NCH)
            st_ref[r0:r0 + n, cs] = jnp.where(_dot(r, ex_ref[:, cs]) == want, 1.0, 0.0).astype(BF16)

    def move(r0, n, cap, slot0, ex_ref):
        ys = ys_ref[:, slot0:slot0 + cap, :].reshape(E * cap, NCH)
        moe = _dot(st_ref[r0:r0 + n, 0:E * cap], ys)
        o_ref[r0:r0 + n, :] = x_ref[r0:r0 + n, :] + gate_ref[...] * moe

    def both(fn):
        @pl.when(g < NG_CTX)
        def _():
            for s in range(SEQ_PER_GRP):
                fn(s * L_CTX, L_CTX, CAP_CTX, s * CAP_CTX, exc_ref)

        @pl.when(g >= NG_CTX)
        def _():
            fn(0, L_S, CAP_S, 0, exs_ref)

    @pl.when(pl.program_id(1) == 0)
    def _():
        both(build)

    both(move)


def _combine(rank_t, ys, x, mod3, consts):
    return pl.pallas_call(
        _combine_kernel,
        grid=(NG, D // NCH),
        in_specs=[
            pl.BlockSpec((GRP, LANES), lambda g, j: (g, 0)),
            pl.BlockSpec((E, SLOTS, NCH), lambda g, j: (0, g, j)),
            pl.BlockSpec((GRP, NCH), lambda g, j: (g, j)),
            pl.BlockSpec((None, 1, NCH), lambda g, j: (6 * g + 5, 0, j)),
            pl.BlockSpec((LANES, E * CAP_CTX), lambda g, j: (0, 0)),
            pl.BlockSpec((LANES, E * CAP_S), lambda g, j: (0, 0)),
        ],
        out_specs=pl.BlockSpec((GRP, NCH), lambda g, j: (g, j)),
        out_shape=jax.ShapeDtypeStruct((T, D), F32),
        scratch_shapes=[pltpu.VMEM((L_S, E * CAP_S), BF16)],
        compiler_params=_cp(("parallel", "arbitrary")),
        name="moe_combine",
    )(rank_t, ys, x, mod3, consts["ex_c"], consts["ex_s"])


def _fnorm_kernel(x_ref, g_ref, o_ref):
    o_ref[...] = _rms(x_ref[...], g_ref[...])


def _final_norm(x, g, blk0, nblk):
    tm = 512
    return pl.pallas_call(
        _fnorm_kernel,
        grid=(nblk,),
        in_specs=[
            pl.BlockSpec((tm, D), lambda i: (blk0 + i, 0)),
            pl.BlockSpec((1, D), lambda i: (0, 0)),
        ],
        out_specs=pl.BlockSpec((tm, D), lambda i: (i, 0)),
        out_shape=jax.ShapeDtypeStruct((nblk * tm, D), F32),
        compiler_params=_cp(("parallel",)),
        name="final_norm",
    )(x, g)


def _np_constants():
    c = {}
    for tag, n in (("c", L_CTX), ("s", L_S)):
        t = np.arange(n)
        band = np.zeros((len(POOL_WINDOWS), n, n), np.float32)
        inv = np.zeros((n, POOL_W), np.float32)
        for k, w in enumerate(POOL_WINDOWS):
            lo = np.clip(t - w // 2, 0, n)
            hi = np.clip(t - w // 2 + w, 0, n)
            band[k] = (t[None, :] >= lo[:, None]) & (t[None, :] < hi[:, None])
            inv[:, k * POOL_GC:(k + 1) * POOL_GC] = (1.0 / (hi - lo).astype(np.float64))[:, None]
        c["band_" + tag] = band
        c["inv_" + tag] = inv
        kk = np.arange(n, dtype=np.float64)[:, None]
        tt = np.arange(n, dtype=np.float64)[None, :]
        ang = np.pi * kk * tt / n
        dft = np.concatenate([np.cos(ang), -np.sin(ang)], axis=0)
        dft[n] = (-1.0) ** np.arange(n)
        c["dft_" + tag] = dft.astype(np.float32)
        c["dftt_" + tag] = np.ascontiguousarray(dft.T).astype(np.float32)
        tl = np.linspace(0.0, 1.0, n, dtype=np.float32)[:, None]
        bands = np.arange(1, POS_BANDS + 1, dtype=np.float32)[None, :]
        feats = np.concatenate([tl, np.sin(2 * np.pi * tl * bands), np.cos(2 * np.pi * tl * bands)], axis=1)
        c["feat_" + tag] = np.pad(feats.astype(np.float32), ((0, 0), (0, LANES - feats.shape[1])))
        c["t_" + tag] = tl
    n_rows = L_S // GRID_W
    row = np.repeat(np.arange(n_rows), GRID_W).astype(np.float32)
    col = np.tile(np.arange(GRID_W), n_rows).astype(np.float32)
    nf = ROPE // 4
    inv_f = (1.0 / ROPE_THETA ** (np.arange(nf, dtype=np.float32) / nf)).astype(np.float32)
    a_row = (row[:, None] * inv_f[None]).astype(np.float32).astype(np.float64)
    a_col = (col[:, None] * inv_f[None]).astype(np.float32).astype(np.float64)
    cos64 = np.concatenate([np.cos(a_row), np.cos(a_row), np.cos(a_col), np.cos(a_col)], axis=1)
    sin64 = np.concatenate([np.sin(a_row), np.sin(a_row), np.sin(a_col), np.sin(a_col)], axis=1)

    def table(reps):
        ident = np.stack([np.ones((L_S, ROPE * reps)), np.zeros((L_S, ROPE * reps))])
        rot = np.stack([np.tile(cos64, (1, reps)), np.tile(sin64, (1, reps))])
        return np.stack([ident, rot]).astype(np.float32)

    c["cs_q"] = table(H)
    c["cs_k"] = table(LANES // ROPE)
    c["tri"] = np.triu(np.ones((L_S, L_S), np.float32), k=1)
    for tag, cap in (("c", CAP_CTX), ("s", CAP_S)):
        ex = np.zeros((LANES, E * cap), np.float32)
        ex[np.arange(E * cap) // cap, np.arange(E * cap)] = 1.0
        c["ex_" + tag] = ex
    return c


def _constants():
    c = {k: jnp.asarray(v) for k, v in _np_constants().items()}
    for k in ("band_c", "band_s", "dft_c", "dftt_c", "dft_s", "dftt_s", "ex_c", "ex_s", "tri"):
        c[k] = c[k].astype(BF16)
    return c


def _rope_swap(w):
    q = ROPE // 4
    return jnp.concatenate([-w[..., q:2 * q], w[..., :q], -w[..., 3 * q:], w[..., 2 * q:3 * q]], axis=-1)


def _dot_hi(a, b):
    return jnp.dot(a, b, preferred_element_type=F32, precision=lax.Precision.HIGHEST)


def _filter_kernel(feat_ref, t_ref, w1_ref, b1_ref, w2_ref, b2_ref, fr_ref, w3f_ref, w3b_ref,
                   ldf_ref, ldb_ref, dft_ref, o_ref, z_ref):
    n = feat_ref.shape[0]

    @pl.when((pl.program_id(1) == 0) & (pl.program_id(2) == 0))
    def _():
        fr = fr_ref[...]
        z = jnp.sin(fr * (_dot_hi(feat_ref[...], w1_ref[...]) + b1_ref[...]))
        z_ref[...] = jnp.sin(fr * (_dot_hi(z, w2_ref[...]) + b2_ref[...]))

    z = z_ref[...]
    t = t_ref[...]
    first = lax.broadcasted_iota(jnp.int32, (n, 1), 0) == 0
    hf = _dot_hi(z, w3f_ref[...]) * jnp.exp(-jnp.exp(ldf_ref[...]) * t)
    hb = _dot_hi(z, w3b_ref[...]) * jnp.exp(-jnp.exp(ldb_ref[...]) * t)
    hb = jnp.where(first, 0.0, hb)
    norm = jnp.sum(jnp.abs(hf), axis=0, keepdims=True) + jnp.sum(jnp.abs(hb), axis=0, keepdims=True) + EPS
    pf = _dot(dft_ref[...], (hf / norm).astype(BF16))
    pb = _dot(dft_ref[...], (hb / norm).astype(BF16))
    sc = jnp.where(first, 0.5 / n, 1.0 / n)
    ka = (pf[:n] + pb[:n]) * sc
    o_ref[0] = ka
    o_ref[1] = jnp.where(first, 0.0, (pf[n:] - pb[n:]) * sc)
    o_ref[2] = jnp.where(first, (pf[n:] + pb[n:]) * sc, ka)


def _hyena_tables(n, feats, tcol, dft, fw):
    tc = 256
    nc = HY_W // tc

    def lay(shape):
        return pl.BlockSpec((None,) + shape, lambda l, o, c: (l, 0, 0))

    def w3(back):
        return pl.BlockSpec((None, LANES, tc), lambda l, o, c: (l, 0, (2 * o + back) * nc + c))

    def ld(back):
        return pl.BlockSpec((None, 1, tc), lambda l, o, c: (l, 0, (2 * o + back) * nc + c))

    return pl.pallas_call(
        _filter_kernel,
        grid=(DEPTH, 2, nc),
        in_specs=[
            pl.BlockSpec((n, LANES), lambda l, o, c: (0, 0)),
            pl.BlockSpec((n, 1), lambda l, o, c: (0, 0)),
            lay((LANES, LANES)), lay((1, LANES)), lay((LANES, LANES)), lay((1, LANES)), lay((1, LANES)),
            w3(0), w3(1), ld(0), ld(1),
            pl.BlockSpec((2 * n, n), lambda l, o, c: (0, 0)),
        ],
        out_specs=pl.BlockSpec((None, None, 3, n, tc), lambda l, o, c: (l, o, 0, 0, c)),
        out_shape=jax.ShapeDtypeStruct((DEPTH, 2, 3, n, HY_W), F32),
        scratch_shapes=[pltpu.VMEM((n, LANES), F32)],
        compiler_params=_cp(("arbitrary", "arbitrary", "arbitrary")),
        name="hyena_filter",
    )(feats, tcol, fw["w1"], fw["b1"], fw["w2"], fw["b2"], fw["freq"], fw["w3"], fw["w3"],
      fw["ld"], fw["ld"], dft)


def _filter_weights(w1, b1, w2, b2, w3, freq, log_decay):
    ph = LANES - FILT_HID
    return dict(
        w1=jnp.pad(w1, ((0, 0), (0, LANES - w1.shape[1]), (0, ph))),
        b1=jnp.pad(b1, ((0, 0), (0, ph))).reshape(DEPTH, 1, LANES),
        w2=jnp.pad(w2, ((0, 0), (0, ph), (0, ph))),
        b2=jnp.pad(b2, ((0, 0), (0, ph))).reshape(DEPTH, 1, LANES),
        freq=jnp.pad(freq, ((0, 0), (0, ph))).reshape(DEPTH, 1, LANES),
        w3=jnp.pad(w3, ((0, 0), (0, ph), (0, 0))),
        ld=log_decay.reshape(DEPTH, 1, 4 * HY_W))


def _prep_weights(w_in, pool_w, pool_scale, hy_short_b, mla_q_norm, mla_kv_norm, mla_w_uq, mla_w_ukv,
                  w_out, router_w, norm1_g, norm2_g):
    n_main = N_U + Q_RANK
    kr_cols = w_in[:, :, IN_COLS - ROPE:]
    kr_swap = _rope_swap(kr_cols)
    w_x = jnp.concatenate([w_in[:, :, n_main:IN_COLS - ROPE], kr_cols, kr_cols, kr_swap, kr_swap], axis=2)
    wq = mla_w_uq.reshape(DEPTH, Q_RANK, H, NOPE + ROPE)
    wq_rope = wq[..., NOPE:]
    wq = jnp.concatenate([wq[..., :NOPE].reshape(DEPTH, Q_RANK, -1), wq_rope.reshape(DEPTH, Q_RANK, -1),
                          _rope_swap(wq_rope).reshape(DEPTH, Q_RANK, -1)], axis=2)
    wkv = mla_w_ukv.reshape(DEPTH, KV_RANK, H, NOPE + VD)
    wkv = jnp.concatenate([wkv[..., :NOPE].reshape(DEPTH, KV_RANK, -1), wkv[..., NOPE:].reshape(DEPTH, KV_RANK, -1)],
                          axis=2)
    return dict(
        w_main=w_in.astype(BF16), w_x=w_x.astype(BF16), wq=wq.astype(BF16), wkv=wkv.astype(BF16),
        w_out=w_out.astype(BF16), rw=jnp.pad(router_w, ((0, 0), (0, 0), (0, LANES - E))).astype(BF16),
        pool_w=pool_w.astype(BF16), pool_scale=pool_scale.reshape(DEPTH, 1, POOL_W),
        sb=hy_short_b.reshape(DEPTH, 1, 3 * HY_W),
        qg=mla_q_norm.reshape(DEPTH, 1, Q_RANK), kvg=mla_kv_norm.reshape(DEPTH, 1, KV_RANK),
        g1=norm1_g.reshape(DEPTH, 1, D), g2=norm2_g.reshape(DEPTH, 1, D))


def _layer(x, layer, mod3, wts, consts, kt_c, kt_s, hy_short_w, hy_skip, cache_ckv, krc,
           exp_w_gate, exp_w_up, exp_w_down):
    u, q, ckv, kr, krr, kv = _inproj(x, mod3, wts, consts, layer)
    y_pool = _pool(u, consts, wts["pool_w"][layer], wts["pool_scale"][layer])
    y_hy = _hyena(u, hy_short_w[layer], wts["sb"][layer], hy_skip[layer], consts, kt_c, kt_s, layer)
    kvc = _cachekv(cache_ckv, layer, wts["wkv"])
    y_mla = _attention(q, kv, krr, kvc, krc)
    x, h2, aff = _wout(y_pool, y_hy, y_mla, x, mod3, wts, layer)

    afft = jnp.swapaxes(aff[:, :E], 0, 1)
    rank = _rank(aff, afft, consts["tri"])
    xs, gs = _gather(rank, afft, h2)
    ys = _ffn(xs, gs, exp_w_gate, exp_w_up, exp_w_down, layer)
    rank_t = jnp.pad(jnp.swapaxes(rank, 0, 1), ((0, 0), (0, LANES - E)))
    return _combine(rank_t, ys, x, mod3, consts), ckv, kr


def kernel(x_prompt, x_sample, cache_ckv, cache_krope, c, c_ctx, ada_w, ada_b, norm1_g, norm2_g, w_in, pool_w, pool_scale, hy_short_w, hy_short_b, hy_ffn_w1, hy_ffn_b1, hy_ffn_w2, hy_ffn_b2, hy_ffn_w3, hy_freq, hy_log_decay, hy_skip, mla_q_norm, mla_kv_norm, mla_w_uq, mla_w_ukv, w_out, router_w, exp_w_gate, exp_w_up, exp_w_down, final_norm_g):
    consts = _constants()
    x = (x_prompt.reshape(T_CTX, D), x_sample.reshape(T - T_CTX, D))
    c16 = jnp.concatenate([jnp.broadcast_to(c_ctx[None], (NG_CTX, D)), c], axis=0)
    mod = _adaln_mod(c16, ada_w, ada_b).reshape(DEPTH, NG * 6, 1, D)

    fw = _filter_weights(hy_ffn_w1, hy_ffn_b1, hy_ffn_w2, hy_ffn_b2, hy_ffn_w3, hy_freq, hy_log_decay)
    kt_c = _hyena_tables(L_CTX, consts["feat_c"], consts["t_c"], consts["dft_c"], fw)
    kt_s = _hyena_tables(L_S, consts["feat_s"], consts["t_s"], consts["dft_s"], fw)

    wts = _prep_weights(w_in, pool_w, pool_scale, hy_short_b, mla_q_norm, mla_kv_norm, mla_w_uq, mla_w_ukv,
                        w_out, router_w, norm1_g, norm2_g)
    krc = jnp.swapaxes(cache_krope, 0, 1).reshape(DEPTH, B_S * L_CTX, ROPE)
    krc = jnp.concatenate([krc, krc], axis=2).astype(BF16)

    ckv_list, kr_list = [], []
    for l in range(DEPTH):
        x, ckv, kr = _layer(x, l, mod[l], wts, consts, kt_c, kt_s, hy_short_w, hy_skip, cache_ckv, krc[l],
                            exp_w_gate, exp_w_up, exp_w_down)
        ckv_list.append(ckv[:T_CTX].reshape(B_CTX, L_CTX, KV_RANK))
        kr_list.append(kr[:T_CTX, :ROPE].reshape(B_CTX, L_CTX, ROPE))

    fg = final_norm_g.reshape(1, D)
    y_prompt = _final_norm(x, fg, 0, T_CTX // 512).reshape(B_CTX, L_CTX, D)
    y_sample = _final_norm(x, fg, T_CTX // 512, (T - T_CTX) // 512).reshape(B_S, L_S, D)
    return (y_prompt, y_sample, jnp.stack(ckv_list, axis=1), jnp.stack(kr_list, axis=1))
```

```python
import functools
import math

import numpy as np
import jax
import jax.numpy as jnp
from jax import lax
from jax.experimental import pallas as pl
from jax.experimental.pallas import tpu as pltpu

F32 = jnp.float32
BF16 = jnp.bfloat16

D = 2048
DEPTH = 2
B_CTX, L_CTX = 32, 256
B_S, L_S = 8, 1024
T_CTX = B_CTX * L_CTX
T = T_CTX + B_S * L_S
GRP = 1024
NG = T // GRP
NG_CTX = T_CTX // GRP
SEQ_PER_GRP = GRP // L_CTX
EPS = 1e-6
GRID_W = 64

POOL_W = 512
POOL_GC = 128
POOL_WINDOWS = (2, 4, 8, 16)
HY_W = 512
POS_BANDS = 8
FILT_HID = 64
H = 8
NOPE = 128
ROPE = 64
VD = 128
Q_RANK = 512
KV_RANK = 256
ROPE_THETA = 10000.0
IN_COLS = 2880
IN_PAD = 3072
E = 16
FF = 1024
CAP_CTX = 2 * L_CTX // E
CAP_S = 2 * L_S // E
SLOTS = GRP * 2 // E
ROWS_E = NG * SLOTS
LANES = 128
NCH = 512
ATT_SCALE = 1.0 / math.sqrt(NOPE + ROPE)
MB = 1024 * 1024


def _cp(sem, vmem_mb=48):
    return pltpu.CompilerParams(dimension_semantics=sem, vmem_limit_bytes=vmem_mb * MB)


def _rms(x, g):
    return x * lax.rsqrt(jnp.mean(x * x, axis=-1, keepdims=True) + EPS) * g


def _dot(a, b):
    return jnp.dot(a, b, preferred_element_type=F32)


def _dot_nt(a, b):
    return lax.dot_general(a, b, (((1,), (1,)), ((), ())), preferred_element_type=F32)


def _mod_kernel(c_ref, w_ref, b_ref, o_ref):
    c = c_ref[...]
    a = (c * jax.nn.sigmoid(c)).astype(BF16)
    o_ref[...] = _dot(a, w_ref[...].astype(BF16)) + b_ref[...]


def _adaln_mod(c16, ada_w, ada_b):
    tn = 1024
    return pl.pallas_call(
        _mod_kernel,
        grid=(DEPTH, 6 * D // tn),
        in_specs=[
            pl.BlockSpec((NG, D), lambda l, j: (0, 0)),
            pl.BlockSpec((None, D, tn), lambda l, j: (l, 0, j)),
            pl.BlockSpec((None, 1, tn), lambda l, j: (l, 0, j)),
        ],
        out_specs=pl.BlockSpec((None, NG, tn), lambda l, j: (l, 0, j)),
        out_shape=jax.ShapeDtypeStruct((DEPTH, NG, 6 * D), F32),
        compiler_params=_cp(("parallel", "parallel")),
        name="adaln_mod",
    )(c16, ada_w, ada_b.reshape(DEPTH, 1, 6 * D))


def _src_specs(x, tm):
    n_ctx = T_CTX // tm
    if isinstance(x, tuple):
        return list(x), [
            pl.BlockSpec((tm, D), lambda i: (jnp.minimum(i, n_ctx - 1), 0)),
            pl.BlockSpec((tm, D), lambda i: (jnp.maximum(i - n_ctx, 0), 0)),
        ]
    return [x], [pl.BlockSpec((tm, D), lambda i: (i, 0))]


def _load_src(x_refs, tm, cs=slice(None)):
    if len(x_refs) == 1:
        return x_refs[0][:, cs]
    return jnp.where(pl.program_id(0) < T_CTX // tm, x_refs[0][:, cs], x_refs[1][:, cs])


def _resident(shape, layer):
    nd = len(shape)
    return pl.BlockSpec((None,) + shape, lambda i: (layer,) + (0,) * nd, pipeline_mode=pl.Buffered(1))


def _mod_row(k, per):
    return pl.BlockSpec((None, 1, D), lambda i: (6 * (i // per) + k, 0, 0))


N_U = POOL_W + 3 * HY_W
N_QR = H * ROPE


def _inproj_kernel(*refs, tm):
    (*x_refs, g_ref, shift_ref, scale_ref, wm_ref, wx_ref, qg_ref, wq_ref, csq_ref, kvg_ref, csk_ref, wkv_ref,
     u_ref, q_ref, ckv_ref, kr_ref, krr_ref, kv_ref) = refs

    y = _rms(_load_src(x_refs, tm), g_ref[...])
    h = (y * (1.0 + scale_ref[...]) + shift_ref[...]).astype(BF16)
    for c in range(N_U // NCH):
        cs = slice(c * NCH, (c + 1) * NCH)
        u_ref[:, cs] = _dot(h, wm_ref[:, cs]).astype(u_ref.dtype)

    qn = _rms(_dot(h, wm_ref[:, N_U:]), qg_ref[...]).astype(BF16)
    nq = H * NOPE
    for c in range(nq // NCH):
        cs = slice(c * NCH, (c + 1) * NCH)
        q_ref[:, cs] = (_dot(qn, wq_ref[:, cs]) * ATT_SCALE).astype(q_ref.dtype)
    rot = _dot(qn, wq_ref[:, nq:nq + N_QR]) * csq_ref[0] + _dot(qn, wq_ref[:, nq + N_QR:]) * csq_ref[1]
    q_ref[:, nq:] = (rot * ATT_SCALE).astype(q_ref.dtype)

    kx = _dot(h, wx_ref[...])
    ckv = _rms(kx[:, :KV_RANK], kvg_ref[...])
    ckv_ref[...] = ckv
    kr = kx[:, KV_RANK:KV_RANK + LANES]
    kr_ref[...] = kr
    krr_ref[...] = (kr * csk_ref[0] + kx[:, KV_RANK + LANES:] * csk_ref[1]).astype(krr_ref.dtype)
    kv_ref[...] = _dot(ckv.astype(BF16), wkv_ref[...]).astype(kv_ref.dtype)


def _inproj(x, mod3, wts, consts, layer):
    tm = 512
    per = GRP // tm
    xs, x_specs = _src_specs(x, tm)

    def kind(i):
        return jnp.where(i >= T_CTX // tm, 1, 0)

    def rows(width):
        return pl.BlockSpec((tm, width), lambda i: (i, 0))

    nkv = H * (NOPE + VD)
    return pl.pallas_call(
        functools.partial(_inproj_kernel, tm=tm),
        grid=(T // tm,),
        in_specs=x_specs + [
            _resident((1, D), layer), _mod_row(0, per), _mod_row(1, per),
            _resident((D, N_U + Q_RANK), layer),
            _resident((D, 4 * LANES), layer),
            _resident((1, Q_RANK), layer),
            _resident((Q_RANK, H * (NOPE + 2 * ROPE)), layer),
            pl.BlockSpec((None, 2, tm, N_QR), lambda i: (kind(i), 0, i % per, 0)),
            _resident((1, KV_RANK), layer),
            pl.BlockSpec((None, 2, tm, LANES), lambda i: (kind(i), 0, i % per, 0)),
            _resident((KV_RANK, nkv), layer),
        ],
        out_specs=[rows(N_U), rows(H * (NOPE + ROPE)), rows(KV_RANK), rows(LANES), rows(LANES), rows(nkv)],
        out_shape=[
            jax.ShapeDtypeStruct((T, N_U), BF16),
            jax.ShapeDtypeStruct((T, H * (NOPE + ROPE)), BF16),
            jax.ShapeDtypeStruct((T, KV_RANK), F32),
            jax.ShapeDtypeStruct((T, LANES), F32),
            jax.ShapeDtypeStruct((T, LANES), BF16),
            jax.ShapeDtypeStruct((T, nkv), BF16),
        ],
        compiler_params=_cp(("parallel",), 58),
        name="in_proj",
    )(*xs, wts["g1"], mod3, mod3, wts["w_main"], wts["w_x"], wts["qg"], wts["wq"], consts["cs_q"],
      wts["kvg"], consts["cs_k"], wts["wkv"])


def _pool_kernel(u_ref, bc_ref, bs_ref, ic_ref, is_ref, pw_ref, ps_ref, o_ref):
    g = pl.program_id(0)

    def seq(r0, n, band_ref, inv_ref):
        for k in range(len(POOL_WINDOWS)):
            cs = slice(k * POOL_GC, (k + 1) * POOL_GC)
            u = u_ref[r0:r0 + n, cs]
            wsum = _dot(band_ref[k], u)
            pooled = wsum * inv_ref[:, cs] - u.astype(F32)
            y = _dot(pooled.astype(BF16), pw_ref[k]) * ps_ref[:, cs]
            o_ref[r0:r0 + n, cs] = y.astype(o_ref.dtype)

    @pl.when(g < NG_CTX)
    def _():
        for s in range(SEQ_PER_GRP):
            seq(s * L_CTX, L_CTX, bc_ref, ic_ref)

    @pl.when(g >= NG_CTX)
    def _():
        seq(0, L_S, bs_ref, is_ref)


def _pool(proj, consts, pw, ps):
    nw = len(POOL_WINDOWS)
    return pl.pallas_call(
        _pool_kernel,
        grid=(NG,),
        in_specs=[
            pl.BlockSpec((GRP, POOL_W), lambda g: (g, 0)),
            pl.BlockSpec((nw, L_CTX, L_CTX), lambda g: (0, 0, 0)),
            pl.BlockSpec((nw, L_S, L_S), lambda g: (0, 0, 0)),
            pl.BlockSpec((L_CTX, POOL_W), lambda g: (0, 0)),
            pl.BlockSpec((L_S, POOL_W), lambda g: (0, 0)),
            pl.BlockSpec((nw, POOL_GC, POOL_GC), lambda g: (0, 0, 0)),
            pl.BlockSpec((1, POOL_W), lambda g: (0, 0)),
        ],
        out_specs=pl.BlockSpec((GRP, POOL_W), lambda g: (g, 0)),
        out_shape=jax.ShapeDtypeStruct((T, POOL_W), BF16),
        compiler_params=_cp(("parallel",)),
        name="pool_mix",
    )(proj, consts["band_c"], consts["band_s"], consts["inv_c"], consts["inv_s"], pw, ps)


HY_CH = HY_W


def _hyena_kernel(v_ref, x1_ref, x2_ref, swv_ref, sw1_ref, sw2_ref, sbv_ref, sb1_ref, sb2_ref,
                  skip_ref, wc_ref, wtc_ref, ws_ref, wts_ref, ktc_ref, kts_ref, o_ref):
    g = pl.program_id(0)

    def sconv(u_ref, sw_ref, sb_ref, r0, n, cs):
        u = u_ref[r0:r0 + n, cs].astype(F32)
        row = lax.broadcasted_iota(jnp.int32, u.shape, 0)
        prev = jnp.where(row == 0, 0.0, pltpu.roll(u, 1, 0))
        nxt = jnp.where(row == n - 1, 0.0, pltpu.roll(u, n - 1, 0))
        return prev * sw_ref[0:1, cs] + u * sw_ref[1:2, cs] + nxt * sw_ref[2:3, cs] + sb_ref[:, cs]

    def lconv(u, o, n, cs, w_ref, wt_ref, kt_ref):
        spec = _dot(w_ref[...], u.astype(BF16))
        pr, pi = spec[:n], spec[n:]
        ka, kb, ka2 = kt_ref[o, 0, :, cs], kt_ref[o, 1, :, cs], kt_ref[o, 2, :, cs]
        yr = pr * ka - pi * kb
        yi = pr * kb + pi * ka2
        prod = jnp.concatenate([yr, yi], axis=0).astype(BF16)
        return _dot(wt_ref[...], prod) + u * skip_ref[o:o + 1, cs]

    def seq(r0, n, cs, w_ref, wt_ref, kt_ref):
        v = sconv(v_ref, swv_ref, sbv_ref, r0, n, cs)
        x1 = sconv(x1_ref, sw1_ref, sb1_ref, r0, n, cs)
        x2 = sconv(x2_ref, sw2_ref, sb2_ref, r0, n, cs)
        z = x1 * lconv(v, 0, n, cs, w_ref, wt_ref, kt_ref)
        y = x2 * lconv(z, 1, n, cs, w_ref, wt_ref, kt_ref)
        o_ref[r0:r0 + n, cs] = y.astype(o_ref.dtype)

    chunks = [slice(c * HY_CH, (c + 1) * HY_CH) for c in range(HY_W // HY_CH)]

    @pl.when(g < NG_CTX)
    def _():
        for s in range(SEQ_PER_GRP):
            for cs in chunks:
                seq(s * L_CTX, L_CTX, cs, wc_ref, wtc_ref, ktc_ref)

    @pl.when(g >= NG_CTX)
    def _():
        for cs in chunks:
            seq(0, L_S, cs, ws_ref, wts_ref, kts_ref)


def _hyena(u, sw, sb, skip, consts, kt_c, kt_s, layer):
    c0 = POOL_W // HY_W

    def part(p):
        return pl.BlockSpec((GRP, HY_W), lambda g: (g, c0 + p))

    def swpart(p):
        return pl.BlockSpec((3, HY_W), lambda g: (0, p))

    def sbpart(p):
        return pl.BlockSpec((1, HY_W), lambda g: (0, p))

    def tables(n):
        return pl.BlockSpec((None, 2, 3, n, HY_W), lambda g: (layer, 0, 0, 0, 0), pipeline_mode=pl.Buffered(1))

    return pl.pallas_call(
        _hyena_kernel,
        grid=(NG,),
        in_specs=[
            part(0), part(1), part(2),
            swpart(0), swpart(1), swpart(2),
            sbpart(0), sbpart(1), sbpart(2),
            pl.BlockSpec((2, HY_W), lambda g: (0, 0)),
            pl.BlockSpec((2 * L_CTX, L_CTX), lambda g: (0, 0)),
            pl.BlockSpec((L_CTX, 2 * L_CTX), lambda g: (0, 0)),
            pl.BlockSpec((2 * L_S, L_S), lambda g: (0, 0)),
            pl.BlockSpec((L_S, 2 * L_S), lambda g: (0, 0)),
            tables(L_CTX), tables(L_S),
        ],
        out_specs=pl.BlockSpec((GRP, HY_W), lambda g: (g, 0)),
        out_shape=jax.ShapeDtypeStruct((T, HY_W), BF16),
        compiler_params=_cp(("parallel",), 56),
        name="hyena_mix",
    )(u, u, u, sw, sw, sw, sb, sb, sb, skip,
      consts["dft_c"], consts["dftt_c"], consts["dft_s"], consts["dftt_s"], kt_c, kt_s)


def _cachekv_kernel(x_ref, w_ref, o_ref):
    o_ref[...] = _dot(x_ref[...].astype(BF16), w_ref[...]).astype(o_ref.dtype)


def _cachekv(cache_ckv, layer, wkv):
    return pl.pallas_call(
        _cachekv_kernel,
        grid=(B_S,),
        in_specs=[
            pl.BlockSpec((None, None, L_CTX, KV_RANK), lambda b: (b, layer, 0, 0)),
            pl.BlockSpec((None, KV_RANK, H * (NOPE + VD)), lambda b: (layer, 0, 0)),
        ],
        out_specs=pl.BlockSpec((L_CTX, H * (NOPE + VD)), lambda b: (b, 0)),
        out_shape=jax.ShapeDtypeStruct((B_S * L_CTX, H * (NOPE + VD)), BF16),
        compiler_params=_cp(("parallel",)),
        name="cache_kv",
    )(cache_ckv, wkv)


ATT_TQ = 512
NK_S = L_S + L_CTX


def _attn_kernel(q_ref, kv_ref, krr_ref, kvc_ref, krc_ref, o_ref, kcat_ref):
    g = pl.program_id(0)
    lane = lax.broadcasted_iota(jnp.int32, (1, LANES), 1)
    hk = NOPE + LANES
    vo = H * NOPE

    krr = krr_ref[...]
    for h in range(H):
        kcat_ref[0:GRP, h * hk:h * hk + NOPE] = kv_ref[:, h * NOPE:(h + 1) * NOPE]
        kcat_ref[0:GRP, h * hk + NOPE:(h + 1) * hk] = krr

    def qcat(rows, h):
        qn = q_ref[rows, h * NOPE:(h + 1) * NOPE]
        pair = q_ref[rows, vo + (h // 2) * LANES:vo + (h // 2 + 1) * LANES].astype(F32)
        keep = (lane < ROPE) if h % 2 == 0 else (lane >= ROPE)
        return jnp.concatenate([qn, jnp.where(keep, pair, 0.0).astype(BF16)], axis=1)

    def probs(sc):
        m = jnp.max(sc, axis=-1, keepdims=True)
        p = jnp.exp(sc - m)
        return p.astype(BF16), 1.0 / jnp.sum(p, axis=-1, keepdims=True)

    @pl.when(g < NG_CTX)
    def _():
        def body(s, carry):
            rows = pl.ds(pl.multiple_of(s * L_CTX, L_CTX), L_CTX)
            for h in range(H):
                p, rl = probs(_dot_nt(qcat(rows, h), kcat_ref[rows, h * hk:(h + 1) * hk]))
                o = _dot(p, kv_ref[rows, vo + h * VD:vo + (h + 1) * VD]) * rl
                o_ref[rows, h * VD:(h + 1) * VD] = o.astype(o_ref.dtype)
            return carry

        lax.fori_loop(0, SEQ_PER_GRP, body, 0)

    @pl.when(g >= NG_CTX)
    def _():
        krc = krc_ref[...]
        for h in range(H):
            kcat_ref[GRP:NK_S, h * hk:h * hk + NOPE] = kvc_ref[:, h * NOPE:(h + 1) * NOPE]
            kcat_ref[GRP:NK_S, h * hk + NOPE:(h + 1) * hk] = krc

        def body(t, carry):
            rows = pl.ds(pl.multiple_of(t * ATT_TQ, ATT_TQ), ATT_TQ)
            for h in range(H):
                p, rl = probs(_dot_nt(qcat(rows, h), kcat_ref[:, h * hk:(h + 1) * hk]))
                o = _dot(p[:, :GRP], kv_ref[:, vo + h * VD:vo + (h + 1) * VD])
                o = o + _dot(p[:, GRP:], kvc_ref[:, vo + h * VD:vo + (h + 1) * VD])
                o_ref[rows, h * VD:(h + 1) * VD] = (o * rl).astype(o_ref.dtype)
            return carry

        lax.fori_loop(0, L_S // ATT_TQ, body, 0)


def _attention(q, kv, krr, kvc, krc):
    def cache_blk(g):
        return jnp.maximum(g - NG_CTX, 0)

    return pl.pallas_call(
        _attn_kernel,
        grid=(NG,),
        in_specs=[
            pl.BlockSpec((GRP, H * (NOPE + ROPE)), lambda g: (g, 0)),
            pl.BlockSpec((GRP, H * (NOPE + VD)), lambda g: (g, 0)),
            pl.BlockSpec((GRP, LANES), lambda g: (g, 0)),
            pl.BlockSpec((L_CTX, H * (NOPE + VD)), lambda g: (cache_blk(g), 0)),
            pl.BlockSpec((L_CTX, LANES), lambda g: (cache_blk(g), 0)),
        ],
        out_specs=pl.BlockSpec((GRP, H * VD), lambda g: (g, 0)),
        out_shape=jax.ShapeDtypeStruct((T, H * VD), BF16),
        scratch_shapes=[pltpu.VMEM((NK_S, H * (NOPE + LANES)), BF16)],
        compiler_params=_cp(("parallel",)),
        name="mla_attention",
    )(q, kv, krr, kvc, krc)


def _wout_kernel(yp_ref, yh_ref, ym_ref, w_ref, gate_ref, g2_ref, shift_ref, scale_ref, rw_ref, *refs, tm):
    *x_refs, o_ref, h_ref, aff_ref = refs
    y = jnp.concatenate([yp_ref[...], yh_ref[...], ym_ref[...]], axis=1)
    for c in range(D // NCH):
        cs = slice(c * NCH, (c + 1) * NCH)
        o_ref[:, cs] = _load_src(x_refs, tm, cs) + gate_ref[:, cs] * _dot(y, w_ref[:, cs])

    hb = (_rms(o_ref[...], g2_ref[...]) * (1.0 + scale_ref[...]) + shift_ref[...]).astype(BF16)
    h_ref[...] = hb
    logits = _dot(hb, rw_ref[...])
    lane = lax.broadcasted_iota(jnp.int32, logits.shape, 1)
    logits = jnp.where(lane < E, logits, -jnp.inf)
    ex = jnp.exp(logits - jnp.max(logits, axis=-1, keepdims=True))
    aff_ref[...] = ex / jnp.sum(ex, axis=-1, keepdims=True)


def _wout(yp, yh, ym, x, mod3, wts, layer):
    tm = 512
    per = GRP // tm
    xs, x_specs = _src_specs(x, tm)

    def rows(width):
        return pl.BlockSpec((tm, width), lambda i: (i, 0))

    return pl.pallas_call(
        functools.partial(_wout_kernel, tm=tm),
        grid=(T // tm,),
        in_specs=[
            rows(POOL_W), rows(HY_W), rows(H * VD),
            _resident((D, D), layer),
            _mod_row(2, per),
            _resident((1, D), layer), _mod_row(3, per), _mod_row(4, per),
            _resident((D, LANES), layer),
        ] + x_specs,
        out_specs=[rows(D), rows(D), rows(LANES)],
        out_shape=[
            jax.ShapeDtypeStruct((T, D), F32),
            jax.ShapeDtypeStruct((T, D), BF16),
            jax.ShapeDtypeStruct((T, LANES), F32),
        ],
        compiler_params=_cp(("parallel",), 56),
        name="out_proj",
    )(yp, yh, ym, wts["w_out"], mod3, wts["g2"], mod3, mod3, wts["rw"], *xs)


RANK_CH = 256
SEARCH_ROUNDS = 14
SEARCH_WAYS = 8
AFF_MAX = 2.0


def _rank_kernel(aff_ref, afft_ref, tri_ref, rank_ref, cnt_ref, cut_ref):
    g = pl.program_id(0)

    def count_ge(a, t):
        return jnp.sum(jnp.where(a >= t, 1.0, 0.0), axis=1, keepdims=True)

    def search(specs, capf):
        acts = [afft_ref[:, r0:r0 + n] for r0, n in specs]
        lo = [jnp.zeros((E, 1), F32) for _ in specs]
        hi = [jnp.full((E, 1), AFF_MAX, F32) for _ in specs]
        for _ in range(SEARCH_ROUNDS):
            for s, a in enumerate(acts):
                step = (hi[s] - lo[s]) * (1.0 / SEARCH_WAYS)
                ts = [lo[s] + step * k for k in range(1, SEARCH_WAYS)]
                ok = [count_ge(a, t) >= capf for t in ts]
                new_lo, new_hi = lo[s], hi[s]
                for t, o in zip(ts, ok):
                    new_lo = jnp.where(o, t, new_lo)
                for t, o in zip(reversed(ts), reversed(ok)):
                    new_hi = jnp.where(o, new_hi, t)
                lo[s], hi[s] = new_lo, new_hi
        open_brackets = jnp.zeros((E, 1), F32)
        for s, a in enumerate(acts):
            top = jnp.max(jnp.where(a >= lo[s], jnp.where(a < hi[s], a, -1.0), -1.0), axis=1, keepdims=True)
            low = jnp.min(jnp.where(a >= lo[s], jnp.where(a < hi[s], a, AFF_MAX), AFF_MAX), axis=1, keepdims=True)
            cut_ref[:, s:s + 1] = top
            open_brackets = open_brackets + jnp.where(top != low, 1.0, 0.0)
        return jnp.sum(open_brackets)

    def exact_cut(s, r0, n, capf):
        for e in range(E):
            row = afft_ref[e:e + 1, r0:r0 + n]
            acc = jnp.zeros((1, n), F32)
            for c in range(n // RANK_CH):
                col = aff_ref[r0 + c * RANK_CH:r0 + (c + 1) * RANK_CH, e:e + 1]
                acc = acc + jnp.sum(jnp.where(col >= row, 1.0, 0.0), axis=0, keepdims=True)
            cnt_ref[e:e + 1, 0:n] = acc
        a = afft_ref[:, r0:r0 + n]
        cut_ref[:, s:s + 1] = jnp.max(jnp.where(cnt_ref[:, 0:n] >= capf, a, -1.0), axis=1, keepdims=True)

    def slots(s, r0, n, capf):
        a = afft_ref[:, r0:r0 + n]
        cut = cut_ref[:, s:s + 1]
        above = a > cut
        tied = a == cut
        above_f = jnp.where(above, 1.0, 0.0)
        n_above = jnp.sum(above_f, axis=1, keepdims=True)
        marks = jnp.concatenate([above_f, jnp.where(tied, 1.0, 0.0)], axis=0).astype(BF16)
        before = _dot(marks, tri_ref[0:n, 0:n])
        tie_slot = n_above + before[E:]
        slot = jnp.where(above, before[:E], jnp.where(tied, jnp.where(tie_slot < capf, tie_slot, n), n))
        rank_ref[:, r0:r0 + n] = slot.astype(jnp.int32)

    def group(specs, cap):
        capf = float(cap)
        unresolved = search(specs, capf)

        @pl.when(unresolved > 0.0)
        def _():
            for s, (r0, n) in enumerate(specs):
                exact_cut(s, r0, n, capf)

        for s, (r0, n) in enumerate(specs):
            slots(s, r0, n, capf)

    @pl.when(g < NG_CTX)
    def _():
        group([(s * L_CTX, L_CTX) for s in range(SEQ_PER_GRP)], CAP_CTX)

    @pl.when(g >= NG_CTX)
    def _():
        group([(0, L_S)], CAP_S)


def _rank(aff, afft, tri):
    return pl.pallas_call(
        _rank_kernel,
        grid=(NG,),
        in_specs=[
            pl.BlockSpec((GRP, LANES), lambda g: (g, 0)),
            pl.BlockSpec((E, GRP), lambda g: (0, g)),
            pl.BlockSpec((L_S, L_S), lambda g: (0, 0)),
        ],
        out_specs=pl.BlockSpec((E, GRP), lambda g: (0, g)),
        out_shape=jax.ShapeDtypeStruct((E, T), jnp.int32),
        scratch_shapes=[pltpu.VMEM((E, GRP), F32), pltpu.VMEM((E, LANES), F32)],
        compiler_params=_cp(("parallel",)),
        name="moe_rank",
    )(aff, afft, tri)


def _gather_kernel(rank_ref, afft_ref, h_ref, xs_ref, gs_ref, sel_ref):
    g = pl.program_id(0)

    def build(s, r0, n, cap, slot0):
        slot_i = lax.broadcasted_iota(jnp.int32, (cap, n), 0)
        for e in range(E):
            hit = slot_i == rank_ref[e:e + 1, r0:r0 + n]
            sel_ref[(s * E + e) * cap:(s * E + e + 1) * cap, 0:n] = jnp.where(hit, 1.0, 0.0).astype(BF16)
            gs_ref[e, slot0:slot0 + cap, :] = jnp.sum(
                jnp.where(hit, afft_ref[e:e + 1, r0:r0 + n], 0.0), axis=1, keepdims=True)

    def move(s, r0, n, cap, slot0):
        res = _dot(sel_ref[s * E * cap:(s + 1) * E * cap, 0:n], h_ref[r0:r0 + n, :])
        for e in range(E):
            xs_ref[e, slot0:slot0 + cap, :] = res[e * cap:(e + 1) * cap].astype(xs_ref.dtype)

    def both(fn):
        @pl.when(g < NG_CTX)
        def _():
            for s in range(SEQ_PER_GRP):
                fn(s, s * L_CTX, L_CTX, CAP_CTX, s * CAP_CTX)

        @pl.when(g >= NG_CTX)
        def _():
            fn(0, 0, L_S, CAP_S, 0)

    @pl.when(pl.program_id(1) == 0)
    def _():
        both(build)

    both(move)


def _gather(rank, afft, h2):
    return pl.pallas_call(
        _gather_kernel,
        grid=(NG, D // NCH),
        in_specs=[
            pl.BlockSpec((E, GRP), lambda g, j: (0, g)),
            pl.BlockSpec((E, GRP), lambda g, j: (0, g)),
            pl.BlockSpec((GRP, NCH), lambda g, j: (g, j)),
        ],
        out_specs=[
            pl.BlockSpec((E, SLOTS, NCH), lambda g, j: (0, g, j)),
            pl.BlockSpec((E, SLOTS, 1), lambda g, j: (0, g, 0)),
        ],
        out_shape=[
            jax.ShapeDtypeStruct((E, ROWS_E, D), BF16),
            jax.ShapeDtypeStruct((E, ROWS_E, 1), F32),
        ],
        scratch_shapes=[pltpu.VMEM((E * CAP_S, L_S), BF16)],
        compiler_params=_cp(("parallel", "arbitrary")),
        name="moe_gather",
    )(rank, afft, h2)


FFN_TF = 256
FFN_TM = 1024


def _ffn_kernel(x_ref, wg_ref, wu_ref, wd_ref, gs_ref, o_ref, acc_ref):
    f = pl.program_id(2)

    @pl.when(f == 0)
    def _():
        acc_ref[...] = jnp.zeros_like(acc_ref)

    x = x_ref[...]
    a = _dot(x, wg_ref[...].astype(BF16))
    b = _dot(x, wu_ref[...].astype(BF16))
    hid = (a * jax.nn.sigmoid(a) * b).astype(BF16)
    acc_ref[...] += _dot(hid, wd_ref[...].astype(BF16))

    @pl.when(f == pl.num_programs(2) - 1)
    def _():
        o_ref[...] = (acc_ref[...] * gs_ref[...]).astype(o_ref.dtype)


def _ffn(xs, gs, w_gate, w_up, w_down, layer):
    return pl.pallas_call(
        _ffn_kernel,
        grid=(E, ROWS_E // FFN_TM, FF // FFN_TF),
        in_specs=[
            pl.BlockSpec((None, FFN_TM, D), lambda e, m, f: (e, m, 0)),
            pl.BlockSpec((None, None, D, FFN_TF), lambda e, m, f: (layer, e, 0, f)),
            pl.BlockSpec((None, None, D, FFN_TF), lambda e, m, f: (layer, e, 0, f)),
            pl.BlockSpec((None, None, FFN_TF, D), lambda e, m, f: (layer, e, f, 0)),
            pl.BlockSpec((None, FFN_TM, 1), lambda e, m, f: (e, m, 0)),
        ],
        out_specs=pl.BlockSpec((None, FFN_TM, D), lambda e, m, f: (e, m, 0)),
        out_shape=jax.ShapeDtypeStruct((E, ROWS_E, D), BF16),
        scratch_shapes=[pltpu.VMEM((FFN_TM, D), F32)],
        compiler_params=_cp(("parallel", "parallel", "arbitrary"), 56),
        name="moe_ffn",
    )(xs, w_gate, w_up, w_down, gs)


CMB_TM = 512


def _combine_kernel(rt_ref, ys_ref, x_ref, gate_ref, exc_ref, exs_ref, *refs, last):
    if last:
        fg_ref, oc_ref, os_ref, st_ref = refs
    else:
        oc_ref, st_ref = refs
        os_ref = oc_ref
    g = pl.program_id(0)
    half = pl.program_id(1)

    def scatter(r0, n, cap, slot0, ex_ref, o_ref):
        ec = E * cap
        r = jnp.minimum(rt_ref[r0:r0 + n, :], cap).astype(F32).astype(BF16)
        want = (lax.broadcasted_iota(jnp.int32, (1, NCH), 1) & (cap - 1)).astype(F32)
        for c in range(ec // NCH):
            cs = slice(c * NCH, (c + 1) * NCH)
            st_ref[r0:r0 + n, cs] = jnp.where(_dot(r, ex_ref[:, cs]) == want, 1.0, 0.0).astype(BF16)
        for c in range(D // NCH):
            cs = slice(c * NCH, (c + 1) * NCH)
            ys = ys_ref[:, pl.ds(slot0, cap), cs].reshape(ec, NCH)
            o_ref[r0:r0 + n, cs] = x_ref[r0:r0 + n, cs] + gate_ref[:, cs] * _dot(st_ref[r0:r0 + n, 0:ec], ys)

    def finish(o_ref):
        if last:
            o_ref[...] = _rms(o_ref[...], fg_ref[...])

    @pl.when(g < NG_CTX)
    def _():
        seqs = CMB_TM // L_CTX
        for s in range(seqs):
            slot0 = pl.multiple_of((half * seqs + s) * CAP_CTX, CAP_CTX)
            scatter(s * L_CTX, L_CTX, CAP_CTX, slot0, exc_ref, oc_ref)
        finish(oc_ref)

    @pl.when(g >= NG_CTX)
    def _():
        scatter(0, CMB_TM, CAP_S, 0, exs_ref, os_ref)
        finish(os_ref)


def _combine(rank_t, ys, x, mod3, consts, final_g=None):
    last = final_g is not None
    per = GRP // CMB_TM

    def row(g, i):
        return (g * per + i, 0)

    in_specs = [
        pl.BlockSpec((CMB_TM, LANES), row),
        pl.BlockSpec((E, SLOTS, D), lambda g, i: (0, g, 0)),
        pl.BlockSpec((CMB_TM, D), row),
        pl.BlockSpec((None, 1, D), lambda g, i: (6 * g + 5, 0, 0)),
        pl.BlockSpec((LANES, E * CAP_CTX), lambda g, i: (0, 0)),
        pl.BlockSpec((LANES, E * CAP_S), lambda g, i: (0, 0)),
    ]
    args = [rank_t, ys, x, mod3, consts["ex_c"], consts["ex_s"]]
    if last:
        n_ctx = T_CTX // CMB_TM
        in_specs.append(pl.BlockSpec((1, D), lambda g, i: (0, 0)))
        args.append(final_g)
        out_specs = [
            pl.BlockSpec((CMB_TM, D), lambda g, i: (jnp.where(g < NG_CTX, g * per + i, n_ctx - 1), 0)),
            pl.BlockSpec((CMB_TM, D), lambda g, i: (jnp.where(g < NG_CTX, 0, (g - NG_CTX) * per + i), 0)),
        ]
        out_shape = [jax.ShapeDtypeStruct((T_CTX, D), F32), jax.ShapeDtypeStruct((T - T_CTX, D), F32)]
    else:
        out_specs = pl.BlockSpec((CMB_TM, D), row)
        out_shape = jax.ShapeDtypeStruct((T, D), F32)
    return pl.pallas_call(
        functools.partial(_combine_kernel, last=last),
        grid=(NG, per),
        in_specs=in_specs,
        out_specs=out_specs,
        out_shape=out_shape,
        scratch_shapes=[pltpu.VMEM((CMB_TM, E * CAP_S), BF16)],
        compiler_params=_cp(("arbitrary", "arbitrary"), 56),
        name="moe_combine",
    )(*args)


def _np_constants():
    c = {}
    for tag, n in (("c", L_CTX), ("s", L_S)):
        t = np.arange(n)
        band = np.zeros((len(POOL_WINDOWS), n, n), np.float32)
        inv = np.zeros((n, POOL_W), np.float32)
        for k, w in enumerate(POOL_WINDOWS):
            lo = np.clip(t - w // 2, 0, n)
            hi = np.clip(t - w // 2 + w, 0, n)
            band[k] = (t[None, :] >= lo[:, None]) & (t[None, :] < hi[:, None])
            inv[:, k * POOL_GC:(k + 1) * POOL_GC] = (1.0 / (hi - lo).astype(np.float64))[:, None]
        c["band_" + tag] = band
        c["inv_" + tag] = inv
        kk = np.arange(n, dtype=np.float64)[:, None]
        tt = np.arange(n, dtype=np.float64)[None, :]
        ang = np.pi * kk * tt / n
        dft = np.concatenate([np.cos(ang), -np.sin(ang)], axis=0)
        dft[n] = (-1.0) ** np.arange(n)
        c["dft_" + tag] = dft.astype(np.float32)
        c["dftt_" + tag] = np.ascontiguousarray(dft.T).astype(np.float32)
        tl = np.linspace(0.0, 1.0, n, dtype=np.float32)[:, None]
        bands = np.arange(1, POS_BANDS + 1, dtype=np.float32)[None, :]
        feats = np.concatenate([tl, np.sin(2 * np.pi * tl * bands), np.cos(2 * np.pi * tl * bands)], axis=1)
        c["feat_" + tag] = np.pad(feats.astype(np.float32), ((0, 0), (0, LANES - feats.shape[1])))
        c["t_" + tag] = tl
    n_rows = L_S // GRID_W
    row = np.repeat(np.arange(n_rows), GRID_W).astype(np.float32)
    col = np.tile(np.arange(GRID_W), n_rows).astype(np.float32)
    nf = ROPE // 4
    inv_f = (1.0 / ROPE_THETA ** (np.arange(nf, dtype=np.float32) / nf)).astype(np.float32)
    a_row = (row[:, None] * inv_f[None]).astype(np.float32).astype(np.float64)
    a_col = (col[:, None] * inv_f[None]).astype(np.float32).astype(np.float64)
    cos64 = np.concatenate([np.cos(a_row), np.cos(a_row), np.cos(a_col), np.cos(a_col)], axis=1)
    sin64 = np.concatenate([np.sin(a_row), np.sin(a_row), np.sin(a_col), np.sin(a_col)], axis=1)

    def table(reps):
        ident = np.stack([np.ones((L_S, ROPE * reps)), np.zeros((L_S, ROPE * reps))])
        rot = np.stack([np.tile(cos64, (1, reps)), np.tile(sin64, (1, reps))])
        return np.stack([ident, rot]).astype(np.float32)

    c["cs_q"] = table(H)
    c["cs_k"] = table(LANES // ROPE)
    c["tri"] = np.triu(np.ones((L_S, L_S), np.float32), k=1)
    for tag, cap in (("c", CAP_CTX), ("s", CAP_S)):
        ex = np.zeros((LANES, E * cap), np.float32)
        ex[np.arange(E * cap) // cap, np.arange(E * cap)] = 1.0
        c["ex_" + tag] = ex
    return c


def _constants():
    c = {k: jnp.asarray(v) for k, v in _np_constants().items()}
    for k in ("band_c", "band_s", "dft_c", "dftt_c", "dft_s", "dftt_s", "ex_c", "ex_s", "tri"):
        c[k] = c[k].astype(BF16)
    return c


def _rope_swap(w):
    q = ROPE // 4
    return jnp.concatenate([-w[..., q:2 * q], w[..., :q], -w[..., 3 * q:], w[..., 2 * q:3 * q]], axis=-1)


def _dot_hi(a, b):
    return jnp.dot(a, b, preferred_element_type=F32, precision=lax.Precision.HIGHEST)


def _filter_kernel(feat_ref, t_ref, w1_ref, b1_ref, w2_ref, b2_ref, fr_ref, w3f_ref, w3b_ref,
                   ldf_ref, ldb_ref, dft_ref, o_ref, z_ref):
    n = feat_ref.shape[0]

    @pl.when((pl.program_id(1) == 0) & (pl.program_id(2) == 0))
    def _():
        fr = fr_ref[...]
        z = jnp.sin(fr * (_dot_hi(feat_ref[...], w1_ref[...]) + b1_ref[...]))
        z_ref[...] = jnp.sin(fr * (_dot_hi(z, w2_ref[...]) + b2_ref[...]))

    z = z_ref[...]
    t = t_ref[...]
    first = lax.broadcasted_iota(jnp.int32, (n, 1), 0) == 0
    hf = _dot_hi(z, w3f_ref[...]) * jnp.exp(-jnp.exp(ldf_ref[...]) * t)
    hb = _dot_hi(z, w3b_ref[...]) * jnp.exp(-jnp.exp(ldb_ref[...]) * t)
    hb = jnp.where(first, 0.0, hb)
    norm = jnp.sum(jnp.abs(hf), axis=0, keepdims=True) + jnp.sum(jnp.abs(hb), axis=0, keepdims=True) + EPS
    pf = _dot(dft_ref[...], (hf / norm).astype(BF16))
    pb = _dot(dft_ref[...], (hb / norm).astype(BF16))
    sc = jnp.where(first, 0.5 / n, 1.0 / n)
    ka = (pf[:n] + pb[:n]) * sc
    o_ref[0] = ka
    o_ref[1] = jnp.where(first, 0.0, (pf[n:] - pb[n:]) * sc)
    o_ref[2] = jnp.where(first, (pf[n:] + pb[n:]) * sc, ka)


def _hyena_tables(n, feats, tcol, dft, fw):
    tc = 256
    nc = HY_W // tc

    def lay(shape):
        return pl.BlockSpec((None,) + shape, lambda l, o, c: (l, 0, 0))

    def w3(back):
        return pl.BlockSpec((None, LANES, tc), lambda l, o, c: (l, 0, (2 * o + back) * nc + c))

    def ld(back):
        return pl.BlockSpec((None, 1, tc), lambda l, o, c: (l, 0, (2 * o + back) * nc + c))

    return pl.pallas_call(
        _filter_kernel,
        grid=(DEPTH, 2, nc),
        in_specs=[
            pl.BlockSpec((n, LANES), lambda l, o, c: (0, 0)),
            pl.BlockSpec((n, 1), lambda l, o, c: (0, 0)),
            lay((LANES, LANES)), lay((1, LANES)), lay((LANES, LANES)), lay((1, LANES)), lay((1, LANES)),
            w3(0), w3(1), ld(0), ld(1),
            pl.BlockSpec((2 * n, n), lambda l, o, c: (0, 0)),
        ],
        out_specs=pl.BlockSpec((None, None, 3, n, tc), lambda l, o, c: (l, o, 0, 0, c)),
        out_shape=jax.ShapeDtypeStruct((DEPTH, 2, 3, n, HY_W), F32),
        scratch_shapes=[pltpu.VMEM((n, LANES), F32)],
        compiler_params=_cp(("arbitrary", "arbitrary", "arbitrary")),
        name="hyena_filter",
    )(feats, tcol, fw["w1"], fw["b1"], fw["w2"], fw["b2"], fw["freq"], fw["w3"], fw["w3"],
      fw["ld"], fw["ld"], dft)


def _filter_weights(w1, b1, w2, b2, w3, freq, log_decay):
    ph = LANES - FILT_HID
    return dict(
        w1=jnp.pad(w1, ((0, 0), (0, LANES - w1.shape[1]), (0, ph))),
        b1=jnp.pad(b1, ((0, 0), (0, ph))).reshape(DEPTH, 1, LANES),
        w2=jnp.pad(w2, ((0, 0), (0, ph), (0, ph))),
        b2=jnp.pad(b2, ((0, 0), (0, ph))).reshape(DEPTH, 1, LANES),
        freq=jnp.pad(freq, ((0, 0), (0, ph))).reshape(DEPTH, 1, LANES),
        w3=jnp.pad(w3, ((0, 0), (0, ph), (0, 0))),
        ld=log_decay.reshape(DEPTH, 1, 4 * HY_W))


def _prep_weights(w_in, pool_w, pool_scale, hy_short_b, mla_q_norm, mla_kv_norm, mla_w_uq, mla_w_ukv,
                  w_out, router_w, norm1_g, norm2_g):
    n_main = N_U + Q_RANK
    kr_cols = w_in[:, :, IN_COLS - ROPE:]
    kr_swap = _rope_swap(kr_cols)
    w_x = jnp.concatenate([w_in[:, :, n_main:IN_COLS - ROPE], kr_cols, kr_cols, kr_swap, kr_swap], axis=2)
    wq = mla_w_uq.reshape(DEPTH, Q_RANK, H, NOPE + ROPE)
    wq_rope = wq[..., NOPE:]
    wq = jnp.concatenate([wq[..., :NOPE].reshape(DEPTH, Q_RANK, -1), wq_rope.reshape(DEPTH, Q_RANK, -1),
                          _rope_swap(wq_rope).reshape(DEPTH, Q_RANK, -1)], axis=2)
    wkv = mla_w_ukv.reshape(DEPTH, KV_RANK, H, NOPE + VD)
    wkv = jnp.concatenate([wkv[..., :NOPE].reshape(DEPTH, KV_RANK, -1), wkv[..., NOPE:].reshape(DEPTH, KV_RANK, -1)],
                          axis=2)
    return dict(
        w_main=w_in.astype(BF16), w_x=w_x.astype(BF16), wq=wq.astype(BF16), wkv=wkv.astype(BF16),
        w_out=w_out.astype(BF16), rw=jnp.pad(router_w, ((0, 0), (0, 0), (0, LANES - E))).astype(BF16),
        pool_w=pool_w.astype(BF16), pool_scale=pool_scale.reshape(DEPTH, 1, POOL_W),
        sb=hy_short_b.reshape(DEPTH, 1, 3 * HY_W),
        qg=mla_q_norm.reshape(DEPTH, 1, Q_RANK), kvg=mla_kv_norm.reshape(DEPTH, 1, KV_RANK),
        g1=norm1_g.reshape(DEPTH, 1, D), g2=norm2_g.reshape(DEPTH, 1, D))


def _layer(x, layer, mod3, wts, consts, kt_c, kt_s, hy_short_w, hy_skip, cache_ckv, krc,
           exp_w_gate, exp_w_up, exp_w_down, final_g):
    u, q, ckv, kr, krr, kv = _inproj(x, mod3, wts, consts, layer)
    y_pool = _pool(u, consts, wts["pool_w"][layer], wts["pool_scale"][layer])
    y_hy = _hyena(u, hy_short_w[layer], wts["sb"][layer], hy_skip[layer], consts, kt_c, kt_s, layer)
    kvc = _cachekv(cache_ckv, layer, wts["wkv"])
    y_mla = _attention(q, kv, krr, kvc, krc)
    x, h2, aff = _wout(y_pool, y_hy, y_mla, x, mod3, wts, layer)

    afft = jnp.swapaxes(aff[:, :E], 0, 1)
    rank = _rank(aff, afft, consts["tri"])
    xs, gs = _gather(rank, afft, h2)
    ys = _ffn(xs, gs, exp_w_gate, exp_w_up, exp_w_down, layer)
    rank_t = jnp.pad(jnp.swapaxes(rank, 0, 1), ((0, 0), (0, LANES - E)))
    return _combine(rank_t, ys, x, mod3, consts, final_g), ckv, kr


def kernel(x_prompt, x_sample, cache_ckv, cache_krope, c, c_ctx, ada_w, ada_b, norm1_g, norm2_g, w_in, pool_w, pool_scale, hy_short_w, hy_short_b, hy_ffn_w1, hy_ffn_b1, hy_ffn_w2, hy_ffn_b2, hy_ffn_w3, hy_freq, hy_log_decay, hy_skip, mla_q_norm, mla_kv_norm, mla_w_uq, mla_w_ukv, w_out, router_w, exp_w_gate, exp_w_up, exp_w_down, final_norm_g):
    consts = _constants()
    x = (x_prompt.reshape(T_CTX, D), x_sample.reshape(T - T_CTX, D))
    c16 = jnp.concatenate([jnp.broadcast_to(c_ctx[None], (NG_CTX, D)), c], axis=0)
    mod = _adaln_mod(c16, ada_w, ada_b).reshape(DEPTH, NG * 6, 1, D)

    fw = _filter_weights(hy_ffn_w1, hy_ffn_b1, hy_ffn_w2, hy_ffn_b2, hy_ffn_w3, hy_freq, hy_log_decay)
    kt_c = _hyena_tables(L_CTX, consts["feat_c"], consts["t_c"], consts["dft_c"], fw)
    kt_s = _hyena_tables(L_S, consts["feat_s"], consts["t_s"], consts["dft_s"], fw)

    wts = _prep_weights(w_in, pool_w, pool_scale, hy_short_b, mla_q_norm, mla_kv_norm, mla_w_uq, mla_w_ukv,
                        w_out, router_w, norm1_g, norm2_g)
    krc = jnp.swapaxes(cache_krope, 0, 1).reshape(DEPTH, B_S * L_CTX, ROPE)
    krc = jnp.concatenate([krc, krc], axis=2).astype(BF16)

    ckv_list, kr_list = [], []
    for l in range(DEPTH):
        final_g = final_norm_g.reshape(1, D) if l == DEPTH - 1 else None
        x, ckv, kr = _layer(x, l, mod[l], wts, consts, kt_c, kt_s, hy_short_w, hy_skip, cache_ckv, krc[l],
                            exp_w_gate, exp_w_up, exp_w_down, final_g)
        ckv_list.append(ckv[:T_CTX].reshape(B_CTX, L_CTX, KV_RANK))
        kr_list.append(kr[:T_CTX, :ROPE].reshape(B_CTX, L_CTX, ROPE))

    y_prompt, y_sample = x
    return (y_prompt.reshape(B_CTX, L_CTX, D), y_sample.reshape(B_S, L_S, D),
            jnp.stack(ckv_list, axis=1), jnp.stack(kr_list, axis=1))
```

```python
import functools
import math

import numpy as np
import jax
import jax.numpy as jnp
from jax import lax
from jax.experimental import pallas as pl
from jax.experimental.pallas import tpu as pltpu

F32 = jnp.float32
BF16 = jnp.bfloat16

D = 2048
DEPTH = 2
B_CTX, L_CTX = 32, 256
B_S, L_S = 8, 1024
T_CTX = B_CTX * L_CTX
T = T_CTX + B_S * L_S
GRP = 1024
NG = T // GRP
NG_CTX = T_CTX // GRP
SEQ_PER_GRP = GRP // L_CTX
EPS = 1e-6
GRID_W = 64

POOL_W = 512
POOL_GC = 128
POOL_WINDOWS = (2, 4, 8, 16)
HY_W = 512
POS_BANDS = 8
FILT_HID = 64
H = 8
NOPE = 128
ROPE = 64
VD = 128
Q_RANK = 512
KV_RANK = 256
ROPE_THETA = 10000.0
IN_COLS = 2880
IN_PAD = 3072
E = 16
FF = 1024
CAP_CTX = 2 * L_CTX // E
CAP_S = 2 * L_S // E
SLOTS = GRP * 2 // E
ROWS_E = NG * SLOTS
LANES = 128
NCH = 512
ATT_SCALE = 1.0 / math.sqrt(NOPE + ROPE)
MB = 1024 * 1024


def _cp(sem, vmem_mb=48):
    return pltpu.CompilerParams(dimension_semantics=sem, vmem_limit_bytes=vmem_mb * MB)


def _rms(x, g):
    return x * lax.rsqrt(jnp.mean(x * x, axis=-1, keepdims=True) + EPS) * g


def _dot(a, b):
    return jnp.dot(a, b, preferred_element_type=F32)


def _dot_nt(a, b):
    return lax.dot_general(a, b, (((1,), (1,)), ((), ())), preferred_element_type=F32)


def _mod_kernel(c_ref, w_ref, b_ref, o_ref):
    c = c_ref[...]
    a = (c * jax.nn.sigmoid(c)).astype(BF16)
    o_ref[...] = _dot(a, w_ref[...].astype(BF16)) + b_ref[...]


def _adaln_mod(c16, ada_w, ada_b):
    tn = 1024
    return pl.pallas_call(
        _mod_kernel,
        grid=(DEPTH, 6 * D // tn),
        in_specs=[
            pl.BlockSpec((NG, D), lambda l, j: (0, 0)),
            pl.BlockSpec((None, D, tn), lambda l, j: (l, 0, j)),
            pl.BlockSpec((None, 1, tn), lambda l, j: (l, 0, j)),
        ],
        out_specs=pl.BlockSpec((None, NG, tn), lambda l, j: (l, 0, j)),
        out_shape=jax.ShapeDtypeStruct((DEPTH, NG, 6 * D), F32),
        compiler_params=_cp(("parallel", "parallel")),
        name="adaln_mod",
    )(c16, ada_w, ada_b.reshape(DEPTH, 1, 6 * D))


def _src_specs(x, tm):
    n_ctx = T_CTX // tm
    n_all = T // tm
    if isinstance(x, tuple):
        return list(x), [
            pl.BlockSpec((tm, D), lambda i: (jnp.minimum(i, n_ctx - 1), 0)),
            pl.BlockSpec((tm, D), lambda i: (jnp.clip(i - n_ctx, 0, n_all - n_ctx - 1), 0)),
        ]
    return [x], [pl.BlockSpec((tm, D), lambda i: (jnp.minimum(i, n_all - 1), 0))]


def _load_src(x_refs, tm, cs=slice(None)):
    if len(x_refs) == 1:
        return x_refs[0][:, cs]
    return jnp.where(pl.program_id(0) < T_CTX // tm, x_refs[0][:, cs], x_refs[1][:, cs])


def _resident(shape, layer):
    nd = len(shape)
    return pl.BlockSpec((None,) + shape, lambda i: (layer,) + (0,) * nd, pipeline_mode=pl.Buffered(1))


def _mod_row(k, per):
    return pl.BlockSpec((None, 1, D), lambda i: (6 * (i // per) + k, 0, 0))


N_U = POOL_W + 3 * HY_W
N_QR = H * ROPE


def _inproj_kernel(*refs, tm):
    (*x_refs, g_ref, shift_ref, scale_ref, wm_ref, wx_ref, qg_ref, wq_ref, csq_ref, kvg_ref, csk_ref, wkv_ref,
     u_ref, q_ref, ckv_ref, kr_ref, krr_ref, kv_ref) = refs

    y = _rms(_load_src(x_refs, tm), g_ref[...])
    h = (y * (1.0 + scale_ref[...]) + shift_ref[...]).astype(BF16)
    for c in range(N_U // NCH):
        cs = slice(c * NCH, (c + 1) * NCH)
        u_ref[:, cs] = _dot(h, wm_ref[:, cs]).astype(u_ref.dtype)

    qn = _rms(_dot(h, wm_ref[:, N_U:]), qg_ref[...]).astype(BF16)
    nq = H * NOPE
    for c in range(nq // NCH):
        cs = slice(c * NCH, (c + 1) * NCH)
        q_ref[:, cs] = (_dot(qn, wq_ref[:, cs]) * ATT_SCALE).astype(q_ref.dtype)
    rot = _dot(qn, wq_ref[:, nq:nq + N_QR]) * csq_ref[0] + _dot(qn, wq_ref[:, nq + N_QR:]) * csq_ref[1]
    q_ref[:, nq:] = (rot * ATT_SCALE).astype(q_ref.dtype)

    kx = _dot(h, wx_ref[...])
    ckv = _rms(kx[:, :KV_RANK], kvg_ref[...])
    ckv_ref[...] = ckv
    kr = kx[:, KV_RANK:KV_RANK + LANES]
    kr_ref[...] = kr
    krr_ref[...] = (kr * csk_ref[0] + kx[:, KV_RANK + LANES:] * csk_ref[1]).astype(krr_ref.dtype)
    kv_ref[...] = _dot(ckv.astype(BF16), wkv_ref[...]).astype(kv_ref.dtype)


def _inproj(x, mod3, wts, consts, layer):
    tm = 512
    per = GRP // tm
    xs, x_specs = _src_specs(x, tm)

    def kind(i):
        return jnp.where(i >= T_CTX // tm, 1, 0)

    def rows(width):
        return pl.BlockSpec((tm, width), lambda i: (i, 0))

    nkv = H * (NOPE + VD)
    return pl.pallas_call(
        functools.partial(_inproj_kernel, tm=tm),
        grid=(T // tm,),
        in_specs=x_specs + [
            _resident((1, D), layer), _mod_row(0, per), _mod_row(1, per),
            _resident((D, N_U + Q_RANK), layer),
            _resident((D, 4 * LANES), layer),
            _resident((1, Q_RANK), layer),
            _resident((Q_RANK, H * (NOPE + 2 * ROPE)), layer),
            pl.BlockSpec((None, 2, tm, N_QR), lambda i: (kind(i), 0, i % per, 0)),
            _resident((1, KV_RANK), layer),
            pl.BlockSpec((None, 2, tm, LANES), lambda i: (kind(i), 0, i % per, 0)),
            _resident((KV_RANK, nkv), layer),
        ],
        out_specs=[rows(N_U), rows(H * (NOPE + ROPE)), rows(KV_RANK), rows(LANES), rows(LANES), rows(nkv)],
        out_shape=[
            jax.ShapeDtypeStruct((T, N_U), BF16),
            jax.ShapeDtypeStruct((T, H * (NOPE + ROPE)), BF16),
            jax.ShapeDtypeStruct((T, KV_RANK), F32),
            jax.ShapeDtypeStruct((T, LANES), F32),
            jax.ShapeDtypeStruct((T, LANES), BF16),
            jax.ShapeDtypeStruct((T, nkv), BF16),
        ],
        compiler_params=_cp(("parallel",), 58),
        name="in_proj",
    )(*xs, wts["g1"], mod3, mod3, wts["w_main"], wts["w_x"], wts["qg"], wts["wq"], consts["cs_q"],
      wts["kvg"], consts["cs_k"], wts["wkv"])


def _pool_kernel(u_ref, ic_ref, is_ref, pw_ref, ps_ref, o_ref):
    g = pl.program_id(0)

    def seq(r0, n, inv_ref):
        row = lax.broadcasted_iota(jnp.int32, (n, POOL_GC), 0)

        def later(x, m):
            return jnp.where(row < n - m, pltpu.roll(x, n - m, 0), 0.0)

        def earlier(x, m):
            return jnp.where(row >= m, pltpu.roll(x, m, 0), 0.0)

        for k, w in enumerate(POOL_WINDOWS):
            cs = slice(k * POOL_GC, (k + 1) * POOL_GC)
            u = u_ref[r0:r0 + n, cs].astype(F32)
            ahead, behind, m = u, earlier(u, 1), 1
            while m < w // 2:
                ahead = ahead + later(ahead, m)
                behind = behind + earlier(behind, m)
                m *= 2
            pooled = (ahead + behind) * inv_ref[:, cs] - u
            y = _dot(pooled.astype(BF16), pw_ref[k]) * ps_ref[:, cs]
            o_ref[r0:r0 + n, cs] = y.astype(o_ref.dtype)

    @pl.when(g < NG_CTX)
    def _():
        for s in range(SEQ_PER_GRP):
            seq(s * L_CTX, L_CTX, ic_ref)

    @pl.when(g >= NG_CTX)
    def _():
        seq(0, L_S, is_ref)


def _pool(proj, consts, pw, ps):
    nw = len(POOL_WINDOWS)
    return pl.pallas_call(
        _pool_kernel,
        grid=(NG,),
        in_specs=[
            pl.BlockSpec((GRP, POOL_W), lambda g: (g, 0)),
            pl.BlockSpec((L_CTX, POOL_W), lambda g: (0, 0)),
            pl.BlockSpec((L_S, POOL_W), lambda g: (0, 0)),
            pl.BlockSpec((nw, POOL_GC, POOL_GC), lambda g: (0, 0, 0)),
            pl.BlockSpec((1, POOL_W), lambda g: (0, 0)),
        ],
        out_specs=pl.BlockSpec((GRP, POOL_W), lambda g: (g, 0)),
        out_shape=jax.ShapeDtypeStruct((T, POOL_W), BF16),
        compiler_params=_cp(("parallel",)),
        name="pool_mix",
    )(proj, consts["inv_c"], consts["inv_s"], pw, ps)


HY_CH = HY_W


def _hyena_kernel(v_ref, x1_ref, x2_ref, swv_ref, sw1_ref, sw2_ref, sbv_ref, sb1_ref, sb2_ref,
                  skip_ref, wc_ref, wtc_ref, ws_ref, wts_ref, ktc_ref, kts_ref, o_ref):
    g = pl.program_id(0)

    def sconv(u_ref, sw_ref, sb_ref, r0, n, cs):
        u = u_ref[r0:r0 + n, cs].astype(F32)
        row = lax.broadcasted_iota(jnp.int32, u.shape, 0)
        prev = jnp.where(row == 0, 0.0, pltpu.roll(u, 1, 0))
        nxt = jnp.where(row == n - 1, 0.0, pltpu.roll(u, n - 1, 0))
        return prev * sw_ref[0:1, cs] + u * sw_ref[1:2, cs] + nxt * sw_ref[2:3, cs] + sb_ref[:, cs]

    def lconv(u, o, n, cs, w_ref, wt_ref, kt_ref):
        spec = _dot(w_ref[...], u.astype(BF16))
        pr, pi = spec[:n], spec[n:]
        ka, kb, ka2 = kt_ref[o, 0, :, cs], kt_ref[o, 1, :, cs], kt_ref[o, 2, :, cs]
        yr = pr * ka - pi * kb
        yi = pr * kb + pi * ka2
        prod = jnp.concatenate([yr, yi], axis=0).astype(BF16)
        return _dot(wt_ref[...], prod) + u * skip_ref[o:o + 1, cs]

    def seq(r0, n, cs, w_ref, wt_ref, kt_ref):
        v = sconv(v_ref, swv_ref, sbv_ref, r0, n, cs)
        x1 = sconv(x1_ref, sw1_ref, sb1_ref, r0, n, cs)
        x2 = sconv(x2_ref, sw2_ref, sb2_ref, r0, n, cs)
        z = x1 * lconv(v, 0, n, cs, w_ref, wt_ref, kt_ref)
        y = x2 * lconv(z, 1, n, cs, w_ref, wt_ref, kt_ref)
        o_ref[r0:r0 + n, cs] = y.astype(o_ref.dtype)

    chunks = [slice(c * HY_CH, (c + 1) * HY_CH) for c in range(HY_W // HY_CH)]

    @pl.when(g < NG_CTX)
    def _():
        for s in range(SEQ_PER_GRP):
            for cs in chunks:
                seq(s * L_CTX, L_CTX, cs, wc_ref, wtc_ref, ktc_ref)

    @pl.when(g >= NG_CTX)
    def _():
        for cs in chunks:
            seq(0, L_S, cs, ws_ref, wts_ref, kts_ref)


def _hyena(u, sw, sb, skip, consts, kt_c, kt_s, layer):
    c0 = POOL_W // HY_W

    def part(p):
        return pl.BlockSpec((GRP, HY_W), lambda g: (g, c0 + p))

    def swpart(p):
        return pl.BlockSpec((3, HY_W), lambda g: (0, p))

    def sbpart(p):
        return pl.BlockSpec((1, HY_W), lambda g: (0, p))

    def tables(n):
        return pl.BlockSpec((None, 2, 3, n, HY_W), lambda g: (layer, 0, 0, 0, 0), pipeline_mode=pl.Buffered(1))

    return pl.pallas_call(
        _hyena_kernel,
        grid=(NG,),
        in_specs=[
            part(0), part(1), part(2),
            swpart(0), swpart(1), swpart(2),
            sbpart(0), sbpart(1), sbpart(2),
            pl.BlockSpec((2, HY_W), lambda g: (0, 0)),
            pl.BlockSpec((2 * L_CTX, L_CTX), lambda g: (0, 0)),
            pl.BlockSpec((L_CTX, 2 * L_CTX), lambda g: (0, 0)),
            pl.BlockSpec((2 * L_S, L_S), lambda g: (0, 0)),
            pl.BlockSpec((L_S, 2 * L_S), lambda g: (0, 0)),
            tables(L_CTX), tables(L_S),
        ],
        out_specs=pl.BlockSpec((GRP, HY_W), lambda g: (g, 0)),
        out_shape=jax.ShapeDtypeStruct((T, HY_W), BF16),
        compiler_params=_cp(("parallel",), 56),
        name="hyena_mix",
    )(u, u, u, sw, sw, sw, sb, sb, sb, skip,
      consts["dft_c"], consts["dftt_c"], consts["dft_s"], consts["dftt_s"], kt_c, kt_s)


def _cachekv_kernel(x_ref, w_ref, o_ref):
    o_ref[...] = _dot(x_ref[...].astype(BF16), w_ref[...]).astype(o_ref.dtype)


def _cachekv(cache_ckv, layer, wkv):
    return pl.pallas_call(
        _cachekv_kernel,
        grid=(B_S,),
        in_specs=[
            pl.BlockSpec((None, None, L_CTX, KV_RANK), lambda b: (b, layer, 0, 0)),
            pl.BlockSpec((None, KV_RANK, H * (NOPE + VD)), lambda b: (layer, 0, 0)),
        ],
        out_specs=pl.BlockSpec((L_CTX, H * (NOPE + VD)), lambda b: (b, 0)),
        out_shape=jax.ShapeDtypeStruct((B_S * L_CTX, H * (NOPE + VD)), BF16),
        compiler_params=_cp(("parallel",)),
        name="cache_kv",
    )(cache_ckv, wkv)


ATT_TQ = 512
NK_S = L_S + L_CTX


def _attn_kernel(q_ref, kv_ref, krr_ref, kvc_ref, krc_ref, o_ref, kcat_ref):
    g = pl.program_id(0)
    lane = lax.broadcasted_iota(jnp.int32, (1, LANES), 1)
    hk = NOPE + LANES
    vo = H * NOPE

    krr = krr_ref[...]
    for h in range(H):
        kcat_ref[0:GRP, h * hk:h * hk + NOPE] = kv_ref[:, h * NOPE:(h + 1) * NOPE]
        kcat_ref[0:GRP, h * hk + NOPE:(h + 1) * hk] = krr

    def qcat(rows, h):
        qn = q_ref[rows, h * NOPE:(h + 1) * NOPE]
        pair = q_ref[rows, vo + (h // 2) * LANES:vo + (h // 2 + 1) * LANES].astype(F32)
        keep = (lane < ROPE) if h % 2 == 0 else (lane >= ROPE)
        return jnp.concatenate([qn, jnp.where(keep, pair, 0.0).astype(BF16)], axis=1)

    def probs(sc):
        m = jnp.max(sc, axis=-1, keepdims=True)
        p = jnp.exp(sc - m)
        return p.astype(BF16), 1.0 / jnp.sum(p, axis=-1, keepdims=True)

    @pl.when(g < NG_CTX)
    def _():
        def body(s, carry):
            rows = pl.ds(pl.multiple_of(s * L_CTX, L_CTX), L_CTX)
            for h in range(H):
                p, rl = probs(_dot_nt(qcat(rows, h), kcat_ref[rows, h * hk:(h + 1) * hk]))
                o = _dot(p, kv_ref[rows, vo + h * VD:vo + (h + 1) * VD]) * rl
                o_ref[rows, h * VD:(h + 1) * VD] = o.astype(o_ref.dtype)
            return carry

        lax.fori_loop(0, SEQ_PER_GRP, body, 0)

    @pl.when(g >= NG_CTX)
    def _():
        krc = krc_ref[...]
        for h in range(H):
            kcat_ref[GRP:NK_S, h * hk:h * hk + NOPE] = kvc_ref[:, h * NOPE:(h + 1) * NOPE]
            kcat_ref[GRP:NK_S, h * hk + NOPE:(h + 1) * hk] = krc

        def body(t, carry):
            rows = pl.ds(pl.multiple_of(t * ATT_TQ, ATT_TQ), ATT_TQ)
            for h in range(H):
                p, rl = probs(_dot_nt(qcat(rows, h), kcat_ref[:, h * hk:(h + 1) * hk]))
                o = _dot(p[:, :GRP], kv_ref[:, vo + h * VD:vo + (h + 1) * VD])
                o = o + _dot(p[:, GRP:], kvc_ref[:, vo + h * VD:vo + (h + 1) * VD])
                o_ref[rows, h * VD:(h + 1) * VD] = (o * rl).astype(o_ref.dtype)
            return carry

        lax.fori_loop(0, L_S // ATT_TQ, body, 0)


def _attention(q, kv, krr, kvc, krc):
    def cache_blk(g):
        return jnp.maximum(g - NG_CTX, 0)

    return pl.pallas_call(
        _attn_kernel,
        grid=(NG,),
        in_specs=[
            pl.BlockSpec((GRP, H * (NOPE + ROPE)), lambda g: (g, 0)),
            pl.BlockSpec((GRP, H * (NOPE + VD)), lambda g: (g, 0)),
            pl.BlockSpec((GRP, LANES), lambda g: (g, 0)),
            pl.BlockSpec((L_CTX, H * (NOPE + VD)), lambda g: (cache_blk(g), 0)),
            pl.BlockSpec((L_CTX, LANES), lambda g: (cache_blk(g), 0)),
        ],
        out_specs=pl.BlockSpec((GRP, H * VD), lambda g: (g, 0)),
        out_shape=jax.ShapeDtypeStruct((T, H * VD), BF16),
        scratch_shapes=[pltpu.VMEM((NK_S, H * (NOPE + LANES)), BF16)],
        compiler_params=_cp(("parallel",)),
        name="mla_attention",
    )(q, kv, krr, kvc, krc)


def _wout_kernel(yp_ref, yh_ref, ym_ref, w_ref, gate_ref, g2_ref, shift_ref, scale_ref, rw_ref, *refs, tm):
    *x_refs, o_ref, h_ref, aff_ref, prev_ref = refs

    @pl.when(pl.program_id(0) == 0)
    def _():
        prev_ref[...] = jnp.zeros_like(prev_ref)

    hb = (_rms(prev_ref[...], g2_ref[...]) * (1.0 + scale_ref[...]) + shift_ref[...]).astype(BF16)
    h_ref[...] = hb
    logits = _dot(hb, rw_ref[...])
    lane = lax.broadcasted_iota(jnp.int32, logits.shape, 1)
    logits = jnp.where(lane < E, logits, -jnp.inf)
    ex = jnp.exp(logits - jnp.max(logits, axis=-1, keepdims=True))
    aff_ref[...] = ex / jnp.sum(ex, axis=-1, keepdims=True)

    y = jnp.concatenate([yp_ref[...], yh_ref[...], ym_ref[...]], axis=1)
    for c in range(D // NCH):
        cs = slice(c * NCH, (c + 1) * NCH)
        v = _load_src(x_refs, tm, cs) + gate_ref[:, cs] * _dot(y, w_ref[:, cs])
        o_ref[:, cs] = v
        prev_ref[:, cs] = v


def _wout(yp, yh, ym, x, mod3, wts, layer):
    tm = 512
    per = GRP // tm
    n = T // tm
    xs, x_specs = _src_specs(x, tm)

    def cur(i):
        return jnp.minimum(i, n - 1)

    def prev(i):
        return jnp.maximum(i - 1, 0)

    def rows(width, blk):
        return pl.BlockSpec((tm, width), lambda i: (blk(i), 0))

    def mod_row(k, blk):
        return pl.BlockSpec((None, 1, D), lambda i: (6 * (blk(i) // per) + k, 0, 0))

    return pl.pallas_call(
        functools.partial(_wout_kernel, tm=tm),
        grid=(n + 1,),
        in_specs=[
            rows(POOL_W, cur), rows(HY_W, cur), rows(H * VD, cur),
            _resident((D, D), layer),
            mod_row(2, cur),
            _resident((1, D), layer), mod_row(3, prev), mod_row(4, prev),
            _resident((D, LANES), layer),
        ] + x_specs,
        out_specs=[rows(D, cur), rows(D, prev), rows(LANES, prev)],
        out_shape=[
            jax.ShapeDtypeStruct((T, D), F32),
            jax.ShapeDtypeStruct((T, D), BF16),
            jax.ShapeDtypeStruct((T, LANES), F32),
        ],
        scratch_shapes=[pltpu.VMEM((tm, D), F32)],
        compiler_params=_cp(("arbitrary",), 56),
        name="out_proj",
    )(yp, yh, ym, wts["w_out"], mod3, wts["g2"], mod3, mod3, wts["rw"], *xs)


RANK_CH = 256
SEARCH_ROUNDS = 14
SEARCH_WAYS = 8
AFF_MAX = 2.0


def _rank_kernel(aff_ref, afft_ref, tri_ref, rank_ref, cnt_ref, cut_ref):
    g = pl.program_id(0)

    def count_ge(a, t):
        return jnp.sum(jnp.where(a >= t, 1.0, 0.0), axis=1, keepdims=True)

    def search(specs, capf):
        acts = [afft_ref[:, r0:r0 + n] for r0, n in specs]
        lo = [jnp.zeros((E, 1), F32) for _ in specs]
        hi = [jnp.full((E, 1), AFF_MAX, F32) for _ in specs]
        for _ in range(SEARCH_ROUNDS):
            for s, a in enumerate(acts):
                step = (hi[s] - lo[s]) * (1.0 / SEARCH_WAYS)
                ts = [lo[s] + step * k for k in range(1, SEARCH_WAYS)]
                ok = [count_ge(a, t) >= capf for t in ts]
                new_lo, new_hi = lo[s], hi[s]
                for t, o in zip(ts, ok):
                    new_lo = jnp.where(o, t, new_lo)
                for t, o in zip(reversed(ts), reversed(ok)):
                    new_hi = jnp.where(o, new_hi, t)
                lo[s], hi[s] = new_lo, new_hi
        open_brackets = jnp.zeros((E, 1), F32)
        for s, a in enumerate(acts):
            top = jnp.max(jnp.where(a >= lo[s], jnp.where(a < hi[s], a, -1.0), -1.0), axis=1, keepdims=True)
            low = jnp.min(jnp.where(a >= lo[s], jnp.where(a < hi[s], a, AFF_MAX), AFF_MAX), axis=1, keepdims=True)
            cut_ref[:, s:s + 1] = top
            open_brackets = open_brackets + jnp.where(top != low, 1.0, 0.0)
        return jnp.sum(open_brackets)

    def exact_cut(s, r0, n, capf):
        for e in range(E):
            row = afft_ref[e:e + 1, r0:r0 + n]
            acc = jnp.zeros((1, n), F32)
            for c in range(n // RANK_CH):
                col = aff_ref[r0 + c * RANK_CH:r0 + (c + 1) * RANK_CH, e:e + 1]
                acc = acc + jnp.sum(jnp.where(col >= row, 1.0, 0.0), axis=0, keepdims=True)
            cnt_ref[e:e + 1, 0:n] = acc
        a = afft_ref[:, r0:r0 + n]
        cut_ref[:, s:s + 1] = jnp.max(jnp.where(cnt_ref[:, 0:n] >= capf, a, -1.0), axis=1, keepdims=True)

    def slots(s, r0, n, capf):
        a = afft_ref[:, r0:r0 + n]
        cut = cut_ref[:, s:s + 1]
        above = a > cut
        tied = a == cut
        above_f = jnp.where(above, 1.0, 0.0)
        n_above = jnp.sum(above_f, axis=1, keepdims=True)
        marks = jnp.concatenate([above_f, jnp.where(tied, 1.0, 0.0)], axis=0).astype(BF16)
        before = _dot(marks, tri_ref[0:n, 0:n])
        tie_slot = n_above + before[E:]
        slot = jnp.where(above, before[:E], jnp.where(tied, jnp.where(tie_slot < capf, tie_slot, n), n))
        rank_ref[:, r0:r0 + n] = slot.astype(jnp.int32)

    def group(specs, cap):
        capf = float(cap)
        unresolved = search(specs, capf)

        @pl.when(unresolved > 0.0)
        def _():
            for s, (r0, n) in enumerate(specs):
                exact_cut(s, r0, n, capf)

        for s, (r0, n) in enumerate(specs):
            slots(s, r0, n, capf)

    @pl.when(g < NG_CTX)
    def _():
        group([(s * L_CTX, L_CTX) for s in range(SEQ_PER_GRP)], CAP_CTX)

    @pl.when(g >= NG_CTX)
    def _():
        group([(0, L_S)], CAP_S)


def _rank(aff, afft, tri):
    return pl.pallas_call(
        _rank_kernel,
        grid=(NG,),
        in_specs=[
            pl.BlockSpec((GRP, LANES), lambda g: (g, 0)),
            pl.BlockSpec((E, GRP), lambda g: (0, g)),
            pl.BlockSpec((L_S, L_S), lambda g: (0, 0)),
        ],
        out_specs=pl.BlockSpec((E, GRP), lambda g: (0, g)),
        out_shape=jax.ShapeDtypeStruct((E, T), jnp.int32),
        scratch_shapes=[pltpu.VMEM((E, GRP), F32), pltpu.VMEM((E, LANES), F32)],
        compiler_params=_cp(("parallel",)),
        name="moe_rank",
    )(aff, afft, tri)


def _gather_kernel(rank_ref, afft_ref, h_ref, xs_ref, gs_ref, sel_ref):
    g = pl.program_id(0)

    def seq(r0, n, cap, slot0):
        slot_i = lax.broadcasted_iota(jnp.int32, (cap, n), 0)
        for e in range(E):
            hit = slot_i == rank_ref[e:e + 1, r0:r0 + n]
            sel_ref[e * cap:(e + 1) * cap, 0:n] = jnp.where(hit, 1.0, 0.0).astype(BF16)
            gs_ref[e, slot0:slot0 + cap, :] = jnp.sum(
                jnp.where(hit, afft_ref[e:e + 1, r0:r0 + n], 0.0), axis=1, keepdims=True)
        for c in range(D // NCH):
            cs = slice(c * NCH, (c + 1) * NCH)
            res = _dot(sel_ref[0:E * cap, 0:n], h_ref[r0:r0 + n, cs])
            for e in range(E):
                xs_ref[e, slot0:slot0 + cap, cs] = res[e * cap:(e + 1) * cap].astype(xs_ref.dtype)

    @pl.when(g < NG_CTX)
    def _():
        for s in range(SEQ_PER_GRP):
            seq(s * L_CTX, L_CTX, CAP_CTX, s * CAP_CTX)

    @pl.when(g >= NG_CTX)
    def _():
        seq(0, L_S, CAP_S, 0)


def _gather(rank, afft, h2):
    return pl.pallas_call(
        _gather_kernel,
        grid=(NG,),
        in_specs=[
            pl.BlockSpec((E, GRP), lambda g: (0, g)),
            pl.BlockSpec((E, GRP), lambda g: (0, g)),
            pl.BlockSpec((GRP, D), lambda g: (g, 0)),
        ],
        out_specs=[
            pl.BlockSpec((E, SLOTS, D), lambda g: (0, g, 0)),
            pl.BlockSpec((E, SLOTS, 1), lambda g: (0, g, 0)),
        ],
        out_shape=[
            jax.ShapeDtypeStruct((E, ROWS_E, D), BF16),
            jax.ShapeDtypeStruct((E, ROWS_E, 1), F32),
        ],
        scratch_shapes=[pltpu.VMEM((E * CAP_S, L_S), BF16)],
        compiler_params=_cp(("parallel",)),
        name="moe_gather",
    )(rank, afft, h2)


FFN_TF = 256
FFN_TM = 1024


def _ffn_kernel(x_ref, wg_ref, wu_ref, wd_ref, gs_ref, o_ref, acc_ref):
    f = pl.program_id(2)

    @pl.when(f == 0)
    def _():
        acc_ref[...] = jnp.zeros_like(acc_ref)

    x = x_ref[...]
    a = _dot(x, wg_ref[...].astype(BF16))
    b = _dot(x, wu_ref[...].astype(BF16))
    hid = (a * jax.nn.sigmoid(a) * b).astype(BF16)
    acc_ref[...] += _dot(hid, wd_ref[...].astype(BF16))

    @pl.when(f == pl.num_programs(2) - 1)
    def _():
        o_ref[...] = (acc_ref[...] * gs_ref[...]).astype(o_ref.dtype)


def _ffn(xs, gs, w_gate, w_up, w_down, layer):
    return pl.pallas_call(
        _ffn_kernel,
        grid=(E, ROWS_E // FFN_TM, FF // FFN_TF),
        in_specs=[
            pl.BlockSpec((None, FFN_TM, D), lambda e, m, f: (e, m, 0)),
            pl.BlockSpec((None, None, D, FFN_TF), lambda e, m, f: (layer, e, 0, f)),
            pl.BlockSpec((None, None, D, FFN_TF), lambda e, m, f: (layer, e, 0, f)),
            pl.BlockSpec((None, None, FFN_TF, D), lambda e, m, f: (layer, e, f, 0)),
            pl.BlockSpec((None, FFN_TM, 1), lambda e, m, f: (e, m, 0)),
        ],
        out_specs=pl.BlockSpec((None, FFN_TM, D), lambda e, m, f: (e, m, 0)),
        out_shape=jax.ShapeDtypeStruct((E, ROWS_E, D), BF16),
        scratch_shapes=[pltpu.VMEM((FFN_TM, D), F32)],
        compiler_params=_cp(("parallel", "parallel", "arbitrary"), 56),
        name="moe_ffn",
    )(xs, w_gate, w_up, w_down, gs)


CMB_TM = 512


def _combine_kernel(rt_ref, ys_ref, x_ref, gate_ref, exc_ref, exs_ref, *refs, last):
    if last:
        fg_ref, oc_ref, os_ref, st_ref = refs
    else:
        oc_ref, st_ref = refs
        os_ref = oc_ref
    g = pl.program_id(0)
    half = pl.program_id(1)

    def scatter(r0, n, cap, slot0, ex_ref, o_ref):
        ec = E * cap
        r = jnp.minimum(rt_ref[r0:r0 + n, :], cap).astype(F32).astype(BF16)
        want = (lax.broadcasted_iota(jnp.int32, (1, NCH), 1) & (cap - 1)).astype(F32)
        for c in range(ec // NCH):
            cs = slice(c * NCH, (c + 1) * NCH)
            st_ref[r0:r0 + n, cs] = jnp.where(_dot(r, ex_ref[:, cs]) == want, 1.0, 0.0).astype(BF16)
        for c in range(D // NCH):
            cs = slice(c * NCH, (c + 1) * NCH)
            ys = ys_ref[:, pl.ds(slot0, cap), cs].reshape(ec, NCH)
            o_ref[r0:r0 + n, cs] = x_ref[r0:r0 + n, cs] + gate_ref[:, cs] * _dot(st_ref[r0:r0 + n, 0:ec], ys)

    def finish(o_ref):
        if last:
            o_ref[...] = _rms(o_ref[...], fg_ref[...])

    @pl.when(g < NG_CTX)
    def _():
        seqs = CMB_TM // L_CTX
        for s in range(seqs):
            slot0 = pl.multiple_of((half * seqs + s) * CAP_CTX, CAP_CTX)
            scatter(s * L_CTX, L_CTX, CAP_CTX, slot0, exc_ref, oc_ref)
        finish(oc_ref)

    @pl.when(g >= NG_CTX)
    def _():
        scatter(0, CMB_TM, CAP_S, 0, exs_ref, os_ref)
        finish(os_ref)


def _combine(rank_t, ys, x, mod3, consts, final_g=None):
    last = final_g is not None
    per = GRP // CMB_TM

    def row(g, i):
        return (g * per + i, 0)

    in_specs = [
        pl.BlockSpec((CMB_TM, LANES), row),
        pl.BlockSpec((E, SLOTS, D), lambda g, i: (0, g, 0)),
        pl.BlockSpec((CMB_TM, D), row),
        pl.BlockSpec((None, 1, D), lambda g, i: (6 * g + 5, 0, 0)),
        pl.BlockSpec((LANES, E * CAP_CTX), lambda g, i: (0, 0)),
        pl.BlockSpec((LANES, E * CAP_S), lambda g, i: (0, 0)),
    ]
    args = [rank_t, ys, x, mod3, consts["ex_c"], consts["ex_s"]]
    if last:
        n_ctx = T_CTX // CMB_TM
        in_specs.append(pl.BlockSpec((1, D), lambda g, i: (0, 0)))
        args.append(final_g)
        out_specs = [
            pl.BlockSpec((CMB_TM, D), lambda g, i: (jnp.where(g < NG_CTX, g * per + i, n_ctx - 1), 0)),
            pl.BlockSpec((CMB_TM, D), lambda g, i: (jnp.where(g < NG_CTX, 0, (g - NG_CTX) * per + i), 0)),
        ]
        out_shape = [jax.ShapeDtypeStruct((T_CTX, D), F32), jax.ShapeDtypeStruct((T - T_CTX, D), F32)]
    else:
        out_specs = pl.BlockSpec((CMB_TM, D), row)
        out_shape = jax.ShapeDtypeStruct((T, D), F32)
    return pl.pallas_call(
        functools.partial(_combine_kernel, last=last),
        grid=(NG, per),
        in_specs=in_specs,
        out_specs=out_specs,
        out_shape=out_shape,
        scratch_shapes=[pltpu.VMEM((CMB_TM, E * CAP_S), BF16)],
        compiler_params=_cp(("arbitrary", "arbitrary"), 56),
        name="moe_combine",
    )(*args)


def _np_constants():
    c = {}
    for tag, n in (("c", L_CTX), ("s", L_S)):
        t = np.arange(n)
        inv = np.zeros((n, POOL_W), np.float32)
        for k, w in enumerate(POOL_WINDOWS):
            lo = np.clip(t - w // 2, 0, n)
            hi = np.clip(t - w // 2 + w, 0, n)
            inv[:, k * POOL_GC:(k + 1) * POOL_GC] = (1.0 / (hi - lo).astype(np.float64))[:, None]
        c["inv_" + tag] = inv
        kk = np.arange(n, dtype=np.float64)[:, None]
        tt = np.arange(n, dtype=np.float64)[None, :]
        ang = np.pi * kk * tt / n
        dft = np.concatenate([np.cos(ang), -np.sin(ang)], axis=0)
        dft[n] = (-1.0) ** np.arange(n)
        c["dft_" + tag] = dft.astype(np.float32)
        c["dftt_" + tag] = np.ascontiguousarray(dft.T).astype(np.float32)
        tl = np.linspace(0.0, 1.0, n, dtype=np.float32)[:, None]
        bands = np.arange(1, POS_BANDS + 1, dtype=np.float32)[None, :]
        feats = np.concatenate([tl, np.sin(2 * np.pi * tl * bands), np.cos(2 * np.pi * tl * bands)], axis=1)
        c["feat_" + tag] = np.pad(feats.astype(np.float32), ((0, 0), (0, LANES - feats.shape[1])))
        c["t_" + tag] = tl
    n_rows = L_S // GRID_W
    row = np.repeat(np.arange(n_rows), GRID_W).astype(np.float32)
    col = np.tile(np.arange(GRID_W), n_rows).astype(np.float32)
    nf = ROPE // 4
    inv_f = (1.0 / ROPE_THETA ** (np.arange(nf, dtype=np.float32) / nf)).astype(np.float32)
    a_row = (row[:, None] * inv_f[None]).astype(np.float32).astype(np.float64)
    a_col = (col[:, None] * inv_f[None]).astype(np.float32).astype(np.float64)
    cos64 = np.concatenate([np.cos(a_row), np.cos(a_row), np.cos(a_col), np.cos(a_col)], axis=1)
    sin64 = np.concatenate([np.sin(a_row), np.sin(a_row), np.sin(a_col), np.sin(a_col)], axis=1)

    def table(reps):
        ident = np.stack([np.ones((L_S, ROPE * reps)), np.zeros((L_S, ROPE * reps))])
        rot = np.stack([np.tile(cos64, (1, reps)), np.tile(sin64, (1, reps))])
        return np.stack([ident, rot]).astype(np.float32)

    c["cs_q"] = table(H)
    c["cs_k"] = table(LANES // ROPE)
    c["tri"] = np.triu(np.ones((L_S, L_S), np.float32), k=1)
    for tag, cap in (("c", CAP_CTX), ("s", CAP_S)):
        ex = np.zeros((LANES, E * cap), np.float32)
        ex[np.arange(E * cap) // cap, np.arange(E * cap)] = 1.0
        c["ex_" + tag] = ex
    return c


def _constants():
    c = {k: jnp.asarray(v) for k, v in _np_constants().items()}
    for k in ("dft_c", "dftt_c", "dft_s", "dftt_s", "ex_c", "ex_s", "tri"):
        c[k] = c[k].astype(BF16)
    return c


def _rope_swap(w):
    q = ROPE // 4
    return jnp.concatenate([-w[..., q:2 * q], w[..., :q], -w[..., 3 * q:], w[..., 2 * q:3 * q]], axis=-1)


def _dot_hi(a, b):
    return jnp.dot(a, b, preferred_element_type=F32, precision=lax.Precision.HIGHEST)


def _filter_kernel(feat_ref, t_ref, w1_ref, b1_ref, w2_ref, b2_ref, fr_ref, w3f_ref, w3b_ref,
                   ldf_ref, ldb_ref, dft_ref, o_ref, z_ref):
    n = feat_ref.shape[0]

    @pl.when((pl.program_id(1) == 0) & (pl.program_id(2) == 0))
    def _():
        fr = fr_ref[...]
        z = jnp.sin(fr * (_dot_hi(feat_ref[...], w1_ref[...]) + b1_ref[...]))
        z_ref[...] = jnp.sin(fr * (_dot_hi(z, w2_ref[...]) + b2_ref[...]))

    z = z_ref[...]
    t = t_ref[...]
    first = lax.broadcasted_iota(jnp.int32, (n, 1), 0) == 0
    hf = _dot_hi(z, w3f_ref[...]) * jnp.exp(-jnp.exp(ldf_ref[...]) * t)
    hb = _dot_hi(z, w3b_ref[...]) * jnp.exp(-jnp.exp(ldb_ref[...]) * t)
    hb = jnp.where(first, 0.0, hb)
    norm = jnp.sum(jnp.abs(hf), axis=0, keepdims=True) + jnp.sum(jnp.abs(hb), axis=0, keepdims=True) + EPS
    both = _dot(dft_ref[...], jnp.concatenate([hf / norm, hb / norm], axis=1).astype(BF16))
    pf, pb = both[:, :hf.shape[1]], both[:, hf.shape[1]:]
    sc = jnp.where(first, 0.5 / n, 1.0 / n)
    ka = (pf[:n] + pb[:n]) * sc
    o_ref[0] = ka
    o_ref[1] = jnp.where(first, 0.0, (pf[n:] - pb[n:]) * sc)
    o_ref[2] = jnp.where(first, (pf[n:] + pb[n:]) * sc, ka)


def _hyena_tables(n, feats, tcol, dft, fw):
    tc = 256
    nc = HY_W // tc

    def lay(shape):
        return pl.BlockSpec((None,) + shape, lambda l, o, c: (l, 0, 0))

    def w3(back):
        return pl.BlockSpec((None, LANES, tc), lambda l, o, c: (l, 0, (2 * o + back) * nc + c))

    def ld(back):
        return pl.BlockSpec((None, 1, tc), lambda l, o, c: (l, 0, (2 * o + back) * nc + c))

    return pl.pallas_call(
        _filter_kernel,
        grid=(DEPTH, 2, nc),
        in_specs=[
            pl.BlockSpec((n, LANES), lambda l, o, c: (0, 0)),
            pl.BlockSpec((n, 1), lambda l, o, c: (0, 0)),
            lay((LANES, LANES)), lay((1, LANES)), lay((LANES, LANES)), lay((1, LANES)), lay((1, LANES)),
            w3(0), w3(1), ld(0), ld(1),
            pl.BlockSpec((2 * n, n), lambda l, o, c: (0, 0)),
        ],
        out_specs=pl.BlockSpec((None, None, 3, n, tc), lambda l, o, c: (l, o, 0, 0, c)),
        out_shape=jax.ShapeDtypeStruct((DEPTH, 2, 3, n, HY_W), F32),
        scratch_shapes=[pltpu.VMEM((n, LANES), F32)],
        compiler_params=_cp(("arbitrary", "arbitrary", "arbitrary")),
        name="hyena_filter",
    )(feats, tcol, fw["w1"], fw["b1"], fw["w2"], fw["b2"], fw["freq"], fw["w3"], fw["w3"],
      fw["ld"], fw["ld"], dft)


def _filter_weights(w1, b1, w2, b2, w3, freq, log_decay):
    ph = LANES - FILT_HID
    return dict(
        w1=jnp.pad(w1, ((0, 0), (0, LANES - w1.shape[1]), (0, ph))),
        b1=jnp.pad(b1, ((0, 0), (0, ph))).reshape(DEPTH, 1, LANES),
        w2=jnp.pad(w2, ((0, 0), (0, ph), (0, ph))),
        b2=jnp.pad(b2, ((0, 0), (0, ph))).reshape(DEPTH, 1, LANES),
        freq=jnp.pad(freq, ((0, 0), (0, ph))).reshape(DEPTH, 1, LANES),
        w3=jnp.pad(w3, ((0, 0), (0, ph), (0, 0))),
        ld=log_decay.reshape(DEPTH, 1, 4 * HY_W))


def _prep_weights(w_in, pool_w, pool_scale, hy_short_b, mla_q_norm, mla_kv_norm, mla_w_uq, mla_w_ukv,
                  w_out, router_w, norm1_g, norm2_g):
    n_main = N_U + Q_RANK
    kr_cols = w_in[:, :, IN_COLS - ROPE:]
    kr_swap = _rope_swap(kr_cols)
    w_x = jnp.concatenate([w_in[:, :, n_main:IN_COLS - ROPE], kr_cols, kr_cols, kr_swap, kr_swap], axis=2)
    wq = mla_w_uq.reshape(DEPTH, Q_RANK, H, NOPE + ROPE)
    wq_rope = wq[..., NOPE:]
    wq = jnp.concatenate([wq[..., :NOPE].reshape(DEPTH, Q_RANK, -1), wq_rope.reshape(DEPTH, Q_RANK, -1),
                          _rope_swap(wq_rope).reshape(DEPTH, Q_RANK, -1)], axis=2)
    wkv = mla_w_ukv.reshape(DEPTH, KV_RANK, H, NOPE + VD)
    wkv = jnp.concatenate([wkv[..., :NOPE].reshape(DEPTH, KV_RANK, -1), wkv[..., NOPE:].reshape(DEPTH, KV_RANK, -1)],
                          axis=2)
    return dict(
        w_main=w_in.astype(BF16), w_x=w_x.astype(BF16), wq=wq.astype(BF16), wkv=wkv.astype(BF16),
        w_out=w_out.astype(BF16), rw=jnp.pad(router_w, ((0, 0), (0, 0), (0, LANES - E))).astype(BF16),
        pool_w=pool_w.astype(BF16), pool_scale=pool_scale.reshape(DEPTH, 1, POOL_W),
        sb=hy_short_b.reshape(DEPTH, 1, 3 * HY_W),
        qg=mla_q_norm.reshape(DEPTH, 1, Q_RANK), kvg=mla_kv_norm.reshape(DEPTH, 1, KV_RANK),
        g1=norm1_g.reshape(DEPTH, 1, D), g2=norm2_g.reshape(DEPTH, 1, D))


def _layer(x, layer, mod3, wts, consts, kt_c, kt_s, hy_short_w, hy_skip, cache_ckv, krc,
           exp_w_gate, exp_w_up, exp_w_down, final_g):
    u, q, ckv, kr, krr, kv = _inproj(x, mod3, wts, consts, layer)
    y_pool = _pool(u, consts, wts["pool_w"][layer], wts["pool_scale"][layer])
    y_hy = _hyena(u, hy_short_w[layer], wts["sb"][layer], hy_skip[layer], consts, kt_c, kt_s, layer)
    kvc = _cachekv(cache_ckv, layer, wts["wkv"])
    y_mla = _attention(q, kv, krr, kvc, krc)
    x, h2, aff = _wout(y_pool, y_hy, y_mla, x, mod3, wts, layer)

    afft = jnp.swapaxes(aff[:, :E], 0, 1)
    rank = _rank(aff, afft, consts["tri"])
    xs, gs = _gather(rank, afft, h2)
    ys = _ffn(xs, gs, exp_w_gate, exp_w_up, exp_w_down, layer)
    rank_t = jnp.pad(jnp.swapaxes(rank, 0, 1), ((0, 0), (0, LANES - E)))
    return _combine(rank_t, ys, x, mod3, consts, final_g), ckv, kr


def kernel(x_prompt, x_sample, cache_ckv, cache_krope, c, c_ctx, ada_w, ada_b, norm1_g, norm2_g, w_in, pool_w, pool_scale, hy_short_w, hy_short_b, hy_ffn_w1, hy_ffn_b1, hy_ffn_w2, hy_ffn_b2, hy_ffn_w3, hy_freq, hy_log_decay, hy_skip, mla_q_norm, mla_kv_norm, mla_w_uq, mla_w_ukv, w_out, router_w, exp_w_gate, exp_w_up, exp_w_down, final_norm_g):
    consts = _constants()
    x = (x_prompt.reshape(T_CTX, D), x_sample.reshape(T - T_CTX, D))
    c16 = jnp.concatenate([jnp.broadcast_to(c_ctx[None], (NG_CTX, D)), c], axis=0)
    mod = _adaln_mod(c16, ada_w, ada_b).reshape(DEPTH, NG * 6, 1, D)

    fw = _filter_weights(hy_ffn_w1, hy_ffn_b1, hy_ffn_w2, hy_ffn_b2, hy_ffn_w3, hy_freq, hy_log_decay)
    kt_c = _hyena_tables(L_CTX, consts["feat_c"], consts["t_c"], consts["dft_c"], fw)
    kt_s = _hyena_tables(L_S, consts["feat_s"], consts["t_s"], consts["dft_s"], fw)

    wts = _prep_weights(w_in, pool_w, pool_scale, hy_short_b, mla_q_norm, mla_kv_norm, mla_w_uq, mla_w_ukv,
                        w_out, router_w, norm1_g, norm2_g)
    krc = jnp.swapaxes(cache_krope, 0, 1).reshape(DEPTH, B_S * L_CTX, ROPE)
    krc = jnp.concatenate([krc, krc], axis=2).astype(BF16)

    ckv_list, kr_list = [], []
    for l in range(DEPTH):
        final_g = final_norm_g.reshape(1, D) if l == DEPTH - 1 else None
        x, ckv, kr = _layer(x, l, mod[l], wts, consts, kt_c, kt_s, hy_short_w, hy_skip, cache_ckv, krc[l],
                            exp_w_gate, exp_w_up, exp_w_down, final_g)
        ckv_list.append(ckv[:T_CTX].reshape(B_CTX, L_CTX, KV_RANK))
        kr_list.append(kr[:T_CTX, :ROPE].reshape(B_CTX, L_CTX, ROPE))

    y_prompt, y_sample = x
    return (y_prompt.reshape(B_CTX, L_CTX, D), y_sample.reshape(B_S, L_S, D),
            jnp.stack(ckv_list, axis=1), jnp.stack(kr_list, axis=1))
```

```python
import functools
import math

import numpy as np
import jax
import jax.numpy as jnp
from jax import lax
from jax.experimental import pallas as pl
from jax.experimental.pallas import tpu as pltpu

F32 = jnp.float32
BF16 = jnp.bfloat16

D = 2048
DEPTH = 2
B_CTX, L_CTX = 32, 256
B_S, L_S = 8, 1024
T_CTX = B_CTX * L_CTX
T = T_CTX + B_S * L_S
GRP = 1024
NG = T // GRP
NG_CTX = T_CTX // GRP
SEQ_PER_GRP = GRP // L_CTX
EPS = 1e-6
GRID_W = 64

POOL_W = 512
POOL_GC = 128
POOL_WINDOWS = (2, 4, 8, 16)
HY_W = 512
POS_BANDS = 8
FILT_HID = 64
H = 8
NOPE = 128
ROPE = 64
VD = 128
Q_RANK = 512
KV_RANK = 256
ROPE_THETA = 10000.0
IN_COLS = 2880
IN_PAD = 3072
E = 16
FF = 1024
CAP_CTX = 2 * L_CTX // E
CAP_S = 2 * L_S // E
SLOTS = GRP * 2 // E
ROWS_E = NG * SLOTS
LANES = 128
NCH = 512
ATT_SCALE = 1.0 / math.sqrt(NOPE + ROPE)
MB = 1024 * 1024


def _cp(sem, vmem_mb=48):
    return pltpu.CompilerParams(dimension_semantics=sem, vmem_limit_bytes=vmem_mb * MB)


def _rms(x, g):
    return x * lax.rsqrt(jnp.mean(x * x, axis=-1, keepdims=True) + EPS) * g


def _dot(a, b):
    return jnp.dot(a, b, preferred_element_type=F32)


def _dot_nt(a, b):
    return lax.dot_general(a, b, (((1,), (1,)), ((), ())), preferred_element_type=F32)


def _mod_kernel(c_ref, w_ref, b_ref, o_ref):
    c = c_ref[...]
    a = (c * jax.nn.sigmoid(c)).astype(BF16)
    o_ref[...] = _dot(a, w_ref[...].astype(BF16)) + b_ref[...]


def _adaln_mod(c16, ada_w, ada_b):
    tn = 1024
    return pl.pallas_call(
        _mod_kernel,
        grid=(DEPTH, 6 * D // tn),
        in_specs=[
            pl.BlockSpec((NG, D), lambda l, j: (0, 0)),
            pl.BlockSpec((None, D, tn), lambda l, j: (l, 0, j)),
            pl.BlockSpec((None, 1, tn), lambda l, j: (l, 0, j)),
        ],
        out_specs=pl.BlockSpec((None, NG, tn), lambda l, j: (l, 0, j)),
        out_shape=jax.ShapeDtypeStruct((DEPTH, NG, 6 * D), F32),
        compiler_params=_cp(("parallel", "parallel")),
        name="adaln_mod",
    )(c16, ada_w, ada_b.reshape(DEPTH, 1, 6 * D))


def _src_specs(x, tm):
    n_ctx = T_CTX // tm
    n_all = T // tm
    if isinstance(x, tuple):
        return list(x), [
            pl.BlockSpec((tm, D), lambda i: (jnp.minimum(i, n_ctx - 1), 0)),
            pl.BlockSpec((tm, D), lambda i: (jnp.clip(i - n_ctx, 0, n_all - n_ctx - 1), 0)),
        ]
    return [x], [pl.BlockSpec((tm, D), lambda i: (jnp.minimum(i, n_all - 1), 0))]


def _load_src(x_refs, tm, cs=slice(None)):
    if len(x_refs) == 1:
        return x_refs[0][:, cs]
    return jnp.where(pl.program_id(0) < T_CTX // tm, x_refs[0][:, cs], x_refs[1][:, cs])


def _resident(shape, layer):
    nd = len(shape)
    return pl.BlockSpec((None,) + shape, lambda i: (layer,) + (0,) * nd, pipeline_mode=pl.Buffered(1))


def _mod_row(k, per):
    return pl.BlockSpec((None, 1, D), lambda i: (6 * (i // per) + k, 0, 0))


N_U = POOL_W + 3 * HY_W
N_QR = H * ROPE


def _inproj_kernel(*refs, tm):
    (*x_refs, g_ref, shift_ref, scale_ref, wm_ref, wx_ref, qg_ref, wq_ref, csq_ref, kvg_ref, csk_ref, wkv_ref,
     u_ref, q_ref, ckv_ref, kr_ref, krr_ref, kv_ref) = refs

    y = _rms(_load_src(x_refs, tm), g_ref[...])
    h = (y * (1.0 + scale_ref[...]) + shift_ref[...]).astype(BF16)
    for c in range(N_U // NCH):
        cs = slice(c * NCH, (c + 1) * NCH)
        u_ref[:, cs] = _dot(h, wm_ref[:, cs]).astype(u_ref.dtype)

    qn = _rms(_dot(h, wm_ref[:, N_U:]), qg_ref[...]).astype(BF16)
    nq = H * NOPE
    for c in range(nq // NCH):
        cs = slice(c * NCH, (c + 1) * NCH)
        q_ref[:, cs] = (_dot(qn, wq_ref[:, cs]) * ATT_SCALE).astype(q_ref.dtype)
    rot = _dot(qn, wq_ref[:, nq:nq + N_QR]) * csq_ref[0] + _dot(qn, wq_ref[:, nq + N_QR:]) * csq_ref[1]
    q_ref[:, nq:] = (rot * ATT_SCALE).astype(q_ref.dtype)

    kx = _dot(h, wx_ref[...])
    ckv = _rms(kx[:, :KV_RANK], kvg_ref[...])
    kr = kx[:, KV_RANK:KV_RANK + LANES]

    @pl.when(pl.program_id(0) < T_CTX // tm)
    def _():
        ckv_ref[...] = ckv
        kr_ref[...] = kr

    krr_ref[...] = (kr * csk_ref[0] + kx[:, KV_RANK + LANES:] * csk_ref[1]).astype(krr_ref.dtype)
    kv_ref[...] = _dot(ckv.astype(BF16), wkv_ref[...]).astype(kv_ref.dtype)


def _inproj(x, mod3, wts, consts, layer):
    tm = 512
    per = GRP // tm
    xs, x_specs = _src_specs(x, tm)

    def kind(i):
        return jnp.where(i >= T_CTX // tm, 1, 0)

    def rows(width):
        return pl.BlockSpec((tm, width), lambda i: (i, 0))

    def ctx_rows(width):
        return pl.BlockSpec((tm, width), lambda i: (jnp.minimum(i, T_CTX // tm - 1), 0))

    nkv = H * (NOPE + VD)
    return pl.pallas_call(
        functools.partial(_inproj_kernel, tm=tm),
        grid=(T // tm,),
        in_specs=x_specs + [
            _resident((1, D), layer), _mod_row(0, per), _mod_row(1, per),
            _resident((D, N_U + Q_RANK), layer),
            _resident((D, 4 * LANES), layer),
            _resident((1, Q_RANK), layer),
            _resident((Q_RANK, H * (NOPE + 2 * ROPE)), layer),
            pl.BlockSpec((None, 2, tm, N_QR), lambda i: (kind(i), 0, i % per, 0)),
            _resident((1, KV_RANK), layer),
            pl.BlockSpec((None, 2, tm, LANES), lambda i: (kind(i), 0, i % per, 0)),
            _resident((KV_RANK, nkv), layer),
        ],
        out_specs=[rows(N_U), rows(H * (NOPE + ROPE)), ctx_rows(KV_RANK), ctx_rows(LANES), rows(LANES), rows(nkv)],
        out_shape=[
            jax.ShapeDtypeStruct((T, N_U), BF16),
            jax.ShapeDtypeStruct((T, H * (NOPE + ROPE)), BF16),
            jax.ShapeDtypeStruct((T_CTX, KV_RANK), F32),
            jax.ShapeDtypeStruct((T_CTX, LANES), F32),
            jax.ShapeDtypeStruct((T, LANES), BF16),
            jax.ShapeDtypeStruct((T, nkv), BF16),
        ],
        compiler_params=_cp(("arbitrary",), 58),
        name="in_proj",
    )(*xs, wts["g1"], mod3, mod3, wts["w_main"], wts["w_x"], wts["qg"], wts["wq"], consts["cs_q"],
      wts["kvg"], consts["cs_k"], wts["wkv"])


def _pool_kernel(u_ref, ic_ref, is_ref, pw_ref, ps_ref, o_ref):
    g = pl.program_id(0)

    def seq(r0, n, inv_ref):
        row = lax.broadcasted_iota(jnp.int32, (n, POOL_GC), 0)

        def later(x, m):
            return jnp.where(row < n - m, pltpu.roll(x, n - m, 0), 0.0)

        def earlier(x, m):
            return jnp.where(row >= m, pltpu.roll(x, m, 0), 0.0)

        for k, w in enumerate(POOL_WINDOWS):
            cs = slice(k * POOL_GC, (k + 1) * POOL_GC)
            u = u_ref[r0:r0 + n, cs].astype(F32)
            ahead, behind, m = u, earlier(u, 1), 1
            while m < w // 2:
                ahead = ahead + later(ahead, m)
                behind = behind + earlier(behind, m)
                m *= 2
            pooled = (ahead + behind) * inv_ref[:, cs] - u
            y = _dot(pooled.astype(BF16), pw_ref[k]) * ps_ref[:, cs]
            o_ref[r0:r0 + n, cs] = y.astype(o_ref.dtype)

    @pl.when(g < NG_CTX)
    def _():
        for s in range(SEQ_PER_GRP):
            seq(s * L_CTX, L_CTX, ic_ref)

    @pl.when(g >= NG_CTX)
    def _():
        seq(0, L_S, is_ref)


def _pool(proj, consts, pw, ps):
    nw = len(POOL_WINDOWS)
    return pl.pallas_call(
        _pool_kernel,
        grid=(NG,),
        in_specs=[
            pl.BlockSpec((GRP, POOL_W), lambda g: (g, 0)),
            pl.BlockSpec((L_CTX, POOL_W), lambda g: (0, 0)),
            pl.BlockSpec((L_S, POOL_W), lambda g: (0, 0)),
            pl.BlockSpec((nw, POOL_GC, POOL_GC), lambda g: (0, 0, 0)),
            pl.BlockSpec((1, POOL_W), lambda g: (0, 0)),
        ],
        out_specs=pl.BlockSpec((GRP, POOL_W), lambda g: (g, 0)),
        out_shape=jax.ShapeDtypeStruct((T, POOL_W), BF16),
        compiler_params=_cp(("parallel",)),
        name="pool_mix",
    )(proj, consts["inv_c"], consts["inv_s"], pw, ps)


HY_CH = HY_W


def _hyena_kernel(v_ref, x1_ref, x2_ref, swv_ref, sw1_ref, sw2_ref, sbv_ref, sb1_ref, sb2_ref,
                  skip_ref, wc_ref, wtc_ref, ws_ref, wts_ref, ktc_ref, kts_ref, o_ref):
    g = pl.program_id(0)

    def sconv(u_ref, sw_ref, sb_ref, r0, n, cs):
        u = u_ref[r0:r0 + n, cs].astype(F32)
        row = lax.broadcasted_iota(jnp.int32, u.shape, 0)
        prev = jnp.where(row == 0, 0.0, pltpu.roll(u, 1, 0))
        nxt = jnp.where(row == n - 1, 0.0, pltpu.roll(u, n - 1, 0))
        return prev * sw_ref[0:1, cs] + u * sw_ref[1:2, cs] + nxt * sw_ref[2:3, cs] + sb_ref[:, cs]

    def lconv(u, o, n, cs, w_ref, wt_ref, kt_ref):
        spec = _dot(w_ref[...], u.astype(BF16))
        pr, pi = spec[:n], spec[n:]
        ka, kb, ka2 = kt_ref[o, 0, :, cs], kt_ref[o, 1, :, cs], kt_ref[o, 2, :, cs]
        yr = pr * ka - pi * kb
        yi = pr * kb + pi * ka2
        prod = jnp.concatenate([yr, yi], axis=0).astype(BF16)
        return _dot(wt_ref[...], prod) + u * skip_ref[o:o + 1, cs]

    def seq(r0, n, cs, w_ref, wt_ref, kt_ref):
        v = sconv(v_ref, swv_ref, sbv_ref, r0, n, cs)
        x1 = sconv(x1_ref, sw1_ref, sb1_ref, r0, n, cs)
        x2 = sconv(x2_ref, sw2_ref, sb2_ref, r0, n, cs)
        z = x1 * lconv(v, 0, n, cs, w_ref, wt_ref, kt_ref)
        y = x2 * lconv(z, 1, n, cs, w_ref, wt_ref, kt_ref)
        o_ref[r0:r0 + n, cs] = y.astype(o_ref.dtype)

    chunks = [slice(c * HY_CH, (c + 1) * HY_CH) for c in range(HY_W // HY_CH)]

    @pl.when(g < NG_CTX)
    def _():
        for s in range(SEQ_PER_GRP):
            for cs in chunks:
                seq(s * L_CTX, L_CTX, cs, wc_ref, wtc_ref, ktc_ref)

    @pl.when(g >= NG_CTX)
    def _():
        for cs in chunks:
            seq(0, L_S, cs, ws_ref, wts_ref, kts_ref)


def _hyena(u, sw, sb, skip, consts, kt_c, kt_s, layer):
    c0 = POOL_W // HY_W

    def part(p):
        return pl.BlockSpec((GRP, HY_W), lambda g: (g, c0 + p))

    def swpart(p):
        return pl.BlockSpec((3, HY_W), lambda g: (0, p))

    def sbpart(p):
        return pl.BlockSpec((1, HY_W), lambda g: (0, p))

    def tables(n):
        return pl.BlockSpec((None, 2, 3, n, HY_W), lambda g: (layer, 0, 0, 0, 0), pipeline_mode=pl.Buffered(1))

    return pl.pallas_call(
        _hyena_kernel,
        grid=(NG,),
        in_specs=[
            part(0), part(1), part(2),
            swpart(0), swpart(1), swpart(2),
            sbpart(0), sbpart(1), sbpart(2),
            pl.BlockSpec((2, HY_W), lambda g: (0, 0)),
            pl.BlockSpec((2 * L_CTX, L_CTX), lambda g: (0, 0)),
            pl.BlockSpec((L_CTX, 2 * L_CTX), lambda g: (0, 0)),
            pl.BlockSpec((2 * L_S, L_S), lambda g: (0, 0)),
            pl.BlockSpec((L_S, 2 * L_S), lambda g: (0, 0)),
            tables(L_CTX), tables(L_S),
        ],
        out_specs=pl.BlockSpec((GRP, HY_W), lambda g: (g, 0)),
        out_shape=jax.ShapeDtypeStruct((T, HY_W), BF16),
        compiler_params=_cp(("parallel",), 56),
        name="hyena_mix",
    )(u, u, u, sw, sw, sw, sb, sb, sb, skip,
      consts["dft_c"], consts["dftt_c"], consts["dft_s"], consts["dftt_s"], kt_c, kt_s)


def _cachekv_kernel(x_ref, w_ref, o_ref):
    o_ref[...] = _dot(x_ref[...].astype(BF16), w_ref[...]).astype(o_ref.dtype)


def _cachekv(cache_ckv, layer, wkv):
    return pl.pallas_call(
        _cachekv_kernel,
        grid=(B_S,),
        in_specs=[
            pl.BlockSpec((None, None, L_CTX, KV_RANK), lambda b: (b, layer, 0, 0)),
            pl.BlockSpec((None, KV_RANK, H * (NOPE + VD)), lambda b: (layer, 0, 0)),
        ],
        out_specs=pl.BlockSpec((L_CTX, H * (NOPE + VD)), lambda b: (b, 0)),
        out_shape=jax.ShapeDtypeStruct((B_S * L_CTX, H * (NOPE + VD)), BF16),
        compiler_params=_cp(("parallel",)),
        name="cache_kv",
    )(cache_ckv, wkv)


ATT_TQ = 512
NK_S = L_S + L_CTX


def _attn_kernel(q_ref, kv_ref, krr_ref, kvc_ref, krc_ref, o_ref, kcat_ref):
    g = pl.program_id(0)
    lane = lax.broadcasted_iota(jnp.int32, (1, LANES), 1)
    hk = NOPE + LANES
    vo = H * NOPE

    krr = krr_ref[...]
    for h in range(H):
        kcat_ref[0:GRP, h * hk:h * hk + NOPE] = kv_ref[:, h * NOPE:(h + 1) * NOPE]
        kcat_ref[0:GRP, h * hk + NOPE:(h + 1) * hk] = krr

    def qcat(rows, h):
        qn = q_ref[rows, h * NOPE:(h + 1) * NOPE]
        pair = q_ref[rows, vo + (h // 2) * LANES:vo + (h // 2 + 1) * LANES].astype(F32)
        keep = (lane < ROPE) if h % 2 == 0 else (lane >= ROPE)
        return jnp.concatenate([qn, jnp.where(keep, pair, 0.0).astype(BF16)], axis=1)

    def probs(sc):
        m = jnp.max(sc, axis=-1, keepdims=True)
        p = jnp.exp(sc - m)
        return p.astype(BF16), 1.0 / jnp.sum(p, axis=-1, keepdims=True)

    @pl.when(g < NG_CTX)
    def _():
        def body(s, carry):
            rows = pl.ds(pl.multiple_of(s * L_CTX, L_CTX), L_CTX)
            for h in range(H):
                p, rl = probs(_dot_nt(qcat(rows, h), kcat_ref[rows, h * hk:(h + 1) * hk]))
                o = _dot(p, kv_ref[rows, vo + h * VD:vo + (h + 1) * VD]) * rl
                o_ref[rows, h * VD:(h + 1) * VD] = o.astype(o_ref.dtype)
            return carry

        lax.fori_loop(0, SEQ_PER_GRP, body, 0)

    @pl.when(g >= NG_CTX)
    def _():
        krc = krc_ref[...]
        for h in range(H):
            kcat_ref[GRP:NK_S, h * hk:h * hk + NOPE] = kvc_ref[:, h * NOPE:(h + 1) * NOPE]
            kcat_ref[GRP:NK_S, h * hk + NOPE:(h + 1) * hk] = krc

        def body(t, carry):
            rows = pl.ds(pl.multiple_of(t * ATT_TQ, ATT_TQ), ATT_TQ)
            for h in range(H):
                p, rl = probs(_dot_nt(qcat(rows, h), kcat_ref[:, h * hk:(h + 1) * hk]))
                o = _dot(p[:, :GRP], kv_ref[:, vo + h * VD:vo + (h + 1) * VD])
                o = o + _dot(p[:, GRP:], kvc_ref[:, vo + h * VD:vo + (h + 1) * VD])
                o_ref[rows, h * VD:(h + 1) * VD] = (o * rl).astype(o_ref.dtype)
            return carry

        lax.fori_loop(0, L_S // ATT_TQ, body, 0)


def _attention(q, kv, krr, kvc, krc):
    def cache_blk(g):
        return jnp.maximum(g - NG_CTX, 0)

    return pl.pallas_call(
        _attn_kernel,
        grid=(NG,),
        in_specs=[
            pl.BlockSpec((GRP, H * (NOPE + ROPE)), lambda g: (g, 0)),
            pl.BlockSpec((GRP, H * (NOPE + VD)), lambda g: (g, 0)),
            pl.BlockSpec((GRP, LANES), lambda g: (g, 0)),
            pl.BlockSpec((L_CTX, H * (NOPE + VD)), lambda g: (cache_blk(g), 0)),
            pl.BlockSpec((L_CTX, LANES), lambda g: (cache_blk(g), 0)),
        ],
        out_specs=pl.BlockSpec((GRP, H * VD), lambda g: (g, 0)),
        out_shape=jax.ShapeDtypeStruct((T, H * VD), BF16),
        scratch_shapes=[pltpu.VMEM((NK_S, H * (NOPE + LANES)), BF16)],
        compiler_params=_cp(("parallel",)),
        name="mla_attention",
    )(q, kv, krr, kvc, krc)


def _wout_kernel(yp_ref, yh_ref, ym_ref, w_ref, gate_ref, g2_ref, shift_ref, scale_ref, rw_ref, *refs, tm):
    *x_refs, o_ref, h_ref, aff_ref, prev_ref = refs

    @pl.when(pl.program_id(0) == 0)
    def _():
        prev_ref[...] = jnp.zeros_like(prev_ref)

    hb = (_rms(prev_ref[...], g2_ref[...]) * (1.0 + scale_ref[...]) + shift_ref[...]).astype(BF16)
    h_ref[...] = hb
    logits = _dot(hb, rw_ref[...])
    lane = lax.broadcasted_iota(jnp.int32, logits.shape, 1)
    logits = jnp.where(lane < E, logits, -jnp.inf)
    ex = jnp.exp(logits - jnp.max(logits, axis=-1, keepdims=True))
    aff_ref[...] = ex / jnp.sum(ex, axis=-1, keepdims=True)

    y = jnp.concatenate([yp_ref[...], yh_ref[...], ym_ref[...]], axis=1)
    for c in range(D // NCH):
        cs = slice(c * NCH, (c + 1) * NCH)
        v = _load_src(x_refs, tm, cs) + gate_ref[:, cs] * _dot(y, w_ref[:, cs])
        o_ref[:, cs] = v.astype(o_ref.dtype)
        prev_ref[:, cs] = v


def _wout(yp, yh, ym, x, mod3, wts, layer):
    tm = 512
    per = GRP // tm
    n = T // tm
    xs, x_specs = _src_specs(x, tm)

    def cur(i):
        return jnp.minimum(i, n - 1)

    def prev(i):
        return jnp.maximum(i - 1, 0)

    def rows(width, blk):
        return pl.BlockSpec((tm, width), lambda i: (blk(i), 0))

    def mod_row(k, blk):
        return pl.BlockSpec((None, 1, D), lambda i: (6 * (blk(i) // per) + k, 0, 0))

    return pl.pallas_call(
        functools.partial(_wout_kernel, tm=tm),
        grid=(n + 1,),
        in_specs=[
            rows(POOL_W, cur), rows(HY_W, cur), rows(H * VD, cur),
            _resident((D, D), layer),
            mod_row(2, cur),
            _resident((1, D), layer), mod_row(3, prev), mod_row(4, prev),
            _resident((D, LANES), layer),
        ] + x_specs,
        out_specs=[rows(D, cur), rows(D, prev), rows(LANES, prev)],
        out_shape=[
            jax.ShapeDtypeStruct((T, D), BF16),
            jax.ShapeDtypeStruct((T, D), BF16),
            jax.ShapeDtypeStruct((T, LANES), F32),
        ],
        scratch_shapes=[pltpu.VMEM((tm, D), F32)],
        compiler_params=_cp(("arbitrary",), 56),
        name="out_proj",
    )(yp, yh, ym, wts["w_out"], mod3, wts["g2"], mod3, mod3, wts["rw"], *xs)


RANK_CH = 256
SEARCH_ROUNDS = 14
SEARCH_WAYS = 8
AFF_MAX = 2.0


def _rank_kernel(aff_ref, afft_ref, tri_ref, rank_ref, cnt_ref, cut_ref):
    g = pl.program_id(0)

    def count_ge(a, t):
        return jnp.sum(jnp.where(a >= t, 1.0, 0.0), axis=1, keepdims=True)

    def search(specs, capf):
        acts = [afft_ref[:, r0:r0 + n] for r0, n in specs]
        lo = [jnp.zeros((E, 1), F32) for _ in specs]
        hi = [jnp.full((E, 1), AFF_MAX, F32) for _ in specs]
        for _ in range(SEARCH_ROUNDS):
            for s, a in enumerate(acts):
                step = (hi[s] - lo[s]) * (1.0 / SEARCH_WAYS)
                ts = [lo[s] + step * k for k in range(1, SEARCH_WAYS)]
                ok = [count_ge(a, t) >= capf for t in ts]
                new_lo, new_hi = lo[s], hi[s]
                for t, o in zip(ts, ok):
                    new_lo = jnp.where(o, t, new_lo)
                for t, o in zip(reversed(ts), reversed(ok)):
                    new_hi = jnp.where(o, new_hi, t)
                lo[s], hi[s] = new_lo, new_hi
        open_brackets = jnp.zeros((E, 1), F32)
        for s, a in enumerate(acts):
            top = jnp.max(jnp.where(a >= lo[s], jnp.where(a < hi[s], a, -1.0), -1.0), axis=1, keepdims=True)
            low = jnp.min(jnp.where(a >= lo[s], jnp.where(a < hi[s], a, AFF_MAX), AFF_MAX), axis=1, keepdims=True)
            cut_ref[:, s:s + 1] = top
            open_brackets = open_brackets + jnp.where(top != low, 1.0, 0.0)
        return jnp.sum(open_brackets)

    def exact_cut(s, r0, n, capf):
        for e in range(E):
            row = afft_ref[e:e + 1, r0:r0 + n]
            acc = jnp.zeros((1, n), F32)
            for c in range(n // RANK_CH):
                col = aff_ref[r0 + c * RANK_CH:r0 + (c + 1) * RANK_CH, e:e + 1]
                acc = acc + jnp.sum(jnp.where(col >= row, 1.0, 0.0), axis=0, keepdims=True)
            cnt_ref[e:e + 1, 0:n] = acc
        a = afft_ref[:, r0:r0 + n]
        cut_ref[:, s:s + 1] = jnp.max(jnp.where(cnt_ref[:, 0:n] >= capf, a, -1.0), axis=1, keepdims=True)

    def slots(s, r0, n, capf):
        a = afft_ref[:, r0:r0 + n]
        cut = cut_ref[:, s:s + 1]
        above = a > cut
        tied = a == cut
        above_f = jnp.where(above, 1.0, 0.0)
        n_above = jnp.sum(above_f, axis=1, keepdims=True)
        marks = jnp.concatenate([above_f, jnp.where(tied, 1.0, 0.0)], axis=0).astype(BF16)
        before = _dot(marks, tri_ref[0:n, 0:n])
        tie_slot = n_above + before[E:]
        slot = jnp.where(above, before[:E], jnp.where(tied, jnp.where(tie_slot < capf, tie_slot, n), n))
        rank_ref[:, r0:r0 + n] = slot.astype(jnp.int32)

    def group(specs, cap):
        capf = float(cap)
        unresolved = search(specs, capf)

        @pl.when(unresolved > 0.0)
        def _():
            for s, (r0, n) in enumerate(specs):
                exact_cut(s, r0, n, capf)

        for s, (r0, n) in enumerate(specs):
            slots(s, r0, n, capf)

    @pl.when(g < NG_CTX)
    def _():
        group([(s * L_CTX, L_CTX) for s in range(SEQ_PER_GRP)], CAP_CTX)

    @pl.when(g >= NG_CTX)
    def _():
        group([(0, L_S)], CAP_S)


def _rank(aff, afft, tri):
    return pl.pallas_call(
        _rank_kernel,
        grid=(NG,),
        in_specs=[
            pl.BlockSpec((GRP, LANES), lambda g: (g, 0)),
            pl.BlockSpec((E, GRP), lambda g: (0, g)),
            pl.BlockSpec((L_S, L_S), lambda g: (0, 0)),
        ],
        out_specs=pl.BlockSpec((E, GRP), lambda g: (0, g)),
        out_shape=jax.ShapeDtypeStruct((E, T), jnp.int32),
        scratch_shapes=[pltpu.VMEM((E, GRP), F32), pltpu.VMEM((E, LANES), F32)],
        compiler_params=_cp(("parallel",)),
        name="moe_rank",
    )(aff, afft, tri)


def _gather_kernel(rank_ref, afft_ref, h_ref, xs_ref, gs_ref, sel_ref):
    g = pl.program_id(0)

    def seq(r0, n, cap, slot0):
        slot_i = lax.broadcasted_iota(jnp.int32, (cap, n), 0)
        for e in range(E):
            hit = slot_i == rank_ref[e:e + 1, r0:r0 + n]
            sel_ref[e * cap:(e + 1) * cap, 0:n] = jnp.where(hit, 1.0, 0.0).astype(BF16)
            gs_ref[e, slot0:slot0 + cap, :] = jnp.sum(
                jnp.where(hit, afft_ref[e:e + 1, r0:r0 + n], 0.0), axis=1, keepdims=True)
        for c in range(D // NCH):
            cs = slice(c * NCH, (c + 1) * NCH)
            res = _dot(sel_ref[0:E * cap, 0:n], h_ref[r0:r0 + n, cs])
            for e in range(E):
                xs_ref[e, slot0:slot0 + cap, cs] = res[e * cap:(e + 1) * cap].astype(xs_ref.dtype)

    @pl.when(g < NG_CTX)
    def _():
        for s in range(SEQ_PER_GRP):
            seq(s * L_CTX, L_CTX, CAP_CTX, s * CAP_CTX)

    @pl.when(g >= NG_CTX)
    def _():
        seq(0, L_S, CAP_S, 0)


def _gather(rank, afft, h2):
    return pl.pallas_call(
        _gather_kernel,
        grid=(NG,),
        in_specs=[
            pl.BlockSpec((E, GRP), lambda g: (0, g)),
            pl.BlockSpec((E, GRP), lambda g: (0, g)),
            pl.BlockSpec((GRP, D), lambda g: (g, 0)),
        ],
        out_specs=[
            pl.BlockSpec((E, SLOTS, D), lambda g: (0, g, 0)),
            pl.BlockSpec((E, SLOTS, 1), lambda g: (0, g, 0)),
        ],
        out_shape=[
            jax.ShapeDtypeStruct((E, ROWS_E, D), BF16),
            jax.ShapeDtypeStruct((E, ROWS_E, 1), F32),
        ],
        scratch_shapes=[pltpu.VMEM((E * CAP_S, L_S), BF16)],
        compiler_params=_cp(("parallel",)),
        name="moe_gather",
    )(rank, afft, h2)


FFN_TF = 256
FFN_TM = 1024


def _ffn_kernel(x_ref, wg_ref, wu_ref, wd_ref, gs_ref, o_ref, acc_ref):
    f = pl.program_id(2)
    nf = pl.num_programs(2)

    def partial_down():
        x = x_ref[...]
        a = _dot(x, wg_ref[...].astype(BF16))
        b = _dot(x, wu_ref[...].astype(BF16))
        hid = (a * jax.nn.sigmoid(a) * b).astype(BF16)
        return _dot(hid, wd_ref[...].astype(BF16))

    @pl.when(f == 0)
    def _():
        acc_ref[...] = partial_down()

    @pl.when((f > 0) & (f < nf - 1))
    def _():
        acc_ref[...] += partial_down()

    @pl.when(f == nf - 1)
    def _():
        o_ref[...] = ((acc_ref[...] + partial_down()) * gs_ref[...]).astype(o_ref.dtype)


def _ffn(xs, gs, w_gate, w_up, w_down, layer):
    return pl.pallas_call(
        _ffn_kernel,
        grid=(E, ROWS_E // FFN_TM, FF // FFN_TF),
        in_specs=[
            pl.BlockSpec((None, FFN_TM, D), lambda e, m, f: (e, m, 0)),
            pl.BlockSpec((None, None, D, FFN_TF), lambda e, m, f: (layer, e, 0, f)),
            pl.BlockSpec((None, None, D, FFN_TF), lambda e, m, f: (layer, e, 0, f)),
            pl.BlockSpec((None, None, FFN_TF, D), lambda e, m, f: (layer, e, f, 0)),
            pl.BlockSpec((None, FFN_TM, 1), lambda e, m, f: (e, m, 0)),
        ],
        out_specs=pl.BlockSpec((None, FFN_TM, D), lambda e, m, f: (e, m, 0)),
        out_shape=jax.ShapeDtypeStruct((E, ROWS_E, D), BF16),
        scratch_shapes=[pltpu.VMEM((FFN_TM, D), F32)],
        compiler_params=_cp(("parallel", "parallel", "arbitrary"), 56),
        name="moe_ffn",
    )(xs, w_gate, w_up, w_down, gs)


CMB_TM = 512


def _combine_kernel(rt_ref, ys_ref, x_ref, gate_ref, exc_ref, exs_ref, *refs, last):
    if last:
        fg_ref, oc_ref, os_ref, st_ref = refs
    else:
        oc_ref, st_ref = refs
        os_ref = oc_ref
    g = pl.program_id(0)
    half = pl.program_id(1)

    def scatter(r0, n, cap, slot0, ex_ref, o_ref):
        ec = E * cap
        r = jnp.minimum(rt_ref[r0:r0 + n, :], cap).astype(F32).astype(BF16)
        want = (lax.broadcasted_iota(jnp.int32, (1, NCH), 1) & (cap - 1)).astype(F32)
        for c in range(ec // NCH):
            cs = slice(c * NCH, (c + 1) * NCH)
            st_ref[r0:r0 + n, cs] = jnp.where(_dot(r, ex_ref[:, cs]) == want, 1.0, 0.0).astype(BF16)
        for c in range(D // NCH):
            cs = slice(c * NCH, (c + 1) * NCH)
            ys = ys_ref[:, pl.ds(slot0, cap), cs].reshape(ec, NCH)
            moe = _dot(st_ref[r0:r0 + n, 0:ec], ys)
            o_ref[r0:r0 + n, cs] = x_ref[r0:r0 + n, cs].astype(F32) + gate_ref[:, cs] * moe

    def finish(o_ref):
        if last:
            o_ref[...] = _rms(o_ref[...], fg_ref[...])

    @pl.when(g < NG_CTX)
    def _():
        seqs = CMB_TM // L_CTX
        for s in range(seqs):
            slot0 = pl.multiple_of((half * seqs + s) * CAP_CTX, CAP_CTX)
            scatter(s * L_CTX, L_CTX, CAP_CTX, slot0, exc_ref, oc_ref)
        finish(oc_ref)

    @pl.when(g >= NG_CTX)
    def _():
        scatter(0, CMB_TM, CAP_S, 0, exs_ref, os_ref)
        finish(os_ref)


def _combine(rank_t, ys, x, mod3, consts, final_g=None):
    last = final_g is not None
    per = GRP // CMB_TM

    def row(g, i):
        return (g * per + i, 0)

    in_specs = [
        pl.BlockSpec((CMB_TM, LANES), row),
        pl.BlockSpec((E, SLOTS, D), lambda g, i: (0, g, 0)),
        pl.BlockSpec((CMB_TM, D), row),
        pl.BlockSpec((None, 1, D), lambda g, i: (6 * g + 5, 0, 0)),
        pl.BlockSpec((LANES, E * CAP_CTX), lambda g, i: (0, 0)),
        pl.BlockSpec((LANES, E * CAP_S), lambda g, i: (0, 0)),
    ]
    args = [rank_t, ys, x, mod3, consts["ex_c"], consts["ex_s"]]
    if last:
        n_ctx = T_CTX // CMB_TM
        in_specs.append(pl.BlockSpec((1, D), lambda g, i: (0, 0)))
        args.append(final_g)
        out_specs = [
            pl.BlockSpec((CMB_TM, D), lambda g, i: (jnp.where(g < NG_CTX, g * per + i, n_ctx - 1), 0)),
            pl.BlockSpec((CMB_TM, D), lambda g, i: (jnp.where(g < NG_CTX, 0, (g - NG_CTX) * per + i), 0)),
        ]
        out_shape = [jax.ShapeDtypeStruct((T_CTX, D), F32), jax.ShapeDtypeStruct((T - T_CTX, D), F32)]
    else:
        out_specs = pl.BlockSpec((CMB_TM, D), row)
        out_shape = jax.ShapeDtypeStruct((T, D), F32)
    return pl.pallas_call(
        functools.partial(_combine_kernel, last=last),
        grid=(NG, per),
        in_specs=in_specs,
        out_specs=out_specs,
        out_shape=out_shape,
        scratch_shapes=[pltpu.VMEM((CMB_TM, E * CAP_S), BF16)],
        compiler_params=_cp(("arbitrary", "arbitrary"), 56),
        name="moe_combine",
    )(*args)


def _np_constants():
    c = {}
    for tag, n in (("c", L_CTX), ("s", L_S)):
        t = np.arange(n)
        inv = np.zeros((n, POOL_W), np.float32)
        for k, w in enumerate(POOL_WINDOWS):
            lo = np.clip(t - w // 2, 0, n)
            hi = np.clip(t - w // 2 + w, 0, n)
            inv[:, k * POOL_GC:(k + 1) * POOL_GC] = (1.0 / (hi - lo).astype(np.float64))[:, None]
        c["inv_" + tag] = inv
        kk = np.arange(n, dtype=np.float64)[:, None]
        tt = np.arange(n, dtype=np.float64)[None, :]
        ang = np.pi * kk * tt / n
        dft = np.concatenate([np.cos(ang), -np.sin(ang)], axis=0)
        dft[n] = (-1.0) ** np.arange(n)
        c["dft_" + tag] = dft.astype(np.float32)
        c["dftt_" + tag] = np.ascontiguousarray(dft.T).astype(np.float32)
        tl = np.linspace(0.0, 1.0, n, dtype=np.float32)[:, None]
        bands = np.arange(1, POS_BANDS + 1, dtype=np.float32)[None, :]
        feats = np.concatenate([tl, np.sin(2 * np.pi * tl * bands), np.cos(2 * np.pi * tl * bands)], axis=1)
        c["feat_" + tag] = np.pad(feats.astype(np.float32), ((0, 0), (0, LANES - feats.shape[1])))
        c["t_" + tag] = tl
    n_rows = L_S // GRID_W
    row = np.repeat(np.arange(n_rows), GRID_W).astype(np.float32)
    col = np.tile(np.arange(GRID_W), n_rows).astype(np.float32)
    nf = ROPE // 4
    inv_f = (1.0 / ROPE_THETA ** (np.arange(nf, dtype=np.float32) / nf)).astype(np.float32)
    a_row = (row[:, None] * inv_f[None]).astype(np.float32).astype(np.float64)
    a_col = (col[:, None] * inv_f[None]).astype(np.float32).astype(np.float64)
    cos64 = np.concatenate([np.cos(a_row), np.cos(a_row), np.cos(a_col), np.cos(a_col)], axis=1)
    sin64 = np.concatenate([np.sin(a_row), np.sin(a_row), np.sin(a_col), np.sin(a_col)], axis=1)

    def table(reps):
        ident = np.stack([np.ones((L_S, ROPE * reps)), np.zeros((L_S, ROPE * reps))])
        rot = np.stack([np.tile(cos64, (1, reps)), np.tile(sin64, (1, reps))])
        return np.stack([ident, rot]).astype(np.float32)

    c["cs_q"] = table(H)
    c["cs_k"] = table(LANES // ROPE)
    c["tri"] = np.triu(np.ones((L_S, L_S), np.float32), k=1)
    for tag, cap in (("c", CAP_CTX), ("s", CAP_S)):
        ex = np.zeros((LANES, E * cap), np.float32)
        ex[np.arange(E * cap) // cap, np.arange(E * cap)] = 1.0
        c["ex_" + tag] = ex
    return c


def _constants():
    c = {k: jnp.asarray(v) for k, v in _np_constants().items()}
    for k in ("dft_c", "dftt_c", "dft_s", "dftt_s", "ex_c", "ex_s", "tri"):
        c[k] = c[k].astype(BF16)
    return c


def _rope_swap(w):
    q = ROPE // 4
    return jnp.concatenate([-w[..., q:2 * q], w[..., :q], -w[..., 3 * q:], w[..., 2 * q:3 * q]], axis=-1)


def _dot_hi(a, b):
    return jnp.dot(a, b, preferred_element_type=F32, precision=lax.Precision.HIGHEST)


def _filter_kernel(feat_ref, t_ref, w1_ref, b1_ref, w2_ref, b2_ref, fr_ref, w3f_ref, w3b_ref,
                   ldf_ref, ldb_ref, dft_ref, o_ref, z_ref):
    n = feat_ref.shape[0]

    @pl.when((pl.program_id(1) == 0) & (pl.program_id(2) == 0))
    def _():
        fr = fr_ref[...]
        z = jnp.sin(fr * (_dot_hi(feat_ref[...], w1_ref[...]) + b1_ref[...]))
        z_ref[...] = jnp.sin(fr * (_dot_hi(z, w2_ref[...]) + b2_ref[...]))

    z = z_ref[...]
    t = t_ref[...]
    first = lax.broadcasted_iota(jnp.int32, (n, 1), 0) == 0
    hf = _dot_hi(z, w3f_ref[...]) * jnp.exp(-jnp.exp(ldf_ref[...]) * t)
    hb = _dot_hi(z, w3b_ref[...]) * jnp.exp(-jnp.exp(ldb_ref[...]) * t)
    hb = jnp.where(first, 0.0, hb)
    norm = jnp.sum(jnp.abs(hf), axis=0, keepdims=True) + jnp.sum(jnp.abs(hb), axis=0, keepdims=True) + EPS
    both = _dot(dft_ref[...], jnp.concatenate([hf / norm, hb / norm], axis=1).astype(BF16))
    pf, pb = both[:, :hf.shape[1]], both[:, hf.shape[1]:]
    sc = jnp.where(first, 0.5 / n, 1.0 / n)
    ka = (pf[:n] + pb[:n]) * sc
    o_ref[0] = ka
    o_ref[1] = jnp.where(first, 0.0, (pf[n:] - pb[n:]) * sc)
    o_ref[2] = jnp.where(first, (pf[n:] + pb[n:]) * sc, ka)


def _hyena_tables(n, feats, tcol, dft, fw):
    tc = 256
    nc = HY_W // tc

    def lay(shape):
        return pl.BlockSpec((None,) + shape, lambda l, o, c: (l, 0, 0))

    def w3(back):
        return pl.BlockSpec((None, LANES, tc), lambda l, o, c: (l, 0, (2 * o + back) * nc + c))

    def ld(back):
        return pl.BlockSpec((None, 1, tc), lambda l, o, c: (l, 0, (2 * o + back) * nc + c))

    return pl.pallas_call(
        _filter_kernel,
        grid=(DEPTH, 2, nc),
        in_specs=[
            pl.BlockSpec((n, LANES), lambda l, o, c: (0, 0)),
            pl.BlockSpec((n, 1), lambda l, o, c: (0, 0)),
            lay((LANES, LANES)), lay((1, LANES)), lay((LANES, LANES)), lay((1, LANES)), lay((1, LANES)),
            w3(0), w3(1), ld(0), ld(1),
            pl.BlockSpec((2 * n, n), lambda l, o, c: (0, 0)),
        ],
        out_specs=pl.BlockSpec((None, None, 3, n, tc), lambda l, o, c: (l, o, 0, 0, c)),
        out_shape=jax.ShapeDtypeStruct((DEPTH, 2, 3, n, HY_W), F32),
        scratch_shapes=[pltpu.VMEM((n, LANES), F32)],
        compiler_params=_cp(("arbitrary", "arbitrary", "arbitrary")),
        name="hyena_filter",
    )(feats, tcol, fw["w1"], fw["b1"], fw["w2"], fw["b2"], fw["freq"], fw["w3"], fw["w3"],
      fw["ld"], fw["ld"], dft)


def _filter_weights(w1, b1, w2, b2, w3, freq, log_decay):
    ph = LANES - FILT_HID
    return dict(
        w1=jnp.pad(w1, ((0, 0), (0, LANES - w1.shape[1]), (0, ph))),
        b1=jnp.pad(b1, ((0, 0), (0, ph))).reshape(DEPTH, 1, LANES),
        w2=jnp.pad(w2, ((0, 0), (0, ph), (0, ph))),
        b2=jnp.pad(b2, ((0, 0), (0, ph))).reshape(DEPTH, 1, LANES),
        freq=jnp.pad(freq, ((0, 0), (0, ph))).reshape(DEPTH, 1, LANES),
        w3=jnp.pad(w3, ((0, 0), (0, ph), (0, 0))),
        ld=log_decay.reshape(DEPTH, 1, 4 * HY_W))


def _prep_weights(w_in, pool_w, pool_scale, hy_short_b, mla_q_norm, mla_kv_norm, mla_w_uq, mla_w_ukv,
                  w_out, router_w, norm1_g, norm2_g):
    n_main = N_U + Q_RANK
    kr_cols = w_in[:, :, IN_COLS - ROPE:]
    kr_swap = _rope_swap(kr_cols)
    w_x = jnp.concatenate([w_in[:, :, n_main:IN_COLS - ROPE], kr_cols, kr_cols, kr_swap, kr_swap], axis=2)
    wq = mla_w_uq.reshape(DEPTH, Q_RANK, H, NOPE + ROPE)
    wq_rope = wq[..., NOPE:]
    wq = jnp.concatenate([wq[..., :NOPE].reshape(DEPTH, Q_RANK, -1), wq_rope.reshape(DEPTH, Q_RANK, -1),
                          _rope_swap(wq_rope).reshape(DEPTH, Q_RANK, -1)], axis=2)
    wkv = mla_w_ukv.reshape(DEPTH, KV_RANK, H, NOPE + VD)
    wkv = jnp.concatenate([wkv[..., :NOPE].reshape(DEPTH, KV_RANK, -1), wkv[..., NOPE:].reshape(DEPTH, KV_RANK, -1)],
                          axis=2)
    return dict(
        w_main=w_in.astype(BF16), w_x=w_x.astype(BF16), wq=wq.astype(BF16), wkv=wkv.astype(BF16),
        w_out=w_out.astype(BF16), rw=jnp.pad(router_w, ((0, 0), (0, 0), (0, LANES - E))).astype(BF16),
        pool_w=pool_w.astype(BF16), pool_scale=pool_scale.reshape(DEPTH, 1, POOL_W),
        sb=hy_short_b.reshape(DEPTH, 1, 3 * HY_W),
        qg=mla_q_norm.reshape(DEPTH, 1, Q_RANK), kvg=mla_kv_norm.reshape(DEPTH, 1, KV_RANK),
        g1=norm1_g.reshape(DEPTH, 1, D), g2=norm2_g.reshape(DEPTH, 1, D))


def _layer(x, layer, mod3, wts, consts, kt_c, kt_s, hy_short_w, hy_skip, cache_ckv, krc,
           exp_w_gate, exp_w_up, exp_w_down, final_g):
    u, q, ckv, kr, krr, kv = _inproj(x, mod3, wts, consts, layer)
    y_pool = _pool(u, consts, wts["pool_w"][layer], wts["pool_scale"][layer])
    y_hy = _hyena(u, hy_short_w[layer], wts["sb"][layer], hy_skip[layer], consts, kt_c, kt_s, layer)
    kvc = _cachekv(cache_ckv, layer, wts["wkv"])
    y_mla = _attention(q, kv, krr, kvc, krc)
    x, h2, aff = _wout(y_pool, y_hy, y_mla, x, mod3, wts, layer)

    afft = jnp.swapaxes(aff[:, :E], 0, 1)
    rank = _rank(aff, afft, consts["tri"])
    xs, gs = _gather(rank, afft, h2)
    ys = _ffn(xs, gs, exp_w_gate, exp_w_up, exp_w_down, layer)
    rank_t = jnp.pad(jnp.swapaxes(rank, 0, 1), ((0, 0), (0, LANES - E)))
    return _combine(rank_t, ys, x, mod3, consts, final_g), ckv, kr


def kernel(x_prompt, x_sample, cache_ckv, cache_krope, c, c_ctx, ada_w, ada_b, norm1_g, norm2_g, w_in, pool_w, pool_scale, hy_short_w, hy_short_b, hy_ffn_w1, hy_ffn_b1, hy_ffn_w2, hy_ffn_b2, hy_ffn_w3, hy_freq, hy_log_decay, hy_skip, mla_q_norm, mla_kv_norm, mla_w_uq, mla_w_ukv, w_out, router_w, exp_w_gate, exp_w_up, exp_w_down, final_norm_g):
    consts = _constants()
    x = (x_prompt.reshape(T_CTX, D), x_sample.reshape(T - T_CTX, D))
    c16 = jnp.concatenate([jnp.broadcast_to(c_ctx[None], (NG_CTX, D)), c], axis=0)
    mod = _adaln_mod(c16, ada_w, ada_b).reshape(DEPTH, NG * 6, 1, D)

    fw = _filter_weights(hy_ffn_w1, hy_ffn_b1, hy_ffn_w2, hy_ffn_b2, hy_ffn_w3, hy_freq, hy_log_decay)
    kt_c = _hyena_tables(L_CTX, consts["feat_c"], consts["t_c"], consts["dft_c"], fw)
    kt_s = _hyena_tables(L_S, consts["feat_s"], consts["t_s"], consts["dft_s"], fw)

    wts = _prep_weights(w_in, pool_w, pool_scale, hy_short_b, mla_q_norm, mla_kv_norm, mla_w_uq, mla_w_ukv,
                        w_out, router_w, norm1_g, norm2_g)
    krc = jnp.swapaxes(cache_krope, 0, 1).reshape(DEPTH, B_S * L_CTX, ROPE)
    krc = jnp.concatenate([krc, krc], axis=2).astype(BF16)

    ckv_list, kr_list = [], []
    for l in range(DEPTH):
        final_g = final_norm_g.reshape(1, D) if l == DEPTH - 1 else None
        x, ckv, kr = _layer(x, l, mod[l], wts, consts, kt_c, kt_s, hy_short_w, hy_skip, cache_ckv, krc[l],
                            exp_w_gate, exp_w_up, exp_w_down, final_g)
        ckv_list.append(ckv.reshape(B_CTX, L_CTX, KV_RANK))
        kr_list.append(kr[:, :ROPE].reshape(B_CTX, L_CTX, ROPE))

    y_prompt, y_sample = x
    return (y_prompt.reshape(B_CTX, L_CTX, D), y_sample.reshape(B_S, L_S, D),
            jnp.stack(ckv_list, axis=1), jnp.stack(kr_list, axis=1))
```

```python
import functools
import math

import numpy as np
import jax
import jax.numpy as jnp
from jax import lax
from jax.experimental import pallas as pl
from jax.experimental.pallas import tpu as pltpu

F32 = jnp.float32
BF16 = jnp.bfloat16

D = 2048
DEPTH = 2
B_CTX, L_CTX = 32, 256
B_S, L_S = 8, 1024
T_CTX = B_CTX * L_CTX
T = T_CTX + B_S * L_S
GRP = 1024
NG = T // GRP
NG_CTX = T_CTX // GRP
SEQ_PER_GRP = GRP // L_CTX
EPS = 1e-6
GRID_W = 64

POOL_W = 512
POOL_GC = 128
POOL_WINDOWS = (2, 4, 8, 16)
HY_W = 512
POS_BANDS = 8
FILT_HID = 64
H = 8
NOPE = 128
ROPE = 64
VD = 128
Q_RANK = 512
KV_RANK = 256
ROPE_THETA = 10000.0
IN_COLS = 2880
IN_PAD = 3072
E = 16
FF = 1024
CAP_CTX = 2 * L_CTX // E
CAP_S = 2 * L_S // E
SLOTS = GRP * 2 // E
ROWS_E = NG * SLOTS
LANES = 128
NCH = 512
ATT_SCALE = 1.0 / math.sqrt(NOPE + ROPE)
MB = 1024 * 1024


def _cp(sem, vmem_mb=48):
    return pltpu.CompilerParams(dimension_semantics=sem, vmem_limit_bytes=vmem_mb * MB)


def _rms(x, g):
    return x * lax.rsqrt(jnp.mean(x * x, axis=-1, keepdims=True) + EPS) * g


def _dot(a, b):
    return jnp.dot(a, b, preferred_element_type=F32)


def _dot_nt(a, b):
    return lax.dot_general(a, b, (((1,), (1,)), ((), ())), preferred_element_type=F32)


def _mod_kernel(c_ref, w_ref, b_ref, o_ref):
    c = c_ref[...]
    a = (c * jax.nn.sigmoid(c)).astype(BF16)
    o_ref[...] = _dot(a, w_ref[...].astype(BF16)) + b_ref[...]


def _adaln_mod(c16, ada_w, ada_b):
    tn = 1024
    return pl.pallas_call(
        _mod_kernel,
        grid=(DEPTH, 6 * D // tn),
        in_specs=[
            pl.BlockSpec((NG, D), lambda l, j: (0, 0)),
            pl.BlockSpec((None, D, tn), lambda l, j: (l, 0, j)),
            pl.BlockSpec((None, 1, tn), lambda l, j: (l, 0, j)),
        ],
        out_specs=pl.BlockSpec((None, NG, tn), lambda l, j: (l, 0, j)),
        out_shape=jax.ShapeDtypeStruct((DEPTH, NG, 6 * D), F32),
        compiler_params=_cp(("parallel", "parallel")),
        name="adaln_mod",
    )(c16, ada_w, ada_b.reshape(DEPTH, 1, 6 * D))


def _src_specs(x, tm):
    n_ctx = T_CTX // tm
    n_all = T // tm
    if isinstance(x, tuple):
        return list(x), [
            pl.BlockSpec((tm, D), lambda i: (jnp.minimum(i, n_ctx - 1), 0)),
            pl.BlockSpec((tm, D), lambda i: (jnp.clip(i - n_ctx, 0, n_all - n_ctx - 1), 0)),
        ]
    return [x], [pl.BlockSpec((tm, D), lambda i: (jnp.minimum(i, n_all - 1), 0))]


def _load_src(x_refs, tm, cs=slice(None)):
    if len(x_refs) == 1:
        return x_refs[0][:, cs]
    return jnp.where(pl.program_id(0) < T_CTX // tm, x_refs[0][:, cs], x_refs[1][:, cs])


def _resident(shape, layer):
    nd = len(shape)
    return pl.BlockSpec((None,) + shape, lambda i: (layer,) + (0,) * nd, pipeline_mode=pl.Buffered(1))


def _mod_row(k, per):
    return pl.BlockSpec((None, 1, D), lambda i: (6 * (i // per) + k, 0, 0))


N_U = POOL_W + 3 * HY_W
N_QR = H * ROPE


def _inproj_kernel(*refs, tm):
    (*x_refs, g_ref, shift_ref, scale_ref, wm_ref, wx_ref, qg_ref, wq_ref, csq_ref, kvg_ref, csk_ref, wkv_ref,
     u_ref, q_ref, ckv_ref, kr_ref, krr_ref, kv_ref) = refs

    y = _rms(_load_src(x_refs, tm), g_ref[...])
    h = (y * (1.0 + scale_ref[...]) + shift_ref[...]).astype(BF16)
    for c in range(N_U // NCH):
        cs = slice(c * NCH, (c + 1) * NCH)
        u_ref[:, cs] = _dot(h, wm_ref[:, cs]).astype(u_ref.dtype)

    qn = _rms(_dot(h, wm_ref[:, N_U:]), qg_ref[...]).astype(BF16)
    nq = H * NOPE
    for c in range(nq // NCH):
        cs = slice(c * NCH, (c + 1) * NCH)
        q_ref[:, cs] = (_dot(qn, wq_ref[:, cs]) * ATT_SCALE).astype(q_ref.dtype)
    rot = _dot(qn, wq_ref[:, nq:nq + N_QR]) * csq_ref[0] + _dot(qn, wq_ref[:, nq + N_QR:]) * csq_ref[1]
    q_ref[:, nq:] = (rot * ATT_SCALE).astype(q_ref.dtype)

    kx = _dot(h, wx_ref[...])
    ckv = _rms(kx[:, :KV_RANK], kvg_ref[...])
    kr = kx[:, KV_RANK:KV_RANK + LANES]

    @pl.when(pl.program_id(0) < T_CTX // tm)
    def _():
        ckv_ref[...] = ckv
        kr_ref[...] = kr

    krr_ref[...] = (kr * csk_ref[0] + kx[:, KV_RANK + LANES:] * csk_ref[1]).astype(krr_ref.dtype)
    kv_ref[...] = _dot(ckv.astype(BF16), wkv_ref[...]).astype(kv_ref.dtype)


def _inproj(x, mod3, wts, consts, layer):
    tm = 512
    per = GRP // tm
    xs, x_specs = _src_specs(x, tm)

    def kind(i):
        return jnp.where(i >= T_CTX // tm, 1, 0)

    def rows(width):
        return pl.BlockSpec((tm, width), lambda i: (i, 0))

    def ctx_rows(width):
        return pl.BlockSpec((tm, width), lambda i: (jnp.minimum(i, T_CTX // tm - 1), 0))

    nkv = H * (NOPE + VD)
    return pl.pallas_call(
        functools.partial(_inproj_kernel, tm=tm),
        grid=(T // tm,),
        in_specs=x_specs + [
            _resident((1, D), layer), _mod_row(0, per), _mod_row(1, per),
            _resident((D, N_U + Q_RANK), layer),
            _resident((D, 4 * LANES), layer),
            _resident((1, Q_RANK), layer),
            _resident((Q_RANK, H * (NOPE + 2 * ROPE)), layer),
            pl.BlockSpec((None, 2, tm, N_QR), lambda i: (kind(i), 0, i % per, 0)),
            _resident((1, KV_RANK), layer),
            pl.BlockSpec((None, 2, tm, LANES), lambda i: (kind(i), 0, i % per, 0)),
            _resident((KV_RANK, nkv), layer),
        ],
        out_specs=[rows(N_U), rows(H * (NOPE + ROPE)), ctx_rows(KV_RANK), ctx_rows(LANES), rows(LANES), rows(nkv)],
        out_shape=[
            jax.ShapeDtypeStruct((T, N_U), BF16),
            jax.ShapeDtypeStruct((T, H * (NOPE + ROPE)), BF16),
            jax.ShapeDtypeStruct((T_CTX, KV_RANK), F32),
            jax.ShapeDtypeStruct((T_CTX, LANES), F32),
            jax.ShapeDtypeStruct((T, LANES), BF16),
            jax.ShapeDtypeStruct((T, nkv), BF16),
        ],
        compiler_params=_cp(("arbitrary",), 58),
        name="in_proj",
    )(*xs, wts["g1"], mod3, mod3, wts["w_main"], wts["w_x"], wts["qg"], wts["wq"], consts["cs_q"],
      wts["kvg"], consts["cs_k"], wts["wkv"])


def _pool_kernel(u_ref, ic_ref, is_ref, pw_ref, ps_ref, o_ref):
    g = pl.program_id(0)

    def seq(r0, n, inv_ref):
        row = lax.broadcasted_iota(jnp.int32, (n, POOL_GC), 0)

        def later(x, m):
            return jnp.where(row < n - m, pltpu.roll(x, n - m, 0), 0.0)

        def earlier(x, m):
            return jnp.where(row >= m, pltpu.roll(x, m, 0), 0.0)

        for k, w in enumerate(POOL_WINDOWS):
            cs = slice(k * POOL_GC, (k + 1) * POOL_GC)
            u = u_ref[r0:r0 + n, cs].astype(F32)
            ahead, behind, m = u, earlier(u, 1), 1
            while m < w // 2:
                ahead = ahead + later(ahead, m)
                behind = behind + earlier(behind, m)
                m *= 2
            pooled = (ahead + behind) * inv_ref[:, cs] - u
            y = _dot(pooled.astype(BF16), pw_ref[k]) * ps_ref[:, cs]
            o_ref[r0:r0 + n, cs] = y.astype(o_ref.dtype)

    @pl.when(g < NG_CTX)
    def _():
        for s in range(SEQ_PER_GRP):
            seq(s * L_CTX, L_CTX, ic_ref)

    @pl.when(g >= NG_CTX)
    def _():
        seq(0, L_S, is_ref)


def _pool(proj, consts, pw, ps):
    nw = len(POOL_WINDOWS)
    return pl.pallas_call(
        _pool_kernel,
        grid=(NG,),
        in_specs=[
            pl.BlockSpec((GRP, POOL_W), lambda g: (g, 0)),
            pl.BlockSpec((L_CTX, POOL_W), lambda g: (0, 0)),
            pl.BlockSpec((L_S, POOL_W), lambda g: (0, 0)),
            pl.BlockSpec((nw, POOL_GC, POOL_GC), lambda g: (0, 0, 0)),
            pl.BlockSpec((1, POOL_W), lambda g: (0, 0)),
        ],
        out_specs=pl.BlockSpec((GRP, POOL_W), lambda g: (g, 0)),
        out_shape=jax.ShapeDtypeStruct((T, POOL_W), BF16),
        compiler_params=_cp(("parallel",)),
        name="pool_mix",
    )(proj, consts["inv_c"], consts["inv_s"], pw, ps)


HY_CH = HY_W


def _hyena_kernel(v_ref, x1_ref, x2_ref, swv_ref, sw1_ref, sw2_ref, sbv_ref, sb1_ref, sb2_ref,
                  skip_ref, wc_ref, wtc_ref, ws_ref, wts_ref, ktc_ref, kts_ref, o_ref):
    g = pl.program_id(0)

    def sconv(u_ref, sw_ref, sb_ref, r0, n, cs):
        u = u_ref[r0:r0 + n, cs].astype(F32)
        row = lax.broadcasted_iota(jnp.int32, u.shape, 0)
        prev = jnp.where(row == 0, 0.0, pltpu.roll(u, 1, 0))
        nxt = jnp.where(row == n - 1, 0.0, pltpu.roll(u, n - 1, 0))
        return prev * sw_ref[0:1, cs] + u * sw_ref[1:2, cs] + nxt * sw_ref[2:3, cs] + sb_ref[:, cs]

    def lconv(u, o, n, cs, w_ref, wt_ref, kt_ref):
        spec = _dot(w_ref[...], u.astype(BF16))
        pr, pi = spec[:n], spec[n:]
        ka, kb, ka2 = kt_ref[o, 0, :, cs], kt_ref[o, 1, :, cs], kt_ref[o, 2, :, cs]
        yr = pr * ka - pi * kb
        yi = pr * kb + pi * ka2
        prod = jnp.concatenate([yr, yi], axis=0).astype(BF16)
        return _dot(wt_ref[...], prod) + u * skip_ref[o:o + 1, cs]

    def seq(r0, n, cs, w_ref, wt_ref, kt_ref):
        v = sconv(v_ref, swv_ref, sbv_ref, r0, n, cs)
        x1 = sconv(x1_ref, sw1_ref, sb1_ref, r0, n, cs)
        x2 = sconv(x2_ref, sw2_ref, sb2_ref, r0, n, cs)
        z = x1 * lconv(v, 0, n, cs, w_ref, wt_ref, kt_ref)
        y = x2 * lconv(z, 1, n, cs, w_ref, wt_ref, kt_ref)
        o_ref[r0:r0 + n, cs] = y.astype(o_ref.dtype)

    chunks = [slice(c * HY_CH, (c + 1) * HY_CH) for c in range(HY_W // HY_CH)]

    @pl.when(g < NG_CTX)
    def _():
        for s in range(SEQ_PER_GRP):
            for cs in chunks:
                seq(s * L_CTX, L_CTX, cs, wc_ref, wtc_ref, ktc_ref)

    @pl.when(g >= NG_CTX)
    def _():
        for cs in chunks:
            seq(0, L_S, cs, ws_ref, wts_ref, kts_ref)


def _hyena(u, sw, sb, skip, consts, kt_c, kt_s, layer):
    c0 = POOL_W // HY_W

    def part(p):
        return pl.BlockSpec((GRP, HY_W), lambda g: (g, c0 + p))

    def swpart(p):
        return pl.BlockSpec((3, HY_W), lambda g: (0, p))

    def sbpart(p):
        return pl.BlockSpec((1, HY_W), lambda g: (0, p))

    def tables(n):
        return pl.BlockSpec((None, 2, 3, n, HY_W), lambda g: (layer, 0, 0, 0, 0), pipeline_mode=pl.Buffered(1))

    return pl.pallas_call(
        _hyena_kernel,
        grid=(NG,),
        in_specs=[
            part(0), part(1), part(2),
            swpart(0), swpart(1), swpart(2),
            sbpart(0), sbpart(1), sbpart(2),
            pl.BlockSpec((2, HY_W), lambda g: (0, 0)),
            pl.BlockSpec((2 * L_CTX, L_CTX), lambda g: (0, 0)),
            pl.BlockSpec((L_CTX, 2 * L_CTX), lambda g: (0, 0)),
            pl.BlockSpec((2 * L_S, L_S), lambda g: (0, 0)),
            pl.BlockSpec((L_S, 2 * L_S), lambda g: (0, 0)),
            tables(L_CTX), tables(L_S),
        ],
        out_specs=pl.BlockSpec((GRP, HY_W), lambda g: (g, 0)),
        out_shape=jax.ShapeDtypeStruct((T, HY_W), BF16),
        compiler_params=_cp(("parallel",), 56),
        name="hyena_mix",
    )(u, u, u, sw, sw, sw, sb, sb, sb, skip,
      consts["dft_c"], consts["dftt_c"], consts["dft_s"], consts["dftt_s"], kt_c, kt_s)


def _cachekv_kernel(x_ref, w_ref, o_ref):
    o_ref[...] = _dot(x_ref[...].astype(BF16), w_ref[...]).astype(o_ref.dtype)


def _cachekv(cache_ckv, layer, wkv):
    return pl.pallas_call(
        _cachekv_kernel,
        grid=(B_S,),
        in_specs=[
            pl.BlockSpec((None, None, L_CTX, KV_RANK), lambda b: (b, layer, 0, 0)),
            pl.BlockSpec((None, KV_RANK, H * (NOPE + VD)), lambda b: (layer, 0, 0)),
        ],
        out_specs=pl.BlockSpec((L_CTX, H * (NOPE + VD)), lambda b: (b, 0)),
        out_shape=jax.ShapeDtypeStruct((B_S * L_CTX, H * (NOPE + VD)), BF16),
        compiler_params=_cp(("parallel",)),
        name="cache_kv",
    )(cache_ckv, wkv)


ATT_TQ = 512
NK_S = L_S + L_CTX


def _attn_kernel(q_ref, kv_ref, krr_ref, kvc_ref, krc_ref, o_ref, kcat_ref):
    g = pl.program_id(0)
    lane = lax.broadcasted_iota(jnp.int32, (1, LANES), 1)
    hk = NOPE + LANES
    vo = H * NOPE

    krr = krr_ref[...]
    for h in range(H):
        kcat_ref[0:GRP, h * hk:h * hk + NOPE] = kv_ref[:, h * NOPE:(h + 1) * NOPE]
        kcat_ref[0:GRP, h * hk + NOPE:(h + 1) * hk] = krr

    def qcat(rows, h):
        qn = q_ref[rows, h * NOPE:(h + 1) * NOPE]
        pair = q_ref[rows, vo + (h // 2) * LANES:vo + (h // 2 + 1) * LANES].astype(F32)
        keep = (lane < ROPE) if h % 2 == 0 else (lane >= ROPE)
        return jnp.concatenate([qn, jnp.where(keep, pair, 0.0).astype(BF16)], axis=1)

    def probs(sc):
        m = jnp.max(sc, axis=-1, keepdims=True)
        p = jnp.exp(sc - m)
        return p.astype(BF16), 1.0 / jnp.sum(p, axis=-1, keepdims=True)

    @pl.when(g < NG_CTX)
    def _():
        def body(s, carry):
            rows = pl.ds(pl.multiple_of(s * L_CTX, L_CTX), L_CTX)
            for h in range(H):
                p, rl = probs(_dot_nt(qcat(rows, h), kcat_ref[rows, h * hk:(h + 1) * hk]))
                o = _dot(p, kv_ref[rows, vo + h * VD:vo + (h + 1) * VD]) * rl
                o_ref[rows, h * VD:(h + 1) * VD] = o.astype(o_ref.dtype)
            return carry

        lax.fori_loop(0, SEQ_PER_GRP, body, 0)

    @pl.when(g >= NG_CTX)
    def _():
        krc = krc_ref[...]
        for h in range(H):
            kcat_ref[GRP:NK_S, h * hk:h * hk + NOPE] = kvc_ref[:, h * NOPE:(h + 1) * NOPE]
            kcat_ref[GRP:NK_S, h * hk + NOPE:(h + 1) * hk] = krc

        def body(t, carry):
            rows = pl.ds(pl.multiple_of(t * ATT_TQ, ATT_TQ), ATT_TQ)
            for h in range(H):
                p, rl = probs(_dot_nt(qcat(rows, h), kcat_ref[:, h * hk:(h + 1) * hk]))
                o = _dot(p[:, :GRP], kv_ref[:, vo + h * VD:vo + (h + 1) * VD])
                o = o + _dot(p[:, GRP:], kvc_ref[:, vo + h * VD:vo + (h + 1) * VD])
                o_ref[rows, h * VD:(h + 1) * VD] = (o * rl).astype(o_ref.dtype)
            return carry

        lax.fori_loop(0, L_S // ATT_TQ, body, 0)


def _attention(q, kv, krr, kvc, krc):
    def cache_blk(g):
        return jnp.maximum(g - NG_CTX, 0)

    return pl.pallas_call(
        _attn_kernel,
        grid=(NG,),
        in_specs=[
            pl.BlockSpec((GRP, H * (NOPE + ROPE)), lambda g: (g, 0)),
            pl.BlockSpec((GRP, H * (NOPE + VD)), lambda g: (g, 0)),
            pl.BlockSpec((GRP, LANES), lambda g: (g, 0)),
            pl.BlockSpec((L_CTX, H * (NOPE + VD)), lambda g: (cache_blk(g), 0)),
            pl.BlockSpec((L_CTX, LANES), lambda g: (cache_blk(g), 0)),
        ],
        out_specs=pl.BlockSpec((GRP, H * VD), lambda g: (g, 0)),
        out_shape=jax.ShapeDtypeStruct((T, H * VD), BF16),
        scratch_shapes=[pltpu.VMEM((NK_S, H * (NOPE + LANES)), BF16)],
        compiler_params=_cp(("parallel",)),
        name="mla_attention",
    )(q, kv, krr, kvc, krc)


def _wout_kernel(yp_ref, yh_ref, ym_ref, w_ref, gate_ref, g2_ref, shift_ref, scale_ref, rw_ref, *refs, tm):
    *x_refs, o_ref, h_ref, aff_ref, prev_ref = refs

    @pl.when(pl.program_id(0) == 0)
    def _():
        prev_ref[...] = jnp.zeros_like(prev_ref)

    hb = (_rms(prev_ref[...], g2_ref[...]) * (1.0 + scale_ref[...]) + shift_ref[...]).astype(BF16)
    h_ref[...] = hb
    logits = _dot(hb, rw_ref[...])
    lane = lax.broadcasted_iota(jnp.int32, logits.shape, 1)
    logits = jnp.where(lane < E, logits, -jnp.inf)
    ex = jnp.exp(logits - jnp.max(logits, axis=-1, keepdims=True))
    aff_ref[...] = ex / jnp.sum(ex, axis=-1, keepdims=True)

    y = jnp.concatenate([yp_ref[...], yh_ref[...], ym_ref[...]], axis=1)
    for c in range(D // NCH):
        cs = slice(c * NCH, (c + 1) * NCH)
        v = _load_src(x_refs, tm, cs) + gate_ref[:, cs] * _dot(y, w_ref[:, cs])
        o_ref[:, cs] = v.astype(o_ref.dtype)
        prev_ref[:, cs] = v


def _wout(yp, yh, ym, x, mod3, wts, layer):
    tm = 512
    per = GRP // tm
    n = T // tm
    xs, x_specs = _src_specs(x, tm)

    def cur(i):
        return jnp.minimum(i, n - 1)

    def prev(i):
        return jnp.maximum(i - 1, 0)

    def rows(width, blk):
        return pl.BlockSpec((tm, width), lambda i: (blk(i), 0))

    def mod_row(k, blk):
        return pl.BlockSpec((None, 1, D), lambda i: (6 * (blk(i) // per) + k, 0, 0))

    return pl.pallas_call(
        functools.partial(_wout_kernel, tm=tm),
        grid=(n + 1,),
        in_specs=[
            rows(POOL_W, cur), rows(HY_W, cur), rows(H * VD, cur),
            _resident((D, D), layer),
            mod_row(2, cur),
            _resident((1, D), layer), mod_row(3, prev), mod_row(4, prev),
            _resident((D, LANES), layer),
        ] + x_specs,
        out_specs=[rows(D, cur), rows(D, prev), rows(LANES, prev)],
        out_shape=[
            jax.ShapeDtypeStruct((T, D), BF16),
            jax.ShapeDtypeStruct((T, D), BF16),
            jax.ShapeDtypeStruct((T, LANES), F32),
        ],
        scratch_shapes=[pltpu.VMEM((tm, D), F32)],
        compiler_params=_cp(("arbitrary",), 56),
        name="out_proj",
    )(yp, yh, ym, wts["w_out"], mod3, wts["g2"], mod3, mod3, wts["rw"], *xs)


RANK_CH = 256
SEARCH_ROUNDS = 14
SEARCH_WAYS = 8
AFF_MAX = 2.0


def _rank_kernel(aff_ref, afft_ref, tri_ref, rank_ref, cnt_ref, cut_ref):
    g = pl.program_id(0)

    def count_ge(a, t):
        return jnp.sum(jnp.where(a >= t, 1.0, 0.0), axis=1, keepdims=True)

    def search(specs, capf):
        acts = [afft_ref[:, r0:r0 + n] for r0, n in specs]
        lo = [jnp.zeros((E, 1), F32) for _ in specs]
        hi = [jnp.full((E, 1), AFF_MAX, F32) for _ in specs]
        for _ in range(SEARCH_ROUNDS):
            for s, a in enumerate(acts):
                step = (hi[s] - lo[s]) * (1.0 / SEARCH_WAYS)
                ts = [lo[s] + step * k for k in range(1, SEARCH_WAYS)]
                ok = [count_ge(a, t) >= capf for t in ts]
                new_lo, new_hi = lo[s], hi[s]
                for t, o in zip(ts, ok):
                    new_lo = jnp.where(o, t, new_lo)
                for t, o in zip(reversed(ts), reversed(ok)):
                    new_hi = jnp.where(o, new_hi, t)
                lo[s], hi[s] = new_lo, new_hi
        open_brackets = jnp.zeros((E, 1), F32)
        for s, a in enumerate(acts):
            top = jnp.max(jnp.where(a >= lo[s], jnp.where(a < hi[s], a, -1.0), -1.0), axis=1, keepdims=True)
            low = jnp.min(jnp.where(a >= lo[s], jnp.where(a < hi[s], a, AFF_MAX), AFF_MAX), axis=1, keepdims=True)
            cut_ref[:, s:s + 1] = top
            open_brackets = open_brackets + jnp.where(top != low, 1.0, 0.0)
        return jnp.sum(open_brackets)

    def exact_cut(s, r0, n, capf):
        for e in range(E):
            row = afft_ref[e:e + 1, r0:r0 + n]
            acc = jnp.zeros((1, n), F32)
            for c in range(n // RANK_CH):
                col = aff_ref[r0 + c * RANK_CH:r0 + (c + 1) * RANK_CH, e:e + 1]
                acc = acc + jnp.sum(jnp.where(col >= row, 1.0, 0.0), axis=0, keepdims=True)
            cnt_ref[e:e + 1, 0:n] = acc
        a = afft_ref[:, r0:r0 + n]
        cut_ref[:, s:s + 1] = jnp.max(jnp.where(cnt_ref[:, 0:n] >= capf, a, -1.0), axis=1, keepdims=True)

    def slots(s, r0, n, capf):
        a = afft_ref[:, r0:r0 + n]
        cut = cut_ref[:, s:s + 1]
        above = a > cut
        tied = a == cut
        above_f = jnp.where(above, 1.0, 0.0)
        n_above = jnp.sum(above_f, axis=1, keepdims=True)
        marks = jnp.concatenate([above_f, jnp.where(tied, 1.0, 0.0)], axis=0).astype(BF16)
        before = _dot(marks, tri_ref[0:n, 0:n])
        tie_slot = n_above + before[E:]
        slot = jnp.where(above, before[:E], jnp.where(tied, jnp.where(tie_slot < capf, tie_slot, n), n))
        rank_ref[:, r0:r0 + n] = slot.astype(jnp.int32)

    def group(specs, cap):
        capf = float(cap)
        unresolved = search(specs, capf)

        @pl.when(unresolved > 0.0)
        def _():
            for s, (r0, n) in enumerate(specs):
                exact_cut(s, r0, n, capf)

        for s, (r0, n) in enumerate(specs):
            slots(s, r0, n, capf)

    @pl.when(g < NG_CTX)
    def _():
        group([(s * L_CTX, L_CTX) for s in range(SEQ_PER_GRP)], CAP_CTX)

    @pl.when(g >= NG_CTX)
    def _():
        group([(0, L_S)], CAP_S)


def _rank(aff, afft, tri):
    return pl.pallas_call(
        _rank_kernel,
        grid=(NG,),
        in_specs=[
            pl.BlockSpec((GRP, LANES), lambda g: (g, 0)),
            pl.BlockSpec((E, GRP), lambda g: (0, g)),
            pl.BlockSpec((L_S, L_S), lambda g: (0, 0)),
        ],
        out_specs=pl.BlockSpec((E, GRP), lambda g: (0, g)),
        out_shape=jax.ShapeDtypeStruct((E, T), jnp.int32),
        scratch_shapes=[pltpu.VMEM((E, GRP), F32), pltpu.VMEM((E, LANES), F32)],
        compiler_params=_cp(("parallel",)),
        name="moe_rank",
    )(aff, afft, tri)


def _gather_kernel(rank_ref, afft_ref, h_ref, xs_ref, gs_ref, sel_ref):
    g = pl.program_id(0)

    def seq(r0, n, cap, slot0):
        slot_i = lax.broadcasted_iota(jnp.int32, (cap, n), 0)
        for e in range(E):
            hit = slot_i == rank_ref[e:e + 1, r0:r0 + n]
            sel_ref[e * cap:(e + 1) * cap, 0:n] = jnp.where(hit, 1.0, 0.0).astype(BF16)
            gs_ref[e, slot0:slot0 + cap, :] = jnp.sum(
                jnp.where(hit, afft_ref[e:e + 1, r0:r0 + n], 0.0), axis=1, keepdims=True)
        for c in range(D // NCH):
            cs = slice(c * NCH, (c + 1) * NCH)
            res = _dot(sel_ref[0:E * cap, 0:n], h_ref[r0:r0 + n, cs])
            for e in range(E):
                xs_ref[e, slot0:slot0 + cap, cs] = res[e * cap:(e + 1) * cap].astype(xs_ref.dtype)

    @pl.when(g < NG_CTX)
    def _():
        for s in range(SEQ_PER_GRP):
            seq(s * L_CTX, L_CTX, CAP_CTX, s * CAP_CTX)

    @pl.when(g >= NG_CTX)
    def _():
        seq(0, L_S, CAP_S, 0)


def _gather(rank, afft, h2):
    return pl.pallas_call(
        _gather_kernel,
        grid=(NG,),
        in_specs=[
            pl.BlockSpec((E, GRP), lambda g: (0, g)),
            pl.BlockSpec((E, GRP), lambda g: (0, g)),
            pl.BlockSpec((GRP, D), lambda g: (g, 0)),
        ],
        out_specs=[
            pl.BlockSpec((E, SLOTS, D), lambda g: (0, g, 0)),
            pl.BlockSpec((E, SLOTS, 1), lambda g: (0, g, 0)),
        ],
        out_shape=[
            jax.ShapeDtypeStruct((E, ROWS_E, D), BF16),
            jax.ShapeDtypeStruct((E, ROWS_E, 1), F32),
        ],
        scratch_shapes=[pltpu.VMEM((E * CAP_S, L_S), BF16)],
        compiler_params=_cp(("parallel",)),
        name="moe_gather",
    )(rank, afft, h2)


FFN_TF = 256
FFN_TN = 512
FFN_UP = FF // FFN_TF
FFN_DOWN = D // FFN_TN


def _ffn_kernel(x_ref, wg_ref, wu_ref, wd_ref, gs_ref, o_ref, hid_ref):
    s = pl.program_id(1)

    @pl.when(s < FFN_UP)
    def _():
        x = x_ref[...]
        a = _dot(x, wg_ref[...].astype(BF16))
        b = _dot(x, wu_ref[...].astype(BF16))
        hid_ref[s] = (a * jax.nn.sigmoid(a) * b).astype(BF16)

    @pl.when(s >= FFN_UP)
    def _():
        wd = wd_ref[...].astype(BF16)
        acc = _dot(hid_ref[0], wd[0:FFN_TF])
        for f in range(1, FFN_UP):
            acc = acc + _dot(hid_ref[f], wd[f * FFN_TF:(f + 1) * FFN_TF])
        o_ref[...] = (acc * gs_ref[...]).astype(o_ref.dtype)


def _ffn(xs, gs, w_gate, w_up, w_down, layer):
    def up_blk(e, s):
        return (layer, e, 0, jnp.minimum(s, FFN_UP - 1))

    def down_blk(s):
        return jnp.maximum(s - FFN_UP, 0)

    return pl.pallas_call(
        _ffn_kernel,
        grid=(E, FFN_UP + FFN_DOWN),
        in_specs=[
            pl.BlockSpec((None, ROWS_E, D), lambda e, s: (e, 0, 0)),
            pl.BlockSpec((None, None, D, FFN_TF), up_blk),
            pl.BlockSpec((None, None, D, FFN_TF), up_blk),
            pl.BlockSpec((None, None, FF, FFN_TN), lambda e, s: (layer, e, 0, down_blk(s))),
            pl.BlockSpec((None, ROWS_E, 1), lambda e, s: (e, 0, 0)),
        ],
        out_specs=pl.BlockSpec((None, ROWS_E, FFN_TN), lambda e, s: (e, 0, down_blk(s))),
        out_shape=jax.ShapeDtypeStruct((E, ROWS_E, D), BF16),
        scratch_shapes=[pltpu.VMEM((FFN_UP, ROWS_E, FFN_TF), BF16)],
        compiler_params=_cp(("arbitrary", "arbitrary"), 56),
        name="moe_ffn",
    )(xs, w_gate, w_up, w_down, gs)


CMB_TM = 512


def _combine_kernel(rt_ref, ys_ref, x_ref, gate_ref, exc_ref, exs_ref, *refs, last):
    if last:
        fg_ref, oc_ref, os_ref, st_ref = refs
    else:
        oc_ref, st_ref = refs
        os_ref = oc_ref
    g = pl.program_id(0)
    half = pl.program_id(1)

    def scatter(r0, n, cap, slot0, ex_ref, o_ref):
        ec = E * cap
        r = jnp.minimum(rt_ref[r0:r0 + n, :], cap).astype(F32).astype(BF16)
        want = (lax.broadcasted_iota(jnp.int32, (1, NCH), 1) & (cap - 1)).astype(F32)
        for c in range(ec // NCH):
            cs = slice(c * NCH, (c + 1) * NCH)
            st_ref[r0:r0 + n, cs] = jnp.where(_dot(r, ex_ref[:, cs]) == want, 1.0, 0.0).astype(BF16)
        for c in range(D // NCH):
            cs = slice(c * NCH, (c + 1) * NCH)
            ys = ys_ref[:, pl.ds(slot0, cap), cs].reshape(ec, NCH)
            moe = _dot(st_ref[r0:r0 + n, 0:ec], ys)
            o_ref[r0:r0 + n, cs] = x_ref[r0:r0 + n, cs].astype(F32) + gate_ref[:, cs] * moe

    def finish(o_ref):
        if last:
            o_ref[...] = _rms(o_ref[...], fg_ref[...])

    @pl.when(g < NG_CTX)
    def _():
        seqs = CMB_TM // L_CTX
        for s in range(seqs):
            slot0 = pl.multiple_of((half * seqs + s) * CAP_CTX, CAP_CTX)
            scatter(s * L_CTX, L_CTX, CAP_CTX, slot0, exc_ref, oc_ref)
        finish(oc_ref)

    @pl.when(g >= NG_CTX)
    def _():
        scatter(0, CMB_TM, CAP_S, 0, exs_ref, os_ref)
        finish(os_ref)


def _combine(rank_t, ys, x, mod3, consts, final_g=None):
    last = final_g is not None
    per = GRP // CMB_TM

    def row(g, i):
        return (g * per + i, 0)

    in_specs = [
        pl.BlockSpec((CMB_TM, LANES), row),
        pl.BlockSpec((E, SLOTS, D), lambda g, i: (0, g, 0)),
        pl.BlockSpec((CMB_TM, D), row),
        pl.BlockSpec((None, 1, D), lambda g, i: (6 * g + 5, 0, 0)),
        pl.BlockSpec((LANES, E * CAP_CTX), lambda g, i: (0, 0)),
        pl.BlockSpec((LANES, E * CAP_S), lambda g, i: (0, 0)),
    ]
    args = [rank_t, ys, x, mod3, consts["ex_c"], consts["ex_s"]]
    if last:
        n_ctx = T_CTX // CMB_TM
        in_specs.append(pl.BlockSpec((1, D), lambda g, i: (0, 0)))
        args.append(final_g)
        out_specs = [
            pl.BlockSpec((CMB_TM, D), lambda g, i: (jnp.where(g < NG_CTX, g * per + i, n_ctx - 1), 0)),
            pl.BlockSpec((CMB_TM, D), lambda g, i: (jnp.where(g < NG_CTX, 0, (g - NG_CTX) * per + i), 0)),
        ]
        out_shape = [jax.ShapeDtypeStruct((T_CTX, D), F32), jax.ShapeDtypeStruct((T - T_CTX, D), F32)]
    else:
        out_specs = pl.BlockSpec((CMB_TM, D), row)
        out_shape = jax.ShapeDtypeStruct((T, D), F32)
    return pl.pallas_call(
        functools.partial(_combine_kernel, last=last),
        grid=(NG, per),
        in_specs=in_specs,
        out_specs=out_specs,
        out_shape=out_shape,
        scratch_shapes=[pltpu.VMEM((CMB_TM, E * CAP_S), BF16)],
        compiler_params=_cp(("arbitrary", "arbitrary"), 56),
        name="moe_combine",
    )(*args)


def _np_constants():
    c = {}
    for tag, n in (("c", L_CTX), ("s", L_S)):
        t = np.arange(n)
        inv = np.zeros((n, POOL_W), np.float32)
        for k, w in enumerate(POOL_WINDOWS):
            lo = np.clip(t - w // 2, 0, n)
            hi = np.clip(t - w // 2 + w, 0, n)
            inv[:, k * POOL_GC:(k + 1) * POOL_GC] = (1.0 / (hi - lo).astype(np.float64))[:, None]
        c["inv_" + tag] = inv
        kk = np.arange(n, dtype=np.float64)[:, None]
        tt = np.arange(n, dtype=np.float64)[None, :]
        ang = np.pi * kk * tt / n
        dft = np.concatenate([np.cos(ang), -np.sin(ang)], axis=0)
        dft[n] = (-1.0) ** np.arange(n)
        c["dft_" + tag] = dft.astype(np.float32)
        c["dftt_" + tag] = np.ascontiguousarray(dft.T).astype(np.float32)
        tl = np.linspace(0.0, 1.0, n, dtype=np.float32)[:, None]
        bands = np.arange(1, POS_BANDS + 1, dtype=np.float32)[None, :]
        feats = np.concatenate([tl, np.sin(2 * np.pi * tl * bands), np.cos(2 * np.pi * tl * bands)], axis=1)
        c["feat_" + tag] = np.pad(feats.astype(np.float32), ((0, 0), (0, LANES - feats.shape[1])))
        c["t_" + tag] = tl
    n_rows = L_S // GRID_W
    row = np.repeat(np.arange(n_rows), GRID_W).astype(np.float32)
    col = np.tile(np.arange(GRID_W), n_rows).astype(np.float32)
    nf = ROPE // 4
    inv_f = (1.0 / ROPE_THETA ** (np.arange(nf, dtype=np.float32) / nf)).astype(np.float32)
    a_row = (row[:, None] * inv_f[None]).astype(np.float32).astype(np.float64)
    a_col = (col[:, None] * inv_f[None]).astype(np.float32).astype(np.float64)
    cos64 = np.concatenate([np.cos(a_row), np.cos(a_row), np.cos(a_col), np.cos(a_col)], axis=1)
    sin64 = np.concatenate([np.sin(a_row), np.sin(a_row), np.sin(a_col), np.sin(a_col)], axis=1)

    def table(reps):
        ident = np.stack([np.ones((L_S, ROPE * reps)), np.zeros((L_S, ROPE * reps))])
        rot = np.stack([np.tile(cos64, (1, reps)), np.tile(sin64, (1, reps))])
        return np.stack([ident, rot]).astype(np.float32)

    c["cs_q"] = table(H)
    c["cs_k"] = table(LANES // ROPE)
    c["tri"] = np.triu(np.ones((L_S, L_S), np.float32), k=1)
    for tag, cap in (("c", CAP_CTX), ("s", CAP_S)):
        ex = np.zeros((LANES, E * cap), np.float32)
        ex[np.arange(E * cap) // cap, np.arange(E * cap)] = 1.0
        c["ex_" + tag] = ex
    return c


def _constants():
    c = {k: jnp.asarray(v) for k, v in _np_constants().items()}
    for k in ("dft_c", "dftt_c", "dft_s", "dftt_s", "ex_c", "ex_s", "tri"):
        c[k] = c[k].astype(BF16)
    return c


def _rope_swap(w):
    q = ROPE // 4
    return jnp.concatenate([-w[..., q:2 * q], w[..., :q], -w[..., 3 * q:], w[..., 2 * q:3 * q]], axis=-1)


def _dot_hi(a, b):
    return jnp.dot(a, b, preferred_element_type=F32, precision=lax.Precision.HIGHEST)


def _filter_kernel(feat_ref, t_ref, w1_ref, b1_ref, w2_ref, b2_ref, fr_ref, w3f_ref, w3b_ref,
                   ldf_ref, ldb_ref, dft_ref, o_ref, z_ref):
    n = feat_ref.shape[0]

    @pl.when((pl.program_id(1) == 0) & (pl.program_id(2) == 0))
    def _():
        fr = fr_ref[...]
        z = jnp.sin(fr * (_dot_hi(feat_ref[...], w1_ref[...]) + b1_ref[...]))
        z_ref[...] = jnp.sin(fr * (_dot_hi(z, w2_ref[...]) + b2_ref[...]))

    z = z_ref[...]
    t = t_ref[...]
    first = lax.broadcasted_iota(jnp.int32, (n, 1), 0) == 0
    hf = _dot_hi(z, w3f_ref[...]) * jnp.exp(-jnp.exp(ldf_ref[...]) * t)
    hb = _dot_hi(z, w3b_ref[...]) * jnp.exp(-jnp.exp(ldb_ref[...]) * t)
    hb = jnp.where(first, 0.0, hb)
    norm = jnp.sum(jnp.abs(hf), axis=0, keepdims=True) + jnp.sum(jnp.abs(hb), axis=0, keepdims=True) + EPS
    both = _dot(dft_ref[...], jnp.concatenate([hf / norm, hb / norm], axis=1).astype(BF16))
    pf, pb = both[:, :hf.shape[1]], both[:, hf.shape[1]:]
    sc = jnp.where(first, 0.5 / n, 1.0 / n)
    ka = (pf[:n] + pb[:n]) * sc
    o_ref[0] = ka
    o_ref[1] = jnp.where(first, 0.0, (pf[n:] - pb[n:]) * sc)
    o_ref[2] = jnp.where(first, (pf[n:] + pb[n:]) * sc, ka)


def _hyena_tables(n, feats, tcol, dft, fw):
    tc = 256
    nc = HY_W // tc

    def lay(shape):
        return pl.BlockSpec((None,) + shape, lambda l, o, c: (l, 0, 0))

    def w3(back):
        return pl.BlockSpec((None, LANES, tc), lambda l, o, c: (l, 0, (2 * o + back) * nc + c))

    def ld(back):
        return pl.BlockSpec((None, 1, tc), lambda l, o, c: (l, 0, (2 * o + back) * nc + c))

    return pl.pallas_call(
        _filter_kernel,
        grid=(DEPTH, 2, nc),
        in_specs=[
            pl.BlockSpec((n, LANES), lambda l, o, c: (0, 0)),
            pl.BlockSpec((n, 1), lambda l, o, c: (0, 0)),
            lay((LANES, LANES)), lay((1, LANES)), lay((LANES, LANES)), lay((1, LANES)), lay((1, LANES)),
            w3(0), w3(1), ld(0), ld(1),
            pl.BlockSpec((2 * n, n), lambda l, o, c: (0, 0)),
        ],
        out_specs=pl.BlockSpec((None, None, 3, n, tc), lambda l, o, c: (l, o, 0, 0, c)),
        out_shape=jax.ShapeDtypeStruct((DEPTH, 2, 3, n, HY_W), F32),
        scratch_shapes=[pltpu.VMEM((n, LANES), F32)],
        compiler_params=_cp(("arbitrary", "arbitrary", "arbitrary")),
        name="hyena_filter",
    )(feats, tcol, fw["w1"], fw["b1"], fw["w2"], fw["b2"], fw["freq"], fw["w3"], fw["w3"],
      fw["ld"], fw["ld"], dft)


def _filter_weights(w1, b1, w2, b2, w3, freq, log_decay):
    ph = LANES - FILT_HID
    return dict(
        w1=jnp.pad(w1, ((0, 0), (0, LANES - w1.shape[1]), (0, ph))),
        b1=jnp.pad(b1, ((0, 0), (0, ph))).reshape(DEPTH, 1, LANES),
        w2=jnp.pad(w2, ((0, 0), (0, ph), (0, ph))),
        b2=jnp.pad(b2, ((0, 0), (0, ph))).reshape(DEPTH, 1, LANES),
        freq=jnp.pad(freq, ((0, 0), (0, ph))).reshape(DEPTH, 1, LANES),
        w3=jnp.pad(w3, ((0, 0), (0, ph), (0, 0))),
        ld=log_decay.reshape(DEPTH, 1, 4 * HY_W))


def _prep_weights(w_in, pool_w, pool_scale, hy_short_b, mla_q_norm, mla_kv_norm, mla_w_uq, mla_w_ukv,
                  w_out, router_w, norm1_g, norm2_g):
    n_main = N_U + Q_RANK
    kr_cols = w_in[:, :, IN_COLS - ROPE:]
    kr_swap = _rope_swap(kr_cols)
    w_x = jnp.concatenate([w_in[:, :, n_main:IN_COLS - ROPE], kr_cols, kr_cols, kr_swap, kr_swap], axis=2)
    wq = mla_w_uq.reshape(DEPTH, Q_RANK, H, NOPE + ROPE)
    wq_rope = wq[..., NOPE:]
    wq = jnp.concatenate([wq[..., :NOPE].reshape(DEPTH, Q_RANK, -1), wq_rope.reshape(DEPTH, Q_RANK, -1),
                          _rope_swap(wq_rope).reshape(DEPTH, Q_RANK, -1)], axis=2)
    wkv = mla_w_ukv.reshape(DEPTH, KV_RANK, H, NOPE + VD)
    wkv = jnp.concatenate([wkv[..., :NOPE].reshape(DEPTH, KV_RANK, -1), wkv[..., NOPE:].reshape(DEPTH, KV_RANK, -1)],
                          axis=2)
    return dict(
        w_main=w_in.astype(BF16), w_x=w_x.astype(BF16), wq=wq.astype(BF16), wkv=wkv.astype(BF16),
        w_out=w_out.astype(BF16), rw=jnp.pad(router_w, ((0, 0), (0, 0), (0, LANES - E))).astype(BF16),
        pool_w=pool_w.astype(BF16), pool_scale=pool_scale.reshape(DEPTH, 1, POOL_W),
        sb=hy_short_b.reshape(DEPTH, 1, 3 * HY_W),
        qg=mla_q_norm.reshape(DEPTH, 1, Q_RANK), kvg=mla_kv_norm.reshape(DEPTH, 1, KV_RANK),
        g1=norm1_g.reshape(DEPTH, 1, D), g2=norm2_g.reshape(DEPTH, 1, D))


def _layer(x, layer, mod3, wts, consts, kt_c, kt_s, hy_short_w, hy_skip, cache_ckv, krc,
           exp_w_gate, exp_w_up, exp_w_down, final_g):
    u, q, ckv, kr, krr, kv = _inproj(x, mod3, wts, consts, layer)
    y_pool = _pool(u, consts, wts["pool_w"][layer], wts["pool_scale"][layer])
    y_hy = _hyena(u, hy_short_w[layer], wts["sb"][layer], hy_skip[layer], consts, kt_c, kt_s, layer)
    kvc = _cachekv(cache_ckv, layer, wts["wkv"])
    y_mla = _attention(q, kv, krr, kvc, krc)
    x, h2, aff = _wout(y_pool, y_hy, y_mla, x, mod3, wts, layer)

    afft = jnp.swapaxes(aff[:, :E], 0, 1)
    rank = _rank(aff, afft, consts["tri"])
    xs, gs = _gather(rank, afft, h2)
    ys = _ffn(xs, gs, exp_w_gate, exp_w_up, exp_w_down, layer)
    rank_t = jnp.pad(jnp.swapaxes(rank, 0, 1), ((0, 0), (0, LANES - E)))
    return _combine(rank_t, ys, x, mod3, consts, final_g), ckv, kr


def kernel(x_prompt, x_sample, cache_ckv, cache_krope, c, c_ctx, ada_w, ada_b, norm1_g, norm2_g, w_in, pool_w, pool_scale, hy_short_w, hy_short_b, hy_ffn_w1, hy_ffn_b1, hy_ffn_w2, hy_ffn_b2, hy_ffn_w3, hy_freq, hy_log_decay, hy_skip, mla_q_norm, mla_kv_norm, mla_w_uq, mla_w_ukv, w_out, router_w, exp_w_gate, exp_w_up, exp_w_down, final_norm_g):
    consts = _constants()
    x = (x_prompt.reshape(T_CTX, D), x_sample.reshape(T - T_CTX, D))
    c16 = jnp.concatenate([jnp.broadcast_to(c_ctx[None], (NG_CTX, D)), c], axis=0)
    mod = _adaln_mod(c16, ada_w, ada_b).reshape(DEPTH, NG * 6, 1, D)

    fw = _filter_weights(hy_ffn_w1, hy_ffn_b1, hy_ffn_w2, hy_ffn_b2, hy_ffn_w3, hy_freq, hy_log_decay)
    kt_c = _hyena_tables(L_CTX, consts["feat_c"], consts["t_c"], consts["dft_c"], fw)
    kt_s = _hyena_tables(L_S, consts["feat_s"], consts["t_s"], consts["dft_s"], fw)

    wts = _prep_weights(w_in, pool_w, pool_scale, hy_short_b, mla_q_norm, mla_kv_norm, mla_w_uq, mla_w_ukv,
                        w_out, router_w, norm1_g, norm2_g)
    krc = jnp.swapaxes(cache_krope, 0, 1).reshape(DEPTH, B_S * L_CTX, ROPE)
    krc = jnp.concatenate([krc, krc], axis=2).astype(BF16)

    ckv_list, kr_list = [], []
    for l in range(DEPTH):
        final_g = final_norm_g.reshape(1, D) if l == DEPTH - 1 else None
        x, ckv, kr = _layer(x, l, mod[l], wts, consts, kt_c, kt_s, hy_short_w, hy_skip, cache_ckv, krc[l],
                            exp_w_gate, exp_w_up, exp_w_down, final_g)
        ckv_list.append(ckv.reshape(B_CTX, L_CTX, KV_RANK))
        kr_list.append(kr[:, :ROPE].reshape(B_CTX, L_CTX, ROPE))

    y_prompt, y_sample = x
    return (y_prompt.reshape(B_CTX, L_CTX, D), y_sample.reshape(B_S, L_S, D),
            jnp.stack(ckv_list, axis=1), jnp.stack(kr_list, axis=1))
```

```python
import functools
import math

import numpy as np
import jax
import jax.numpy as jnp
from jax import lax
from jax.experimental import pallas as pl
from jax.experimental.pallas import tpu as pltpu

F32 = jnp.float32
BF16 = jnp.bfloat16

D = 2048
DEPTH = 2
B_CTX, L_CTX = 32, 256
B_S, L_S = 8, 1024
T_CTX = B_CTX * L_CTX
T = T_CTX + B_S * L_S
GRP = 1024
NG = T // GRP
NG_CTX = T_CTX // GRP
SEQ_PER_GRP = GRP // L_CTX
EPS = 1e-6
GRID_W = 64

POOL_W = 512
POOL_GC = 128
POOL_WINDOWS = (2, 4, 8, 16)
HY_W = 512
POS_BANDS = 8
FILT_HID = 64
H = 8
NOPE = 128
ROPE = 64
VD = 128
Q_RANK = 512
KV_RANK = 256
ROPE_THETA = 10000.0
IN_COLS = 2880
IN_PAD = 3072
E = 16
FF = 1024
CAP_CTX = 2 * L_CTX // E
CAP_S = 2 * L_S // E
SLOTS = GRP * 2 // E
ROWS_E = NG * SLOTS
LANES = 128
NCH = 512
ATT_SCALE = 1.0 / math.sqrt(NOPE + ROPE)
MB = 1024 * 1024


def _cp(sem, vmem_mb=48):
    return pltpu.CompilerParams(dimension_semantics=sem, vmem_limit_bytes=vmem_mb * MB)


def _rms(x, g):
    return x * lax.rsqrt(jnp.mean(x * x, axis=-1, keepdims=True) + EPS) * g


def _dot(a, b):
    return jnp.dot(a, b, preferred_element_type=F32)


def _dot_nt(a, b):
    return lax.dot_general(a, b, (((1,), (1,)), ((), ())), preferred_element_type=F32)


def _mod_kernel(c_ref, w_ref, b_ref, o_ref):
    c = c_ref[...]
    a = (c * jax.nn.sigmoid(c)).astype(BF16)
    o_ref[...] = _dot(a, w_ref[...].astype(BF16)) + b_ref[...]


def _adaln_mod(c16, ada_w, ada_b):
    tn = 1024
    return pl.pallas_call(
        _mod_kernel,
        grid=(DEPTH, 6 * D // tn),
        in_specs=[
            pl.BlockSpec((NG, D), lambda l, j: (0, 0)),
            pl.BlockSpec((None, D, tn), lambda l, j: (l, 0, j)),
            pl.BlockSpec((None, 1, tn), lambda l, j: (l, 0, j)),
        ],
        out_specs=pl.BlockSpec((None, NG, tn), lambda l, j: (l, 0, j)),
        out_shape=jax.ShapeDtypeStruct((DEPTH, NG, 6 * D), F32),
        compiler_params=_cp(("parallel", "parallel")),
        name="adaln_mod",
    )(c16, ada_w, ada_b.reshape(DEPTH, 1, 6 * D))


def _src_specs(x, tm):
    n_ctx = T_CTX // tm
    n_all = T // tm
    if isinstance(x, tuple):
        return list(x), [
            pl.BlockSpec((tm, D), lambda i: (jnp.minimum(i, n_ctx - 1), 0)),
            pl.BlockSpec((tm, D), lambda i: (jnp.clip(i - n_ctx, 0, n_all - n_ctx - 1), 0)),
        ]
    return [x], [pl.BlockSpec((tm, D), lambda i: (jnp.minimum(i, n_all - 1), 0))]


def _load_src(x_refs, tm, cs=slice(None)):
    if len(x_refs) == 1:
        return x_refs[0][:, cs]
    return jnp.where(pl.program_id(0) < T_CTX // tm, x_refs[0][:, cs], x_refs[1][:, cs])


def _resident(shape, layer):
    nd = len(shape)
    return pl.BlockSpec((None,) + shape, lambda i: (layer,) + (0,) * nd, pipeline_mode=pl.Buffered(1))


def _mod_row(k, per):
    return pl.BlockSpec((None, 1, D), lambda i: (6 * (i // per) + k, 0, 0))


N_U = POOL_W + 3 * HY_W
N_QR = H * ROPE


def _inproj_kernel(*refs, tm):
    (*x_refs, g_ref, shift_ref, scale_ref, wm_ref, wx_ref, qg_ref, wq_ref, csq_ref, kvg_ref, csk_ref, wkv_ref,
     up_ref, uv_ref, u1_ref, u2_ref, q_ref, ckv_ref, kr_ref, krr_ref, kv_ref) = refs

    y = _rms(_load_src(x_refs, tm), g_ref[...])
    h = (y * (1.0 + scale_ref[...]) + shift_ref[...]).astype(BF16)
    for c, u_ref in enumerate((up_ref, uv_ref, u1_ref, u2_ref)):
        u_ref[...] = _dot(h, wm_ref[:, c * HY_W:(c + 1) * HY_W]).astype(u_ref.dtype)

    qn = _rms(_dot(h, wm_ref[:, N_U:]), qg_ref[...]).astype(BF16)
    nq = H * NOPE
    for c in range(nq // NCH):
        cs = slice(c * NCH, (c + 1) * NCH)
        q_ref[:, cs] = (_dot(qn, wq_ref[:, cs]) * ATT_SCALE).astype(q_ref.dtype)
    rot = _dot(qn, wq_ref[:, nq:nq + N_QR]) * csq_ref[0] + _dot(qn, wq_ref[:, nq + N_QR:]) * csq_ref[1]
    q_ref[:, nq:] = (rot * ATT_SCALE).astype(q_ref.dtype)

    kx = _dot(h, wx_ref[...])
    ckv = _rms(kx[:, :KV_RANK], kvg_ref[...])
    kr = kx[:, KV_RANK:KV_RANK + LANES]

    @pl.when(pl.program_id(0) < T_CTX // tm)
    def _():
        ckv_ref[...] = ckv
        kr_ref[...] = kr

    krr_ref[...] = (kr * csk_ref[0] + kx[:, KV_RANK + LANES:] * csk_ref[1]).astype(krr_ref.dtype)
    kv_ref[...] = _dot(ckv.astype(BF16), wkv_ref[...]).astype(kv_ref.dtype)


def _inproj(x, mod3, wts, consts, layer):
    tm = 512
    per = GRP // tm
    xs, x_specs = _src_specs(x, tm)

    def kind(i):
        return jnp.where(i >= T_CTX // tm, 1, 0)

    def rows(width):
        return pl.BlockSpec((tm, width), lambda i: (i, 0))

    def ctx_rows(width):
        return pl.BlockSpec((tm, width), lambda i: (jnp.minimum(i, T_CTX // tm - 1), 0))

    nkv = H * (NOPE + VD)
    return pl.pallas_call(
        functools.partial(_inproj_kernel, tm=tm),
        grid=(T // tm,),
        in_specs=x_specs + [
            _resident((1, D), layer), _mod_row(0, per), _mod_row(1, per),
            _resident((D, N_U + Q_RANK), layer),
            _resident((D, 4 * LANES), layer),
            _resident((1, Q_RANK), layer),
            _resident((Q_RANK, H * (NOPE + 2 * ROPE)), layer),
            pl.BlockSpec((None, 2, tm, N_QR), lambda i: (kind(i), 0, i % per, 0)),
            _resident((1, KV_RANK), layer),
            pl.BlockSpec((None, 2, tm, LANES), lambda i: (kind(i), 0, i % per, 0)),
            _resident((KV_RANK, nkv), layer),
        ],
        out_specs=[rows(HY_W)] * 4 + [rows(H * (NOPE + ROPE)), ctx_rows(KV_RANK), ctx_rows(LANES), rows(LANES),
                                      rows(nkv)],
        out_shape=[jax.ShapeDtypeStruct((T, HY_W), BF16)] * 4 + [
            jax.ShapeDtypeStruct((T, H * (NOPE + ROPE)), BF16),
            jax.ShapeDtypeStruct((T_CTX, KV_RANK), F32),
            jax.ShapeDtypeStruct((T_CTX, LANES), F32),
            jax.ShapeDtypeStruct((T, LANES), BF16),
            jax.ShapeDtypeStruct((T, nkv), BF16),
        ],
        compiler_params=_cp(("arbitrary",), 58),
        name="in_proj",
    )(*xs, wts["g1"], mod3, mod3, wts["w_main"], wts["w_x"], wts["qg"], wts["wq"], consts["cs_q"],
      wts["kvg"], consts["cs_k"], wts["wkv"])


def _pool_kernel(u_ref, ic_ref, is_ref, pw_ref, ps_ref, o_ref):
    g = pl.program_id(0)

    def seq(r0, n, inv_ref):
        row = lax.broadcasted_iota(jnp.int32, (n, POOL_GC), 0)

        def later(x, m):
            return jnp.where(row < n - m, pltpu.roll(x, n - m, 0), 0.0)

        def earlier(x, m):
            return jnp.where(row >= m, pltpu.roll(x, m, 0), 0.0)

        for k, w in enumerate(POOL_WINDOWS):
            cs = slice(k * POOL_GC, (k + 1) * POOL_GC)
            u = u_ref[r0:r0 + n, cs].astype(F32)
            ahead, behind, m = u, earlier(u, 1), 1
            while m < w // 2:
                ahead = ahead + later(ahead, m)
                behind = behind + earlier(behind, m)
                m *= 2
            pooled = (ahead + behind) * inv_ref[:, cs] - u
            y = _dot(pooled.astype(BF16), pw_ref[k]) * ps_ref[:, cs]
            o_ref[r0:r0 + n, cs] = y.astype(o_ref.dtype)

    @pl.when(g < NG_CTX)
    def _():
        for s in range(SEQ_PER_GRP):
            seq(s * L_CTX, L_CTX, ic_ref)

    @pl.when(g >= NG_CTX)
    def _():
        seq(0, L_S, is_ref)


def _pool(proj, consts, pw, ps):
    nw = len(POOL_WINDOWS)
    return pl.pallas_call(
        _pool_kernel,
        grid=(NG,),
        in_specs=[
            pl.BlockSpec((GRP, POOL_W), lambda g: (g, 0)),
            pl.BlockSpec((L_CTX, POOL_W), lambda g: (0, 0)),
            pl.BlockSpec((L_S, POOL_W), lambda g: (0, 0)),
            pl.BlockSpec((nw, POOL_GC, POOL_GC), lambda g: (0, 0, 0)),
            pl.BlockSpec((1, POOL_W), lambda g: (0, 0)),
        ],
        out_specs=pl.BlockSpec((GRP, POOL_W), lambda g: (g, 0)),
        out_shape=jax.ShapeDtypeStruct((T, POOL_W), BF16),
        compiler_params=_cp(("parallel",)),
        name="pool_mix",
    )(proj, consts["inv_c"], consts["inv_s"], pw, ps)


HY_CH = HY_W


def _hyena_kernel(v_ref, x1_ref, x2_ref, swv_ref, sw1_ref, sw2_ref, sbv_ref, sb1_ref, sb2_ref,
                  skip_ref, wc_ref, wtc_ref, ws_ref, wts_ref, ktc_ref, kts_ref, o_ref):
    g = pl.program_id(0)

    def sconv(u_ref, sw_ref, sb_ref, r0, n, cs):
        u = u_ref[r0:r0 + n, cs].astype(F32)
        row = lax.broadcasted_iota(jnp.int32, u.shape, 0)
        prev = jnp.where(row == 0, 0.0, pltpu.roll(u, 1, 0))
        nxt = jnp.where(row == n - 1, 0.0, pltpu.roll(u, n - 1, 0))
        return prev * sw_ref[0:1, cs] + u * sw_ref[1:2, cs] + nxt * sw_ref[2:3, cs] + sb_ref[:, cs]

    def lconv(u, o, n, cs, w_ref, wt_ref, kt_ref):
        spec = _dot(w_ref[...], u.astype(BF16))
        pr, pi = spec[:n], spec[n:]
        ka, kb, ka2 = kt_ref[o, 0, :, cs], kt_ref[o, 1, :, cs], kt_ref[o, 2, :, cs]
        yr = pr * ka - pi * kb
        yi = pr * kb + pi * ka2
        prod = jnp.concatenate([yr, yi], axis=0).astype(BF16)
        return _dot(wt_ref[...], prod) + u * skip_ref[o:o + 1, cs]

    def seq(r0, n, cs, w_ref, wt_ref, kt_ref):
        v = sconv(v_ref, swv_ref, sbv_ref, r0, n, cs)
        x1 = sconv(x1_ref, sw1_ref, sb1_ref, r0, n, cs)
        x2 = sconv(x2_ref, sw2_ref, sb2_ref, r0, n, cs)
        z = x1 * lconv(v, 0, n, cs, w_ref, wt_ref, kt_ref)
        y = x2 * lconv(z, 1, n, cs, w_ref, wt_ref, kt_ref)
        o_ref[r0:r0 + n, cs] = y.astype(o_ref.dtype)

    chunks = [slice(c * HY_CH, (c + 1) * HY_CH) for c in range(HY_W // HY_CH)]

    @pl.when(g < NG_CTX)
    def _():
        for s in range(SEQ_PER_GRP):
            for cs in chunks:
                seq(s * L_CTX, L_CTX, cs, wc_ref, wtc_ref, ktc_ref)

    @pl.when(g >= NG_CTX)
    def _():
        for cs in chunks:
            seq(0, L_S, cs, ws_ref, wts_ref, kts_ref)


def _hyena(uv, u1, u2, sw, sb, skip, consts, kt_c, kt_s, layer):
    def part(p):
        return pl.BlockSpec((GRP, HY_W), lambda g: (g, 0))

    def swpart(p):
        return pl.BlockSpec((3, HY_W), lambda g: (0, p))

    def sbpart(p):
        return pl.BlockSpec((1, HY_W), lambda g: (0, p))

    def tables(n):
        return pl.BlockSpec((None, 2, 3, n, HY_W), lambda g: (layer, 0, 0, 0, 0), pipeline_mode=pl.Buffered(1))

    return pl.pallas_call(
        _hyena_kernel,
        grid=(NG,),
        in_specs=[
            part(0), part(1), part(2),
            swpart(0), swpart(1), swpart(2),
            sbpart(0), sbpart(1), sbpart(2),
            pl.BlockSpec((2, HY_W), lambda g: (0, 0)),
            pl.BlockSpec((2 * L_CTX, L_CTX), lambda g: (0, 0)),
            pl.BlockSpec((L_CTX, 2 * L_CTX), lambda g: (0, 0)),
            pl.BlockSpec((2 * L_S, L_S), lambda g: (0, 0)),
            pl.BlockSpec((L_S, 2 * L_S), lambda g: (0, 0)),
            tables(L_CTX), tables(L_S),
        ],
        out_specs=pl.BlockSpec((GRP, HY_W), lambda g: (g, 0)),
        out_shape=jax.ShapeDtypeStruct((T, HY_W), BF16),
        compiler_params=_cp(("parallel",), 56),
        name="hyena_mix",
    )(uv, u1, u2, sw, sw, sw, sb, sb, sb, skip,
      consts["dft_c"], consts["dftt_c"], consts["dft_s"], consts["dftt_s"], kt_c, kt_s)


def _cachekv_kernel(x_ref, w_ref, o_ref):
    o_ref[...] = _dot(x_ref[...].astype(BF16), w_ref[...]).astype(o_ref.dtype)


def _cachekv(cache_ckv, layer, wkv):
    return pl.pallas_call(
        _cachekv_kernel,
        grid=(B_S,),
        in_specs=[
            pl.BlockSpec((None, None, L_CTX, KV_RANK), lambda b: (b, layer, 0, 0)),
            pl.BlockSpec((None, KV_RANK, H * (NOPE + VD)), lambda b: (layer, 0, 0)),
        ],
        out_specs=pl.BlockSpec((L_CTX, H * (NOPE + VD)), lambda b: (b, 0)),
        out_shape=jax.ShapeDtypeStruct((B_S * L_CTX, H * (NOPE + VD)), BF16),
        compiler_params=_cp(("parallel",)),
        name="cache_kv",
    )(cache_ckv, wkv)


ATT_TQ = 512
NK_S = L_S + L_CTX


def _attn_kernel(q_ref, kv_ref, krr_ref, kvc_ref, krc_ref, o_ref, kcat_ref):
    g = pl.program_id(0)
    lane = lax.broadcasted_iota(jnp.int32, (1, LANES), 1)
    hk = NOPE + LANES
    vo = H * NOPE

    krr = krr_ref[...]
    for h in range(H):
        kcat_ref[0:GRP, h * hk:h * hk + NOPE] = kv_ref[:, h * NOPE:(h + 1) * NOPE]
        kcat_ref[0:GRP, h * hk + NOPE:(h + 1) * hk] = krr

    def qcat(rows, h):
        qn = q_ref[rows, h * NOPE:(h + 1) * NOPE]
        pair = q_ref[rows, vo + (h // 2) * LANES:vo + (h // 2 + 1) * LANES].astype(F32)
        keep = (lane < ROPE) if h % 2 == 0 else (lane >= ROPE)
        return jnp.concatenate([qn, jnp.where(keep, pair, 0.0).astype(BF16)], axis=1)

    def probs(sc):
        m = jnp.max(sc, axis=-1, keepdims=True)
        p = jnp.exp(sc - m)
        return p.astype(BF16), 1.0 / jnp.sum(p, axis=-1, keepdims=True)

    @pl.when(g < NG_CTX)
    def _():
        def body(s, carry):
            rows = pl.ds(pl.multiple_of(s * L_CTX, L_CTX), L_CTX)
            for h in range(H):
                p, rl = probs(_dot_nt(qcat(rows, h), kcat_ref[rows, h * hk:(h + 1) * hk]))
                o = _dot(p, kv_ref[rows, vo + h * VD:vo + (h + 1) * VD]) * rl
                o_ref[rows, h * VD:(h + 1) * VD] = o.astype(o_ref.dtype)
            return carry

        lax.fori_loop(0, SEQ_PER_GRP, body, 0)

    @pl.when(g >= NG_CTX)
    def _():
        krc = krc_ref[...]
        for h in range(H):
            kcat_ref[GRP:NK_S, h * hk:h * hk + NOPE] = kvc_ref[:, h * NOPE:(h + 1) * NOPE]
            kcat_ref[GRP:NK_S, h * hk + NOPE:(h + 1) * hk] = krc

        def body(t, carry):
            rows = pl.ds(pl.multiple_of(t * ATT_TQ, ATT_TQ), ATT_TQ)
            for h in range(H):
                p, rl = probs(_dot_nt(qcat(rows, h), kcat_ref[:, h * hk:(h + 1) * hk]))
                o = _dot(p[:, :GRP], kv_ref[:, vo + h * VD:vo + (h + 1) * VD])
                o = o + _dot(p[:, GRP:], kvc_ref[:, vo + h * VD:vo + (h + 1) * VD])
                o_ref[rows, h * VD:(h + 1) * VD] = (o * rl).astype(o_ref.dtype)
            return carry

        lax.fori_loop(0, L_S // ATT_TQ, body, 0)


def _attention(q, kv, krr, kvc, krc):
    def cache_blk(g):
        return jnp.maximum(g - NG_CTX, 0)

    return pl.pallas_call(
        _attn_kernel,
        grid=(NG,),
        in_specs=[
            pl.BlockSpec((GRP, H * (NOPE + ROPE)), lambda g: (g, 0)),
            pl.BlockSpec((GRP, H * (NOPE + VD)), lambda g: (g, 0)),
            pl.BlockSpec((GRP, LANES), lambda g: (g, 0)),
            pl.BlockSpec((L_CTX, H * (NOPE + VD)), lambda g: (cache_blk(g), 0)),
            pl.BlockSpec((L_CTX, LANES), lambda g: (cache_blk(g), 0)),
        ],
        out_specs=pl.BlockSpec((GRP, H * VD), lambda g: (g, 0)),
        out_shape=jax.ShapeDtypeStruct((T, H * VD), BF16),
        scratch_shapes=[pltpu.VMEM((NK_S, H * (NOPE + LANES)), BF16)],
        compiler_params=_cp(("parallel",)),
        name="mla_attention",
    )(q, kv, krr, kvc, krc)


def _wout_kernel(yp_ref, yh_ref, ym_ref, w_ref, gate_ref, g2_ref, shift_ref, scale_ref, rw_ref, *refs, tm):
    *x_refs, o_ref, h_ref, aff_ref, afft_ref, prev_ref = refs

    @pl.when(pl.program_id(0) == 0)
    def _():
        prev_ref[...] = jnp.zeros_like(prev_ref)

    hb = (_rms(prev_ref[...], g2_ref[...]) * (1.0 + scale_ref[...]) + shift_ref[...]).astype(BF16)
    h_ref[...] = hb
    logits = _dot(hb, rw_ref[...])
    lane = lax.broadcasted_iota(jnp.int32, logits.shape, 1)
    logits = jnp.where(lane < E, logits, -jnp.inf)
    ex = jnp.exp(logits - jnp.max(logits, axis=-1, keepdims=True))
    aff = ex / jnp.sum(ex, axis=-1, keepdims=True)
    aff_ref[...] = aff
    afft_ref[...] = aff.T[:E]

    y = jnp.concatenate([yp_ref[...], yh_ref[...], ym_ref[...]], axis=1)
    for c in range(D // NCH):
        cs = slice(c * NCH, (c + 1) * NCH)
        v = _load_src(x_refs, tm, cs) + gate_ref[:, cs] * _dot(y, w_ref[:, cs])
        o_ref[:, cs] = v.astype(o_ref.dtype)
        prev_ref[:, cs] = v


def _wout(yp, yh, ym, x, mod3, wts, layer):
    tm = 512
    per = GRP // tm
    n = T // tm
    xs, x_specs = _src_specs(x, tm)

    def cur(i):
        return jnp.minimum(i, n - 1)

    def prev(i):
        return jnp.maximum(i - 1, 0)

    def rows(width, blk):
        return pl.BlockSpec((tm, width), lambda i: (blk(i), 0))

    def mod_row(k, blk):
        return pl.BlockSpec((None, 1, D), lambda i: (6 * (blk(i) // per) + k, 0, 0))

    return pl.pallas_call(
        functools.partial(_wout_kernel, tm=tm),
        grid=(n + 1,),
        in_specs=[
            rows(POOL_W, cur), rows(HY_W, cur), rows(H * VD, cur),
            _resident((D, D), layer),
            mod_row(2, cur),
            _resident((1, D), layer), mod_row(3, prev), mod_row(4, prev),
            _resident((D, LANES), layer),
        ] + x_specs,
        out_specs=[rows(D, cur), rows(D, prev), rows(LANES, prev), pl.BlockSpec((E, tm), lambda i: (0, prev(i)))],
        out_shape=[
            jax.ShapeDtypeStruct((T, D), BF16),
            jax.ShapeDtypeStruct((T, D), BF16),
            jax.ShapeDtypeStruct((T, LANES), F32),
            jax.ShapeDtypeStruct((E, T), F32),
        ],
        scratch_shapes=[pltpu.VMEM((tm, D), F32)],
        compiler_params=_cp(("arbitrary",), 56),
        name="out_proj",
    )(yp, yh, ym, wts["w_out"], mod3, wts["g2"], mod3, mod3, wts["rw"], *xs)


RANK_CH = 256
SEARCH_ROUNDS = 14
SEARCH_WAYS = 8
AFF_MAX = 2.0


def _rank_kernel(aff_ref, afft_ref, tri_ref, rank_ref, rank_t_ref, cnt_ref, cut_ref):
    g = pl.program_id(0)

    def count_ge(a, t):
        return jnp.sum(jnp.where(a >= t, 1.0, 0.0), axis=1, keepdims=True)

    def search(specs, capf):
        acts = [afft_ref[:, r0:r0 + n] for r0, n in specs]
        lo = [jnp.zeros((E, 1), F32) for _ in specs]
        hi = [jnp.full((E, 1), AFF_MAX, F32) for _ in specs]
        for _ in range(SEARCH_ROUNDS):
            for s, a in enumerate(acts):
                step = (hi[s] - lo[s]) * (1.0 / SEARCH_WAYS)
                ts = [lo[s] + step * k for k in range(1, SEARCH_WAYS)]
                ok = [count_ge(a, t) >= capf for t in ts]
                new_lo, new_hi = lo[s], hi[s]
                for t, o in zip(ts, ok):
                    new_lo = jnp.where(o, t, new_lo)
                for t, o in zip(reversed(ts), reversed(ok)):
                    new_hi = jnp.where(o, new_hi, t)
                lo[s], hi[s] = new_lo, new_hi
        open_brackets = jnp.zeros((E, 1), F32)
        for s, a in enumerate(acts):
            top = jnp.max(jnp.where(a >= lo[s], jnp.where(a < hi[s], a, -1.0), -1.0), axis=1, keepdims=True)
            low = jnp.min(jnp.where(a >= lo[s], jnp.where(a < hi[s], a, AFF_MAX), AFF_MAX), axis=1, keepdims=True)
            cut_ref[:, s:s + 1] = top
            open_brackets = open_brackets + jnp.where(top != low, 1.0, 0.0)
        return jnp.sum(open_brackets)

    def exact_cut(s, r0, n, capf):
        for e in range(E):
            row = afft_ref[e:e + 1, r0:r0 + n]
            acc = jnp.zeros((1, n), F32)
            for c in range(n // RANK_CH):
                col = aff_ref[r0 + c * RANK_CH:r0 + (c + 1) * RANK_CH, e:e + 1]
                acc = acc + jnp.sum(jnp.where(col >= row, 1.0, 0.0), axis=0, keepdims=True)
            cnt_ref[e:e + 1, 0:n] = acc
        a = afft_ref[:, r0:r0 + n]
        cut_ref[:, s:s + 1] = jnp.max(jnp.where(cnt_ref[:, 0:n] >= capf, a, -1.0), axis=1, keepdims=True)

    def slots(s, r0, n, capf):
        a = afft_ref[:, r0:r0 + n]
        cut = cut_ref[:, s:s + 1]
        above = a > cut
        tied = a == cut
        above_f = jnp.where(above, 1.0, 0.0)
        n_above = jnp.sum(above_f, axis=1, keepdims=True)
        marks = jnp.concatenate([above_f, jnp.where(tied, 1.0, 0.0)], axis=0).astype(BF16)
        before = _dot(marks, tri_ref[0:n, 0:n])
        tie_slot = n_above + before[E:]
        slot = jnp.where(above, before[:E], jnp.where(tied, jnp.where(tie_slot < capf, tie_slot, n), n))
        rank_ref[:, r0:r0 + n] = slot.astype(jnp.int32)
        wide = jnp.concatenate([slot, jnp.zeros((LANES - E, n), F32)], axis=0)
        rank_t_ref[r0:r0 + n, :] = wide.T.astype(jnp.int32)

    def group(specs, cap):
        capf = float(cap)
        unresolved = search(specs, capf)

        @pl.when(unresolved > 0.0)
        def _():
            for s, (r0, n) in enumerate(specs):
                exact_cut(s, r0, n, capf)

        for s, (r0, n) in enumerate(specs):
            slots(s, r0, n, capf)

    @pl.when(g < NG_CTX)
    def _():
        group([(s * L_CTX, L_CTX) for s in range(SEQ_PER_GRP)], CAP_CTX)

    @pl.when(g >= NG_CTX)
    def _():
        group([(0, L_S)], CAP_S)


def _rank(aff, afft, tri):
    return pl.pallas_call(
        _rank_kernel,
        grid=(NG,),
        in_specs=[
            pl.BlockSpec((GRP, LANES), lambda g: (g, 0)),
            pl.BlockSpec((E, GRP), lambda g: (0, g)),
            pl.BlockSpec((L_S, L_S), lambda g: (0, 0)),
        ],
        out_specs=[pl.BlockSpec((E, GRP), lambda g: (0, g)), pl.BlockSpec((GRP, LANES), lambda g: (g, 0))],
        out_shape=[jax.ShapeDtypeStruct((E, T), jnp.int32), jax.ShapeDtypeStruct((T, LANES), jnp.int32)],
        scratch_shapes=[pltpu.VMEM((E, GRP), F32), pltpu.VMEM((E, LANES), F32)],
        compiler_params=_cp(("parallel",)),
        name="moe_rank",
    )(aff, afft, tri)


def _gather_kernel(rank_ref, afft_ref, h_ref, xs_ref, gs_ref, sel_ref):
    g = pl.program_id(0)

    def seq(r0, n, cap, slot0):
        slot_i = lax.broadcasted_iota(jnp.int32, (cap, n), 0)
        for e in range(E):
            hit = slot_i == rank_ref[e:e + 1, r0:r0 + n]
            sel_ref[e * cap:(e + 1) * cap, 0:n] = jnp.where(hit, 1.0, 0.0).astype(BF16)
            gs_ref[e, slot0:slot0 + cap, :] = jnp.sum(
                jnp.where(hit, afft_ref[e:e + 1, r0:r0 + n], 0.0), axis=1, keepdims=True)
        for c in range(D // NCH):
            cs = slice(c * NCH, (c + 1) * NCH)
            res = _dot(sel_ref[0:E * cap, 0:n], h_ref[r0:r0 + n, cs])
            for e in range(E):
                xs_ref[e, slot0:slot0 + cap, cs] = res[e * cap:(e + 1) * cap].astype(xs_ref.dtype)

    @pl.when(g < NG_CTX)
    def _():
        for s in range(SEQ_PER_GRP):
            seq(s * L_CTX, L_CTX, CAP_CTX, s * CAP_CTX)

    @pl.when(g >= NG_CTX)
    def _():
        seq(0, L_S, CAP_S, 0)


def _gather(rank, afft, h2):
    return pl.pallas_call(
        _gather_kernel,
        grid=(NG,),
        in_specs=[
            pl.BlockSpec((E, GRP), lambda g: (0, g)),
            pl.BlockSpec((E, GRP), lambda g: (0, g)),
            pl.BlockSpec((GRP, D), lambda g: (g, 0)),
        ],
        out_specs=[
            pl.BlockSpec((E, SLOTS, D), lambda g: (0, g, 0)),
            pl.BlockSpec((E, SLOTS, 1), lambda g: (0, g, 0)),
        ],
        out_shape=[
            jax.ShapeDtypeStruct((E, ROWS_E, D), BF16),
            jax.ShapeDtypeStruct((E, ROWS_E, 1), F32),
        ],
        scratch_shapes=[pltpu.VMEM((E * CAP_S, L_S), BF16)],
        compiler_params=_cp(("parallel",)),
        name="moe_gather",
    )(rank, afft, h2)


FFN_TF = 256
FFN_TW = 512
FFN_TM = 1024


def _ffn_kernel(x_ref, wg_ref, wu_ref, wd_ref, gs_ref, o_ref, acc_ref):
    f = pl.program_id(2)
    nf = pl.num_programs(2)

    def partial_down():
        x = x_ref[...]
        cols = pl.ds(pl.multiple_of((f % (FFN_TW // FFN_TF)) * FFN_TF, FFN_TF), FFN_TF)
        a = _dot(x, wg_ref[:, cols].astype(BF16))
        b = _dot(x, wu_ref[:, cols].astype(BF16))
        hid = (a * jax.nn.sigmoid(a) * b).astype(BF16)
        return _dot(hid, wd_ref[...].astype(BF16))

    @pl.when(f == 0)
    def _():
        acc_ref[...] = partial_down()

    @pl.when((f > 0) & (f < nf - 1))
    def _():
        acc_ref[...] += partial_down()

    @pl.when(f == nf - 1)
    def _():
        o_ref[...] = ((acc_ref[...] + partial_down()) * gs_ref[...]).astype(o_ref.dtype)


def _ffn(xs, gs, w_gate, w_up, w_down, layer):
    return pl.pallas_call(
        _ffn_kernel,
        grid=(E, ROWS_E // FFN_TM, FF // FFN_TF),
        in_specs=[
            pl.BlockSpec((None, FFN_TM, D), lambda e, m, f: (e, m, 0)),
            pl.BlockSpec((None, None, D, FFN_TW), lambda e, m, f: (layer, e, 0, f // (FFN_TW // FFN_TF))),
            pl.BlockSpec((None, None, D, FFN_TW), lambda e, m, f: (layer, e, 0, f // (FFN_TW // FFN_TF))),
            pl.BlockSpec((None, None, FFN_TF, D), lambda e, m, f: (layer, e, f, 0)),
            pl.BlockSpec((None, FFN_TM, 1), lambda e, m, f: (e, m, 0)),
        ],
        out_specs=pl.BlockSpec((None, FFN_TM, D), lambda e, m, f: (e, m, 0)),
        out_shape=jax.ShapeDtypeStruct((E, ROWS_E, D), BF16),
        scratch_shapes=[pltpu.VMEM((FFN_TM, D), F32)],
        compiler_params=_cp(("parallel", "parallel", "arbitrary"), 56),
        name="moe_ffn",
    )(xs, w_gate, w_up, w_down, gs)


CMB_TM = 512


def _combine_kernel(rt_ref, ys_ref, x_ref, gate_ref, exc_ref, exs_ref, *refs, last):
    if last:
        fg_ref, oc_ref, os_ref, st_ref = refs
    else:
        oc_ref, st_ref = refs
        os_ref = oc_ref
    g = pl.program_id(0)
    half = pl.program_id(1)

    def scatter(r0, n, cap, slot0, ex_ref, o_ref):
        ec = E * cap
        r = jnp.minimum(rt_ref[r0:r0 + n, :], cap).astype(F32).astype(BF16)
        want = (lax.broadcasted_iota(jnp.int32, (1, NCH), 1) & (cap - 1)).astype(F32)
        for c in range(ec // NCH):
            cs = slice(c * NCH, (c + 1) * NCH)
            st_ref[r0:r0 + n, cs] = jnp.where(_dot(r, ex_ref[:, cs]) == want, 1.0, 0.0).astype(BF16)
        for c in range(D // NCH):
            cs = slice(c * NCH, (c + 1) * NCH)
            ys = ys_ref[:, pl.ds(slot0, cap), cs].reshape(ec, NCH)
            moe = _dot(st_ref[r0:r0 + n, 0:ec], ys)
            o_ref[r0:r0 + n, cs] = x_ref[r0:r0 + n, cs].astype(F32) + gate_ref[:, cs] * moe

    def finish(o_ref):
        if last:
            o_ref[...] = _rms(o_ref[...], fg_ref[...])

    @pl.when(g < NG_CTX)
    def _():
        seqs = CMB_TM // L_CTX
        for s in range(seqs):
            slot0 = pl.multiple_of((half * seqs + s) * CAP_CTX, CAP_CTX)
            scatter(s * L_CTX, L_CTX, CAP_CTX, slot0, exc_ref, oc_ref)
        finish(oc_ref)

    @pl.when(g >= NG_CTX)
    def _():
        scatter(0, CMB_TM, CAP_S, 0, exs_ref, os_ref)
        finish(os_ref)


def _combine(rank_t, ys, x, mod3, consts, final_g=None):
    last = final_g is not None
    per = GRP // CMB_TM

    def row(g, i):
        return (g * per + i, 0)

    in_specs = [
        pl.BlockSpec((CMB_TM, LANES), row),
        pl.BlockSpec((E, SLOTS, D), lambda g, i: (0, g, 0)),
        pl.BlockSpec((CMB_TM, D), row),
        pl.BlockSpec((None, 1, D), lambda g, i: (6 * g + 5, 0, 0)),
        pl.BlockSpec((LANES, E * CAP_CTX), lambda g, i: (0, 0)),
        pl.BlockSpec((LANES, E * CAP_S), lambda g, i: (0, 0)),
    ]
    args = [rank_t, ys, x, mod3, consts["ex_c"], consts["ex_s"]]
    if last:
        n_ctx = T_CTX // CMB_TM
        in_specs.append(pl.BlockSpec((1, D), lambda g, i: (0, 0)))
        args.append(final_g)
        out_specs = [
            pl.BlockSpec((CMB_TM, D), lambda g, i: (jnp.where(g < NG_CTX, g * per + i, n_ctx - 1), 0)),
            pl.BlockSpec((CMB_TM, D), lambda g, i: (jnp.where(g < NG_CTX, 0, (g - NG_CTX) * per + i), 0)),
        ]
        out_shape = [jax.ShapeDtypeStruct((T_CTX, D), F32), jax.ShapeDtypeStruct((T - T_CTX, D), F32)]
    else:
        out_specs = pl.BlockSpec((CMB_TM, D), row)
        out_shape = jax.ShapeDtypeStruct((T, D), F32)
    return pl.pallas_call(
        functools.partial(_combine_kernel, last=last),
        grid=(NG, per),
        in_specs=in_specs,
        out_specs=out_specs,
        out_shape=out_shape,
        scratch_shapes=[pltpu.VMEM((CMB_TM, E * CAP_S), BF16)],
        compiler_params=_cp(("arbitrary", "arbitrary"), 56),
        name="moe_combine",
    )(*args)


def _np_constants():
    c = {}
    for tag, n in (("c", L_CTX), ("s", L_S)):
        t = np.arange(n)
        inv = np.zeros((n, POOL_W), np.float32)
        for k, w in enumerate(POOL_WINDOWS):
            lo = np.clip(t - w // 2, 0, n)
            hi = np.clip(t - w // 2 + w, 0, n)
            inv[:, k * POOL_GC:(k + 1) * POOL_GC] = (1.0 / (hi - lo).astype(np.float64))[:, None]
        c["inv_" + tag] = inv
        kk = np.arange(n, dtype=np.float64)[:, None]
        tt = np.arange(n, dtype=np.float64)[None, :]
        ang = np.pi * kk * tt / n
        dft = np.concatenate([np.cos(ang), -np.sin(ang)], axis=0)
        dft[n] = (-1.0) ** np.arange(n)
        c["dft_" + tag] = dft.astype(np.float32)
        c["dftt_" + tag] = np.ascontiguousarray(dft.T).astype(np.float32)
        tl = np.linspace(0.0, 1.0, n, dtype=np.float32)[:, None]
        bands = np.arange(1, POS_BANDS + 1, dtype=np.float32)[None, :]
        feats = np.concatenate([tl, np.sin(2 * np.pi * tl * bands), np.cos(2 * np.pi * tl * bands)], axis=1)
        c["feat_" + tag] = np.pad(feats.astype(np.float32), ((0, 0), (0, LANES - feats.shape[1])))
        c["t_" + tag] = tl
    n_rows = L_S // GRID_W
    row = np.repeat(np.arange(n_rows), GRID_W).astype(np.float32)
    col = np.tile(np.arange(GRID_W), n_rows).astype(np.float32)
    nf = ROPE // 4
    inv_f = (1.0 / ROPE_THETA ** (np.arange(nf, dtype=np.float32) / nf)).astype(np.float32)
    a_row = (row[:, None] * inv_f[None]).astype(np.float32).astype(np.float64)
    a_col = (col[:, None] * inv_f[None]).astype(np.float32).astype(np.float64)
    cos64 = np.concatenate([np.cos(a_row), np.cos(a_row), np.cos(a_col), np.cos(a_col)], axis=1)
    sin64 = np.concatenate([np.sin(a_row), np.sin(a_row), np.sin(a_col), np.sin(a_col)], axis=1)

    def table(reps):
        ident = np.stack([np.ones((L_S, ROPE * reps)), np.zeros((L_S, ROPE * reps))])
        rot = np.stack([np.tile(cos64, (1, reps)), np.tile(sin64, (1, reps))])
        return np.stack([ident, rot]).astype(np.float32)

    c["cs_q"] = table(H)
    c["cs_k"] = table(LANES // ROPE)
    c["tri"] = np.triu(np.ones((L_S, L_S), np.float32), k=1)
    for tag, cap in (("c", CAP_CTX), ("s", CAP_S)):
        ex = np.zeros((LANES, E * cap), np.float32)
        ex[np.arange(E * cap) // cap, np.arange(E * cap)] = 1.0
        c["ex_" + tag] = ex
    return c


def _constants():
    c = {k: jnp.asarray(v) for k, v in _np_constants().items()}
    for k in ("dft_c", "dftt_c", "dft_s", "dftt_s", "ex_c", "ex_s", "tri"):
        c[k] = c[k].astype(BF16)
    return c


def _rope_swap(w):
    q = ROPE // 4
    return jnp.concatenate([-w[..., q:2 * q], w[..., :q], -w[..., 3 * q:], w[..., 2 * q:3 * q]], axis=-1)


def _dot_hi(a, b):
    return jnp.dot(a, b, preferred_element_type=F32, precision=lax.Precision.HIGHEST)


def _filter_kernel(feat_ref, t_ref, w1_ref, b1_ref, w2_ref, b2_ref, fr_ref, w3f_ref, w3b_ref,
                   ldf_ref, ldb_ref, dft_ref, o_ref, z_ref):
    n = feat_ref.shape[0]

    @pl.when((pl.program_id(1) == 0) & (pl.program_id(2) == 0))
    def _():
        fr = fr_ref[...]
        z = jnp.sin(fr * (_dot_hi(feat_ref[...], w1_ref[...]) + b1_ref[...]))
        z_ref[...] = jnp.sin(fr * (_dot_hi(z, w2_ref[...]) + b2_ref[...]))

    z = z_ref[...]
    t = t_ref[...]
    first = lax.broadcasted_iota(jnp.int32, (n, 1), 0) == 0
    hf = _dot_hi(z, w3f_ref[...]) * jnp.exp(-jnp.exp(ldf_ref[...]) * t)
    hb = _dot_hi(z, w3b_ref[...]) * jnp.exp(-jnp.exp(ldb_ref[...]) * t)
    hb = jnp.where(first, 0.0, hb)
    norm = jnp.sum(jnp.abs(hf), axis=0, keepdims=True) + jnp.sum(jnp.abs(hb), axis=0, keepdims=True) + EPS
    both = _dot(dft_ref[...], jnp.concatenate([hf / norm, hb / norm], axis=1).astype(BF16))
    pf, pb = both[:, :hf.shape[1]], both[:, hf.shape[1]:]
    sc = jnp.where(first, 0.5 / n, 1.0 / n)
    ka = (pf[:n] + pb[:n]) * sc
    o_ref[0] = ka
    o_ref[1] = jnp.where(first, 0.0, (pf[n:] - pb[n:]) * sc)
    o_ref[2] = jnp.where(first, (pf[n:] + pb[n:]) * sc, ka)


def _hyena_tables(n, feats, tcol, dft, fw):
    tc = 256
    nc = HY_W // tc

    def lay(shape):
        return pl.BlockSpec((None,) + shape, lambda l, o, c: (l, 0, 0))

    def w3(back):
        return pl.BlockSpec((None, LANES, tc), lambda l, o, c: (l, 0, (2 * o + back) * nc + c))

    def ld(back):
        return pl.BlockSpec((None, 1, tc), lambda l, o, c: (l, 0, (2 * o + back) * nc + c))

    return pl.pallas_call(
        _filter_kernel,
        grid=(DEPTH, 2, nc),
        in_specs=[
            pl.BlockSpec((n, LANES), lambda l, o, c: (0, 0)),
            pl.BlockSpec((n, 1), lambda l, o, c: (0, 0)),
            lay((LANES, LANES)), lay((1, LANES)), lay((LANES, LANES)), lay((1, LANES)), lay((1, LANES)),
            w3(0), w3(1), ld(0), ld(1),
            pl.BlockSpec((2 * n, n), lambda l, o, c: (0, 0)),
        ],
        out_specs=pl.BlockSpec((None, None, 3, n, tc), lambda l, o, c: (l, o, 0, 0, c)),
        out_shape=jax.ShapeDtypeStruct((DEPTH, 2, 3, n, HY_W), F32),
        scratch_shapes=[pltpu.VMEM((n, LANES), F32)],
        compiler_params=_cp(("arbitrary", "arbitrary", "arbitrary")),
        name="hyena_filter",
    )(feats, tcol, fw["w1"], fw["b1"], fw["w2"], fw["b2"], fw["freq"], fw["w3"], fw["w3"],
      fw["ld"], fw["ld"], dft)


def _filter_weights(w1, b1, w2, b2, w3, freq, log_decay):
    ph = LANES - FILT_HID
    return dict(
        w1=jnp.pad(w1, ((0, 0), (0, LANES - w1.shape[1]), (0, ph))),
        b1=jnp.pad(b1, ((0, 0), (0, ph))).reshape(DEPTH, 1, LANES),
        w2=jnp.pad(w2, ((0, 0), (0, ph), (0, ph))),
        b2=jnp.pad(b2, ((0, 0), (0, ph))).reshape(DEPTH, 1, LANES),
        freq=jnp.pad(freq, ((0, 0), (0, ph))).reshape(DEPTH, 1, LANES),
        w3=jnp.pad(w3, ((0, 0), (0, ph), (0, 0))),
        ld=log_decay.reshape(DEPTH, 1, 4 * HY_W))


def _prep_weights(w_in, pool_w, pool_scale, hy_short_b, mla_q_norm, mla_kv_norm, mla_w_uq, mla_w_ukv,
                  w_out, router_w, norm1_g, norm2_g):
    n_main = N_U + Q_RANK
    kr_cols = w_in[:, :, IN_COLS - ROPE:]
    kr_swap = _rope_swap(kr_cols)
    w_x = jnp.concatenate([w_in[:, :, n_main:IN_COLS - ROPE], kr_cols, kr_cols, kr_swap, kr_swap], axis=2)
    wq = mla_w_uq.reshape(DEPTH, Q_RANK, H, NOPE + ROPE)
    wq_rope = wq[..., NOPE:]
    wq = jnp.concatenate([wq[..., :NOPE].reshape(DEPTH, Q_RANK, -1), wq_rope.reshape(DEPTH, Q_RANK, -1),
                          _rope_swap(wq_rope).reshape(DEPTH, Q_RANK, -1)], axis=2)
    wkv = mla_w_ukv.reshape(DEPTH, KV_RANK, H, NOPE + VD)
    wkv = jnp.concatenate([wkv[..., :NOPE].reshape(DEPTH, KV_RANK, -1), wkv[..., NOPE:].reshape(DEPTH, KV_RANK, -1)],
                          axis=2)
    return dict(
        w_main=w_in.astype(BF16), w_x=w_x.astype(BF16), wq=wq.astype(BF16), wkv=wkv.astype(BF16),
        w_out=w_out.astype(BF16), rw=jnp.pad(router_w, ((0, 0), (0, 0), (0, LANES - E))).astype(BF16),
        pool_w=pool_w.astype(BF16), pool_scale=pool_scale.reshape(DEPTH, 1, POOL_W),
        sb=hy_short_b.reshape(DEPTH, 1, 3 * HY_W),
        qg=mla_q_norm.reshape(DEPTH, 1, Q_RANK), kvg=mla_kv_norm.reshape(DEPTH, 1, KV_RANK),
        g1=norm1_g.reshape(DEPTH, 1, D), g2=norm2_g.reshape(DEPTH, 1, D))


def _layer(x, layer, mod3, wts, consts, kt_c, kt_s, hy_short_w, hy_skip, cache_ckv, krc,
           exp_w_gate, exp_w_up, exp_w_down, final_g):
    up, uv, u1, u2, q, ckv, kr, krr, kv = _inproj(x, mod3, wts, consts, layer)
    y_pool = _pool(up, consts, wts["pool_w"][layer], wts["pool_scale"][layer])
    y_hy = _hyena(uv, u1, u2, hy_short_w[layer], wts["sb"][layer], hy_skip[layer], consts, kt_c, kt_s, layer)
    kvc = _cachekv(cache_ckv, layer, wts["wkv"])
    y_mla = _attention(q, kv, krr, kvc, krc)
    x, h2, aff, afft = _wout(y_pool, y_hy, y_mla, x, mod3, wts, layer)
    rank, rank_t = _rank(aff, afft, consts["tri"])
    xs, gs = _gather(rank, afft, h2)
    ys = _ffn(xs, gs, exp_w_gate, exp_w_up, exp_w_down, layer)
    return _combine(rank_t, ys, x, mod3, consts, final_g), ckv, kr


def kernel(x_prompt, x_sample, cache_ckv, cache_krope, c, c_ctx, ada_w, ada_b, norm1_g, norm2_g, w_in, pool_w, pool_scale, hy_short_w, hy_short_b, hy_ffn_w1, hy_ffn_b1, hy_ffn_w2, hy_ffn_b2, hy_ffn_w3, hy_freq, hy_log_decay, hy_skip, mla_q_norm, mla_kv_norm, mla_w_uq, mla_w_ukv, w_out, router_w, exp_w_gate, exp_w_up, exp_w_down, final_norm_g):
    consts = _constants()
    x = (x_prompt.reshape(T_CTX, D), x_sample.reshape(T - T_CTX, D))
    c16 = jnp.concatenate([jnp.broadcast_to(c_ctx[None], (NG_CTX, D)), c], axis=0)
    mod = _adaln_mod(c16, ada_w, ada_b).reshape(DEPTH, NG * 6, 1, D)

    fw = _filter_weights(hy_ffn_w1, hy_ffn_b1, hy_ffn_w2, hy_ffn_b2, hy_ffn_w3, hy_freq, hy_log_decay)
    kt_c = _hyena_tables(L_CTX, consts["feat_c"], consts["t_c"], consts["dft_c"], fw)
    kt_s = _hyena_tables(L_S, consts["feat_s"], consts["t_s"], consts["dft_s"], fw)

    wts = _prep_weights(w_in, pool_w, pool_scale, hy_short_b, mla_q_norm, mla_kv_norm, mla_w_uq, mla_w_ukv,
                        w_out, router_w, norm1_g, norm2_g)
    krc = jnp.swapaxes(cache_krope, 0, 1).reshape(DEPTH, B_S * L_CTX, ROPE)
    krc = jnp.concatenate([krc, krc], axis=2).astype(BF16)

    ckv_list, kr_list = [], []
    for l in range(DEPTH):
        final_g = final_norm_g.reshape(1, D) if l == DEPTH - 1 else None
        x, ckv, kr = _layer(x, l, mod[l], wts, consts, kt_c, kt_s, hy_short_w, hy_skip, cache_ckv, krc[l],
                            exp_w_gate, exp_w_up, exp_w_down, final_g)
        ckv_list.append(ckv.reshape(B_CTX, L_CTX, KV_RANK))
        kr_list.append(kr[:, :ROPE].reshape(B_CTX, L_CTX, ROPE))

    y_prompt, y_sample = x
    return (y_prompt.reshape(B_CTX, L_CTX, D), y_sample.reshape(B_S, L_S, D),
            jnp.stack(ckv_list, axis=1), jnp.stack(kr_list, axis=1))
```

```python
import functools
import math

import numpy as np
import jax
import jax.numpy as jnp
from jax import lax
from jax.experimental import pallas as pl
from jax.experimental.pallas import tpu as pltpu

F32 = jnp.float32
BF16 = jnp.bfloat16

D = 2048
DEPTH = 2
B_CTX, L_CTX = 32, 256
B_S, L_S = 8, 1024
T_CTX = B_CTX * L_CTX
T = T_CTX + B_S * L_S
GRP = 1024
NG = T // GRP
NG_CTX = T_CTX // GRP
SEQ_PER_GRP = GRP // L_CTX
EPS = 1e-6
GRID_W = 64

POOL_W = 512
POOL_GC = 128
POOL_WINDOWS = (2, 4, 8, 16)
HY_W = 512
POS_BANDS = 8
FILT_HID = 64
H = 8
NOPE = 128
ROPE = 64
VD = 128
Q_RANK = 512
KV_RANK = 256
ROPE_THETA = 10000.0
IN_COLS = 2880
IN_PAD = 3072
E = 16
FF = 1024
CAP_CTX = 2 * L_CTX // E
CAP_S = 2 * L_S // E
SLOTS = GRP * 2 // E
ROWS_E = NG * SLOTS
LANES = 128
NCH = 512
ATT_SCALE = 1.0 / math.sqrt(NOPE + ROPE)
MB = 1024 * 1024


def _cp(sem, vmem_mb=48):
    return pltpu.CompilerParams(dimension_semantics=sem, vmem_limit_bytes=vmem_mb * MB)


def _rms(x, g):
    return x * lax.rsqrt(jnp.mean(x * x, axis=-1, keepdims=True) + EPS) * g


def _dot(a, b):
    return jnp.dot(a, b, preferred_element_type=F32)


def _dot_nt(a, b):
    return lax.dot_general(a, b, (((1,), (1,)), ((), ())), preferred_element_type=F32)


def _mod_kernel(c_ref, w_ref, b_ref, o_ref):
    c = c_ref[...]
    a = (c * jax.nn.sigmoid(c)).astype(BF16)
    o_ref[...] = _dot(a, w_ref[...].astype(BF16)) + b_ref[...]


def _adaln_mod(c16, ada_w, ada_b):
    tn = 1024
    return pl.pallas_call(
        _mod_kernel,
        grid=(DEPTH, 6 * D // tn),
        in_specs=[
            pl.BlockSpec((NG, D), lambda l, j: (0, 0)),
            pl.BlockSpec((None, D, tn), lambda l, j: (l, 0, j)),
            pl.BlockSpec((None, 1, tn), lambda l, j: (l, 0, j)),
        ],
        out_specs=pl.BlockSpec((None, NG, tn), lambda l, j: (l, 0, j)),
        out_shape=jax.ShapeDtypeStruct((DEPTH, NG, 6 * D), F32),
        compiler_params=_cp(("parallel", "parallel")),
        name="adaln_mod",
    )(c16, ada_w, ada_b.reshape(DEPTH, 1, 6 * D))


def _src_specs(x, tm):
    n_ctx = T_CTX // tm
    n_all = T // tm
    if isinstance(x, tuple):
        return list(x), [
            pl.BlockSpec((tm, D), lambda i: (jnp.minimum(i, n_ctx - 1), 0)),
            pl.BlockSpec((tm, D), lambda i: (jnp.clip(i - n_ctx, 0, n_all - n_ctx - 1), 0)),
        ]
    return [x], [pl.BlockSpec((tm, D), lambda i: (jnp.minimum(i, n_all - 1), 0))]


def _load_src(x_refs, tm, cs=slice(None)):
    if len(x_refs) == 1:
        return x_refs[0][:, cs]
    return jnp.where(pl.program_id(0) < T_CTX // tm, x_refs[0][:, cs], x_refs[1][:, cs])


def _resident(shape, layer):
    nd = len(shape)
    return pl.BlockSpec((None,) + shape, lambda i: (layer,) + (0,) * nd, pipeline_mode=pl.Buffered(1))


def _mod_row(k, per):
    return pl.BlockSpec((None, 1, D), lambda i: (6 * (i // per) + k, 0, 0))


N_U = POOL_W + 3 * HY_W
N_QR = H * ROPE


def _inproj_kernel(*refs, tm):
    (*x_refs, g_ref, shift_ref, scale_ref, wm_ref, wx_ref, qg_ref, wq_ref, csq_ref, kvg_ref, csk_ref, wkv_ref,
     up_ref, uv_ref, u1_ref, u2_ref, q_ref, ckv_ref, kr_ref, krr_ref, kv_ref) = refs

    y = _rms(_load_src(x_refs, tm), g_ref[...])
    h = (y * (1.0 + scale_ref[...]) + shift_ref[...]).astype(BF16)
    for c, u_ref in enumerate((up_ref, uv_ref, u1_ref, u2_ref)):
        u_ref[...] = _dot(h, wm_ref[:, c * HY_W:(c + 1) * HY_W]).astype(u_ref.dtype)

    qn = _rms(_dot(h, wm_ref[:, N_U:]), qg_ref[...]).astype(BF16)
    nq = H * NOPE
    for c in range(nq // NCH):
        cs = slice(c * NCH, (c + 1) * NCH)
        q_ref[:, cs] = (_dot(qn, wq_ref[:, cs]) * ATT_SCALE).astype(q_ref.dtype)
    rot = _dot(qn, wq_ref[:, nq:nq + N_QR]) * csq_ref[0] + _dot(qn, wq_ref[:, nq + N_QR:]) * csq_ref[1]
    q_ref[:, nq:] = (rot * ATT_SCALE).astype(q_ref.dtype)

    kx = _dot(h, wx_ref[...])
    ckv = _rms(kx[:, :KV_RANK], kvg_ref[...])
    kr = kx[:, KV_RANK:KV_RANK + LANES]

    @pl.when(pl.program_id(0) < T_CTX // tm)
    def _():
        ckv_ref[...] = ckv
        kr_ref[...] = kr

    krr_ref[...] = (kr * csk_ref[0] + kx[:, KV_RANK + LANES:] * csk_ref[1]).astype(krr_ref.dtype)
    kv_ref[...] = _dot(ckv.astype(BF16), wkv_ref[...]).astype(kv_ref.dtype)


def _inproj(x, mod3, wts, consts, layer):
    tm = 512
    per = GRP // tm
    xs, x_specs = _src_specs(x, tm)

    def kind(i):
        return jnp.where(i >= T_CTX // tm, 1, 0)

    def rows(width):
        return pl.BlockSpec((tm, width), lambda i: (i, 0))

    def ctx_rows(width):
        return pl.BlockSpec((tm, width), lambda i: (jnp.minimum(i, T_CTX // tm - 1), 0))

    nkv = H * (NOPE + VD)
    return pl.pallas_call(
        functools.partial(_inproj_kernel, tm=tm),
        grid=(T // tm,),
        in_specs=x_specs + [
            _resident((1, D), layer), _mod_row(0, per), _mod_row(1, per),
            _resident((D, N_U + Q_RANK), layer),
            _resident((D, 4 * LANES), layer),
            _resident((1, Q_RANK), layer),
            _resident((Q_RANK, H * (NOPE + 2 * ROPE)), layer),
            pl.BlockSpec((None, 2, tm, N_QR), lambda i: (kind(i), 0, i % per, 0)),
            _resident((1, KV_RANK), layer),
            pl.BlockSpec((None, 2, tm, LANES), lambda i: (kind(i), 0, i % per, 0)),
            _resident((KV_RANK, nkv), layer),
        ],
        out_specs=[rows(HY_W)] * 4 + [rows(H * (NOPE + ROPE)), ctx_rows(KV_RANK), ctx_rows(LANES), rows(LANES),
                                      rows(nkv)],
        out_shape=[jax.ShapeDtypeStruct((T, HY_W), BF16)] * 4 + [
            jax.ShapeDtypeStruct((T, H * (NOPE + ROPE)), BF16),
            jax.ShapeDtypeStruct((T_CTX, KV_RANK), F32),
            jax.ShapeDtypeStruct((T_CTX, LANES), F32),
            jax.ShapeDtypeStruct((T, LANES), BF16),
            jax.ShapeDtypeStruct((T, nkv), BF16),
        ],
        compiler_params=_cp(("arbitrary",), 58),
        name="in_proj",
    )(*xs, wts["g1"], mod3, mod3, wts["w_main"], wts["w_x"], wts["qg"], wts["wq"], consts["cs_q"],
      wts["kvg"], consts["cs_k"], wts["wkv"])


def _pool_kernel(u_ref, ic_ref, is_ref, pw_ref, ps_ref, o_ref):
    g = pl.program_id(0)

    def seq(r0, n, inv_ref):
        row = lax.broadcasted_iota(jnp.int32, (n, POOL_GC), 0)

        def later(x, m):
            return jnp.where(row < n - m, pltpu.roll(x, n - m, 0), 0.0)

        def earlier(x, m):
            return jnp.where(row >= m, pltpu.roll(x, m, 0), 0.0)

        for k, w in enumerate(POOL_WINDOWS):
            cs = slice(k * POOL_GC, (k + 1) * POOL_GC)
            u = u_ref[r0:r0 + n, cs].astype(F32)
            ahead, behind, m = u, earlier(u, 1), 1
            while m < w // 2:
                ahead = ahead + later(ahead, m)
                behind = behind + earlier(behind, m)
                m *= 2
            pooled = (ahead + behind) * inv_ref[:, cs] - u
            y = _dot(pooled.astype(BF16), pw_ref[k]) * ps_ref[:, cs]
            o_ref[r0:r0 + n, cs] = y.astype(o_ref.dtype)

    @pl.when(g < NG_CTX)
    def _():
        for s in range(SEQ_PER_GRP):
            seq(s * L_CTX, L_CTX, ic_ref)

    @pl.when(g >= NG_CTX)
    def _():
        seq(0, L_S, is_ref)


def _pool(proj, consts, pw, ps):
    nw = len(POOL_WINDOWS)
    return pl.pallas_call(
        _pool_kernel,
        grid=(NG,),
        in_specs=[
            pl.BlockSpec((GRP, POOL_W), lambda g: (g, 0)),
            pl.BlockSpec((L_CTX, POOL_W), lambda g: (0, 0)),
            pl.BlockSpec((L_S, POOL_W), lambda g: (0, 0)),
            pl.BlockSpec((nw, POOL_GC, POOL_GC), lambda g: (0, 0, 0)),
            pl.BlockSpec((1, POOL_W), lambda g: (0, 0)),
        ],
        out_specs=pl.BlockSpec((GRP, POOL_W), lambda g: (g, 0)),
        out_shape=jax.ShapeDtypeStruct((T, POOL_W), BF16),
        compiler_params=_cp(("parallel",)),
        name="pool_mix",
    )(proj, consts["inv_c"], consts["inv_s"], pw, ps)


HY_CH = HY_W


def _hyena_kernel(v_ref, x1_ref, x2_ref, swv_ref, sw1_ref, sw2_ref, sbv_ref, sb1_ref, sb2_ref,
                  skip_ref, wc_ref, wtc_ref, ws_ref, wts_ref, ktc_ref, kts_ref, o_ref):
    g = pl.program_id(0)

    def sconv(u_ref, sw_ref, sb_ref, r0, n, cs):
        u = u_ref[r0:r0 + n, cs].astype(F32)
        row = lax.broadcasted_iota(jnp.int32, u.shape, 0)
        prev = jnp.where(row == 0, 0.0, pltpu.roll(u, 1, 0))
        nxt = jnp.where(row == n - 1, 0.0, pltpu.roll(u, n - 1, 0))
        return prev * sw_ref[0:1, cs] + u * sw_ref[1:2, cs] + nxt * sw_ref[2:3, cs] + sb_ref[:, cs]

    def lconv(u, o, n, cs, w_ref, wt_ref, kt_ref):
        spec = _dot(w_ref[...], u.astype(BF16))
        pr, pi = spec[:n], spec[n:]
        ka, kb, ka2 = kt_ref[o, 0, :, cs], kt_ref[o, 1, :, cs], kt_ref[o, 2, :, cs]
        yr = pr * ka - pi * kb
        yi = pr * kb + pi * ka2
        prod = jnp.concatenate([yr, yi], axis=0).astype(BF16)
        return _dot(wt_ref[...], prod) + u * skip_ref[o:o + 1, cs]

    def seq(r0, n, cs, w_ref, wt_ref, kt_ref):
        v = sconv(v_ref, swv_ref, sbv_ref, r0, n, cs)
        x1 = sconv(x1_ref, sw1_ref, sb1_ref, r0, n, cs)
        x2 = sconv(x2_ref, sw2_ref, sb2_ref, r0, n, cs)
        z = x1 * lconv(v, 0, n, cs, w_ref, wt_ref, kt_ref)
        y = x2 * lconv(z, 1, n, cs, w_ref, wt_ref, kt_ref)
        o_ref[r0:r0 + n, cs] = y.astype(o_ref.dtype)

    chunks = [slice(c * HY_CH, (c + 1) * HY_CH) for c in range(HY_W // HY_CH)]

    @pl.when(g < NG_CTX)
    def _():
        for s in range(SEQ_PER_GRP):
            for cs in chunks:
                seq(s * L_CTX, L_CTX, cs, wc_ref, wtc_ref, ktc_ref)

    @pl.when(g >= NG_CTX)
    def _():
        for cs in chunks:
            seq(0, L_S, cs, ws_ref, wts_ref, kts_ref)


def _hyena(uv, u1, u2, sw, sb, skip, consts, kt_c, kt_s, layer):
    def part(p):
        return pl.BlockSpec((GRP, HY_W), lambda g: (g, 0))

    def swpart(p):
        return pl.BlockSpec((3, HY_W), lambda g: (0, p))

    def sbpart(p):
        return pl.BlockSpec((1, HY_W), lambda g: (0, p))

    def tables(n):
        return pl.BlockSpec((None, 2, 3, n, HY_W), lambda g: (layer, 0, 0, 0, 0), pipeline_mode=pl.Buffered(1))

    return pl.pallas_call(
        _hyena_kernel,
        grid=(NG,),
        in_specs=[
            part(0), part(1), part(2),
            swpart(0), swpart(1), swpart(2),
            sbpart(0), sbpart(1), sbpart(2),
            pl.BlockSpec((2, HY_W), lambda g: (0, 0)),
            pl.BlockSpec((2 * L_CTX, L_CTX), lambda g: (0, 0)),
            pl.BlockSpec((L_CTX, 2 * L_CTX), lambda g: (0, 0)),
            pl.BlockSpec((2 * L_S, L_S), lambda g: (0, 0)),
            pl.BlockSpec((L_S, 2 * L_S), lambda g: (0, 0)),
            tables(L_CTX), tables(L_S),
        ],
        out_specs=pl.BlockSpec((GRP, HY_W), lambda g: (g, 0)),
        out_shape=jax.ShapeDtypeStruct((T, HY_W), BF16),
        compiler_params=_cp(("parallel",), 56),
        name="hyena_mix",
    )(uv, u1, u2, sw, sw, sw, sb, sb, sb, skip,
      consts["dft_c"], consts["dftt_c"], consts["dft_s"], consts["dftt_s"], kt_c, kt_s)


def _cachekv_kernel(x_ref, w_ref, o_ref):
    o_ref[...] = _dot(x_ref[...].astype(BF16), w_ref[...]).astype(o_ref.dtype)


def _cachekv(cache_ckv, layer, wkv):
    return pl.pallas_call(
        _cachekv_kernel,
        grid=(B_S,),
        in_specs=[
            pl.BlockSpec((None, None, L_CTX, KV_RANK), lambda b: (b, layer, 0, 0)),
            pl.BlockSpec((None, KV_RANK, H * (NOPE + VD)), lambda b: (layer, 0, 0)),
        ],
        out_specs=pl.BlockSpec((L_CTX, H * (NOPE + VD)), lambda b: (b, 0)),
        out_shape=jax.ShapeDtypeStruct((B_S * L_CTX, H * (NOPE + VD)), BF16),
        compiler_params=_cp(("parallel",)),
        name="cache_kv",
    )(cache_ckv, wkv)


ATT_TQ = 512
NK_S = L_S + L_CTX


def _attn_kernel(q_ref, kv_ref, krr_ref, kvc_ref, krc_ref, o_ref, kcat_ref):
    g = pl.program_id(0)
    lane = lax.broadcasted_iota(jnp.int32, (1, LANES), 1)
    hk = NOPE + LANES
    vo = H * NOPE

    krr = krr_ref[...]
    for h in range(H):
        kcat_ref[0:GRP, h * hk:h * hk + NOPE] = kv_ref[:, h * NOPE:(h + 1) * NOPE]
        kcat_ref[0:GRP, h * hk + NOPE:(h + 1) * hk] = krr

    def qcat(rows, h):
        qn = q_ref[rows, h * NOPE:(h + 1) * NOPE]
        pair = q_ref[rows, vo + (h // 2) * LANES:vo + (h // 2 + 1) * LANES].astype(F32)
        keep = (lane < ROPE) if h % 2 == 0 else (lane >= ROPE)
        return jnp.concatenate([qn, jnp.where(keep, pair, 0.0).astype(BF16)], axis=1)

    def probs(sc):
        m = jnp.max(sc, axis=-1, keepdims=True)
        p = jnp.exp(sc - m)
        return p.astype(BF16), 1.0 / jnp.sum(p, axis=-1, keepdims=True)

    @pl.when(g < NG_CTX)
    def _():
        def body(s, carry):
            rows = pl.ds(pl.multiple_of(s * L_CTX, L_CTX), L_CTX)
            for h in range(H):
                p, rl = probs(_dot_nt(qcat(rows, h), kcat_ref[rows, h * hk:(h + 1) * hk]))
                o = _dot(p, kv_ref[rows, vo + h * VD:vo + (h + 1) * VD]) * rl
                o_ref[rows, h * VD:(h + 1) * VD] = o.astype(o_ref.dtype)
            return carry

        lax.fori_loop(0, SEQ_PER_GRP, body, 0)

    @pl.when(g >= NG_CTX)
    def _():
        krc = krc_ref[...]
        for h in range(H):
            kcat_ref[GRP:NK_S, h * hk:h * hk + NOPE] = kvc_ref[:, h * NOPE:(h + 1) * NOPE]
            kcat_ref[GRP:NK_S, h * hk + NOPE:(h + 1) * hk] = krc

        def body(t, carry):
            rows = pl.ds(pl.multiple_of(t * ATT_TQ, ATT_TQ), ATT_TQ)
            for h in range(H):
                p, rl = probs(_dot_nt(qcat(rows, h), kcat_ref[:, h * hk:(h + 1) * hk]))
                o = _dot(p[:, :GRP], kv_ref[:, vo + h * VD:vo + (h + 1) * VD])
                o = o + _dot(p[:, GRP:], kvc_ref[:, vo + h * VD:vo + (h + 1) * VD])
                o_ref[rows, h * VD:(h + 1) * VD] = (o * rl).astype(o_ref.dtype)
            return carry

        lax.fori_loop(0, L_S // ATT_TQ, body, 0)


def _attention(q, kv, krr, kvc, krc):
    def cache_blk(g):
        return jnp.maximum(g - NG_CTX, 0)

    return pl.pallas_call(
        _attn_kernel,
        grid=(NG,),
        in_specs=[
            pl.BlockSpec((GRP, H * (NOPE + ROPE)), lambda g: (g, 0)),
            pl.BlockSpec((GRP, H * (NOPE + VD)), lambda g: (g, 0)),
            pl.BlockSpec((GRP, LANES), lambda g: (g, 0)),
            pl.BlockSpec((L_CTX, H * (NOPE + VD)), lambda g: (cache_blk(g), 0)),
            pl.BlockSpec((L_CTX, LANES), lambda g: (cache_blk(g), 0)),
        ],
        out_specs=pl.BlockSpec((GRP, H * VD), lambda g: (g, 0)),
        out_shape=jax.ShapeDtypeStruct((T, H * VD), BF16),
        scratch_shapes=[pltpu.VMEM((NK_S, H * (NOPE + LANES)), BF16)],
        compiler_params=_cp(("parallel",)),
        name="mla_attention",
    )(q, kv, krr, kvc, krc)


def _wout_kernel(yp_ref, yh_ref, ym_ref, w_ref, gate_ref, g2_ref, shift_ref, scale_ref, rw_ref, *refs, tm):
    *x_refs, o_ref, h_ref, aff_ref, afft_ref, prev_ref = refs

    @pl.when(pl.program_id(0) == 0)
    def _():
        prev_ref[...] = jnp.zeros_like(prev_ref)

    hb = (_rms(prev_ref[...], g2_ref[...]) * (1.0 + scale_ref[...]) + shift_ref[...]).astype(BF16)
    h_ref[...] = hb
    logits = _dot(hb, rw_ref[...])
    lane = lax.broadcasted_iota(jnp.int32, logits.shape, 1)
    logits = jnp.where(lane < E, logits, -jnp.inf)
    ex = jnp.exp(logits - jnp.max(logits, axis=-1, keepdims=True))
    aff = ex / jnp.sum(ex, axis=-1, keepdims=True)
    aff_ref[...] = aff
    afft_ref[...] = aff.T[:E]

    y = jnp.concatenate([yp_ref[...], yh_ref[...], ym_ref[...]], axis=1)
    for c in range(D // NCH):
        cs = slice(c * NCH, (c + 1) * NCH)
        v = _load_src(x_refs, tm, cs) + gate_ref[:, cs] * _dot(y, w_ref[:, cs])
        o_ref[:, cs] = v.astype(o_ref.dtype)
        prev_ref[:, cs] = v


def _wout(yp, yh, ym, x, mod3, wts, layer):
    tm = 512
    per = GRP // tm
    n = T // tm
    xs, x_specs = _src_specs(x, tm)

    def cur(i):
        return jnp.minimum(i, n - 1)

    def prev(i):
        return jnp.maximum(i - 1, 0)

    def rows(width, blk):
        return pl.BlockSpec((tm, width), lambda i: (blk(i), 0))

    def mod_row(k, blk):
        return pl.BlockSpec((None, 1, D), lambda i: (6 * (blk(i) // per) + k, 0, 0))

    return pl.pallas_call(
        functools.partial(_wout_kernel, tm=tm),
        grid=(n + 1,),
        in_specs=[
            rows(POOL_W, cur), rows(HY_W, cur), rows(H * VD, cur),
            _resident((D, D), layer),
            mod_row(2, cur),
            _resident((1, D), layer), mod_row(3, prev), mod_row(4, prev),
            _resident((D, LANES), layer),
        ] + x_specs,
        out_specs=[rows(D, cur), rows(D, prev), rows(LANES, prev), pl.BlockSpec((E, tm), lambda i: (0, prev(i)))],
        out_shape=[
            jax.ShapeDtypeStruct((T, D), BF16),
            jax.ShapeDtypeStruct((T, D), BF16),
            jax.ShapeDtypeStruct((T, LANES), F32),
            jax.ShapeDtypeStruct((E, T), F32),
        ],
        scratch_shapes=[pltpu.VMEM((tm, D), F32)],
        compiler_params=_cp(("arbitrary",), 56),
        name="out_proj",
    )(yp, yh, ym, wts["w_out"], mod3, wts["g2"], mod3, mod3, wts["rw"], *xs)


RANK_CH = 256
SEARCH_ROUNDS = 14
SEARCH_WAYS = 8
AFF_MAX = 2.0


def _rank_kernel(aff_ref, afft_ref, tri_ref, rank_ref, rank_t_ref, cnt_ref, cut_ref):
    g = pl.program_id(0)

    def count_ge(a, t):
        return jnp.sum(jnp.where(a >= t, 1.0, 0.0), axis=1, keepdims=True)

    def search(specs, capf):
        acts = [afft_ref[:, r0:r0 + n] for r0, n in specs]
        lo = [jnp.zeros((E, 1), F32) for _ in specs]
        hi = [jnp.full((E, 1), AFF_MAX, F32) for _ in specs]
        for _ in range(SEARCH_ROUNDS):
            for s, a in enumerate(acts):
                step = (hi[s] - lo[s]) * (1.0 / SEARCH_WAYS)
                ts = [lo[s] + step * k for k in range(1, SEARCH_WAYS)]
                ok = [count_ge(a, t) >= capf for t in ts]
                new_lo, new_hi = lo[s], hi[s]
                for t, o in zip(ts, ok):
                    new_lo = jnp.where(o, t, new_lo)
                for t, o in zip(reversed(ts), reversed(ok)):
                    new_hi = jnp.where(o, new_hi, t)
                lo[s], hi[s] = new_lo, new_hi
        open_brackets = jnp.zeros((E, 1), F32)
        for s, a in enumerate(acts):
            top = jnp.max(jnp.where(a >= lo[s], jnp.where(a < hi[s], a, -1.0), -1.0), axis=1, keepdims=True)
            low = jnp.min(jnp.where(a >= lo[s], jnp.where(a < hi[s], a, AFF_MAX), AFF_MAX), axis=1, keepdims=True)
            cut_ref[:, s:s + 1] = top
            open_brackets = open_brackets + jnp.where(top != low, 1.0, 0.0)
        return jnp.sum(open_brackets)

    def exact_cut(s, r0, n, capf):
        for e in range(E):
            row = afft_ref[e:e + 1, r0:r0 + n]
            acc = jnp.zeros((1, n), F32)
            for c in range(n // RANK_CH):
                col = aff_ref[r0 + c * RANK_CH:r0 + (c + 1) * RANK_CH, e:e + 1]
                acc = acc + jnp.sum(jnp.where(col >= row, 1.0, 0.0), axis=0, keepdims=True)
            cnt_ref[e:e + 1, 0:n] = acc
        a = afft_ref[:, r0:r0 + n]
        cut_ref[:, s:s + 1] = jnp.max(jnp.where(cnt_ref[:, 0:n] >= capf, a, -1.0), axis=1, keepdims=True)

    def slots(s, r0, n, capf):
        a = afft_ref[:, r0:r0 + n]
        cut = cut_ref[:, s:s + 1]
        above = a > cut
        tied = a == cut
        above_f = jnp.where(above, 1.0, 0.0)
        n_above = jnp.sum(above_f, axis=1, keepdims=True)
        marks = jnp.concatenate([above_f, jnp.where(tied, 1.0, 0.0)], axis=0).astype(BF16)
        before = _dot(marks, tri_ref[0:n, 0:n])
        tie_slot = n_above + before[E:]
        slot = jnp.where(above, before[:E], jnp.where(tied, jnp.where(tie_slot < capf, tie_slot, n), n))
        rank_ref[:, r0:r0 + n] = slot.astype(jnp.int32)
        wide = jnp.concatenate([slot, jnp.zeros((LANES - E, n), F32)], axis=0)
        rank_t_ref[r0:r0 + n, :] = wide.T.astype(jnp.int32)

    def group(specs, cap):
        capf = float(cap)
        unresolved = search(specs, capf)

        @pl.when(unresolved > 0.0)
        def _():
            for s, (r0, n) in enumerate(specs):
                exact_cut(s, r0, n, capf)

        for s, (r0, n) in enumerate(specs):
            slots(s, r0, n, capf)

    @pl.when(g < NG_CTX)
    def _():
        group([(s * L_CTX, L_CTX) for s in range(SEQ_PER_GRP)], CAP_CTX)

    @pl.when(g >= NG_CTX)
    def _():
        group([(0, L_S)], CAP_S)


def _rank(aff, afft, tri):
    return pl.pallas_call(
        _rank_kernel,
        grid=(NG,),
        in_specs=[
            pl.BlockSpec((GRP, LANES), lambda g: (g, 0)),
            pl.BlockSpec((E, GRP), lambda g: (0, g)),
            pl.BlockSpec((L_S, L_S), lambda g: (0, 0)),
        ],
        out_specs=[pl.BlockSpec((E, GRP), lambda g: (0, g)), pl.BlockSpec((GRP, LANES), lambda g: (g, 0))],
        out_shape=[jax.ShapeDtypeStruct((E, T), jnp.int32), jax.ShapeDtypeStruct((T, LANES), jnp.int32)],
        scratch_shapes=[pltpu.VMEM((E, GRP), F32), pltpu.VMEM((E, LANES), F32)],
        compiler_params=_cp(("parallel",)),
        name="moe_rank",
    )(aff, afft, tri)


def _gather_kernel(rank_ref, afft_ref, h_ref, xs_ref, gs_ref, sel_ref):
    g = pl.program_id(0)

    def seq(r0, n, cap, slot0):
        slot_i = lax.broadcasted_iota(jnp.int32, (cap, n), 0)
        for e in range(E):
            hit = slot_i == rank_ref[e:e + 1, r0:r0 + n]
            sel_ref[e * cap:(e + 1) * cap, 0:n] = jnp.where(hit, 1.0, 0.0).astype(BF16)
            gs_ref[e, slot0:slot0 + cap, :] = jnp.sum(
                jnp.where(hit, afft_ref[e:e + 1, r0:r0 + n], 0.0), axis=1, keepdims=True)
        for c in range(D // NCH):
            cs = slice(c * NCH, (c + 1) * NCH)
            res = _dot(sel_ref[0:E * cap, 0:n], h_ref[r0:r0 + n, cs])
            for e in range(E):
                xs_ref[e, slot0:slot0 + cap, cs] = res[e * cap:(e + 1) * cap].astype(xs_ref.dtype)

    @pl.when(g < NG_CTX)
    def _():
        for s in range(SEQ_PER_GRP):
            seq(s * L_CTX, L_CTX, CAP_CTX, s * CAP_CTX)

    @pl.when(g >= NG_CTX)
    def _():
        seq(0, L_S, CAP_S, 0)


def _gather(rank, afft, h2):
    return pl.pallas_call(
        _gather_kernel,
        grid=(NG,),
        in_specs=[
            pl.BlockSpec((E, GRP), lambda g: (0, g)),
            pl.BlockSpec((E, GRP), lambda g: (0, g)),
            pl.BlockSpec((GRP, D), lambda g: (g, 0)),
        ],
        out_specs=[
            pl.BlockSpec((E, SLOTS, D), lambda g: (0, g, 0)),
            pl.BlockSpec((E, SLOTS, 1), lambda g: (0, g, 0)),
        ],
        out_shape=[
            jax.ShapeDtypeStruct((E, ROWS_E, D), BF16),
            jax.ShapeDtypeStruct((E, ROWS_E, 1), F32),
        ],
        scratch_shapes=[pltpu.VMEM((E * CAP_S, L_S), BF16)],
        compiler_params=_cp(("parallel",)),
        name="moe_gather",
    )(rank, afft, h2)


FFN_TF = 256
FFN_TM = 1024


def _ffn_kernel(x_ref, wg_ref, wu_ref, wd_ref, gs_ref, o_ref, acc_ref):
    f = pl.program_id(2)
    nf = pl.num_programs(2)

    def partial_down():
        x = x_ref[...]
        a = _dot(x, wg_ref[...].astype(BF16))
        b = _dot(x, wu_ref[...].astype(BF16))
        hid = (a * jax.nn.sigmoid(a) * b).astype(BF16)
        return _dot(hid, wd_ref[...].astype(BF16))

    @pl.when(f == 0)
    def _():
        acc_ref[...] = partial_down()

    @pl.when((f > 0) & (f < nf - 1))
    def _():
        acc_ref[...] += partial_down()

    @pl.when(f == nf - 1)
    def _():
        o_ref[...] = ((acc_ref[...] + partial_down()) * gs_ref[...]).astype(o_ref.dtype)


def _ffn(xs, gs, w_gate, w_up, w_down, layer):
    return pl.pallas_call(
        _ffn_kernel,
        grid=(E, ROWS_E // FFN_TM, FF // FFN_TF),
        in_specs=[
            pl.BlockSpec((None, FFN_TM, D), lambda e, m, f: (e, m, 0)),
            pl.BlockSpec((None, None, D, FFN_TF), lambda e, m, f: (layer, e, 0, f)),
            pl.BlockSpec((None, None, D, FFN_TF), lambda e, m, f: (layer, e, 0, f)),
            pl.BlockSpec((None, None, FFN_TF, D), lambda e, m, f: (layer, e, f, 0)),
            pl.BlockSpec((None, FFN_TM, 1), lambda e, m, f: (e, m, 0)),
        ],
        out_specs=pl.BlockSpec((None, FFN_TM, D), lambda e, m, f: (e, m, 0)),
        out_shape=jax.ShapeDtypeStruct((E, ROWS_E, D), BF16),
        scratch_shapes=[pltpu.VMEM((FFN_TM, D), F32)],
        compiler_params=_cp(("parallel", "parallel", "arbitrary"), 56),
        name="moe_ffn",
    )(xs, w_gate, w_up, w_down, gs)


def _combine_kernel(rt_ref, ys_ref, x_ref, gate_ref, exc_ref, exs_ref, *refs, last, tm):
    if last:
        fg_ref, oc_ref, os_ref, st_ref = refs
    else:
        oc_ref, st_ref = refs
        os_ref = oc_ref
    g = pl.program_id(0)
    half = pl.program_id(1)

    def scatter(r0, n, cap, slot0, ex_ref, o_ref):
        ec = E * cap
        r = jnp.minimum(rt_ref[r0:r0 + n, :], cap).astype(F32).astype(BF16)
        want = (lax.broadcasted_iota(jnp.int32, (1, NCH), 1) & (cap - 1)).astype(F32)
        for c in range(ec // NCH):
            cs = slice(c * NCH, (c + 1) * NCH)
            st_ref[r0:r0 + n, cs] = jnp.where(_dot(r, ex_ref[:, cs]) == want, 1.0, 0.0).astype(BF16)
        for c in range(D // NCH):
            cs = slice(c * NCH, (c + 1) * NCH)
            ys = ys_ref[:, pl.ds(slot0, cap), cs].reshape(ec, NCH)
            moe = _dot(st_ref[r0:r0 + n, 0:ec], ys)
            o_ref[r0:r0 + n, cs] = x_ref[r0:r0 + n, cs].astype(F32) + gate_ref[:, cs] * moe

    def finish(o_ref):
        if last:
            o_ref[...] = _rms(o_ref[...], fg_ref[...])

    @pl.when(g < NG_CTX)
    def _():
        seqs = tm // L_CTX
        for s in range(seqs):
            slot0 = pl.multiple_of((half * seqs + s) * CAP_CTX, CAP_CTX)
            scatter(s * L_CTX, L_CTX, CAP_CTX, slot0, exc_ref, oc_ref)
        finish(oc_ref)

    @pl.when(g >= NG_CTX)
    def _():
        scatter(0, tm, CAP_S, 0, exs_ref, os_ref)
        finish(os_ref)


def _combine(rank_t, ys, x, mod3, consts, final_g=None):
    last = final_g is not None
    tm = GRP // 2 if last else GRP
    per = GRP // tm

    def row(g, i):
        return (g * per + i, 0)

    in_specs = [
        pl.BlockSpec((tm, LANES), row),
        pl.BlockSpec((E, SLOTS, D), lambda g, i: (0, g, 0)),
        pl.BlockSpec((tm, D), row),
        pl.BlockSpec((None, 1, D), lambda g, i: (6 * g + 5, 0, 0)),
        pl.BlockSpec((LANES, E * CAP_CTX), lambda g, i: (0, 0)),
        pl.BlockSpec((LANES, E * CAP_S), lambda g, i: (0, 0)),
    ]
    args = [rank_t, ys, x, mod3, consts["ex_c"], consts["ex_s"]]
    if last:
        n_ctx = T_CTX // tm
        in_specs.append(pl.BlockSpec((1, D), lambda g, i: (0, 0)))
        args.append(final_g)
        out_specs = [
            pl.BlockSpec((tm, D), lambda g, i: (jnp.where(g < NG_CTX, g * per + i, n_ctx - 1), 0)),
            pl.BlockSpec((tm, D), lambda g, i: (jnp.where(g < NG_CTX, 0, (g - NG_CTX) * per + i), 0)),
        ]
        out_shape = [jax.ShapeDtypeStruct((T_CTX, D), F32), jax.ShapeDtypeStruct((T - T_CTX, D), F32)]
    else:
        out_specs = pl.BlockSpec((tm, D), row)
        out_shape = jax.ShapeDtypeStruct((T, D), F32)
    return pl.pallas_call(
        functools.partial(_combine_kernel, last=last, tm=tm),
        grid=(NG, per),
        in_specs=in_specs,
        out_specs=out_specs,
        out_shape=out_shape,
        scratch_shapes=[pltpu.VMEM((tm, E * CAP_S), BF16)],
        compiler_params=_cp(("arbitrary", "arbitrary"), 56),
        name="moe_combine",
    )(*args)


def _np_constants():
    c = {}
    for tag, n in (("c", L_CTX), ("s", L_S)):
        t = np.arange(n)
        inv = np.zeros((n, POOL_W), np.float32)
        for k, w in enumerate(POOL_WINDOWS):
            lo = np.clip(t - w // 2, 0, n)
            hi = np.clip(t - w // 2 + w, 0, n)
            inv[:, k * POOL_GC:(k + 1) * POOL_GC] = (1.0 / (hi - lo).astype(np.float64))[:, None]
        c["inv_" + tag] = inv
        kk = np.arange(n, dtype=np.float64)[:, None]
        tt = np.arange(n, dtype=np.float64)[None, :]
        ang = np.pi * kk * tt / n
        dft = np.concatenate([np.cos(ang), -np.sin(ang)], axis=0)
        dft[n] = (-1.0) ** np.arange(n)
        c["dft_" + tag] = dft.astype(np.float32)
        c["dftt_" + tag] = np.ascontiguousarray(dft.T).astype(np.float32)
        tl = np.linspace(0.0, 1.0, n, dtype=np.float32)[:, None]
        bands = np.arange(1, POS_BANDS + 1, dtype=np.float32)[None, :]
        feats = np.concatenate([tl, np.sin(2 * np.pi * tl * bands), np.cos(2 * np.pi * tl * bands)], axis=1)
        c["feat_" + tag] = np.pad(feats.astype(np.float32), ((0, 0), (0, LANES - feats.shape[1])))
        c["t_" + tag] = tl
    n_rows = L_S // GRID_W
    row = np.repeat(np.arange(n_rows), GRID_W).astype(np.float32)
    col = np.tile(np.arange(GRID_W), n_rows).astype(np.float32)
    nf = ROPE // 4
    inv_f = (1.0 / ROPE_THETA ** (np.arange(nf, dtype=np.float32) / nf)).astype(np.float32)
    a_row = (row[:, None] * inv_f[None]).astype(np.float32).astype(np.float64)
    a_col = (col[:, None] * inv_f[None]).astype(np.float32).astype(np.float64)
    cos64 = np.concatenate([np.cos(a_row), np.cos(a_row), np.cos(a_col), np.cos(a_col)], axis=1)
    sin64 = np.concatenate([np.sin(a_row), np.sin(a_row), np.sin(a_col), np.sin(a_col)], axis=1)

    def table(reps):
        ident = np.stack([np.ones((L_S, ROPE * reps)), np.zeros((L_S, ROPE * reps))])
        rot = np.stack([np.tile(cos64, (1, reps)), np.tile(sin64, (1, reps))])
        return np.stack([ident, rot]).astype(np.float32)

    c["cs_q"] = table(H)
    c["cs_k"] = table(LANES // ROPE)
    c["tri"] = np.triu(np.ones((L_S, L_S), np.float32), k=1)
    for tag, cap in (("c", CAP_CTX), ("s", CAP_S)):
        ex = np.zeros((LANES, E * cap), np.float32)
        ex[np.arange(E * cap) // cap, np.arange(E * cap)] = 1.0
        c["ex_" + tag] = ex
    return c


def _constants():
    c = {k: jnp.asarray(v) for k, v in _np_constants().items()}
    for k in ("dft_c", "dftt_c", "dft_s", "dftt_s", "ex_c", "ex_s", "tri"):
        c[k] = c[k].astype(BF16)
    return c


def _rope_swap(w):
    q = ROPE // 4
    return jnp.concatenate([-w[..., q:2 * q], w[..., :q], -w[..., 3 * q:], w[..., 2 * q:3 * q]], axis=-1)


def _dot_hi(a, b):
    return jnp.dot(a, b, preferred_element_type=F32, precision=lax.Precision.HIGHEST)


def _filter_kernel(feat_ref, t_ref, w1_ref, b1_ref, w2_ref, b2_ref, fr_ref, w3f_ref, w3b_ref,
                   ldf_ref, ldb_ref, dft_ref, o_ref, z_ref):
    n = feat_ref.shape[0]

    @pl.when((pl.program_id(1) == 0) & (pl.program_id(2) == 0))
    def _():
        fr = fr_ref[...]
        z = jnp.sin(fr * (_dot_hi(feat_ref[...], w1_ref[...]) + b1_ref[...]))
        z_ref[...] = jnp.sin(fr * (_dot_hi(z, w2_ref[...]) + b2_ref[...]))

    z = z_ref[...]
    t = t_ref[...]
    first = lax.broadcasted_iota(jnp.int32, (n, 1), 0) == 0
    hf = _dot_hi(z, w3f_ref[...]) * jnp.exp(-jnp.exp(ldf_ref[...]) * t)
    hb = _dot_hi(z, w3b_ref[...]) * jnp.exp(-jnp.exp(ldb_ref[...]) * t)
    hb = jnp.where(first, 0.0, hb)
    norm = jnp.sum(jnp.abs(hf), axis=0, keepdims=True) + jnp.sum(jnp.abs(hb), axis=0, keepdims=True) + EPS
    both = _dot(dft_ref[...], jnp.concatenate([hf / norm, hb / norm], axis=1).astype(BF16))
    pf, pb = both[:, :hf.shape[1]], both[:, hf.shape[1]:]
    sc = jnp.where(first, 0.5 / n, 1.0 / n)
    ka = (pf[:n] + pb[:n]) * sc
    o_ref[0] = ka
    o_ref[1] = jnp.where(first, 0.0, (pf[n:] - pb[n:]) * sc)
    o_ref[2] = jnp.where(first, (pf[n:] + pb[n:]) * sc, ka)


def _hyena_tables(n, feats, tcol, dft, fw):
    tc = 256
    nc = HY_W // tc

    def lay(shape):
        return pl.BlockSpec((None,) + shape, lambda l, o, c: (l, 0, 0))

    def w3(back):
        return pl.BlockSpec((None, LANES, tc), lambda l, o, c: (l, 0, (2 * o + back) * nc + c))

    def ld(back):
        return pl.BlockSpec((None, 1, tc), lambda l, o, c: (l, 0, (2 * o + back) * nc + c))

    return pl.pallas_call(
        _filter_kernel,
        grid=(DEPTH, 2, nc),
        in_specs=[
            pl.BlockSpec((n, LANES), lambda l, o, c: (0, 0)),
            pl.BlockSpec((n, 1), lambda l, o, c: (0, 0)),
            lay((LANES, LANES)), lay((1, LANES)), lay((LANES, LANES)), lay((1, LANES)), lay((1, LANES)),
            w3(0), w3(1), ld(0), ld(1),
            pl.BlockSpec((2 * n, n), lambda l, o, c: (0, 0)),
        ],
        out_specs=pl.BlockSpec((None, None, 3, n, tc), lambda l, o, c: (l, o, 0, 0, c)),
        out_shape=jax.ShapeDtypeStruct((DEPTH, 2, 3, n, HY_W), F32),
        scratch_shapes=[pltpu.VMEM((n, LANES), F32)],
        compiler_params=_cp(("arbitrary", "arbitrary", "arbitrary")),
        name="hyena_filter",
    )(feats, tcol, fw["w1"], fw["b1"], fw["w2"], fw["b2"], fw["freq"], fw["w3"], fw["w3"],
      fw["ld"], fw["ld"], dft)


def _filter_weights(w1, b1, w2, b2, w3, freq, log_decay):
    ph = LANES - FILT_HID
    return dict(
        w1=jnp.pad(w1, ((0, 0), (0, LANES - w1.shape[1]), (0, ph))),
        b1=jnp.pad(b1, ((0, 0), (0, ph))).reshape(DEPTH, 1, LANES),
        w2=jnp.pad(w2, ((0, 0), (0, ph), (0, ph))),
        b2=jnp.pad(b2, ((0, 0), (0, ph))).reshape(DEPTH, 1, LANES),
        freq=jnp.pad(freq, ((0, 0), (0, ph))).reshape(DEPTH, 1, LANES),
        w3=jnp.pad(w3, ((0, 0), (0, ph), (0, 0))),
        ld=log_decay.reshape(DEPTH, 1, 4 * HY_W))


def _prep_weights(w_in, pool_w, pool_scale, hy_short_b, mla_q_norm, mla_kv_norm, mla_w_uq, mla_w_ukv,
                  w_out, router_w, norm1_g, norm2_g):
    n_main = N_U + Q_RANK
    kr_cols = w_in[:, :, IN_COLS - ROPE:]
    kr_swap = _rope_swap(kr_cols)
    w_x = jnp.concatenate([w_in[:, :, n_main:IN_COLS - ROPE], kr_cols, kr_cols, kr_swap, kr_swap], axis=2)
    wq = mla_w_uq.reshape(DEPTH, Q_RANK, H, NOPE + ROPE)
    wq_rope = wq[..., NOPE:]
    wq = jnp.concatenate([wq[..., :NOPE].reshape(DEPTH, Q_RANK, -1), wq_rope.reshape(DEPTH, Q_RANK, -1),
                          _rope_swap(wq_rope).reshape(DEPTH, Q_RANK, -1)], axis=2)
    wkv = mla_w_ukv.reshape(DEPTH, KV_RANK, H, NOPE + VD)
    wkv = jnp.concatenate([wkv[..., :NOPE].reshape(DEPTH, KV_RANK, -1), wkv[..., NOPE:].reshape(DEPTH, KV_RANK, -1)],
                          axis=2)
    return dict(
        w_main=w_in.astype(BF16), w_x=w_x.astype(BF16), wq=wq.astype(BF16), wkv=wkv.astype(BF16),
        w_out=w_out.astype(BF16), rw=jnp.pad(router_w, ((0, 0), (0, 0), (0, LANES - E))).astype(BF16),
        pool_w=pool_w.astype(BF16), pool_scale=pool_scale.reshape(DEPTH, 1, POOL_W),
        sb=hy_short_b.reshape(DEPTH, 1, 3 * HY_W),
        qg=mla_q_norm.reshape(DEPTH, 1, Q_RANK), kvg=mla_kv_norm.reshape(DEPTH, 1, KV_RANK),
        g1=norm1_g.reshape(DEPTH, 1, D), g2=norm2_g.reshape(DEPTH, 1, D))


def _layer(x, layer, mod3, wts, consts, kt_c, kt_s, hy_short_w, hy_skip, cache_ckv, krc,
           exp_w_gate, exp_w_up, exp_w_down, final_g):
    up, uv, u1, u2, q, ckv, kr, krr, kv = _inproj(x, mod3, wts, consts, layer)
    y_pool = _pool(up, consts, wts["pool_w"][layer], wts["pool_scale"][layer])
    y_hy = _hyena(uv, u1, u2, hy_short_w[layer], wts["sb"][layer], hy_skip[layer], consts, kt_c, kt_s, layer)
    kvc = _cachekv(cache_ckv, layer, wts["wkv"])
    y_mla = _attention(q, kv, krr, kvc, krc)
    x, h2, aff, afft = _wout(y_pool, y_hy, y_mla, x, mod3, wts, layer)
    rank, rank_t = _rank(aff, afft, consts["tri"])
    xs, gs = _gather(rank, afft, h2)
    ys = _ffn(xs, gs, exp_w_gate, exp_w_up, exp_w_down, layer)
    return _combine(rank_t, ys, x, mod3, consts, final_g), ckv, kr


def kernel(x_prompt, x_sample, cache_ckv, cache_krope, c, c_ctx, ada_w, ada_b, norm1_g, norm2_g, w_in, pool_w, pool_scale, hy_short_w, hy_short_b, hy_ffn_w1, hy_ffn_b1, hy_ffn_w2, hy_ffn_b2, hy_ffn_w3, hy_freq, hy_log_decay, hy_skip, mla_q_norm, mla_kv_norm, mla_w_uq, mla_w_ukv, w_out, router_w, exp_w_gate, exp_w_up, exp_w_down, final_norm_g):
    consts = _constants()
    x = (x_prompt.reshape(T_CTX, D), x_sample.reshape(T - T_CTX, D))
    c16 = jnp.concatenate([jnp.broadcast_to(c_ctx[None], (NG_CTX, D)), c], axis=0)
    mod = _adaln_mod(c16, ada_w, ada_b).reshape(DEPTH, NG * 6, 1, D)

    fw = _filter_weights(hy_ffn_w1, hy_ffn_b1, hy_ffn_w2, hy_ffn_b2, hy_ffn_w3, hy_freq, hy_log_decay)
    kt_c = _hyena_tables(L_CTX, consts["feat_c"], consts["t_c"], consts["dft_c"], fw)
    kt_s = _hyena_tables(L_S, consts["feat_s"], consts["t_s"], consts["dft_s"], fw)

    wts = _prep_weights(w_in, pool_w, pool_scale, hy_short_b, mla_q_norm, mla_kv_norm, mla_w_uq, mla_w_ukv,
                        w_out, router_w, norm1_g, norm2_g)
    krc = jnp.swapaxes(cache_krope, 0, 1).reshape(DEPTH, B_S * L_CTX, ROPE)
    krc = jnp.concatenate([krc, krc], axis=2).astype(BF16)

    ckv_list, kr_list = [], []
    for l in range(DEPTH):
        final_g = final_norm_g.reshape(1, D) if l == DEPTH - 1 else None
        x, ckv, kr = _layer(x, l, mod[l], wts, consts, kt_c, kt_s, hy_short_w, hy_skip, cache_ckv, krc[l],
                            exp_w_gate, exp_w_up, exp_w_down, final_g)
        ckv_list.append(ckv.reshape(B_CTX, L_CTX, KV_RANK))
        kr_list.append(kr[:, :ROPE].reshape(B_CTX, L_CTX, ROPE))

    y_prompt, y_sample = x
    return (y_prompt.reshape(B_CTX, L_CTX, D), y_sample.reshape(B_S, L_S, D),
            jnp.stack(ckv_list, axis=1), jnp.stack(kr_list, axis=1))
```

```python
import functools
import math

import numpy as np
import jax
import jax.numpy as jnp
from jax import lax
from jax.experimental import pallas as pl
from jax.experimental.pallas import tpu as pltpu

F32 = jnp.float32
BF16 = jnp.bfloat16

D = 2048
DEPTH = 2
B_CTX, L_CTX = 32, 256
B_S, L_S = 8, 1024
T_CTX = B_CTX * L_CTX
T = T_CTX + B_S * L_S
GRP = 1024
NG = T // GRP
NG_CTX = T_CTX // GRP
SEQ_PER_GRP = GRP // L_CTX
EPS = 1e-6
GRID_W = 64

POOL_W = 512
POOL_GC = 128
POOL_WINDOWS = (2, 4, 8, 16)
HY_W = 512
POS_BANDS = 8
FILT_HID = 64
H = 8
NOPE = 128
ROPE = 64
VD = 128
Q_RANK = 512
KV_RANK = 256
ROPE_THETA = 10000.0
IN_COLS = 2880
IN_PAD = 3072
E = 16
FF = 1024
CAP_CTX = 2 * L_CTX // E
CAP_S = 2 * L_S // E
SLOTS = GRP * 2 // E
ROWS_E = NG * SLOTS
LANES = 128
NCH = 512
ATT_SCALE = 1.0 / math.sqrt(NOPE + ROPE)
MB = 1024 * 1024


def _cp(sem, vmem_mb=48):
    return pltpu.CompilerParams(dimension_semantics=sem, vmem_limit_bytes=vmem_mb * MB)


def _rms(x, g):
    return x * lax.rsqrt(jnp.mean(x * x, axis=-1, keepdims=True) + EPS) * g


def _dot(a, b):
    return jnp.dot(a, b, preferred_element_type=F32)


def _dot_nt(a, b):
    return lax.dot_general(a, b, (((1,), (1,)), ((), ())), preferred_element_type=F32)


def _mod_kernel(c_ref, w_ref, b_ref, o_ref):
    c = c_ref[...]
    a = (c * jax.nn.sigmoid(c)).astype(BF16)
    o_ref[...] = _dot(a, w_ref[...].astype(BF16)) + b_ref[...]


def _adaln_mod(c16, ada_w, ada_b):
    tn = 1024
    return pl.pallas_call(
        _mod_kernel,
        grid=(DEPTH, 6 * D // tn),
        in_specs=[
            pl.BlockSpec((NG, D), lambda l, j: (0, 0)),
            pl.BlockSpec((None, D, tn), lambda l, j: (l, 0, j)),
            pl.BlockSpec((None, 1, tn), lambda l, j: (l, 0, j)),
        ],
        out_specs=pl.BlockSpec((None, NG, tn), lambda l, j: (l, 0, j)),
        out_shape=jax.ShapeDtypeStruct((DEPTH, NG, 6 * D), F32),
        compiler_params=_cp(("parallel", "parallel")),
        name="adaln_mod",
    )(c16, ada_w, ada_b.reshape(DEPTH, 1, 6 * D))


def _src_specs(x, tm):
    n_ctx = T_CTX // tm
    n_all = T // tm
    if isinstance(x, tuple):
        return list(x), [
            pl.BlockSpec((tm, D), lambda i: (jnp.minimum(i, n_ctx - 1), 0)),
            pl.BlockSpec((tm, D), lambda i: (jnp.clip(i - n_ctx, 0, n_all - n_ctx - 1), 0)),
        ]
    return [x], [pl.BlockSpec((tm, D), lambda i: (jnp.minimum(i, n_all - 1), 0))]


def _load_src(x_refs, tm, cs=slice(None)):
    if len(x_refs) == 1:
        return x_refs[0][:, cs]
    return jnp.where(pl.program_id(0) < T_CTX // tm, x_refs[0][:, cs], x_refs[1][:, cs])


def _resident(shape, layer):
    nd = len(shape)
    return pl.BlockSpec((None,) + shape, lambda i: (layer,) + (0,) * nd, pipeline_mode=pl.Buffered(1))


def _mod_row(k, per):
    return pl.BlockSpec((None, 1, D), lambda i: (6 * (i // per) + k, 0, 0))


N_U = POOL_W + 3 * HY_W
N_QR = H * ROPE


def _inproj_kernel(*refs, tm):
    (*x_refs, g_ref, shift_ref, scale_ref, wm_ref, wx_ref, qg_ref, wq_ref, csq_ref, kvg_ref, csk_ref, wkv_ref,
     up_ref, uv_ref, u1_ref, u2_ref, q_ref, ckv_ref, kr_ref, krr_ref, kv_ref) = refs

    y = _rms(_load_src(x_refs, tm), g_ref[...])
    h = (y * (1.0 + scale_ref[...]) + shift_ref[...]).astype(BF16)
    for c, u_ref in enumerate((up_ref, uv_ref, u1_ref, u2_ref)):
        u_ref[...] = _dot(h, wm_ref[:, c * HY_W:(c + 1) * HY_W]).astype(u_ref.dtype)

    qn = _rms(_dot(h, wm_ref[:, N_U:]), qg_ref[...]).astype(BF16)
    nq = H * NOPE
    for c in range(nq // NCH):
        cs = slice(c * NCH, (c + 1) * NCH)
        q_ref[:, cs] = (_dot(qn, wq_ref[:, cs]) * ATT_SCALE).astype(q_ref.dtype)
    rot = _dot(qn, wq_ref[:, nq:nq + N_QR]) * csq_ref[0] + _dot(qn, wq_ref[:, nq + N_QR:]) * csq_ref[1]
    q_ref[:, nq:] = (rot * ATT_SCALE).astype(q_ref.dtype)

    kx = _dot(h, wx_ref[...])
    ckv = _rms(kx[:, :KV_RANK], kvg_ref[...])
    kr = kx[:, KV_RANK:KV_RANK + LANES]

    @pl.when(pl.program_id(0) < T_CTX // tm)
    def _():
        ckv_ref[...] = ckv
        kr_ref[...] = kr

    krr_ref[...] = (kr * csk_ref[0] + kx[:, KV_RANK + LANES:] * csk_ref[1]).astype(krr_ref.dtype)
    kv_ref[...] = _dot(ckv.astype(BF16), wkv_ref[...]).astype(kv_ref.dtype)


def _inproj(x, mod3, wts, consts, layer):
    tm = 512
    per = GRP // tm
    xs, x_specs = _src_specs(x, tm)

    def kind(i):
        return jnp.where(i >= T_CTX // tm, 1, 0)

    def rows(width):
        return pl.BlockSpec((tm, width), lambda i: (i, 0))

    def ctx_rows(width):
        return pl.BlockSpec((tm, width), lambda i: (jnp.minimum(i, T_CTX // tm - 1), 0))

    nkv = H * (NOPE + VD)
    return pl.pallas_call(
        functools.partial(_inproj_kernel, tm=tm),
        grid=(T // tm,),
        in_specs=x_specs + [
            _resident((1, D), layer), _mod_row(0, per), _mod_row(1, per),
            _resident((D, N_U + Q_RANK), layer),
            _resident((D, 4 * LANES), layer),
            _resident((1, Q_RANK), layer),
            _resident((Q_RANK, H * (NOPE + 2 * ROPE)), layer),
            pl.BlockSpec((None, 2, tm, N_QR), lambda i: (kind(i), 0, i % per, 0)),
            _resident((1, KV_RANK), layer),
            pl.BlockSpec((None, 2, tm, LANES), lambda i: (kind(i), 0, i % per, 0)),
            _resident((KV_RANK, nkv), layer),
        ],
        out_specs=[rows(HY_W)] * 4 + [rows(H * (NOPE + ROPE)), ctx_rows(KV_RANK), ctx_rows(LANES), rows(LANES),
                                      rows(nkv)],
        out_shape=[jax.ShapeDtypeStruct((T, HY_W), BF16)] * 4 + [
            jax.ShapeDtypeStruct((T, H * (NOPE + ROPE)), BF16),
            jax.ShapeDtypeStruct((T_CTX, KV_RANK), F32),
            jax.ShapeDtypeStruct((T_CTX, LANES), F32),
            jax.ShapeDtypeStruct((T, LANES), BF16),
            jax.ShapeDtypeStruct((T, nkv), BF16),
        ],
        compiler_params=_cp(("arbitrary",), 58),
        name="in_proj",
    )(*xs, wts["g1"], mod3, mod3, wts["w_main"], wts["w_x"], wts["qg"], wts["wq"], consts["cs_q"],
      wts["kvg"], consts["cs_k"], wts["wkv"])


def _pool_kernel(u_ref, ic_ref, is_ref, pw_ref, ps_ref, o_ref):
    g = pl.program_id(0)

    def seq(r0, n, inv_ref):
        row = lax.broadcasted_iota(jnp.int32, (n, POOL_GC), 0)

        def later(x, m):
            return jnp.where(row < n - m, pltpu.roll(x, n - m, 0), 0.0)

        def earlier(x, m):
            return jnp.where(row >= m, pltpu.roll(x, m, 0), 0.0)

        for k, w in enumerate(POOL_WINDOWS):
            cs = slice(k * POOL_GC, (k + 1) * POOL_GC)
            u = u_ref[r0:r0 + n, cs].astype(F32)
            ahead, behind, m = u, earlier(u, 1), 1
            while m < w // 2:
                ahead = ahead + later(ahead, m)
                behind = behind + earlier(behind, m)
                m *= 2
            pooled = (ahead + behind) * inv_ref[:, cs] - u
            y = _dot(pooled.astype(BF16), pw_ref[k]) * ps_ref[:, cs]
            o_ref[r0:r0 + n, cs] = y.astype(o_ref.dtype)

    @pl.when(g < NG_CTX)
    def _():
        for s in range(SEQ_PER_GRP):
            seq(s * L_CTX, L_CTX, ic_ref)

    @pl.when(g >= NG_CTX)
    def _():
        seq(0, L_S, is_ref)


def _pool(proj, consts, pw, ps):
    nw = len(POOL_WINDOWS)
    return pl.pallas_call(
        _pool_kernel,
        grid=(NG,),
        in_specs=[
            pl.BlockSpec((GRP, POOL_W), lambda g: (g, 0)),
            pl.BlockSpec((L_CTX, POOL_W), lambda g: (0, 0)),
            pl.BlockSpec((L_S, POOL_W), lambda g: (0, 0)),
            pl.BlockSpec((nw, POOL_GC, POOL_GC), lambda g: (0, 0, 0)),
            pl.BlockSpec((1, POOL_W), lambda g: (0, 0)),
        ],
        out_specs=pl.BlockSpec((GRP, POOL_W), lambda g: (g, 0)),
        out_shape=jax.ShapeDtypeStruct((T, POOL_W), BF16),
        compiler_params=_cp(("parallel",)),
        name="pool_mix",
    )(proj, consts["inv_c"], consts["inv_s"], pw, ps)


HY_CH = HY_W


def _hyena_kernel(v_ref, x1_ref, x2_ref, swv_ref, sw1_ref, sw2_ref, sbv_ref, sb1_ref, sb2_ref,
                  skip_ref, wc_ref, wtc_ref, ws_ref, wts_ref, ktc_ref, kts_ref, o_ref):
    g = pl.program_id(0)

    def sconv(u_ref, sw_ref, sb_ref, r0, n, cs):
        u = u_ref[r0:r0 + n, cs].astype(F32)
        row = lax.broadcasted_iota(jnp.int32, u.shape, 0)
        prev = jnp.where(row == 0, 0.0, pltpu.roll(u, 1, 0))
        nxt = jnp.where(row == n - 1, 0.0, pltpu.roll(u, n - 1, 0))
        return prev * sw_ref[0:1, cs] + u * sw_ref[1:2, cs] + nxt * sw_ref[2:3, cs] + sb_ref[:, cs]

    def lconv(u, o, n, cs, w_ref, wt_ref, kt_ref):
        spec = _dot(w_ref[...], u.astype(BF16))
        pr, pi = spec[:n], spec[n:]
        ka, kb, ka2 = kt_ref[o, 0, :, cs], kt_ref[o, 1, :, cs], kt_ref[o, 2, :, cs]
        yr = pr * ka - pi * kb
        yi = pr * kb + pi * ka2
        prod = jnp.concatenate([yr, yi], axis=0).astype(BF16)
        return _dot(wt_ref[...], prod) + u * skip_ref[o:o + 1, cs]

    def seq(r0, n, cs, w_ref, wt_ref, kt_ref):
        v = sconv(v_ref, swv_ref, sbv_ref, r0, n, cs)
        x1 = sconv(x1_ref, sw1_ref, sb1_ref, r0, n, cs)
        x2 = sconv(x2_ref, sw2_ref, sb2_ref, r0, n, cs)
        z = x1 * lconv(v, 0, n, cs, w_ref, wt_ref, kt_ref)
        y = x2 * lconv(z, 1, n, cs, w_ref, wt_ref, kt_ref)
        o_ref[r0:r0 + n, cs] = y.astype(o_ref.dtype)

    chunks = [slice(c * HY_CH, (c + 1) * HY_CH) for c in range(HY_W // HY_CH)]

    @pl.when(g < NG_CTX)
    def _():
        for s in range(SEQ_PER_GRP):
            for cs in chunks:
                seq(s * L_CTX, L_CTX, cs, wc_ref, wtc_ref, ktc_ref)

    @pl.when(g >= NG_CTX)
    def _():
        for cs in chunks:
            seq(0, L_S, cs, ws_ref, wts_ref, kts_ref)


def _hyena(uv, u1, u2, sw, sb, skip, consts, kt_c, kt_s, layer):
    def part(p):
        return pl.BlockSpec((GRP, HY_W), lambda g: (g, 0))

    def swpart(p):
        return pl.BlockSpec((3, HY_W), lambda g: (0, p))

    def sbpart(p):
        return pl.BlockSpec((1, HY_W), lambda g: (0, p))

    def tables(n):
        return pl.BlockSpec((None, 2, 3, n, HY_W), lambda g: (layer, 0, 0, 0, 0), pipeline_mode=pl.Buffered(1))

    return pl.pallas_call(
        _hyena_kernel,
        grid=(NG,),
        in_specs=[
            part(0), part(1), part(2),
            swpart(0), swpart(1), swpart(2),
            sbpart(0), sbpart(1), sbpart(2),
            pl.BlockSpec((2, HY_W), lambda g: (0, 0)),
            pl.BlockSpec((2 * L_CTX, L_CTX), lambda g: (0, 0)),
            pl.BlockSpec((L_CTX, 2 * L_CTX), lambda g: (0, 0)),
            pl.BlockSpec((2 * L_S, L_S), lambda g: (0, 0)),
            pl.BlockSpec((L_S, 2 * L_S), lambda g: (0, 0)),
            tables(L_CTX), tables(L_S),
        ],
        out_specs=pl.BlockSpec((GRP, HY_W), lambda g: (g, 0)),
        out_shape=jax.ShapeDtypeStruct((T, HY_W), BF16),
        compiler_params=_cp(("parallel",), 56),
        name="hyena_mix",
    )(uv, u1, u2, sw, sw, sw, sb, sb, sb, skip,
      consts["dft_c"], consts["dftt_c"], consts["dft_s"], consts["dftt_s"], kt_c, kt_s)


def _cachekv_kernel(x_ref, w_ref, o_ref):
    o_ref[...] = _dot(x_ref[...].astype(BF16), w_ref[...]).astype(o_ref.dtype)


def _cachekv(cache_ckv, layer, wkv):
    return pl.pallas_call(
        _cachekv_kernel,
        grid=(B_S,),
        in_specs=[
            pl.BlockSpec((None, None, L_CTX, KV_RANK), lambda b: (b, layer, 0, 0)),
            pl.BlockSpec((None, KV_RANK, H * (NOPE + VD)), lambda b: (layer, 0, 0)),
        ],
        out_specs=pl.BlockSpec((L_CTX, H * (NOPE + VD)), lambda b: (b, 0)),
        out_shape=jax.ShapeDtypeStruct((B_S * L_CTX, H * (NOPE + VD)), BF16),
        compiler_params=_cp(("parallel",)),
        name="cache_kv",
    )(cache_ckv, wkv)


ATT_TQ = 512
NK_S = L_S + L_CTX


def _attn_kernel(q_ref, kv_ref, krr_ref, kvc_ref, krc_ref, o_ref, kcat_ref):
    g = pl.program_id(0)
    lane = lax.broadcasted_iota(jnp.int32, (1, LANES), 1)
    hk = NOPE + LANES
    vo = H * NOPE

    krr = krr_ref[...]
    for h in range(H):
        kcat_ref[0:GRP, h * hk:h * hk + NOPE] = kv_ref[:, h * NOPE:(h + 1) * NOPE]
        kcat_ref[0:GRP, h * hk + NOPE:(h + 1) * hk] = krr

    def qcat(rows, h):
        qn = q_ref[rows, h * NOPE:(h + 1) * NOPE]
        pair = q_ref[rows, vo + (h // 2) * LANES:vo + (h // 2 + 1) * LANES].astype(F32)
        keep = (lane < ROPE) if h % 2 == 0 else (lane >= ROPE)
        return jnp.concatenate([qn, jnp.where(keep, pair, 0.0).astype(BF16)], axis=1)

    def probs(sc):
        m = jnp.max(sc, axis=-1, keepdims=True)
        p = jnp.exp(sc - m)
        return p.astype(BF16), 1.0 / jnp.sum(p, axis=-1, keepdims=True)

    @pl.when(g < NG_CTX)
    def _():
        def body(s, carry):
            rows = pl.ds(pl.multiple_of(s * L_CTX, L_CTX), L_CTX)
            for h in range(H):
                p, rl = probs(_dot_nt(qcat(rows, h), kcat_ref[rows, h * hk:(h + 1) * hk]))
                o = _dot(p, kv_ref[rows, vo + h * VD:vo + (h + 1) * VD]) * rl
                o_ref[rows, h * VD:(h + 1) * VD] = o.astype(o_ref.dtype)
            return carry

        lax.fori_loop(0, SEQ_PER_GRP, body, 0)

    @pl.when(g >= NG_CTX)
    def _():
        krc = krc_ref[...]
        for h in range(H):
            kcat_ref[GRP:NK_S, h * hk:h * hk + NOPE] = kvc_ref[:, h * NOPE:(h + 1) * NOPE]
            kcat_ref[GRP:NK_S, h * hk + NOPE:(h + 1) * hk] = krc

        def body(t, carry):
            rows = pl.ds(pl.multiple_of(t * ATT_TQ, ATT_TQ), ATT_TQ)
            for h in range(H):
                p, rl = probs(_dot_nt(qcat(rows, h), kcat_ref[:, h * hk:(h + 1) * hk]))
                o = _dot(p[:, :GRP], kv_ref[:, vo + h * VD:vo + (h + 1) * VD])
                o = o + _dot(p[:, GRP:], kvc_ref[:, vo + h * VD:vo + (h + 1) * VD])
                o_ref[rows, h * VD:(h + 1) * VD] = (o * rl).astype(o_ref.dtype)
            return carry

        lax.fori_loop(0, L_S // ATT_TQ, body, 0)


def _attention(q, kv, krr, kvc, krc):
    def cache_blk(g):
        return jnp.maximum(g - NG_CTX, 0)

    return pl.pallas_call(
        _attn_kernel,
        grid=(NG,),
        in_specs=[
            pl.BlockSpec((GRP, H * (NOPE + ROPE)), lambda g: (g, 0)),
            pl.BlockSpec((GRP, H * (NOPE + VD)), lambda g: (g, 0)),
            pl.BlockSpec((GRP, LANES), lambda g: (g, 0)),
            pl.BlockSpec((L_CTX, H * (NOPE + VD)), lambda g: (cache_blk(g), 0)),
            pl.BlockSpec((L_CTX, LANES), lambda g: (cache_blk(g), 0)),
        ],
        out_specs=pl.BlockSpec((GRP, H * VD), lambda g: (g, 0)),
        out_shape=jax.ShapeDtypeStruct((T, H * VD), BF16),
        scratch_shapes=[pltpu.VMEM((NK_S, H * (NOPE + LANES)), BF16)],
        compiler_params=_cp(("parallel",)),
        name="mla_attention",
    )(q, kv, krr, kvc, krc)


def _wout_kernel(yp_ref, yh_ref, ym_ref, w_ref, gate_ref, g2_ref, shift_ref, scale_ref, rw_ref, *refs, tm):
    *x_refs, o_ref, h_ref, aff_ref, afft_ref, prev_ref = refs

    @pl.when(pl.program_id(0) == 0)
    def _():
        prev_ref[...] = jnp.zeros_like(prev_ref)

    hb = (_rms(prev_ref[...], g2_ref[...]) * (1.0 + scale_ref[...]) + shift_ref[...]).astype(BF16)
    h_ref[...] = hb
    logits = _dot(hb, rw_ref[...])
    lane = lax.broadcasted_iota(jnp.int32, logits.shape, 1)
    logits = jnp.where(lane < E, logits, -jnp.inf)
    ex = jnp.exp(logits - jnp.max(logits, axis=-1, keepdims=True))
    aff = ex / jnp.sum(ex, axis=-1, keepdims=True)
    aff_ref[...] = aff
    afft_ref[...] = aff.T[:E]

    y = jnp.concatenate([yp_ref[...], yh_ref[...], ym_ref[...]], axis=1)
    for c in range(D // NCH):
        cs = slice(c * NCH, (c + 1) * NCH)
        v = _load_src(x_refs, tm, cs) + gate_ref[:, cs] * _dot(y, w_ref[:, cs])
        o_ref[:, cs] = v.astype(o_ref.dtype)
        prev_ref[:, cs] = v


def _wout(yp, yh, ym, x, mod3, wts, layer):
    tm = 512
    per = GRP // tm
    n = T // tm
    xs, x_specs = _src_specs(x, tm)

    def cur(i):
        return jnp.minimum(i, n - 1)

    def prev(i):
        return jnp.maximum(i - 1, 0)

    def rows(width, blk):
        return pl.BlockSpec((tm, width), lambda i: (blk(i), 0))

    def mod_row(k, blk):
        return pl.BlockSpec((None, 1, D), lambda i: (6 * (blk(i) // per) + k, 0, 0))

    return pl.pallas_call(
        functools.partial(_wout_kernel, tm=tm),
        grid=(n + 1,),
        in_specs=[
            rows(POOL_W, cur), rows(HY_W, cur), rows(H * VD, cur),
            _resident((D, D), layer),
            mod_row(2, cur),
            _resident((1, D), layer), mod_row(3, prev), mod_row(4, prev),
            _resident((D, LANES), layer),
        ] + x_specs,
        out_specs=[rows(D, cur), rows(D, prev), rows(LANES, prev), pl.BlockSpec((E, tm), lambda i: (0, prev(i)))],
        out_shape=[
            jax.ShapeDtypeStruct((T, D), BF16),
            jax.ShapeDtypeStruct((T, D), BF16),
            jax.ShapeDtypeStruct((T, LANES), F32),
            jax.ShapeDtypeStruct((E, T), F32),
        ],
        scratch_shapes=[pltpu.VMEM((tm, D), F32)],
        compiler_params=_cp(("arbitrary",), 56),
        name="out_proj",
    )(yp, yh, ym, wts["w_out"], mod3, wts["g2"], mod3, mod3, wts["rw"], *xs)


RANK_CH = 256
SEARCH_ROUNDS = 14
SEARCH_WAYS = 8
AFF_MAX = 2.0


def _rank_kernel(aff_ref, afft_ref, tri_ref, rank_ref, rank_t_ref, cnt_ref, cut_ref):
    g = pl.program_id(0)

    def count_ge(a, t):
        return jnp.sum(jnp.where(a >= t, 1.0, 0.0), axis=1, keepdims=True)

    def search(specs, capf):
        acts = [afft_ref[:, r0:r0 + n] for r0, n in specs]
        lo = [jnp.zeros((E, 1), F32) for _ in specs]
        hi = [jnp.full((E, 1), AFF_MAX, F32) for _ in specs]
        for _ in range(SEARCH_ROUNDS):
            for s, a in enumerate(acts):
                step = (hi[s] - lo[s]) * (1.0 / SEARCH_WAYS)
                ts = [lo[s] + step * k for k in range(1, SEARCH_WAYS)]
                ok = [count_ge(a, t) >= capf for t in ts]
                new_lo, new_hi = lo[s], hi[s]
                for t, o in zip(ts, ok):
                    new_lo = jnp.where(o, t, new_lo)
                for t, o in zip(reversed(ts), reversed(ok)):
                    new_hi = jnp.where(o, new_hi, t)
                lo[s], hi[s] = new_lo, new_hi
        open_brackets = jnp.zeros((E, 1), F32)
        for s, a in enumerate(acts):
            top = jnp.max(jnp.where(a >= lo[s], jnp.where(a < hi[s], a, -1.0), -1.0), axis=1, keepdims=True)
            low = jnp.min(jnp.where(a >= lo[s], jnp.where(a < hi[s], a, AFF_MAX), AFF_MAX), axis=1, keepdims=True)
            cut_ref[:, s:s + 1] = top
            open_brackets = open_brackets + jnp.where(top != low, 1.0, 0.0)
        return jnp.sum(open_brackets)

    def exact_cut(s, r0, n, capf):
        for e in range(E):
            row = afft_ref[e:e + 1, r0:r0 + n]
            acc = jnp.zeros((1, n), F32)
            for c in range(n // RANK_CH):
                col = aff_ref[r0 + c * RANK_CH:r0 + (c + 1) * RANK_CH, e:e + 1]
                acc = acc + jnp.sum(jnp.where(col >= row, 1.0, 0.0), axis=0, keepdims=True)
            cnt_ref[e:e + 1, 0:n] = acc
        a = afft_ref[:, r0:r0 + n]
        cut_ref[:, s:s + 1] = jnp.max(jnp.where(cnt_ref[:, 0:n] >= capf, a, -1.0), axis=1, keepdims=True)

    def slots(s, r0, n, capf):
        a = afft_ref[:, r0:r0 + n]
        cut = cut_ref[:, s:s + 1]
        above = a > cut
        tied = a == cut
        above_f = jnp.where(above, 1.0, 0.0)
        n_above = jnp.sum(above_f, axis=1, keepdims=True)
        marks = jnp.concatenate([above_f, jnp.where(tied, 1.0, 0.0)], axis=0).astype(BF16)
        before = _dot(marks, tri_ref[0:n, 0:n])
        tie_slot = n_above + before[E:]
        slot = jnp.where(above, before[:E], jnp.where(tied, jnp.where(tie_slot < capf, tie_slot, n), n))
        rank_ref[:, r0:r0 + n] = slot.astype(jnp.int32)
        wide = jnp.concatenate([slot, jnp.zeros((LANES - E, n), F32)], axis=0)
        rank_t_ref[r0:r0 + n, :] = wide.T.astype(jnp.int32)

    def group(specs, cap):
        capf = float(cap)
        unresolved = search(specs, capf)

        @pl.when(unresolved > 0.0)
        def _():
            for s, (r0, n) in enumerate(specs):
                exact_cut(s, r0, n, capf)

        for s, (r0, n) in enumerate(specs):
            slots(s, r0, n, capf)

    @pl.when(g < NG_CTX)
    def _():
        group([(s * L_CTX, L_CTX) for s in range(SEQ_PER_GRP)], CAP_CTX)

    @pl.when(g >= NG_CTX)
    def _():
        group([(0, L_S)], CAP_S)


def _rank(aff, afft, tri):
    return pl.pallas_call(
        _rank_kernel,
        grid=(NG,),
        in_specs=[
            pl.BlockSpec((GRP, LANES), lambda g: (g, 0)),
            pl.BlockSpec((E, GRP), lambda g: (0, g)),
            pl.BlockSpec((L_S, L_S), lambda g: (0, 0)),
        ],
        out_specs=[pl.BlockSpec((E, GRP), lambda g: (0, g)), pl.BlockSpec((GRP, LANES), lambda g: (g, 0))],
        out_shape=[jax.ShapeDtypeStruct((E, T), jnp.int32), jax.ShapeDtypeStruct((T, LANES), jnp.int32)],
        scratch_shapes=[pltpu.VMEM((E, GRP), F32), pltpu.VMEM((E, LANES), F32)],
        compiler_params=_cp(("parallel",)),
        name="moe_rank",
    )(aff, afft, tri)


def _gather_kernel(rank_ref, afft_ref, h_ref, xs_ref, gs_ref, sel_ref):
    g = pl.program_id(0)

    def seq(r0, n, cap, slot0):
        slot_i = lax.broadcasted_iota(jnp.int32, (cap, n), 0)
        for e in range(E):
            hit = slot_i == rank_ref[e:e + 1, r0:r0 + n]
            sel_ref[e * cap:(e + 1) * cap, 0:n] = jnp.where(hit, 1.0, 0.0).astype(BF16)
            gs_ref[e, slot0:slot0 + cap, :] = jnp.sum(
                jnp.where(hit, afft_ref[e:e + 1, r0:r0 + n], 0.0), axis=1, keepdims=True)
        for c in range(D // NCH):
            cs = slice(c * NCH, (c + 1) * NCH)
            res = _dot(sel_ref[0:E * cap, 0:n], h_ref[r0:r0 + n, cs])
            for e in range(E):
                xs_ref[e, slot0:slot0 + cap, cs] = res[e * cap:(e + 1) * cap].astype(xs_ref.dtype)

    @pl.when(g < NG_CTX)
    def _():
        for s in range(SEQ_PER_GRP):
            seq(s * L_CTX, L_CTX, CAP_CTX, s * CAP_CTX)

    @pl.when(g >= NG_CTX)
    def _():
        seq(0, L_S, CAP_S, 0)


def _gather(rank, afft, h2):
    return pl.pallas_call(
        _gather_kernel,
        grid=(NG,),
        in_specs=[
            pl.BlockSpec((E, GRP), lambda g: (0, g)),
            pl.BlockSpec((E, GRP), lambda g: (0, g)),
            pl.BlockSpec((GRP, D), lambda g: (g, 0)),
        ],
        out_specs=[
            pl.BlockSpec((E, SLOTS, D), lambda g: (0, g, 0)),
            pl.BlockSpec((E, SLOTS, 1), lambda g: (0, g, 0)),
        ],
        out_shape=[
            jax.ShapeDtypeStruct((E, ROWS_E, D), BF16),
            jax.ShapeDtypeStruct((E, ROWS_E, 1), F32),
        ],
        scratch_shapes=[pltpu.VMEM((E * CAP_S, L_S), BF16)],
        compiler_params=_cp(("parallel",)),
        name="moe_gather",
    )(rank, afft, h2)


FFN_TF = 256
FFN_TM = 1024


def _ffn_kernel(x_ref, wg0_ref, wg1_ref, wu0_ref, wu1_ref, wd0_ref, wd1_ref, gs_ref, o_ref, acc_ref):
    f = pl.program_id(2)
    nf = pl.num_programs(2)
    hd = D // 2
    halves = ((slice(0, hd), wd0_ref), (slice(hd, D), wd1_ref))

    def hidden():
        xa, xb = x_ref[:, :hd], x_ref[:, hd:]
        a = _dot(xa, wg0_ref[...].astype(BF16)) + _dot(xb, wg1_ref[...].astype(BF16))
        b = _dot(xa, wu0_ref[...].astype(BF16)) + _dot(xb, wu1_ref[...].astype(BF16))
        return (a * jax.nn.sigmoid(a) * b).astype(BF16)

    @pl.when(f == 0)
    def _():
        hid = hidden()
        for cs, wd_ref in halves:
            acc_ref[:, cs] = _dot(hid, wd_ref[...].astype(BF16))

    @pl.when((f > 0) & (f < nf - 1))
    def _():
        hid = hidden()
        for cs, wd_ref in halves:
            acc_ref[:, cs] += _dot(hid, wd_ref[...].astype(BF16))

    @pl.when(f == nf - 1)
    def _():
        hid = hidden()
        for cs, wd_ref in halves:
            down = acc_ref[:, cs] + _dot(hid, wd_ref[...].astype(BF16))
            o_ref[:, cs] = (down * gs_ref[...]).astype(o_ref.dtype)


def _ffn(xs, gs, w_gate, w_up, w_down, layer):
    def up_half(k):
        return pl.BlockSpec((None, None, D // 2, FFN_TF), lambda e, m, f: (layer, e, k, f))

    def down_half(k):
        return pl.BlockSpec((None, None, FFN_TF, D // 2), lambda e, m, f: (layer, e, f, k))

    return pl.pallas_call(
        _ffn_kernel,
        grid=(E, ROWS_E // FFN_TM, FF // FFN_TF),
        in_specs=[
            pl.BlockSpec((None, FFN_TM, D), lambda e, m, f: (e, m, 0)),
            up_half(0), up_half(1), up_half(0), up_half(1), down_half(0), down_half(1),
            pl.BlockSpec((None, FFN_TM, 1), lambda e, m, f: (e, m, 0)),
        ],
        out_specs=pl.BlockSpec((None, FFN_TM, D), lambda e, m, f: (e, m, 0)),
        out_shape=jax.ShapeDtypeStruct((E, ROWS_E, D), BF16),
        scratch_shapes=[pltpu.VMEM((FFN_TM, D), F32)],
        compiler_params=_cp(("parallel", "parallel", "arbitrary"), 56),
        name="moe_ffn",
    )(xs, w_gate, w_gate, w_up, w_up, w_down, w_down, gs)


def _combine_kernel(rt_ref, ys_ref, x_ref, gate_ref, exc_ref, exs_ref, *refs, last, g0):
    if last:
        fg_ref, o_ref, st_ref = refs
    else:
        o_ref, st_ref = refs
    g = pl.program_id(0) + g0

    def scatter(r0, n, cap, slot0, ex_ref):
        ec = E * cap
        r = jnp.minimum(rt_ref[r0:r0 + n, :], cap).astype(F32).astype(BF16)
        want = (lax.broadcasted_iota(jnp.int32, (1, NCH), 1) & (cap - 1)).astype(F32)
        for c in range(ec // NCH):
            cs = slice(c * NCH, (c + 1) * NCH)
            st_ref[r0:r0 + n, cs] = jnp.where(_dot(r, ex_ref[:, cs]) == want, 1.0, 0.0).astype(BF16)
        for c in range(D // NCH):
            cs = slice(c * NCH, (c + 1) * NCH)
            ys = ys_ref[:, slot0:slot0 + cap, cs].reshape(ec, NCH)
            moe = _dot(st_ref[r0:r0 + n, 0:ec], ys)
            o_ref[r0:r0 + n, cs] = x_ref[r0:r0 + n, cs].astype(F32) + gate_ref[:, cs] * moe

    def finish():
        if last:
            o_ref[...] = _rms(o_ref[...], fg_ref[...])

    @pl.when(g < NG_CTX)
    def _():
        for s in range(SEQ_PER_GRP):
            scatter(s * L_CTX, L_CTX, CAP_CTX, s * CAP_CTX, exc_ref)
        finish()

    @pl.when(g >= NG_CTX)
    def _():
        scatter(0, L_S, CAP_S, 0, exs_ref)
        finish()


def _combine_groups(rank_t, ys, x, mod3, consts, final_g, g0, ng):
    last = final_g is not None
    in_specs = [
        pl.BlockSpec((GRP, LANES), lambda g: (g + g0, 0)),
        pl.BlockSpec((E, SLOTS, D), lambda g: (0, g + g0, 0)),
        pl.BlockSpec((GRP, D), lambda g: (g + g0, 0)),
        pl.BlockSpec((None, 1, D), lambda g: (6 * (g + g0) + 5, 0, 0)),
        pl.BlockSpec((LANES, E * CAP_CTX), lambda g: (0, 0)),
        pl.BlockSpec((LANES, E * CAP_S), lambda g: (0, 0)),
    ]
    args = [rank_t, ys, x, mod3, consts["ex_c"], consts["ex_s"]]
    if last:
        in_specs.append(pl.BlockSpec((1, D), lambda g: (0, 0)))
        args.append(final_g)
    return pl.pallas_call(
        functools.partial(_combine_kernel, last=last, g0=g0),
        grid=(ng,),
        in_specs=in_specs,
        out_specs=pl.BlockSpec((GRP, D), lambda g: ((g if last else g + g0), 0)),
        out_shape=jax.ShapeDtypeStruct(((ng * GRP if last else T), D), F32),
        scratch_shapes=[pltpu.VMEM((GRP, E * CAP_S), BF16)],
        compiler_params=_cp(("parallel",), 56),
        name="moe_combine",
    )(*args)


def _combine(rank_t, ys, x, mod3, consts, final_g=None):
    if final_g is None:
        return _combine_groups(rank_t, ys, x, mod3, consts, None, 0, NG)
    return (_combine_groups(rank_t, ys, x, mod3, consts, final_g, 0, NG_CTX),
            _combine_groups(rank_t, ys, x, mod3, consts, final_g, NG_CTX, NG - NG_CTX))


def _np_constants():
    c = {}
    for tag, n in (("c", L_CTX), ("s", L_S)):
        t = np.arange(n)
        inv = np.zeros((n, POOL_W), np.float32)
        for k, w in enumerate(POOL_WINDOWS):
            lo = np.clip(t - w // 2, 0, n)
            hi = np.clip(t - w // 2 + w, 0, n)
            inv[:, k * POOL_GC:(k + 1) * POOL_GC] = (1.0 / (hi - lo).astype(np.float64))[:, None]
        c["inv_" + tag] = inv
        kk = np.arange(n, dtype=np.float64)[:, None]
        tt = np.arange(n, dtype=np.float64)[None, :]
        ang = np.pi * kk * tt / n
        dft = np.concatenate([np.cos(ang), -np.sin(ang)], axis=0)
        dft[n] = (-1.0) ** np.arange(n)
        c["dft_" + tag] = dft.astype(np.float32)
        c["dftt_" + tag] = np.ascontiguousarray(dft.T).astype(np.float32)
        tl = np.linspace(0.0, 1.0, n, dtype=np.float32)[:, None]
        bands = np.arange(1, POS_BANDS + 1, dtype=np.float32)[None, :]
        feats = np.concatenate([tl, np.sin(2 * np.pi * tl * bands), np.cos(2 * np.pi * tl * bands)], axis=1)
        c["feat_" + tag] = np.pad(feats.astype(np.float32), ((0, 0), (0, LANES - feats.shape[1])))
        c["t_" + tag] = tl
    n_rows = L_S // GRID_W
    row = np.repeat(np.arange(n_rows), GRID_W).astype(np.float32)
    col = np.tile(np.arange(GRID_W), n_rows).astype(np.float32)
    nf = ROPE // 4
    inv_f = (1.0 / ROPE_THETA ** (np.arange(nf, dtype=np.float32) / nf)).astype(np.float32)
    a_row = (row[:, None] * inv_f[None]).astype(np.float32).astype(np.float64)
    a_col = (col[:, None] * inv_f[None]).astype(np.float32).astype(np.float64)
    cos64 = np.concatenate([np.cos(a_row), np.cos(a_row), np.cos(a_col), np.cos(a_col)], axis=1)
    sin64 = np.concatenate([np.sin(a_row), np.sin(a_row), np.sin(a_col), np.sin(a_col)], axis=1)

    def table(reps):
        ident = np.stack([np.ones((L_S, ROPE * reps)), np.zeros((L_S, ROPE * reps))])
        rot = np.stack([np.tile(cos64, (1, reps)), np.tile(sin64, (1, reps))])
        return np.stack([ident, rot]).astype(np.float32)

    c["cs_q"] = table(H)
    c["cs_k"] = table(LANES // ROPE)
    c["tri"] = np.triu(np.ones((L_S, L_S), np.float32), k=1)
    for tag, cap in (("c", CAP_CTX), ("s", CAP_S)):
        ex = np.zeros((LANES, E * cap), np.float32)
        ex[np.arange(E * cap) // cap, np.arange(E * cap)] = 1.0
        c["ex_" + tag] = ex
    return c


def _constants():
    c = {k: jnp.asarray(v) for k, v in _np_constants().items()}
    for k in ("dft_c", "dftt_c", "dft_s", "dftt_s", "ex_c", "ex_s", "tri"):
        c[k] = c[k].astype(BF16)
    return c


def _rope_swap(w):
    q = ROPE // 4
    return jnp.concatenate([-w[..., q:2 * q], w[..., :q], -w[..., 3 * q:], w[..., 2 * q:3 * q]], axis=-1)


def _dot_hi(a, b):
    return jnp.dot(a, b, preferred_element_type=F32, precision=lax.Precision.HIGHEST)


def _filter_kernel(feat_ref, t_ref, w1_ref, b1_ref, w2_ref, b2_ref, fr_ref, w3f_ref, w3b_ref,
                   ldf_ref, ldb_ref, dft_ref, o_ref, z_ref):
    n = feat_ref.shape[0]

    @pl.when((pl.program_id(1) == 0) & (pl.program_id(2) == 0))
    def _():
        fr = fr_ref[...]
        z = jnp.sin(fr * (_dot_hi(feat_ref[...], w1_ref[...]) + b1_ref[...]))
        z_ref[...] = jnp.sin(fr * (_dot_hi(z, w2_ref[...]) + b2_ref[...]))

    z = z_ref[...]
    t = t_ref[...]
    first = lax.broadcasted_iota(jnp.int32, (n, 1), 0) == 0
    hf = _dot_hi(z, w3f_ref[...]) * jnp.exp(-jnp.exp(ldf_ref[...]) * t)
    hb = _dot_hi(z, w3b_ref[...]) * jnp.exp(-jnp.exp(ldb_ref[...]) * t)
    hb = jnp.where(first, 0.0, hb)
    norm = jnp.sum(jnp.abs(hf), axis=0, keepdims=True) + jnp.sum(jnp.abs(hb), axis=0, keepdims=True) + EPS
    both = _dot(dft_ref[...], jnp.concatenate([hf / norm, hb / norm], axis=1).astype(BF16))
    pf, pb = both[:, :hf.shape[1]], both[:, hf.shape[1]:]
    sc = jnp.where(first, 0.5 / n, 1.0 / n)
    ka = (pf[:n] + pb[:n]) * sc
    o_ref[0] = ka
    o_ref[1] = jnp.where(first, 0.0, (pf[n:] - pb[n:]) * sc)
    o_ref[2] = jnp.where(first, (pf[n:] + pb[n:]) * sc, ka)


def _hyena_tables(n, feats, tcol, dft, fw):
    tc = 256
    nc = HY_W // tc

    def lay(shape):
        return pl.BlockSpec((None,) + shape, lambda l, o, c: (l, 0, 0))

    def w3(back):
        return pl.BlockSpec((None, LANES, tc), lambda l, o, c: (l, 0, (2 * o + back) * nc + c))

    def ld(back):
        return pl.BlockSpec((None, 1, tc), lambda l, o, c: (l, 0, (2 * o + back) * nc + c))

    return pl.pallas_call(
        _filter_kernel,
        grid=(DEPTH, 2, nc),
        in_specs=[
            pl.BlockSpec((n, LANES), lambda l, o, c: (0, 0)),
            pl.BlockSpec((n, 1), lambda l, o, c: (0, 0)),
            lay((LANES, LANES)), lay((1, LANES)), lay((LANES, LANES)), lay((1, LANES)), lay((1, LANES)),
            w3(0), w3(1), ld(0), ld(1),
            pl.BlockSpec((2 * n, n), lambda l, o, c: (0, 0)),
        ],
        out_specs=pl.BlockSpec((None, None, 3, n, tc), lambda l, o, c: (l, o, 0, 0, c)),
        out_shape=jax.ShapeDtypeStruct((DEPTH, 2, 3, n, HY_W), F32),
        scratch_shapes=[pltpu.VMEM((n, LANES), F32)],
        compiler_params=_cp(("arbitrary", "arbitrary", "arbitrary")),
        name="hyena_filter",
    )(feats, tcol, fw["w1"], fw["b1"], fw["w2"], fw["b2"], fw["freq"], fw["w3"], fw["w3"],
      fw["ld"], fw["ld"], dft)


def _filter_weights(w1, b1, w2, b2, w3, freq, log_decay):
    ph = LANES - FILT_HID
    return dict(
        w1=jnp.pad(w1, ((0, 0), (0, LANES - w1.shape[1]), (0, ph))),
        b1=jnp.pad(b1, ((0, 0), (0, ph))).reshape(DEPTH, 1, LANES),
        w2=jnp.pad(w2, ((0, 0), (0, ph), (0, ph))),
        b2=jnp.pad(b2, ((0, 0), (0, ph))).reshape(DEPTH, 1, LANES),
        freq=jnp.pad(freq, ((0, 0), (0, ph))).reshape(DEPTH, 1, LANES),
        w3=jnp.pad(w3, ((0, 0), (0, ph), (0, 0))),
        ld=log_decay.reshape(DEPTH, 1, 4 * HY_W))


def _prep_weights(w_in, pool_w, pool_scale, hy_short_b, mla_q_norm, mla_kv_norm, mla_w_uq, mla_w_ukv,
                  w_out, router_w, norm1_g, norm2_g):
    n_main = N_U + Q_RANK
    kr_cols = w_in[:, :, IN_COLS - ROPE:]
    kr_swap = _rope_swap(kr_cols)
    w_x = jnp.concatenate([w_in[:, :, n_main:IN_COLS - ROPE], kr_cols, kr_cols, kr_swap, kr_swap], axis=2)
    wq = mla_w_uq.reshape(DEPTH, Q_RANK, H, NOPE + ROPE)
    wq_rope = wq[..., NOPE:]
    wq = jnp.concatenate([wq[..., :NOPE].reshape(DEPTH, Q_RANK, -1), wq_rope.reshape(DEPTH, Q_RANK, -1),
                          _rope_swap(wq_rope).reshape(DEPTH, Q_RANK, -1)], axis=2)
    wkv = mla_w_ukv.reshape(DEPTH, KV_RANK, H, NOPE + VD)
    wkv = jnp.concatenate([wkv[..., :NOPE].reshape(DEPTH, KV_RANK, -1), wkv[..., NOPE:].reshape(DEPTH, KV_RANK, -1)],
                          axis=2)
    return dict(
        w_main=w_in.astype(BF16), w_x=w_x.astype(BF16), wq=wq.astype(BF16), wkv=wkv.astype(BF16),
        w_out=w_out.astype(BF16), rw=jnp.pad(router_w, ((0, 0), (0, 0), (0, LANES - E))).astype(BF16),
        pool_w=pool_w.astype(BF16), pool_scale=pool_scale.reshape(DEPTH, 1, POOL_W),
        sb=hy_short_b.reshape(DEPTH, 1, 3 * HY_W),
        qg=mla_q_norm.reshape(DEPTH, 1, Q_RANK), kvg=mla_kv_norm.reshape(DEPTH, 1, KV_RANK),
        g1=norm1_g.reshape(DEPTH, 1, D), g2=norm2_g.reshape(DEPTH, 1, D))


def _layer(x, layer, mod3, wts, consts, kt_c, kt_s, hy_short_w, hy_skip, cache_ckv, krc,
           exp_w_gate, exp_w_up, exp_w_down, final_g):
    up, uv, u1, u2, q, ckv, kr, krr, kv = _inproj(x, mod3, wts, consts, layer)
    y_pool = _pool(up, consts, wts["pool_w"][layer], wts["pool_scale"][layer])
    y_hy = _hyena(uv, u1, u2, hy_short_w[layer], wts["sb"][layer], hy_skip[layer], consts, kt_c, kt_s, layer)
    kvc = _cachekv(cache_ckv, layer, wts["wkv"])
    y_mla = _attention(q, kv, krr, kvc, krc)
    x, h2, aff, afft = _wout(y_pool, y_hy, y_mla, x, mod3, wts, layer)
    rank, rank_t = _rank(aff, afft, consts["tri"])
    xs, gs = _gather(rank, afft, h2)
    ys = _ffn(xs, gs, exp_w_gate, exp_w_up, exp_w_down, layer)
    return _combine(rank_t, ys, x, mod3, consts, final_g), ckv, kr


def kernel(x_prompt, x_sample, cache_ckv, cache_krope, c, c_ctx, ada_w, ada_b, norm1_g, norm2_g, w_in, pool_w, pool_scale, hy_short_w, hy_short_b, hy_ffn_w1, hy_ffn_b1, hy_ffn_w2, hy_ffn_b2, hy_ffn_w3, hy_freq, hy_log_decay, hy_skip, mla_q_norm, mla_kv_norm, mla_w_uq, mla_w_ukv, w_out, router_w, exp_w_gate, exp_w_up, exp_w_down, final_norm_g):
    consts = _constants()
    x = (x_prompt.reshape(T_CTX, D), x_sample.reshape(T - T_CTX, D))
    c16 = jnp.concatenate([jnp.broadcast_to(c_ctx[None], (NG_CTX, D)), c], axis=0)
    mod = _adaln_mod(c16, ada_w, ada_b).reshape(DEPTH, NG * 6, 1, D)

    fw = _filter_weights(hy_ffn_w1, hy_ffn_b1, hy_ffn_w2, hy_ffn_b2, hy_ffn_w3, hy_freq, hy_log_decay)
    kt_c = _hyena_tables(L_CTX, consts["feat_c"], consts["t_c"], consts["dft_c"], fw)
    kt_s = _hyena_tables(L_S, consts["feat_s"], consts["t_s"], consts["dft_s"], fw)

    wts = _prep_weights(w_in, pool_w, pool_scale, hy_short_b, mla_q_norm, mla_kv_norm, mla_w_uq, mla_w_ukv,
                        w_out, router_w, norm1_g, norm2_g)
    krc = jnp.swapaxes(cache_krope, 0, 1).reshape(DEPTH, B_S * L_CTX, ROPE)
    krc = jnp.concatenate([krc, krc], axis=2).astype(BF16)

    ckv_list, kr_list = [], []
    for l in range(DEPTH):
        final_g = final_norm_g.reshape(1, D) if l == DEPTH - 1 else None
        x, ckv, kr = _layer(x, l, mod[l], wts, consts, kt_c, kt_s, hy_short_w, hy_skip, cache_ckv, krc[l],
                            exp_w_gate, exp_w_up, exp_w_down, final_g)
        ckv_list.append(ckv.reshape(B_CTX, L_CTX, KV_RANK))
        kr_list.append(kr[:, :ROPE].reshape(B_CTX, L_CTX, ROPE))

    y_prompt, y_sample = x
    return (y_prompt.reshape(B_CTX, L_CTX, D), y_sample.reshape(B_S, L_S, D),
            jnp.stack(ckv_list, axis=1), jnp.stack(kr_list, axis=1))
```

```python
import functools
import math

import numpy as np
import jax
import jax.numpy as jnp
from jax import lax
from jax.experimental import pallas as pl
from jax.experimental.pallas import tpu as pltpu

F32 = jnp.float32
BF16 = jnp.bfloat16

D = 2048
DEPTH = 2
B_CTX, L_CTX = 32, 256
B_S, L_S = 8, 1024
T_CTX = B_CTX * L_CTX
T = T_CTX + B_S * L_S
GRP = 1024
NG = T // GRP
NG_CTX = T_CTX // GRP
SEQ_PER_GRP = GRP // L_CTX
EPS = 1e-6
GRID_W = 64

POOL_W = 512
POOL_GC = 128
POOL_WINDOWS = (2, 4, 8, 16)
HY_W = 512
POS_BANDS = 8
FILT_HID = 64
H = 8
NOPE = 128
ROPE = 64
VD = 128
Q_RANK = 512
KV_RANK = 256
ROPE_THETA = 10000.0
IN_COLS = 2880
IN_PAD = 3072
E = 16
FF = 1024
CAP_CTX = 2 * L_CTX // E
CAP_S = 2 * L_S // E
SLOTS = GRP * 2 // E
ROWS_E = NG * SLOTS
LANES = 128
NCH = 512
ATT_SCALE = 1.0 / math.sqrt(NOPE + ROPE)
MB = 1024 * 1024


def _cp(sem, vmem_mb=48):
    return pltpu.CompilerParams(dimension_semantics=sem, vmem_limit_bytes=vmem_mb * MB)


def _rms(x, g):
    return x * lax.rsqrt(jnp.mean(x * x, axis=-1, keepdims=True) + EPS) * g


def _dot(a, b):
    return jnp.dot(a, b, preferred_element_type=F32)


def _dot_nt(a, b):
    return lax.dot_general(a, b, (((1,), (1,)), ((), ())), preferred_element_type=F32)


def _mod_kernel(c_ref, w_ref, b_ref, o_ref):
    c = c_ref[...]
    a = (c * jax.nn.sigmoid(c)).astype(BF16)
    o_ref[...] = _dot(a, w_ref[...].astype(BF16)) + b_ref[...]


def _adaln_mod(c16, ada_w, ada_b):
    tn = 1024
    return pl.pallas_call(
        _mod_kernel,
        grid=(DEPTH, 6 * D // tn),
        in_specs=[
            pl.BlockSpec((NG, D), lambda l, j: (0, 0)),
            pl.BlockSpec((None, D, tn), lambda l, j: (l, 0, j)),
            pl.BlockSpec((None, 1, tn), lambda l, j: (l, 0, j)),
        ],
        out_specs=pl.BlockSpec((None, NG, tn), lambda l, j: (l, 0, j)),
        out_shape=jax.ShapeDtypeStruct((DEPTH, NG, 6 * D), F32),
        compiler_params=_cp(("parallel", "parallel")),
        name="adaln_mod",
    )(c16, ada_w, ada_b.reshape(DEPTH, 1, 6 * D))


def _src_specs(x, tm):
    n_ctx = T_CTX // tm
    n_all = T // tm
    if isinstance(x, tuple):
        return list(x), [
            pl.BlockSpec((tm, D), lambda i: (jnp.minimum(i, n_ctx - 1), 0)),
            pl.BlockSpec((tm, D), lambda i: (jnp.clip(i - n_ctx, 0, n_all - n_ctx - 1), 0)),
        ]
    return [x], [pl.BlockSpec((tm, D), lambda i: (jnp.minimum(i, n_all - 1), 0))]


def _load_src(x_refs, tm, cs=slice(None)):
    if len(x_refs) == 1:
        return x_refs[0][:, cs]
    return jnp.where(pl.program_id(0) < T_CTX // tm, x_refs[0][:, cs], x_refs[1][:, cs])


def _resident(shape, layer):
    nd = len(shape)
    return pl.BlockSpec((None,) + shape, lambda i: (layer,) + (0,) * nd, pipeline_mode=pl.Buffered(1))


def _mod_row(k, per):
    return pl.BlockSpec((None, 1, D), lambda i: (6 * (i // per) + k, 0, 0))


N_U = POOL_W + 3 * HY_W
N_QR = H * ROPE


def _inproj_kernel(*refs, tm):
    (*x_refs, g_ref, shift_ref, scale_ref, wm_ref, wx_ref, qg_ref, wq_ref, csq_ref, kvg_ref, csk_ref, wkv_ref,
     up_ref, uv_ref, u1_ref, u2_ref, q_ref, ckv_ref, kr_ref, krr_ref, kv_ref) = refs

    y = _rms(_load_src(x_refs, tm), g_ref[...])
    h = (y * (1.0 + scale_ref[...]) + shift_ref[...]).astype(BF16)
    for c, u_ref in enumerate((up_ref, uv_ref, u1_ref, u2_ref)):
        u_ref[...] = _dot(h, wm_ref[:, c * HY_W:(c + 1) * HY_W]).astype(u_ref.dtype)

    qn = _rms(_dot(h, wm_ref[:, N_U:]), qg_ref[...]).astype(BF16)
    nq = H * NOPE
    for c in range(nq // NCH):
        cs = slice(c * NCH, (c + 1) * NCH)
        q_ref[:, cs] = (_dot(qn, wq_ref[:, cs]) * ATT_SCALE).astype(q_ref.dtype)
    rot = _dot(qn, wq_ref[:, nq:nq + N_QR]) * csq_ref[0] + _dot(qn, wq_ref[:, nq + N_QR:]) * csq_ref[1]
    q_ref[:, nq:] = (rot * ATT_SCALE).astype(q_ref.dtype)

    kx = _dot(h, wx_ref[...])
    ckv = _rms(kx[:, :KV_RANK], kvg_ref[...])
    kr = kx[:, KV_RANK:KV_RANK + LANES]

    @pl.when(pl.program_id(0) < T_CTX // tm)
    def _():
        ckv_ref[...] = ckv
        kr_ref[...] = kr

    krr_ref[...] = (kr * csk_ref[0] + kx[:, KV_RANK + LANES:] * csk_ref[1]).astype(krr_ref.dtype)
    kv_ref[...] = _dot(ckv.astype(BF16), wkv_ref[...]).astype(kv_ref.dtype)


def _inproj(x, mod3, wts, consts, layer):
    tm = 512
    per = GRP // tm
    xs, x_specs = _src_specs(x, tm)

    def kind(i):
        return jnp.where(i >= T_CTX // tm, 1, 0)

    def rows(width):
        return pl.BlockSpec((tm, width), lambda i: (i, 0))

    def ctx_rows(width):
        return pl.BlockSpec((tm, width), lambda i: (jnp.minimum(i, T_CTX // tm - 1), 0))

    nkv = H * (NOPE + VD)
    return pl.pallas_call(
        functools.partial(_inproj_kernel, tm=tm),
        grid=(T // tm,),
        in_specs=x_specs + [
            _resident((1, D), layer), _mod_row(0, per), _mod_row(1, per),
            _resident((D, N_U + Q_RANK), layer),
            _resident((D, 4 * LANES), layer),
            _resident((1, Q_RANK), layer),
            _resident((Q_RANK, H * (NOPE + 2 * ROPE)), layer),
            pl.BlockSpec((None, 2, tm, N_QR), lambda i: (kind(i), 0, i % per, 0)),
            _resident((1, KV_RANK), layer),
            pl.BlockSpec((None, 2, tm, LANES), lambda i: (kind(i), 0, i % per, 0)),
            _resident((KV_RANK, nkv), layer),
        ],
        out_specs=[rows(HY_W)] * 4 + [rows(H * (NOPE + ROPE)), ctx_rows(KV_RANK), ctx_rows(LANES), rows(LANES),
                                      rows(nkv)],
        out_shape=[jax.ShapeDtypeStruct((T, HY_W), BF16)] * 4 + [
            jax.ShapeDtypeStruct((T, H * (NOPE + ROPE)), BF16),
            jax.ShapeDtypeStruct((T_CTX, KV_RANK), F32),
            jax.ShapeDtypeStruct((T_CTX, LANES), F32),
            jax.ShapeDtypeStruct((T, LANES), BF16),
            jax.ShapeDtypeStruct((T, nkv), BF16),
        ],
        compiler_params=_cp(("arbitrary",), 58),
        name="in_proj",
    )(*xs, wts["g1"], mod3, mod3, wts["w_main"], wts["w_x"], wts["qg"], wts["wq"], consts["cs_q"],
      wts["kvg"], consts["cs_k"], wts["wkv"])


def _pool_kernel(u_ref, ic_ref, is_ref, pw_ref, ps_ref, o_ref):
    g = pl.program_id(0)

    def seq(r0, n, inv_ref):
        row = lax.broadcasted_iota(jnp.int32, (n, POOL_GC), 0)

        def later(x, m):
            return jnp.where(row < n - m, pltpu.roll(x, n - m, 0), 0.0)

        def earlier(x, m):
            return jnp.where(row >= m, pltpu.roll(x, m, 0), 0.0)

        for k, w in enumerate(POOL_WINDOWS):
            cs = slice(k * POOL_GC, (k + 1) * POOL_GC)
            u = u_ref[r0:r0 + n, cs].astype(F32)
            ahead, behind, m = u, earlier(u, 1), 1
            while m < w // 2:
                ahead = ahead + later(ahead, m)
                behind = behind + earlier(behind, m)
                m *= 2
            pooled = (ahead + behind) * inv_ref[:, cs] - u
            y = _dot(pooled.astype(BF16), pw_ref[k]) * ps_ref[:, cs]
            o_ref[r0:r0 + n, cs] = y.astype(o_ref.dtype)

    @pl.when(g < NG_CTX)
    def _():
        for s in range(SEQ_PER_GRP):
            seq(s * L_CTX, L_CTX, ic_ref)

    @pl.when(g >= NG_CTX)
    def _():
        seq(0, L_S, is_ref)


def _pool(proj, consts, pw, ps):
    nw = len(POOL_WINDOWS)
    return pl.pallas_call(
        _pool_kernel,
        grid=(NG,),
        in_specs=[
            pl.BlockSpec((GRP, POOL_W), lambda g: (g, 0)),
            pl.BlockSpec((L_CTX, POOL_W), lambda g: (0, 0)),
            pl.BlockSpec((L_S, POOL_W), lambda g: (0, 0)),
            pl.BlockSpec((nw, POOL_GC, POOL_GC), lambda g: (0, 0, 0)),
            pl.BlockSpec((1, POOL_W), lambda g: (0, 0)),
        ],
        out_specs=pl.BlockSpec((GRP, POOL_W), lambda g: (g, 0)),
        out_shape=jax.ShapeDtypeStruct((T, POOL_W), BF16),
        compiler_params=_cp(("parallel",)),
        name="pool_mix",
    )(proj, consts["inv_c"], consts["inv_s"], pw, ps)


HY_CH = HY_W


def _hyena_kernel(v_ref, x1_ref, x2_ref, swv_ref, sw1_ref, sw2_ref, sbv_ref, sb1_ref, sb2_ref,
                  skip_ref, wc_ref, wtc_ref, ws_ref, wts_ref, ktc_ref, kts_ref, o_ref):
    g = pl.program_id(0)

    def sconv(u_ref, sw_ref, sb_ref, r0, n, cs):
        u = u_ref[r0:r0 + n, cs].astype(F32)
        row = lax.broadcasted_iota(jnp.int32, u.shape, 0)
        prev = jnp.where(row == 0, 0.0, pltpu.roll(u, 1, 0))
        nxt = jnp.where(row == n - 1, 0.0, pltpu.roll(u, n - 1, 0))
        return prev * sw_ref[0:1, cs] + u * sw_ref[1:2, cs] + nxt * sw_ref[2:3, cs] + sb_ref[:, cs]

    def lconv(u, o, n, cs, w_ref, wt_ref, kt_ref):
        spec = _dot(w_ref[...], u.astype(BF16))
        pr, pi = spec[:n], spec[n:]
        ka, kb, ka2 = kt_ref[o, 0, :, cs], kt_ref[o, 1, :, cs], kt_ref[o, 2, :, cs]
        yr = pr * ka - pi * kb
        yi = pr * kb + pi * ka2
        prod = jnp.concatenate([yr, yi], axis=0).astype(BF16)
        return _dot(wt_ref[...], prod) + u * skip_ref[o:o + 1, cs]

    def seq(r0, n, cs, w_ref, wt_ref, kt_ref):
        v = sconv(v_ref, swv_ref, sbv_ref, r0, n, cs)
        x1 = sconv(x1_ref, sw1_ref, sb1_ref, r0, n, cs)
        x2 = sconv(x2_ref, sw2_ref, sb2_ref, r0, n, cs)
        z = x1 * lconv(v, 0, n, cs, w_ref, wt_ref, kt_ref)
        y = x2 * lconv(z, 1, n, cs, w_ref, wt_ref, kt_ref)
        o_ref[r0:r0 + n, cs] = y.astype(o_ref.dtype)

    chunks = [slice(c * HY_CH, (c + 1) * HY_CH) for c in range(HY_W // HY_CH)]

    @pl.when(g < NG_CTX)
    def _():
        for s in range(SEQ_PER_GRP):
            for cs in chunks:
                seq(s * L_CTX, L_CTX, cs, wc_ref, wtc_ref, ktc_ref)

    @pl.when(g >= NG_CTX)
    def _():
        for cs in chunks:
            seq(0, L_S, cs, ws_ref, wts_ref, kts_ref)


def _hyena(uv, u1, u2, sw, sb, skip, consts, kt_c, kt_s, layer):
    def part(p):
        return pl.BlockSpec((GRP, HY_W), lambda g: (g, 0))

    def swpart(p):
        return pl.BlockSpec((3, HY_W), lambda g: (0, p))

    def sbpart(p):
        return pl.BlockSpec((1, HY_W), lambda g: (0, p))

    def tables(n):
        return pl.BlockSpec((None, 2, 3, n, HY_W), lambda g: (layer, 0, 0, 0, 0), pipeline_mode=pl.Buffered(1))

    return pl.pallas_call(
        _hyena_kernel,
        grid=(NG,),
        in_specs=[
            part(0), part(1), part(2),
            swpart(0), swpart(1), swpart(2),
            sbpart(0), sbpart(1), sbpart(2),
            pl.BlockSpec((2, HY_W), lambda g: (0, 0)),
            pl.BlockSpec((2 * L_CTX, L_CTX), lambda g: (0, 0)),
            pl.BlockSpec((L_CTX, 2 * L_CTX), lambda g: (0, 0)),
            pl.BlockSpec((2 * L_S, L_S), lambda g: (0, 0)),
            pl.BlockSpec((L_S, 2 * L_S), lambda g: (0, 0)),
            tables(L_CTX), tables(L_S),
        ],
        out_specs=pl.BlockSpec((GRP, HY_W), lambda g: (g, 0)),
        out_shape=jax.ShapeDtypeStruct((T, HY_W), BF16),
        compiler_params=_cp(("parallel",), 56),
        name="hyena_mix",
    )(uv, u1, u2, sw, sw, sw, sb, sb, sb, skip,
      consts["dft_c"], consts["dftt_c"], consts["dft_s"], consts["dftt_s"], kt_c, kt_s)


def _cachekv_kernel(x_ref, w_ref, o_ref):
    o_ref[...] = _dot(x_ref[...].astype(BF16), w_ref[...]).astype(o_ref.dtype)


def _cachekv(cache_ckv, layer, wkv):
    return pl.pallas_call(
        _cachekv_kernel,
        grid=(B_S,),
        in_specs=[
            pl.BlockSpec((None, None, L_CTX, KV_RANK), lambda b: (b, layer, 0, 0)),
            pl.BlockSpec((None, KV_RANK, H * (NOPE + VD)), lambda b: (layer, 0, 0)),
        ],
        out_specs=pl.BlockSpec((L_CTX, H * (NOPE + VD)), lambda b: (b, 0)),
        out_shape=jax.ShapeDtypeStruct((B_S * L_CTX, H * (NOPE + VD)), BF16),
        compiler_params=_cp(("parallel",)),
        name="cache_kv",
    )(cache_ckv, wkv)


ATT_TQ = 512
NK_S = L_S + L_CTX


def _attn_kernel(q_ref, kv_ref, krr_ref, kvc_ref, krc_ref, o_ref, kcat_ref):
    g = pl.program_id(0)
    lane = lax.broadcasted_iota(jnp.int32, (1, LANES), 1)
    hk = NOPE + LANES
    vo = H * NOPE

    krr = krr_ref[...]
    for h in range(H):
        kcat_ref[0:GRP, h * hk:h * hk + NOPE] = kv_ref[:, h * NOPE:(h + 1) * NOPE]
        kcat_ref[0:GRP, h * hk + NOPE:(h + 1) * hk] = krr

    def qcat(rows, h):
        qn = q_ref[rows, h * NOPE:(h + 1) * NOPE]
        pair = q_ref[rows, vo + (h // 2) * LANES:vo + (h // 2 + 1) * LANES].astype(F32)
        keep = (lane < ROPE) if h % 2 == 0 else (lane >= ROPE)
        return jnp.concatenate([qn, jnp.where(keep, pair, 0.0).astype(BF16)], axis=1)

    def probs(sc):
        m = jnp.max(sc, axis=-1, keepdims=True)
        p = jnp.exp(sc - m)
        return p.astype(BF16), 1.0 / jnp.sum(p, axis=-1, keepdims=True)

    @pl.when(g < NG_CTX)
    def _():
        def body(s, carry):
            rows = pl.ds(pl.multiple_of(s * L_CTX, L_CTX), L_CTX)
            for h in range(H):
                p, rl = probs(_dot_nt(qcat(rows, h), kcat_ref[rows, h * hk:(h + 1) * hk]))
                o = _dot(p, kv_ref[rows, vo + h * VD:vo + (h + 1) * VD]) * rl
                o_ref[rows, h * VD:(h + 1) * VD] = o.astype(o_ref.dtype)
            return carry

        lax.fori_loop(0, SEQ_PER_GRP, body, 0)

    @pl.when(g >= NG_CTX)
    def _():
        krc = krc_ref[...]
        for h in range(H):
            kcat_ref[GRP:NK_S, h * hk:h * hk + NOPE] = kvc_ref[:, h * NOPE:(h + 1) * NOPE]
            kcat_ref[GRP:NK_S, h * hk + NOPE:(h + 1) * hk] = krc

        def body(t, carry):
            rows = pl.ds(pl.multiple_of(t * ATT_TQ, ATT_TQ), ATT_TQ)
            for h in range(H):
                p, rl = probs(_dot_nt(qcat(rows, h), kcat_ref[:, h * hk:(h + 1) * hk]))
                o = _dot(p[:, :GRP], kv_ref[:, vo + h * VD:vo + (h + 1) * VD])
                o = o + _dot(p[:, GRP:], kvc_ref[:, vo + h * VD:vo + (h + 1) * VD])
                o_ref[rows, h * VD:(h + 1) * VD] = (o * rl).astype(o_ref.dtype)
            return carry

        lax.fori_loop(0, L_S // ATT_TQ, body, 0)


def _attention(q, kv, krr, kvc, krc):
    def cache_blk(g):
        return jnp.maximum(g - NG_CTX, 0)

    return pl.pallas_call(
        _attn_kernel,
        grid=(NG,),
        in_specs=[
            pl.BlockSpec((GRP, H * (NOPE + ROPE)), lambda g: (g, 0)),
            pl.BlockSpec((GRP, H * (NOPE + VD)), lambda g: (g, 0)),
            pl.BlockSpec((GRP, LANES), lambda g: (g, 0)),
            pl.BlockSpec((L_CTX, H * (NOPE + VD)), lambda g: (cache_blk(g), 0)),
            pl.BlockSpec((L_CTX, LANES), lambda g: (cache_blk(g), 0)),
        ],
        out_specs=pl.BlockSpec((GRP, H * VD), lambda g: (g, 0)),
        out_shape=jax.ShapeDtypeStruct((T, H * VD), BF16),
        scratch_shapes=[pltpu.VMEM((NK_S, H * (NOPE + LANES)), BF16)],
        compiler_params=_cp(("parallel",)),
        name="mla_attention",
    )(q, kv, krr, kvc, krc)


def _wout_kernel(yp_ref, yh_ref, ym_ref, w_ref, gate_ref, g2_ref, shift_ref, scale_ref, rw_ref, *refs, tm):
    *x_refs, o_ref, h_ref, aff_ref, afft_ref, prev_ref = refs

    @pl.when(pl.program_id(0) == 0)
    def _():
        prev_ref[...] = jnp.zeros_like(prev_ref)

    hb = (_rms(prev_ref[...], g2_ref[...]) * (1.0 + scale_ref[...]) + shift_ref[...]).astype(BF16)
    h_ref[...] = hb
    logits = _dot(hb, rw_ref[...])
    lane = lax.broadcasted_iota(jnp.int32, logits.shape, 1)
    logits = jnp.where(lane < E, logits, -jnp.inf)
    ex = jnp.exp(logits - jnp.max(logits, axis=-1, keepdims=True))
    aff = ex / jnp.sum(ex, axis=-1, keepdims=True)
    aff_ref[...] = aff
    afft_ref[...] = aff.T[:E]

    y = jnp.concatenate([yp_ref[...], yh_ref[...], ym_ref[...]], axis=1)
    for c in range(D // NCH):
        cs = slice(c * NCH, (c + 1) * NCH)
        v = _load_src(x_refs, tm, cs) + gate_ref[:, cs] * _dot(y, w_ref[:, cs])
        o_ref[:, cs] = v.astype(o_ref.dtype)
        prev_ref[:, cs] = v


def _wout(yp, yh, ym, x, mod3, wts, layer):
    tm = 512
    per = GRP // tm
    n = T // tm
    xs, x_specs = _src_specs(x, tm)

    def cur(i):
        return jnp.minimum(i, n - 1)

    def prev(i):
        return jnp.maximum(i - 1, 0)

    def rows(width, blk):
        return pl.BlockSpec((tm, width), lambda i: (blk(i), 0))

    def mod_row(k, blk):
        return pl.BlockSpec((None, 1, D), lambda i: (6 * (blk(i) // per) + k, 0, 0))

    return pl.pallas_call(
        functools.partial(_wout_kernel, tm=tm),
        grid=(n + 1,),
        in_specs=[
            rows(POOL_W, cur), rows(HY_W, cur), rows(H * VD, cur),
            _resident((D, D), layer),
            mod_row(2, cur),
            _resident((1, D), layer), mod_row(3, prev), mod_row(4, prev),
            _resident((D, LANES), layer),
        ] + x_specs,
        out_specs=[rows(D, cur), rows(D, prev), rows(LANES, prev), pl.BlockSpec((E, tm), lambda i: (0, prev(i)))],
        out_shape=[
            jax.ShapeDtypeStruct((T, D), BF16),
            jax.ShapeDtypeStruct((T, D), BF16),
            jax.ShapeDtypeStruct((T, LANES), F32),
            jax.ShapeDtypeStruct((E, T), F32),
        ],
        scratch_shapes=[pltpu.VMEM((tm, D), F32)],
        compiler_params=_cp(("arbitrary",), 56),
        name="out_proj",
    )(yp, yh, ym, wts["w_out"], mod3, wts["g2"], mod3, mod3, wts["rw"], *xs)


RANK_CH = 256
SEARCH_ROUNDS = 14
SEARCH_WAYS = 8
AFF_MAX = 2.0


def _rank_kernel(aff_ref, afft_ref, tri_ref, rank_ref, rank_t_ref, cnt_ref, cut_ref):
    g = pl.program_id(0)

    def count_ge(a, t):
        return jnp.sum(jnp.where(a >= t, 1.0, 0.0), axis=1, keepdims=True)

    def search(specs, capf):
        acts = [afft_ref[:, r0:r0 + n] for r0, n in specs]
        lo = [jnp.zeros((E, 1), F32) for _ in specs]
        hi = [jnp.full((E, 1), AFF_MAX, F32) for _ in specs]
        for _ in range(SEARCH_ROUNDS):
            for s, a in enumerate(acts):
                step = (hi[s] - lo[s]) * (1.0 / SEARCH_WAYS)
                ts = [lo[s] + step * k for k in range(1, SEARCH_WAYS)]
                ok = [count_ge(a, t) >= capf for t in ts]
                new_lo, new_hi = lo[s], hi[s]
                for t, o in zip(ts, ok):
                    new_lo = jnp.where(o, t, new_lo)
                for t, o in zip(reversed(ts), reversed(ok)):
                    new_hi = jnp.where(o, new_hi, t)
                lo[s], hi[s] = new_lo, new_hi
        open_brackets = jnp.zeros((E, 1), F32)
        for s, a in enumerate(acts):
            top = jnp.max(jnp.where(a >= lo[s], jnp.where(a < hi[s], a, -1.0), -1.0), axis=1, keepdims=True)
            low = jnp.min(jnp.where(a >= lo[s], jnp.where(a < hi[s], a, AFF_MAX), AFF_MAX), axis=1, keepdims=True)
            cut_ref[:, s:s + 1] = top
            open_brackets = open_brackets + jnp.where(top != low, 1.0, 0.0)
        return jnp.sum(open_brackets)

    def exact_cut(s, r0, n, capf):
        for e in range(E):
            row = afft_ref[e:e + 1, r0:r0 + n]
            acc = jnp.zeros((1, n), F32)
            for c in range(n // RANK_CH):
                col = aff_ref[r0 + c * RANK_CH:r0 + (c + 1) * RANK_CH, e:e + 1]
                acc = acc + jnp.sum(jnp.where(col >= row, 1.0, 0.0), axis=0, keepdims=True)
            cnt_ref[e:e + 1, 0:n] = acc
        a = afft_ref[:, r0:r0 + n]
        cut_ref[:, s:s + 1] = jnp.max(jnp.where(cnt_ref[:, 0:n] >= capf, a, -1.0), axis=1, keepdims=True)

    def slots(s, r0, n, capf):
        a = afft_ref[:, r0:r0 + n]
        cut = cut_ref[:, s:s + 1]
        above = a > cut
        tied = a == cut
        above_f = jnp.where(above, 1.0, 0.0)
        n_above = jnp.sum(above_f, axis=1, keepdims=True)
        marks = jnp.concatenate([above_f, jnp.where(tied, 1.0, 0.0)], axis=0).astype(BF16)
        before = _dot(marks, tri_ref[0:n, 0:n])
        tie_slot = n_above + before[E:]
        slot = jnp.where(above, before[:E], jnp.where(tied, jnp.where(tie_slot < capf, tie_slot, n), n))
        rank_ref[:, r0:r0 + n] = slot.astype(jnp.int32)
        wide = jnp.concatenate([slot, jnp.zeros((LANES - E, n), F32)], axis=0)
        rank_t_ref[r0:r0 + n, :] = wide.T.astype(jnp.int32)

    def group(specs, cap):
        capf = float(cap)
        unresolved = search(specs, capf)

        @pl.when(unresolved > 0.0)
        def _():
            for s, (r0, n) in enumerate(specs):
                exact_cut(s, r0, n, capf)

        for s, (r0, n) in enumerate(specs):
            slots(s, r0, n, capf)

    @pl.when(g < NG_CTX)
    def _():
        group([(s * L_CTX, L_CTX) for s in range(SEQ_PER_GRP)], CAP_CTX)

    @pl.when(g >= NG_CTX)
    def _():
        group([(0, L_S)], CAP_S)


def _rank(aff, afft, tri):
    return pl.pallas_call(
        _rank_kernel,
        grid=(NG,),
        in_specs=[
            pl.BlockSpec((GRP, LANES), lambda g: (g, 0)),
            pl.BlockSpec((E, GRP), lambda g: (0, g)),
            pl.BlockSpec((L_S, L_S), lambda g: (0, 0)),
        ],
        out_specs=[pl.BlockSpec((E, GRP), lambda g: (0, g)), pl.BlockSpec((GRP, LANES), lambda g: (g, 0))],
        out_shape=[jax.ShapeDtypeStruct((E, T), jnp.int32), jax.ShapeDtypeStruct((T, LANES), jnp.int32)],
        scratch_shapes=[pltpu.VMEM((E, GRP), F32), pltpu.VMEM((E, LANES), F32)],
        compiler_params=_cp(("parallel",)),
        name="moe_rank",
    )(aff, afft, tri)


def _gather_kernel(rank_ref, afft_ref, h_ref, xs_ref, gs_ref, sel_ref):
    g = pl.program_id(0)

    def seq(r0, n, cap, slot0):
        slot_i = lax.broadcasted_iota(jnp.int32, (cap, n), 0)
        for e in range(E):
            hit = slot_i == rank_ref[e:e + 1, r0:r0 + n]
            sel_ref[e * cap:(e + 1) * cap, 0:n] = jnp.where(hit, 1.0, 0.0).astype(BF16)
            gs_ref[e, slot0:slot0 + cap, :] = jnp.sum(
                jnp.where(hit, afft_ref[e:e + 1, r0:r0 + n], 0.0), axis=1, keepdims=True)
        for c in range(D // NCH):
            cs = slice(c * NCH, (c + 1) * NCH)
            res = _dot(sel_ref[0:E * cap, 0:n], h_ref[r0:r0 + n, cs])
            for e in range(E):
                xs_ref[e, slot0:slot0 + cap, cs] = res[e * cap:(e + 1) * cap].astype(xs_ref.dtype)

    @pl.when(g < NG_CTX)
    def _():
        for s in range(SEQ_PER_GRP):
            seq(s * L_CTX, L_CTX, CAP_CTX, s * CAP_CTX)

    @pl.when(g >= NG_CTX)
    def _():
        seq(0, L_S, CAP_S, 0)


def _gather(rank, afft, h2):
    return pl.pallas_call(
        _gather_kernel,
        grid=(NG,),
        in_specs=[
            pl.BlockSpec((E, GRP), lambda g: (0, g)),
            pl.BlockSpec((E, GRP), lambda g: (0, g)),
            pl.BlockSpec((GRP, D), lambda g: (g, 0)),
        ],
        out_specs=[
            pl.BlockSpec((E, SLOTS, D), lambda g: (0, g, 0)),
            pl.BlockSpec((E, SLOTS, 1), lambda g: (0, g, 0)),
        ],
        out_shape=[
            jax.ShapeDtypeStruct((E, ROWS_E, D), BF16),
            jax.ShapeDtypeStruct((E, ROWS_E, 1), F32),
        ],
        scratch_shapes=[pltpu.VMEM((E * CAP_S, L_S), BF16)],
        compiler_params=_cp(("parallel",)),
        name="moe_gather",
    )(rank, afft, h2)


FFN_TF = 256
FFN_TM = 1024


FFN_NF = FF // FFN_TF


def _ffn_kernel(x_ref, wg_ref, wu_ref, wd_ref, gs_ref, o_ref, acc_ref, cg_ref, cu_ref, cd_ref):
    m = pl.program_id(1)
    f = pl.program_id(2)

    def step(fresh, role):
        if fresh:
            wg, wu, wd = (r[...].astype(BF16) for r in (wg_ref, wu_ref, wd_ref))
            cg_ref[f], cu_ref[f], cd_ref[f] = wg, wu, wd
        else:
            wg, wu, wd = cg_ref[f], cu_ref[f], cd_ref[f]
        x = x_ref[...]
        a = _dot(x, wg)
        b = _dot(x, wu)
        hid = (a * jax.nn.sigmoid(a) * b).astype(BF16)
        for c in range(D // NCH):
            cs = slice(c * NCH, (c + 1) * NCH)
            down = _dot(hid, wd[:, cs])
            if role == "first":
                acc_ref[:, cs] = down
            elif role == "middle":
                acc_ref[:, cs] += down
            else:
                o_ref[:, cs] = ((acc_ref[:, cs] + down) * gs_ref[...]).astype(o_ref.dtype)

    for fresh in (True, False):
        on_half = (m == 0) if fresh else (m != 0)
        pl.when(on_half & (f == 0))(functools.partial(step, fresh, "first"))
        pl.when(on_half & (f > 0) & (f < FFN_NF - 1))(functools.partial(step, fresh, "middle"))
        pl.when(on_half & (f == FFN_NF - 1))(functools.partial(step, fresh, "last"))


def _ffn(xs, gs, w_gate, w_up, w_down, layer):
    def blk(m, f):
        return jnp.where(m == 0, f, FFN_NF - 1)

    return pl.pallas_call(
        _ffn_kernel,
        grid=(E, ROWS_E // FFN_TM, FFN_NF),
        in_specs=[
            pl.BlockSpec((None, FFN_TM, D), lambda e, m, f: (e, m, 0)),
            pl.BlockSpec((None, None, D, FFN_TF), lambda e, m, f: (layer, e, 0, blk(m, f))),
            pl.BlockSpec((None, None, D, FFN_TF), lambda e, m, f: (layer, e, 0, blk(m, f))),
            pl.BlockSpec((None, None, FFN_TF, D), lambda e, m, f: (layer, e, blk(m, f), 0)),
            pl.BlockSpec((None, FFN_TM, 1), lambda e, m, f: (e, m, 0)),
        ],
        out_specs=pl.BlockSpec((None, FFN_TM, D), lambda e, m, f: (e, m, 0)),
        out_shape=jax.ShapeDtypeStruct((E, ROWS_E, D), BF16),
        scratch_shapes=[
            pltpu.VMEM((FFN_TM, D), F32),
            pltpu.VMEM((FFN_NF, D, FFN_TF), BF16),
            pltpu.VMEM((FFN_NF, D, FFN_TF), BF16),
            pltpu.VMEM((FFN_NF, FFN_TF, D), BF16),
        ],
        compiler_params=_cp(("arbitrary", "arbitrary", "arbitrary"), 58),
        name="moe_ffn",
    )(xs, w_gate, w_up, w_down, gs)


def _combine_kernel(rt_ref, ys_ref, x_ref, gate_ref, exc_ref, exs_ref, *refs, last, g0):
    if last:
        fg_ref, o_ref, st_ref = refs
    else:
        o_ref, st_ref = refs
    g = pl.program_id(0) + g0

    def scatter(r0, n, cap, slot0, ex_ref):
        ec = E * cap
        r = jnp.minimum(rt_ref[r0:r0 + n, :], cap).astype(F32).astype(BF16)
        want = (lax.broadcasted_iota(jnp.int32, (1, NCH), 1) & (cap - 1)).astype(F32)
        for c in range(ec // NCH):
            cs = slice(c * NCH, (c + 1) * NCH)
            st_ref[r0:r0 + n, cs] = jnp.where(_dot(r, ex_ref[:, cs]) == want, 1.0, 0.0).astype(BF16)
        for c in range(D // NCH):
            cs = slice(c * NCH, (c + 1) * NCH)
            ys = ys_ref[:, slot0:slot0 + cap, cs].reshape(ec, NCH)
            moe = _dot(st_ref[r0:r0 + n, 0:ec], ys)
            o_ref[r0:r0 + n, cs] = x_ref[r0:r0 + n, cs].astype(F32) + gate_ref[:, cs] * moe

    def finish():
        if last:
            o_ref[...] = _rms(o_ref[...], fg_ref[...])

    @pl.when(g < NG_CTX)
    def _():
        for s in range(SEQ_PER_GRP):
            scatter(s * L_CTX, L_CTX, CAP_CTX, s * CAP_CTX, exc_ref)
        finish()

    @pl.when(g >= NG_CTX)
    def _():
        scatter(0, L_S, CAP_S, 0, exs_ref)
        finish()


def _combine_groups(rank_t, ys, x, mod3, consts, final_g, g0, ng):
    last = final_g is not None
    in_specs = [
        pl.BlockSpec((GRP, LANES), lambda g: (g + g0, 0)),
        pl.BlockSpec((E, SLOTS, D), lambda g: (0, g + g0, 0)),
        pl.BlockSpec((GRP, D), lambda g: (g + g0, 0)),
        pl.BlockSpec((None, 1, D), lambda g: (6 * (g + g0) + 5, 0, 0)),
        pl.BlockSpec((LANES, E * CAP_CTX), lambda g: (0, 0)),
        pl.BlockSpec((LANES, E * CAP_S), lambda g: (0, 0)),
    ]
    args = [rank_t, ys, x, mod3, consts["ex_c"], consts["ex_s"]]
    if last:
        in_specs.append(pl.BlockSpec((1, D), lambda g: (0, 0)))
        args.append(final_g)
    return pl.pallas_call(
        functools.partial(_combine_kernel, last=last, g0=g0),
        grid=(ng,),
        in_specs=in_specs,
        out_specs=pl.BlockSpec((GRP, D), lambda g: ((g if last else g + g0), 0)),
        out_shape=jax.ShapeDtypeStruct(((ng * GRP if last else T), D), F32),
        scratch_shapes=[pltpu.VMEM((GRP, E * CAP_S), BF16)],
        compiler_params=_cp(("parallel",), 56),
        name="moe_combine",
    )(*args)


def _combine(rank_t, ys, x, mod3, consts, final_g=None):
    if final_g is None:
        return _combine_groups(rank_t, ys, x, mod3, consts, None, 0, NG)
    return (_combine_groups(rank_t, ys, x, mod3, consts, final_g, 0, NG_CTX),
            _combine_groups(rank_t, ys, x, mod3, consts, final_g, NG_CTX, NG - NG_CTX))


def _np_constants():
    c = {}
    for tag, n in (("c", L_CTX), ("s", L_S)):
        t = np.arange(n)
        inv = np.zeros((n, POOL_W), np.float32)
        for k, w in enumerate(POOL_WINDOWS):
            lo = np.clip(t - w // 2, 0, n)
            hi = np.clip(t - w // 2 + w, 0, n)
            inv[:, k * POOL_GC:(k + 1) * POOL_GC] = (1.0 / (hi - lo).astype(np.float64))[:, None]
        c["inv_" + tag] = inv
        kk = np.arange(n, dtype=np.float64)[:, None]
        tt = np.arange(n, dtype=np.float64)[None, :]
        ang = np.pi * kk * tt / n
        dft = np.concatenate([np.cos(ang), -np.sin(ang)], axis=0)
        dft[n] = (-1.0) ** np.arange(n)
        c["dft_" + tag] = dft.astype(np.float32)
        c["dftt_" + tag] = np.ascontiguousarray(dft.T).astype(np.float32)
        tl = np.linspace(0.0, 1.0, n, dtype=np.float32)[:, None]
        bands = np.arange(1, POS_BANDS + 1, dtype=np.float32)[None, :]
        feats = np.concatenate([tl, np.sin(2 * np.pi * tl * bands), np.cos(2 * np.pi * tl * bands)], axis=1)
        c["feat_" + tag] = np.pad(feats.astype(np.float32), ((0, 0), (0, LANES - feats.shape[1])))
        c["t_" + tag] = tl
    n_rows = L_S // GRID_W
    row = np.repeat(np.arange(n_rows), GRID_W).astype(np.float32)
    col = np.tile(np.arange(GRID_W), n_rows).astype(np.float32)
    nf = ROPE // 4
    inv_f = (1.0 / ROPE_THETA ** (np.arange(nf, dtype=np.float32) / nf)).astype(np.float32)
    a_row = (row[:, None] * inv_f[None]).astype(np.float32).astype(np.float64)
    a_col = (col[:, None] * inv_f[None]).astype(np.float32).astype(np.float64)
    cos64 = np.concatenate([np.cos(a_row), np.cos(a_row), np.cos(a_col), np.cos(a_col)], axis=1)
    sin64 = np.concatenate([np.sin(a_row), np.sin(a_row), np.sin(a_col), np.sin(a_col)], axis=1)

    def table(reps):
        ident = np.stack([np.ones((L_S, ROPE * reps)), np.zeros((L_S, ROPE * reps))])
        rot = np.stack([np.tile(cos64, (1, reps)), np.tile(sin64, (1, reps))])
        return np.stack([ident, rot]).astype(np.float32)

    c["cs_q"] = table(H)
    c["cs_k"] = table(LANES // ROPE)
    c["tri"] = np.triu(np.ones((L_S, L_S), np.float32), k=1)
    for tag, cap in (("c", CAP_CTX), ("s", CAP_S)):
        ex = np.zeros((LANES, E * cap), np.float32)
        ex[np.arange(E * cap) // cap, np.arange(E * cap)] = 1.0
        c["ex_" + tag] = ex
    return c


def _constants():
    c = {k: jnp.asarray(v) for k, v in _np_constants().items()}
    for k in ("dft_c", "dftt_c", "dft_s", "dftt_s", "ex_c", "ex_s", "tri"):
        c[k] = c[k].astype(BF16)
    return c


def _rope_swap(w):
    q = ROPE // 4
    return jnp.concatenate([-w[..., q:2 * q], w[..., :q], -w[..., 3 * q:], w[..., 2 * q:3 * q]], axis=-1)


def _dot_hi(a, b):
    return jnp.dot(a, b, preferred_element_type=F32, precision=lax.Precision.HIGHEST)


def _filter_kernel(feat_ref, t_ref, w1_ref, b1_ref, w2_ref, b2_ref, fr_ref, w3f_ref, w3b_ref,
                   ldf_ref, ldb_ref, dft_ref, o_ref, z_ref):
    n = feat_ref.shape[0]

    @pl.when((pl.program_id(1) == 0) & (pl.program_id(2) == 0))
    def _():
        fr = fr_ref[...]
        z = jnp.sin(fr * (_dot_hi(feat_ref[...], w1_ref[...]) + b1_ref[...]))
        z_ref[...] = jnp.sin(fr * (_dot_hi(z, w2_ref[...]) + b2_ref[...]))

    z = z_ref[...]
    t = t_ref[...]
    first = lax.broadcasted_iota(jnp.int32, (n, 1), 0) == 0
    hf = _dot_hi(z, w3f_ref[...]) * jnp.exp(-jnp.exp(ldf_ref[...]) * t)
    hb = _dot_hi(z, w3b_ref[...]) * jnp.exp(-jnp.exp(ldb_ref[...]) * t)
    hb = jnp.where(first, 0.0, hb)
    norm = jnp.sum(jnp.abs(hf), axis=0, keepdims=True) + jnp.sum(jnp.abs(hb), axis=0, keepdims=True) + EPS
    both = _dot(dft_ref[...], jnp.concatenate([hf / norm, hb / norm], axis=1).astype(BF16))
    pf, pb = both[:, :hf.shape[1]], both[:, hf.shape[1]:]
    sc = jnp.where(first, 0.5 / n, 1.0 / n)
    ka = (pf[:n] + pb[:n]) * sc
    o_ref[0] = ka
    o_ref[1] = jnp.where(first, 0.0, (pf[n:] - pb[n:]) * sc)
    o_ref[2] = jnp.where(first, (pf[n:] + pb[n:]) * sc, ka)


def _hyena_tables(n, feats, tcol, dft, fw):
    tc = 256
    nc = HY_W // tc

    def lay(shape):
        return pl.BlockSpec((None,) + shape, lambda l, o, c: (l, 0, 0))

    def w3(back):
        return pl.BlockSpec((None, LANES, tc), lambda l, o, c: (l, 0, (2 * o + back) * nc + c))

    def ld(back):
        return pl.BlockSpec((None, 1, tc), lambda l, o, c: (l, 0, (2 * o + back) * nc + c))

    return pl.pallas_call(
        _filter_kernel,
        grid=(DEPTH, 2, nc),
        in_specs=[
            pl.BlockSpec((n, LANES), lambda l, o, c: (0, 0)),
            pl.BlockSpec((n, 1), lambda l, o, c: (0, 0)),
            lay((LANES, LANES)), lay((1, LANES)), lay((LANES, LANES)), lay((1, LANES)), lay((1, LANES)),
            w3(0), w3(1), ld(0), ld(1),
            pl.BlockSpec((2 * n, n), lambda l, o, c: (0, 0)),
        ],
        out_specs=pl.BlockSpec((None, None, 3, n, tc), lambda l, o, c: (l, o, 0, 0, c)),
        out_shape=jax.ShapeDtypeStruct((DEPTH, 2, 3, n, HY_W), F32),
        scratch_shapes=[pltpu.VMEM((n, LANES), F32)],
        compiler_params=_cp(("arbitrary", "arbitrary", "arbitrary")),
        name="hyena_filter",
    )(feats, tcol, fw["w1"], fw["b1"], fw["w2"], fw["b2"], fw["freq"], fw["w3"], fw["w3"],
      fw["ld"], fw["ld"], dft)


def _filter_weights(w1, b1, w2, b2, w3, freq, log_decay):
    ph = LANES - FILT_HID
    return dict(
        w1=jnp.pad(w1, ((0, 0), (0, LANES - w1.shape[1]), (0, ph))),
        b1=jnp.pad(b1, ((0, 0), (0, ph))).reshape(DEPTH, 1, LANES),
        w2=jnp.pad(w2, ((0, 0), (0, ph), (0, ph))),
        b2=jnp.pad(b2, ((0, 0), (0, ph))).reshape(DEPTH, 1, LANES),
        freq=jnp.pad(freq, ((0, 0), (0, ph))).reshape(DEPTH, 1, LANES),
        w3=jnp.pad(w3, ((0, 0), (0, ph), (0, 0))),
        ld=log_decay.reshape(DEPTH, 1, 4 * HY_W))


def _prep_weights(w_in, pool_w, pool_scale, hy_short_b, mla_q_norm, mla_kv_norm, mla_w_uq, mla_w_ukv,
                  w_out, router_w, norm1_g, norm2_g):
    n_main = N_U + Q_RANK
    kr_cols = w_in[:, :, IN_COLS - ROPE:]
    kr_swap = _rope_swap(kr_cols)
    w_x = jnp.concatenate([w_in[:, :, n_main:IN_COLS - ROPE], kr_cols, kr_cols, kr_swap, kr_swap], axis=2)
    wq = mla_w_uq.reshape(DEPTH, Q_RANK, H, NOPE + ROPE)
    wq_rope = wq[..., NOPE:]
    wq = jnp.concatenate([wq[..., :NOPE].reshape(DEPTH, Q_RANK, -1), wq_rope.reshape(DEPTH, Q_RANK, -1),
                          _rope_swap(wq_rope).reshape(DEPTH, Q_RANK, -1)], axis=2)
    wkv = mla_w_ukv.reshape(DEPTH, KV_RANK, H, NOPE + VD)
    wkv = jnp.concatenate([wkv[..., :NOPE].reshape(DEPTH, KV_RANK, -1), wkv[..., NOPE:].reshape(DEPTH, KV_RANK, -1)],
                          axis=2)
    return dict(
        w_main=w_in.astype(BF16), w_x=w_x.astype(BF16), wq=wq.astype(BF16), wkv=wkv.astype(BF16),
        w_out=w_out.astype(BF16), rw=jnp.pad(router_w, ((0, 0), (0, 0), (0, LANES - E))).astype(BF16),
        pool_w=pool_w.astype(BF16), pool_scale=pool_scale.reshape(DEPTH, 1, POOL_W),
        sb=hy_short_b.reshape(DEPTH, 1, 3 * HY_W),
        qg=mla_q_norm.reshape(DEPTH, 1, Q_RANK), kvg=mla_kv_norm.reshape(DEPTH, 1, KV_RANK),
        g1=norm1_g.reshape(DEPTH, 1, D), g2=norm2_g.reshape(DEPTH, 1, D))


def _layer(x, layer, mod3, wts, consts, kt_c, kt_s, hy_short_w, hy_skip, cache_ckv, krc,
           exp_w_gate, exp_w_up, exp_w_down, final_g):
    up, uv, u1, u2, q, ckv, kr, krr, kv = _inproj(x, mod3, wts, consts, layer)
    y_pool = _pool(up, consts, wts["pool_w"][layer], wts["pool_scale"][layer])
    y_hy = _hyena(uv, u1, u2, hy_short_w[layer], wts["sb"][layer], hy_skip[layer], consts, kt_c, kt_s, layer)
    kvc = _cachekv(cache_ckv, layer, wts["wkv"])
    y_mla = _attention(q, kv, krr, kvc, krc)
    x, h2, aff, afft = _wout(y_pool, y_hy, y_mla, x, mod3, wts, layer)
    rank, rank_t = _rank(aff, afft, consts["tri"])
    xs, gs = _gather(rank, afft, h2)
    ys = _ffn(xs, gs, exp_w_gate, exp_w_up, exp_w_down, layer)
    return _combine(rank_t, ys, x, mod3, consts, final_g), ckv, kr


def kernel(x_prompt, x_sample, cache_ckv, cache_krope, c, c_ctx, ada_w, ada_b, norm1_g, norm2_g, w_in, pool_w, pool_scale, hy_short_w, hy_short_b, hy_ffn_w1, hy_ffn_b1, hy_ffn_w2, hy_ffn_b2, hy_ffn_w3, hy_freq, hy_log_decay, hy_skip, mla_q_norm, mla_kv_norm, mla_w_uq, mla_w_ukv, w_out, router_w, exp_w_gate, exp_w_up, exp_w_down, final_norm_g):
    consts = _constants()
    x = (x_prompt.reshape(T_CTX, D), x_sample.reshape(T - T_CTX, D))
    c16 = jnp.concatenate([jnp.broadcast_to(c_ctx[None], (NG_CTX, D)), c], axis=0)
    mod = _adaln_mod(c16, ada_w, ada_b).reshape(DEPTH, NG * 6, 1, D)

    fw = _filter_weights(hy_ffn_w1, hy_ffn_b1, hy_ffn_w2, hy_ffn_b2, hy_ffn_w3, hy_freq, hy_log_decay)
    kt_c = _hyena_tables(L_CTX, consts["feat_c"], consts["t_c"], consts["dft_c"], fw)
    kt_s = _hyena_tables(L_S, consts["feat_s"], consts["t_s"], consts["dft_s"], fw)

    wts = _prep_weights(w_in, pool_w, pool_scale, hy_short_b, mla_q_norm, mla_kv_norm, mla_w_uq, mla_w_ukv,
                        w_out, router_w, norm1_g, norm2_g)
    krc = jnp.swapaxes(cache_krope, 0, 1).reshape(DEPTH, B_S * L_CTX, ROPE)
    krc = jnp.concatenate([krc, krc], axis=2).astype(BF16)

    ckv_list, kr_list = [], []
    for l in range(DEPTH):
        final_g = final_norm_g.reshape(1, D) if l == DEPTH - 1 else None
        x, ckv, kr = _layer(x, l, mod[l], wts, consts, kt_c, kt_s, hy_short_w, hy_skip, cache_ckv, krc[l],
                            exp_w_gate, exp_w_up, exp_w_down, final_g)
        ckv_list.append(ckv.reshape(B_CTX, L_CTX, KV_RANK))
        kr_list.append(kr[:, :ROPE].reshape(B_CTX, L_CTX, ROPE))

    y_prompt, y_sample = x
    return (y_prompt.reshape(B_CTX, L_CTX, D), y_sample.reshape(B_S, L_S, D),
            jnp.stack(ckv_list, axis=1), jnp.stack(kr_list, axis=1))
```

```python
import functools
import math

import numpy as np
import jax
import jax.numpy as jnp
from jax import lax
from jax.experimental import pallas as pl
from jax.experimental.pallas import tpu as pltpu

F32 = jnp.float32
BF16 = jnp.bfloat16

D = 2048
DEPTH = 2
B_CTX, L_CTX = 32, 256
B_S, L_S = 8, 1024
T_CTX = B_CTX * L_CTX
T = T_CTX + B_S * L_S
GRP = 1024
NG = T // GRP
NG_CTX = T_CTX // GRP
SEQ_PER_GRP = GRP // L_CTX
EPS = 1e-6
GRID_W = 64

POOL_W = 512
POOL_GC = 128
POOL_WINDOWS = (2, 4, 8, 16)
HY_W = 512
POS_BANDS = 8
FILT_HID = 64
H = 8
NOPE = 128
ROPE = 64
VD = 128
Q_RANK = 512
KV_RANK = 256
ROPE_THETA = 10000.0
IN_COLS = 2880
E = 16
FF = 1024
CAP_CTX = 2 * L_CTX // E
CAP_S = 2 * L_S // E
SLOTS = GRP * 2 // E
ROWS_E = NG * SLOTS
LANES = 128
NCH = 512
ATT_SCALE = 1.0 / math.sqrt(NOPE + ROPE)
MB = 1024 * 1024


def _cp(sem, vmem_mb=48):
    return pltpu.CompilerParams(dimension_semantics=sem, vmem_limit_bytes=vmem_mb * MB)


def _rms(x, g):
    return x * lax.rsqrt(jnp.mean(x * x, axis=-1, keepdims=True) + EPS) * g


def _dot(a, b):
    return jnp.dot(a, b, preferred_element_type=F32)


def _dot_nt(a, b):
    return lax.dot_general(a, b, (((1,), (1,)), ((), ())), preferred_element_type=F32)


def _mod_kernel(c_ref, w_ref, b_ref, o_ref):
    c = c_ref[...]
    a = (c * jax.nn.sigmoid(c)).astype(BF16)
    o_ref[...] = _dot(a, w_ref[...].astype(BF16)) + b_ref[...]


def _adaln_mod(c16, ada_w, ada_b):
    tn = 1024
    return pl.pallas_call(
        _mod_kernel,
        grid=(DEPTH, 6 * D // tn),
        in_specs=[
            pl.BlockSpec((NG, D), lambda l, j: (0, 0)),
            pl.BlockSpec((None, D, tn), lambda l, j: (l, 0, j)),
            pl.BlockSpec((None, 1, tn), lambda l, j: (l, 0, j)),
        ],
        out_specs=pl.BlockSpec((None, NG, tn), lambda l, j: (l, 0, j)),
        out_shape=jax.ShapeDtypeStruct((DEPTH, NG, 6 * D), F32),
        compiler_params=_cp(("parallel", "parallel")),
        name="adaln_mod",
    )(c16, ada_w, ada_b.reshape(DEPTH, 1, 6 * D))


def _src_specs(x, tm):
    n_ctx = T_CTX // tm
    n_all = T // tm
    if isinstance(x, tuple):
        return list(x), [
            pl.BlockSpec((tm, D), lambda i: (jnp.minimum(i, n_ctx - 1), 0)),
            pl.BlockSpec((tm, D), lambda i: (jnp.clip(i - n_ctx, 0, n_all - n_ctx - 1), 0)),
        ]
    return [x], [pl.BlockSpec((tm, D), lambda i: (jnp.minimum(i, n_all - 1), 0))]


def _load_src(x_refs, tm, cs=slice(None)):
    if len(x_refs) == 1:
        return x_refs[0][:, cs]
    return jnp.where(pl.program_id(0) < T_CTX // tm, x_refs[0][:, cs], x_refs[1][:, cs])


def _resident(shape, layer):
    nd = len(shape)
    return pl.BlockSpec((None,) + shape, lambda i: (layer,) + (0,) * nd, pipeline_mode=pl.Buffered(1))


def _mod_row(k, per):
    return pl.BlockSpec((None, 1, D), lambda i: (6 * (i // per) + k, 0, 0))


N_U = POOL_W + 3 * HY_W
N_QR = H * ROPE


def _inproj_kernel(*refs, tm):
    (*x_refs, g_ref, shift_ref, scale_ref, wm_ref, wx_ref, qg_ref, wq_ref, csq_ref, kvg_ref, csk_ref, wkv_ref,
     up_ref, uv_ref, u1_ref, u2_ref, q_ref, ckv_ref, kr_ref, krr_ref, kv_ref) = refs

    y = _rms(_load_src(x_refs, tm), g_ref[...])
    h = (y * (1.0 + scale_ref[...]) + shift_ref[...]).astype(BF16)
    for c, u_ref in enumerate((up_ref, uv_ref, u1_ref, u2_ref)):
        u_ref[...] = _dot(h, wm_ref[:, c * HY_W:(c + 1) * HY_W]).astype(u_ref.dtype)

    qn = _rms(_dot(h, wm_ref[:, N_U:]), qg_ref[...]).astype(BF16)
    nq = H * NOPE
    for c in range(nq // NCH):
        cs = slice(c * NCH, (c + 1) * NCH)
        q_ref[:, cs] = (_dot(qn, wq_ref[:, cs]) * ATT_SCALE).astype(q_ref.dtype)
    rot = _dot(qn, wq_ref[:, nq:nq + N_QR]) * csq_ref[0] + _dot(qn, wq_ref[:, nq + N_QR:]) * csq_ref[1]
    q_ref[:, nq:] = (rot * ATT_SCALE).astype(q_ref.dtype)

    kx = _dot(h, wx_ref[...])
    ckv = _rms(kx[:, :KV_RANK], kvg_ref[...])
    kr = kx[:, KV_RANK:KV_RANK + LANES]

    @pl.when(pl.program_id(0) < T_CTX // tm)
    def _():
        ckv_ref[...] = ckv
        kr_ref[...] = kr

    krr_ref[...] = (kr * csk_ref[0] + kx[:, KV_RANK + LANES:] * csk_ref[1]).astype(krr_ref.dtype)
    kv_ref[...] = _dot(ckv.astype(BF16), wkv_ref[...]).astype(kv_ref.dtype)


def _inproj(x, mod3, wts, consts, layer):
    tm = 512
    per = GRP // tm
    xs, x_specs = _src_specs(x, tm)

    def kind(i):
        return jnp.where(i >= T_CTX // tm, 1, 0)

    def rows(width):
        return pl.BlockSpec((tm, width), lambda i: (i, 0))

    def ctx_rows(width):
        return pl.BlockSpec((tm, width), lambda i: (jnp.minimum(i, T_CTX // tm - 1), 0))

    nkv = H * (NOPE + VD)
    return pl.pallas_call(
        functools.partial(_inproj_kernel, tm=tm),
        grid=(T // tm,),
        in_specs=x_specs + [
            _resident((1, D), layer), _mod_row(0, per), _mod_row(1, per),
            _resident((D, N_U + Q_RANK), layer),
            _resident((D, 4 * LANES), layer),
            _resident((1, Q_RANK), layer),
            _resident((Q_RANK, H * (NOPE + 2 * ROPE)), layer),
            pl.BlockSpec((None, 2, tm, N_QR), lambda i: (kind(i), 0, i % per, 0)),
            _resident((1, KV_RANK), layer),
            pl.BlockSpec((None, 2, tm, LANES), lambda i: (kind(i), 0, i % per, 0)),
            _resident((KV_RANK, nkv), layer),
        ],
        out_specs=[rows(HY_W)] * 4 + [rows(H * (NOPE + ROPE)), ctx_rows(KV_RANK), ctx_rows(LANES), rows(LANES),
                                      rows(nkv)],
        out_shape=[jax.ShapeDtypeStruct((T, HY_W), BF16)] * 4 + [
            jax.ShapeDtypeStruct((T, H * (NOPE + ROPE)), BF16),
            jax.ShapeDtypeStruct((T_CTX, KV_RANK), F32),
            jax.ShapeDtypeStruct((T_CTX, LANES), F32),
            jax.ShapeDtypeStruct((T, LANES), BF16),
            jax.ShapeDtypeStruct((T, nkv), BF16),
        ],
        compiler_params=_cp(("arbitrary",), 58),
        name="in_proj",
    )(*xs, wts["g1"], mod3, mod3, wts["w_main"], wts["w_x"], wts["qg"], wts["wq"], consts["cs_q"],
      wts["kvg"], consts["cs_k"], wts["wkv"])


def _pool_kernel(u_ref, ic_ref, is_ref, pw_ref, ps_ref, o_ref):
    g = pl.program_id(0)

    def seq(r0, n, inv_ref):
        row = lax.broadcasted_iota(jnp.int32, (n, POOL_GC), 0)

        def later(x, m):
            return jnp.where(row < n - m, pltpu.roll(x, n - m, 0), 0.0)

        def earlier(x, m):
            return jnp.where(row >= m, pltpu.roll(x, m, 0), 0.0)

        for k, w in enumerate(POOL_WINDOWS):
            cs = slice(k * POOL_GC, (k + 1) * POOL_GC)
            u = u_ref[r0:r0 + n, cs].astype(F32)
            ahead, behind, m = u, earlier(u, 1), 1
            while m < w // 2:
                ahead = ahead + later(ahead, m)
                behind = behind + earlier(behind, m)
                m *= 2
            pooled = (ahead + behind) * inv_ref[:, cs] - u
            y = _dot(pooled.astype(BF16), pw_ref[k]) * ps_ref[:, cs]
            o_ref[r0:r0 + n, cs] = y.astype(o_ref.dtype)

    @pl.when(g < NG_CTX)
    def _():
        for s in range(SEQ_PER_GRP):
            seq(s * L_CTX, L_CTX, ic_ref)

    @pl.when(g >= NG_CTX)
    def _():
        seq(0, L_S, is_ref)


def _pool(proj, consts, pw, ps):
    nw = len(POOL_WINDOWS)
    return pl.pallas_call(
        _pool_kernel,
        grid=(NG,),
        in_specs=[
            pl.BlockSpec((GRP, POOL_W), lambda g: (g, 0)),
            pl.BlockSpec((L_CTX, POOL_W), lambda g: (0, 0)),
            pl.BlockSpec((L_S, POOL_W), lambda g: (0, 0)),
            pl.BlockSpec((nw, POOL_GC, POOL_GC), lambda g: (0, 0, 0)),
            pl.BlockSpec((1, POOL_W), lambda g: (0, 0)),
        ],
        out_specs=pl.BlockSpec((GRP, POOL_W), lambda g: (g, 0)),
        out_shape=jax.ShapeDtypeStruct((T, POOL_W), BF16),
        compiler_params=_cp(("parallel",)),
        name="pool_mix",
    )(proj, consts["inv_c"], consts["inv_s"], pw, ps)


HY_CH = HY_W


def _hyena_kernel(v_ref, x1_ref, x2_ref, swv_ref, sw1_ref, sw2_ref, sbv_ref, sb1_ref, sb2_ref,
                  skip_ref, wc_ref, wtc_ref, ws_ref, wts_ref, ktc_ref, kts_ref, o_ref):
    g = pl.program_id(0)

    def sconv(u_ref, sw_ref, sb_ref, r0, n, cs):
        u = u_ref[r0:r0 + n, cs].astype(F32)
        row = lax.broadcasted_iota(jnp.int32, u.shape, 0)
        prev = jnp.where(row == 0, 0.0, pltpu.roll(u, 1, 0))
        nxt = jnp.where(row == n - 1, 0.0, pltpu.roll(u, n - 1, 0))
        return prev * sw_ref[0:1, cs] + u * sw_ref[1:2, cs] + nxt * sw_ref[2:3, cs] + sb_ref[:, cs]

    def lconv(u, o, n, cs, w_ref, wt_ref, kt_ref):
        spec = _dot(w_ref[...], u.astype(BF16))
        pr, pi = spec[:n], spec[n:]
        ka, kb, ka2 = kt_ref[o, 0, :, cs], kt_ref[o, 1, :, cs], kt_ref[o, 2, :, cs]
        yr = pr * ka - pi * kb
        yi = pr * kb + pi * ka2
        prod = jnp.concatenate([yr, yi], axis=0).astype(BF16)
        return _dot(wt_ref[...], prod) + u * skip_ref[o:o + 1, cs]

    def seq(r0, n, cs, w_ref, wt_ref, kt_ref):
        v = sconv(v_ref, swv_ref, sbv_ref, r0, n, cs)
        x1 = sconv(x1_ref, sw1_ref, sb1_ref, r0, n, cs)
        x2 = sconv(x2_ref, sw2_ref, sb2_ref, r0, n, cs)
        z = x1 * lconv(v, 0, n, cs, w_ref, wt_ref, kt_ref)
        y = x2 * lconv(z, 1, n, cs, w_ref, wt_ref, kt_ref)
        o_ref[r0:r0 + n, cs] = y.astype(o_ref.dtype)

    chunks = [slice(c * HY_CH, (c + 1) * HY_CH) for c in range(HY_W // HY_CH)]

    @pl.when(g < NG_CTX)
    def _():
        for s in range(SEQ_PER_GRP):
            for cs in chunks:
                seq(s * L_CTX, L_CTX, cs, wc_ref, wtc_ref, ktc_ref)

    @pl.when(g >= NG_CTX)
    def _():
        for cs in chunks:
            seq(0, L_S, cs, ws_ref, wts_ref, kts_ref)


def _hyena(uv, u1, u2, sw, sb, skip, consts, kt_c, kt_s, layer):
    rows = pl.BlockSpec((GRP, HY_W), lambda g: (g, 0))

    def swpart(p):
        return pl.BlockSpec((3, HY_W), lambda g: (0, p))

    def sbpart(p):
        return pl.BlockSpec((1, HY_W), lambda g: (0, p))

    def tables(n):
        return pl.BlockSpec((None, 2, 3, n, HY_W), lambda g: (layer, 0, 0, 0, 0), pipeline_mode=pl.Buffered(1))

    return pl.pallas_call(
        _hyena_kernel,
        grid=(NG,),
        in_specs=[
            rows, rows, rows,
            swpart(0), swpart(1), swpart(2),
            sbpart(0), sbpart(1), sbpart(2),
            pl.BlockSpec((2, HY_W), lambda g: (0, 0)),
            pl.BlockSpec((2 * L_CTX, L_CTX), lambda g: (0, 0)),
            pl.BlockSpec((L_CTX, 2 * L_CTX), lambda g: (0, 0)),
            pl.BlockSpec((2 * L_S, L_S), lambda g: (0, 0)),
            pl.BlockSpec((L_S, 2 * L_S), lambda g: (0, 0)),
            tables(L_CTX), tables(L_S),
        ],
        out_specs=pl.BlockSpec((GRP, HY_W), lambda g: (g, 0)),
        out_shape=jax.ShapeDtypeStruct((T, HY_W), BF16),
        compiler_params=_cp(("parallel",), 56),
        name="hyena_mix",
    )(uv, u1, u2, sw, sw, sw, sb, sb, sb, skip,
      consts["dft_c"], consts["dftt_c"], consts["dft_s"], consts["dftt_s"], kt_c, kt_s)


def _cachekv_kernel(x_ref, w_ref, o_ref):
    o_ref[...] = _dot(x_ref[...].astype(BF16), w_ref[...]).astype(o_ref.dtype)


def _cachekv(cache_ckv, layer, wkv):
    return pl.pallas_call(
        _cachekv_kernel,
        grid=(B_S,),
        in_specs=[
            pl.BlockSpec((None, None, L_CTX, KV_RANK), lambda b: (b, layer, 0, 0)),
            pl.BlockSpec((None, KV_RANK, H * (NOPE + VD)), lambda b: (layer, 0, 0)),
        ],
        out_specs=pl.BlockSpec((L_CTX, H * (NOPE + VD)), lambda b: (b, 0)),
        out_shape=jax.ShapeDtypeStruct((B_S * L_CTX, H * (NOPE + VD)), BF16),
        compiler_params=_cp(("parallel",)),
        name="cache_kv",
    )(cache_ckv, wkv)


ATT_TQ = 512
NK_S = L_S + L_CTX


def _attn_kernel(q_ref, kv_ref, krr_ref, kvc_ref, krc_ref, o_ref, kcat_ref):
    g = pl.program_id(0)
    lane = lax.broadcasted_iota(jnp.int32, (1, LANES), 1)
    hk = NOPE + LANES
    vo = H * NOPE

    krr = krr_ref[...]
    for h in range(H):
        kcat_ref[0:GRP, h * hk:h * hk + NOPE] = kv_ref[:, h * NOPE:(h + 1) * NOPE]
        kcat_ref[0:GRP, h * hk + NOPE:(h + 1) * hk] = krr

    def qcat(rows, h):
        qn = q_ref[rows, h * NOPE:(h + 1) * NOPE]
        pair = q_ref[rows, vo + (h // 2) * LANES:vo + (h // 2 + 1) * LANES].astype(F32)
        keep = (lane < ROPE) if h % 2 == 0 else (lane >= ROPE)
        return jnp.concatenate([qn, jnp.where(keep, pair, 0.0).astype(BF16)], axis=1)

    def probs(sc):
        m = jnp.max(sc, axis=-1, keepdims=True)
        p = jnp.exp(sc - m)
        return p.astype(BF16), 1.0 / jnp.sum(p, axis=-1, keepdims=True)

    @pl.when(g < NG_CTX)
    def _():
        def body(s, carry):
            rows = pl.ds(pl.multiple_of(s * L_CTX, L_CTX), L_CTX)
            for h in range(H):
                p, rl = probs(_dot_nt(qcat(rows, h), kcat_ref[rows, h * hk:(h + 1) * hk]))
                o = _dot(p, kv_ref[rows, vo + h * VD:vo + (h + 1) * VD]) * rl
                o_ref[rows, h * VD:(h + 1) * VD] = o.astype(o_ref.dtype)
            return carry

        lax.fori_loop(0, SEQ_PER_GRP, body, 0)

    @pl.when(g >= NG_CTX)
    def _():
        krc = krc_ref[...]
        for h in range(H):
            kcat_ref[GRP:NK_S, h * hk:h * hk + NOPE] = kvc_ref[:, h * NOPE:(h + 1) * NOPE]
            kcat_ref[GRP:NK_S, h * hk + NOPE:(h + 1) * hk] = krc

        def body(t, carry):
            rows = pl.ds(pl.multiple_of(t * ATT_TQ, ATT_TQ), ATT_TQ)
            for h in range(H):
                p, rl = probs(_dot_nt(qcat(rows, h), kcat_ref[:, h * hk:(h + 1) * hk]))
                o = _dot(p[:, :GRP], kv_ref[:, vo + h * VD:vo + (h + 1) * VD])
                o = o + _dot(p[:, GRP:], kvc_ref[:, vo + h * VD:vo + (h + 1) * VD])
                o_ref[rows, h * VD:(h + 1) * VD] = (o * rl).astype(o_ref.dtype)
            return carry

        lax.fori_loop(0, L_S // ATT_TQ, body, 0)


def _attention(q, kv, krr, kvc, krc):
    def cache_blk(g):
        return jnp.maximum(g - NG_CTX, 0)

    return pl.pallas_call(
        _attn_kernel,
        grid=(NG,),
        in_specs=[
            pl.BlockSpec((GRP, H * (NOPE + ROPE)), lambda g: (g, 0)),
            pl.BlockSpec((GRP, H * (NOPE + VD)), lambda g: (g, 0)),
            pl.BlockSpec((GRP, LANES), lambda g: (g, 0)),
            pl.BlockSpec((L_CTX, H * (NOPE + VD)), lambda g: (cache_blk(g), 0)),
            pl.BlockSpec((L_CTX, LANES), lambda g: (cache_blk(g), 0)),
        ],
        out_specs=pl.BlockSpec((GRP, H * VD), lambda g: (g, 0)),
        out_shape=jax.ShapeDtypeStruct((T, H * VD), BF16),
        scratch_shapes=[pltpu.VMEM((NK_S, H * (NOPE + LANES)), BF16)],
        compiler_params=_cp(("parallel",)),
        name="mla_attention",
    )(q, kv, krr, kvc, krc)


def _wout_kernel(yp_ref, yh_ref, ym_ref, w_ref, gate_ref, g2_ref, shift_ref, scale_ref, rw_ref, *refs, tm):
    *x_refs, o_ref, h_ref, aff_ref, afft_ref, prev_ref = refs

    @pl.when(pl.program_id(0) == 0)
    def _():
        prev_ref[...] = jnp.zeros_like(prev_ref)

    hb = (_rms(prev_ref[...], g2_ref[...]) * (1.0 + scale_ref[...]) + shift_ref[...]).astype(BF16)
    h_ref[...] = hb
    logits = _dot(hb, rw_ref[...])
    lane = lax.broadcasted_iota(jnp.int32, logits.shape, 1)
    logits = jnp.where(lane < E, logits, -jnp.inf)
    ex = jnp.exp(logits - jnp.max(logits, axis=-1, keepdims=True))
    aff = ex / jnp.sum(ex, axis=-1, keepdims=True)
    aff_ref[...] = aff
    afft_ref[...] = aff.T[:E]

    y = jnp.concatenate([yp_ref[...], yh_ref[...], ym_ref[...]], axis=1)
    for c in range(D // NCH):
        cs = slice(c * NCH, (c + 1) * NCH)
        v = _load_src(x_refs, tm, cs) + gate_ref[:, cs] * _dot(y, w_ref[:, cs])
        o_ref[:, cs] = v.astype(o_ref.dtype)
        prev_ref[:, cs] = v


def _wout(yp, yh, ym, x, mod3, wts, layer):
    tm = 512
    per = GRP // tm
    n = T // tm
    xs, x_specs = _src_specs(x, tm)

    def cur(i):
        return jnp.minimum(i, n - 1)

    def prev(i):
        return jnp.maximum(i - 1, 0)

    def rows(width, blk):
        return pl.BlockSpec((tm, width), lambda i: (blk(i), 0))

    def mod_row(k, blk):
        return pl.BlockSpec((None, 1, D), lambda i: (6 * (blk(i) // per) + k, 0, 0))

    return pl.pallas_call(
        functools.partial(_wout_kernel, tm=tm),
        grid=(n + 1,),
        in_specs=[
            rows(POOL_W, cur), rows(HY_W, cur), rows(H * VD, cur),
            _resident((D, D), layer),
            mod_row(2, cur),
            _resident((1, D), layer), mod_row(3, prev), mod_row(4, prev),
            _resident((D, LANES), layer),
        ] + x_specs,
        out_specs=[rows(D, cur), rows(D, prev), rows(LANES, prev), pl.BlockSpec((E, tm), lambda i: (0, prev(i)))],
        out_shape=[
            jax.ShapeDtypeStruct((T, D), BF16),
            jax.ShapeDtypeStruct((T, D), BF16),
            jax.ShapeDtypeStruct((T, LANES), F32),
            jax.ShapeDtypeStruct((E, T), F32),
        ],
        scratch_shapes=[pltpu.VMEM((tm, D), F32)],
        compiler_params=_cp(("arbitrary",), 56),
        name="out_proj",
    )(yp, yh, ym, wts["w_out"], mod3, wts["g2"], mod3, mod3, wts["rw"], *xs)


RANK_CH = 256
SEARCH_ROUNDS = 14
SEARCH_WAYS = 8
AFF_MAX = 2.0


def _rank_kernel(aff_ref, afft_ref, tri_ref, rank_ref, rank_t_ref, cnt_ref, cut_ref):
    g = pl.program_id(0)

    def count_ge(a, t):
        return jnp.sum(jnp.where(a >= t, 1.0, 0.0), axis=1, keepdims=True)

    def search(specs, capf):
        acts = [afft_ref[:, r0:r0 + n] for r0, n in specs]
        lo = [jnp.zeros((E, 1), F32) for _ in specs]
        hi = [jnp.full((E, 1), AFF_MAX, F32) for _ in specs]
        for _ in range(SEARCH_ROUNDS):
            for s, a in enumerate(acts):
                step = (hi[s] - lo[s]) * (1.0 / SEARCH_WAYS)
                ts = [lo[s] + step * k for k in range(1, SEARCH_WAYS)]
                ok = [count_ge(a, t) >= capf for t in ts]
                new_lo, new_hi = lo[s], hi[s]
                for t, o in zip(ts, ok):
                    new_lo = jnp.where(o, t, new_lo)
                for t, o in zip(reversed(ts), reversed(ok)):
                    new_hi = jnp.where(o, new_hi, t)
                lo[s], hi[s] = new_lo, new_hi
        open_brackets = jnp.zeros((E, 1), F32)
        for s, a in enumerate(acts):
            top = jnp.max(jnp.where(a >= lo[s], jnp.where(a < hi[s], a, -1.0), -1.0), axis=1, keepdims=True)
            low = jnp.min(jnp.where(a >= lo[s], jnp.where(a < hi[s], a, AFF_MAX), AFF_MAX), axis=1, keepdims=True)
            cut_ref[:, s:s + 1] = top
            open_brackets = open_brackets + jnp.where(top != low, 1.0, 0.0)
        return jnp.sum(open_brackets)

    def exact_cut(s, r0, n, capf):
        for e in range(E):
            row = afft_ref[e:e + 1, r0:r0 + n]
            acc = jnp.zeros((1, n), F32)
            for c in range(n // RANK_CH):
                col = aff_ref[r0 + c * RANK_CH:r0 + (c + 1) * RANK_CH, e:e + 1]
                acc = acc + jnp.sum(jnp.where(col >= row, 1.0, 0.0), axis=0, keepdims=True)
            cnt_ref[e:e + 1, 0:n] = acc
        a = afft_ref[:, r0:r0 + n]
        cut_ref[:, s:s + 1] = jnp.max(jnp.where(cnt_ref[:, 0:n] >= capf, a, -1.0), axis=1, keepdims=True)

    def slots(s, r0, n, capf):
        a = afft_ref[:, r0:r0 + n]
        cut = cut_ref[:, s:s + 1]
        above = a > cut
        tied = a == cut
        above_f = jnp.where(above, 1.0, 0.0)
        n_above = jnp.sum(above_f, axis=1, keepdims=True)
        marks = jnp.concatenate([above_f, jnp.where(tied, 1.0, 0.0)], axis=0).astype(BF16)
        before = _dot(marks, tri_ref[0:n, 0:n])
        tie_slot = n_above + before[E:]
        slot = jnp.where(above, before[:E], jnp.where(tied, jnp.where(tie_slot < capf, tie_slot, n), n))
        rank_ref[:, r0:r0 + n] = slot.astype(jnp.int32)
        wide = jnp.concatenate([slot, jnp.zeros((LANES - E, n), F32)], axis=0)
        rank_t_ref[r0:r0 + n, :] = wide.T.astype(jnp.int32)

    def group(specs, cap):
        capf = float(cap)
        unresolved = search(specs, capf)

        @pl.when(unresolved > 0.0)
        def _():
            for s, (r0, n) in enumerate(specs):
                exact_cut(s, r0, n, capf)

        for s, (r0, n) in enumerate(specs):
            slots(s, r0, n, capf)

    @pl.when(g < NG_CTX)
    def _():
        group([(s * L_CTX, L_CTX) for s in range(SEQ_PER_GRP)], CAP_CTX)

    @pl.when(g >= NG_CTX)
    def _():
        group([(0, L_S)], CAP_S)


def _rank(aff, afft, tri):
    return pl.pallas_call(
        _rank_kernel,
        grid=(NG,),
        in_specs=[
            pl.BlockSpec((GRP, LANES), lambda g: (g, 0)),
            pl.BlockSpec((E, GRP), lambda g: (0, g)),
            pl.BlockSpec((L_S, L_S), lambda g: (0, 0)),
        ],
        out_specs=[pl.BlockSpec((E, GRP), lambda g: (0, g)), pl.BlockSpec((GRP, LANES), lambda g: (g, 0))],
        out_shape=[jax.ShapeDtypeStruct((E, T), jnp.int32), jax.ShapeDtypeStruct((T, LANES), jnp.int32)],
        scratch_shapes=[pltpu.VMEM((E, GRP), F32), pltpu.VMEM((E, LANES), F32)],
        compiler_params=_cp(("parallel",)),
        name="moe_rank",
    )(aff, afft, tri)


def _gather_kernel(rank_ref, afft_ref, h_ref, xs_ref, gs_ref, sel_ref):
    g = pl.program_id(0)

    def seq(r0, n, cap, slot0):
        slot_i = lax.broadcasted_iota(jnp.int32, (cap, n), 0)
        for e in range(E):
            hit = slot_i == rank_ref[e:e + 1, r0:r0 + n]
            sel_ref[e * cap:(e + 1) * cap, 0:n] = jnp.where(hit, 1.0, 0.0).astype(BF16)
            gs_ref[e, slot0:slot0 + cap, :] = jnp.sum(
                jnp.where(hit, afft_ref[e:e + 1, r0:r0 + n], 0.0), axis=1, keepdims=True)
        for c in range(D // NCH):
            cs = slice(c * NCH, (c + 1) * NCH)
            res = _dot(sel_ref[0:E * cap, 0:n], h_ref[r0:r0 + n, cs])
            for e in range(E):
                xs_ref[e, slot0:slot0 + cap, cs] = res[e * cap:(e + 1) * cap].astype(xs_ref.dtype)

    @pl.when(g < NG_CTX)
    def _():
        for s in range(SEQ_PER_GRP):
            seq(s * L_CTX, L_CTX, CAP_CTX, s * CAP_CTX)

    @pl.when(g >= NG_CTX)
    def _():
        seq(0, L_S, CAP_S, 0)


def _gather(rank, afft, h2):
    return pl.pallas_call(
        _gather_kernel,
        grid=(NG,),
        in_specs=[
            pl.BlockSpec((E, GRP), lambda g: (0, g)),
            pl.BlockSpec((E, GRP), lambda g: (0, g)),
            pl.BlockSpec((GRP, D), lambda g: (g, 0)),
        ],
        out_specs=[
            pl.BlockSpec((E, SLOTS, D), lambda g: (0, g, 0)),
            pl.BlockSpec((E, SLOTS, 1), lambda g: (0, g, 0)),
        ],
        out_shape=[
            jax.ShapeDtypeStruct((E, ROWS_E, D), BF16),
            jax.ShapeDtypeStruct((E, ROWS_E, 1), F32),
        ],
        scratch_shapes=[pltpu.VMEM((E * CAP_S, L_S), BF16)],
        compiler_params=_cp(("parallel",)),
        name="moe_gather",
    )(rank, afft, h2)


FFN_TF = 256
FFN_TM = 1024


FFN_NF = FF // FFN_TF


def _ffn_kernel(x_ref, wg_ref, wu_ref, wd_ref, gs_ref, o_ref, acc_ref, cg_ref, cu_ref, cd_ref):
    m = pl.program_id(1)
    f = pl.program_id(2)

    def step(fresh, role):
        if fresh:
            wg, wu, wd = (r[...].astype(BF16) for r in (wg_ref, wu_ref, wd_ref))
            cg_ref[f], cu_ref[f], cd_ref[f] = wg, wu, wd
        else:
            wg, wu, wd = cg_ref[f], cu_ref[f], cd_ref[f]
        x = x_ref[...]
        a = _dot(x, wg)
        b = _dot(x, wu)
        hid = (a * jax.nn.sigmoid(a) * b).astype(BF16)
        for c in range(D // NCH):
            cs = slice(c * NCH, (c + 1) * NCH)
            down = _dot(hid, wd[:, cs])
            if role == "first":
                acc_ref[:, cs] = down
            elif role == "middle":
                acc_ref[:, cs] += down
            else:
                o_ref[:, cs] = ((acc_ref[:, cs] + down) * gs_ref[...]).astype(o_ref.dtype)

    for fresh in (True, False):
        on_half = (m == 0) if fresh else (m != 0)
        pl.when(on_half & (f == 0))(functools.partial(step, fresh, "first"))
        pl.when(on_half & (f > 0) & (f < FFN_NF - 1))(functools.partial(step, fresh, "middle"))
        pl.when(on_half & (f == FFN_NF - 1))(functools.partial(step, fresh, "last"))


def _ffn(xs, gs, w_gate, w_up, w_down, layer):
    def blk(m, f):
        return jnp.where(m == 0, f, FFN_NF - 1)

    return pl.pallas_call(
        _ffn_kernel,
        grid=(E, ROWS_E // FFN_TM, FFN_NF),
        in_specs=[
            pl.BlockSpec((None, FFN_TM, D), lambda e, m, f: (e, m, 0)),
            pl.BlockSpec((None, None, D, FFN_TF), lambda e, m, f: (layer, e, 0, blk(m, f))),
            pl.BlockSpec((None, None, D, FFN_TF), lambda e, m, f: (layer, e, 0, blk(m, f))),
            pl.BlockSpec((None, None, FFN_TF, D), lambda e, m, f: (layer, e, blk(m, f), 0)),
            pl.BlockSpec((None, FFN_TM, 1), lambda e, m, f: (e, m, 0)),
        ],
        out_specs=pl.BlockSpec((None, FFN_TM, D), lambda e, m, f: (e, m, 0)),
        out_shape=jax.ShapeDtypeStruct((E, ROWS_E, D), BF16),
        scratch_shapes=[
            pltpu.VMEM((FFN_TM, D), F32),
            pltpu.VMEM((FFN_NF, D, FFN_TF), BF16),
            pltpu.VMEM((FFN_NF, D, FFN_TF), BF16),
            pltpu.VMEM((FFN_NF, FFN_TF, D), BF16),
        ],
        compiler_params=_cp(("arbitrary", "arbitrary", "arbitrary"), 58),
        name="moe_ffn",
    )(xs, w_gate, w_up, w_down, gs)


def _combine_kernel(rt_ref, ys_ref, x_ref, gate_ref, exc_ref, exs_ref, *refs, last, g0):
    if last:
        fg_ref, o_ref, st_ref = refs
    else:
        o_ref, st_ref = refs
    g = pl.program_id(0) + g0

    def scatter(r0, n, cap, slot0, ex_ref):
        ec = E * cap
        r = jnp.minimum(rt_ref[r0:r0 + n, :], cap).astype(F32).astype(BF16)
        want = (lax.broadcasted_iota(jnp.int32, (1, NCH), 1) & (cap - 1)).astype(F32)
        for c in range(ec // NCH):
            cs = slice(c * NCH, (c + 1) * NCH)
            st_ref[r0:r0 + n, cs] = jnp.where(_dot(r, ex_ref[:, cs]) == want, 1.0, 0.0).astype(BF16)
        for c in range(D // NCH):
            cs = slice(c * NCH, (c + 1) * NCH)
            ys = ys_ref[:, slot0:slot0 + cap, cs].reshape(ec, NCH)
            moe = _dot(st_ref[r0:r0 + n, 0:ec], ys)
            o_ref[r0:r0 + n, cs] = x_ref[r0:r0 + n, cs].astype(F32) + gate_ref[:, cs] * moe

    def finish():
        if last:
            o_ref[...] = _rms(o_ref[...], fg_ref[...])

    @pl.when(g < NG_CTX)
    def _():
        for s in range(SEQ_PER_GRP):
            scatter(s * L_CTX, L_CTX, CAP_CTX, s * CAP_CTX, exc_ref)
        finish()

    @pl.when(g >= NG_CTX)
    def _():
        scatter(0, L_S, CAP_S, 0, exs_ref)
        finish()


def _combine_groups(rank_t, ys, x, mod3, consts, final_g, g0, ng):
    last = final_g is not None
    in_specs = [
        pl.BlockSpec((GRP, LANES), lambda g: (g + g0, 0)),
        pl.BlockSpec((E, SLOTS, D), lambda g: (0, g + g0, 0)),
        pl.BlockSpec((GRP, D), lambda g: (g + g0, 0)),
        pl.BlockSpec((None, 1, D), lambda g: (6 * (g + g0) + 5, 0, 0)),
        pl.BlockSpec((LANES, E * CAP_CTX), lambda g: (0, 0)),
        pl.BlockSpec((LANES, E * CAP_S), lambda g: (0, 0)),
    ]
    args = [rank_t, ys, x, mod3, consts["ex_c"], consts["ex_s"]]
    if last:
        in_specs.append(pl.BlockSpec((1, D), lambda g: (0, 0)))
        args.append(final_g)
    return pl.pallas_call(
        functools.partial(_combine_kernel, last=last, g0=g0),
        grid=(ng,),
        in_specs=in_specs,
        out_specs=pl.BlockSpec((GRP, D), lambda g: ((g if last else g + g0), 0)),
        out_shape=jax.ShapeDtypeStruct(((ng * GRP if last else T), D), F32),
        scratch_shapes=[pltpu.VMEM((GRP, E * CAP_S), BF16)],
        compiler_params=_cp(("parallel",), 56),
        name="moe_combine",
    )(*args)


def _combine(rank_t, ys, x, mod3, consts, final_g=None):
    if final_g is None:
        return _combine_groups(rank_t, ys, x, mod3, consts, None, 0, NG)
    return (_combine_groups(rank_t, ys, x, mod3, consts, final_g, 0, NG_CTX),
            _combine_groups(rank_t, ys, x, mod3, consts, final_g, NG_CTX, NG - NG_CTX))


def _np_constants():
    c = {}
    for tag, n in (("c", L_CTX), ("s", L_S)):
        t = np.arange(n)
        inv = np.zeros((n, POOL_W), np.float32)
        for k, w in enumerate(POOL_WINDOWS):
            lo = np.clip(t - w // 2, 0, n)
            hi = np.clip(t - w // 2 + w, 0, n)
            inv[:, k * POOL_GC:(k + 1) * POOL_GC] = (1.0 / (hi - lo).astype(np.float64))[:, None]
        c["inv_" + tag] = inv
        kk = np.arange(n, dtype=np.float64)[:, None]
        tt = np.arange(n, dtype=np.float64)[None, :]
        ang = np.pi * kk * tt / n
        dft = np.concatenate([np.cos(ang), -np.sin(ang)], axis=0)
        dft[n] = (-1.0) ** np.arange(n)
        c["dft_" + tag] = dft.astype(np.float32)
        c["dftt_" + tag] = np.ascontiguousarray(dft.T).astype(np.float32)
        tl = np.linspace(0.0, 1.0, n, dtype=np.float32)[:, None]
        bands = np.arange(1, POS_BANDS + 1, dtype=np.float32)[None, :]
        feats = np.concatenate([tl, np.sin(2 * np.pi * tl * bands), np.cos(2 * np.pi * tl * bands)], axis=1)
        c["feat_" + tag] = np.pad(feats.astype(np.float32), ((0, 0), (0, LANES - feats.shape[1])))
        c["t_" + tag] = tl
    n_rows = L_S // GRID_W
    row = np.repeat(np.arange(n_rows), GRID_W).astype(np.float32)
    col = np.tile(np.arange(GRID_W), n_rows).astype(np.float32)
    nf = ROPE // 4
    inv_f = (1.0 / ROPE_THETA ** (np.arange(nf, dtype=np.float32) / nf)).astype(np.float32)
    a_row = (row[:, None] * inv_f[None]).astype(np.float32).astype(np.float64)
    a_col = (col[:, None] * inv_f[None]).astype(np.float32).astype(np.float64)
    cos64 = np.concatenate([np.cos(a_row), np.cos(a_row), np.cos(a_col), np.cos(a_col)], axis=1)
    sin64 = np.concatenate([np.sin(a_row), np.sin(a_row), np.sin(a_col), np.sin(a_col)], axis=1)

    def table(reps):
        ident = np.stack([np.ones((L_S, ROPE * reps)), np.zeros((L_S, ROPE * reps))])
        rot = np.stack([np.tile(cos64, (1, reps)), np.tile(sin64, (1, reps))])
        return np.stack([ident, rot]).astype(np.float32)

    c["cs_q"] = table(H)
    c["cs_k"] = table(LANES // ROPE)
    c["tri"] = np.triu(np.ones((L_S, L_S), np.float32), k=1)
    for tag, cap in (("c", CAP_CTX), ("s", CAP_S)):
        ex = np.zeros((LANES, E * cap), np.float32)
        ex[np.arange(E * cap) // cap, np.arange(E * cap)] = 1.0
        c["ex_" + tag] = ex
    return c


def _constants():
    c = {k: jnp.asarray(v) for k, v in _np_constants().items()}
    for k in ("dft_c", "dftt_c", "dft_s", "dftt_s", "ex_c", "ex_s", "tri"):
        c[k] = c[k].astype(BF16)
    return c


def _rope_swap(w):
    q = ROPE // 4
    return jnp.concatenate([-w[..., q:2 * q], w[..., :q], -w[..., 3 * q:], w[..., 2 * q:3 * q]], axis=-1)


def _dot_hi(a, b):
    return jnp.dot(a, b, preferred_element_type=F32, precision=lax.Precision.HIGHEST)


def _filter_kernel(feat_ref, t_ref, w1_ref, b1_ref, w2_ref, b2_ref, fr_ref, w3f_ref, w3b_ref,
                   ldf_ref, ldb_ref, dft_ref, o_ref, z_ref):
    n = feat_ref.shape[0]

    @pl.when((pl.program_id(1) == 0) & (pl.program_id(2) == 0))
    def _():
        fr = fr_ref[...]
        z = jnp.sin(fr * (_dot_hi(feat_ref[...], w1_ref[...]) + b1_ref[...]))
        z_ref[...] = jnp.sin(fr * (_dot_hi(z, w2_ref[...]) + b2_ref[...]))

    z = z_ref[...]
    t = t_ref[...]
    first = lax.broadcasted_iota(jnp.int32, (n, 1), 0) == 0
    hf = _dot_hi(z, w3f_ref[...]) * jnp.exp(-jnp.exp(ldf_ref[...]) * t)
    hb = _dot_hi(z, w3b_ref[...]) * jnp.exp(-jnp.exp(ldb_ref[...]) * t)
    hb = jnp.where(first, 0.0, hb)
    norm = jnp.sum(jnp.abs(hf), axis=0, keepdims=True) + jnp.sum(jnp.abs(hb), axis=0, keepdims=True) + EPS
    both = _dot(dft_ref[...], jnp.concatenate([hf / norm, hb / norm], axis=1).astype(BF16))
    pf, pb = both[:, :hf.shape[1]], both[:, hf.shape[1]:]
    sc = jnp.where(first, 0.5 / n, 1.0 / n)
    ka = (pf[:n] + pb[:n]) * sc
    o_ref[0] = ka
    o_ref[1] = jnp.where(first, 0.0, (pf[n:] - pb[n:]) * sc)
    o_ref[2] = jnp.where(first, (pf[n:] + pb[n:]) * sc, ka)


def _hyena_tables(n, feats, tcol, dft, fw):
    tc = 256
    nc = HY_W // tc

    def lay(shape):
        return pl.BlockSpec((None,) + shape, lambda l, o, c: (l, 0, 0))

    def w3(back):
        return pl.BlockSpec((None, LANES, tc), lambda l, o, c: (l, 0, (2 * o + back) * nc + c))

    def ld(back):
        return pl.BlockSpec((None, 1, tc), lambda l, o, c: (l, 0, (2 * o + back) * nc + c))

    return pl.pallas_call(
        _filter_kernel,
        grid=(DEPTH, 2, nc),
        in_specs=[
            pl.BlockSpec((n, LANES), lambda l, o, c: (0, 0)),
            pl.BlockSpec((n, 1), lambda l, o, c: (0, 0)),
            lay((LANES, LANES)), lay((1, LANES)), lay((LANES, LANES)), lay((1, LANES)), lay((1, LANES)),
            w3(0), w3(1), ld(0), ld(1),
            pl.BlockSpec((2 * n, n), lambda l, o, c: (0, 0)),
        ],
        out_specs=pl.BlockSpec((None, None, 3, n, tc), lambda l, o, c: (l, o, 0, 0, c)),
        out_shape=jax.ShapeDtypeStruct((DEPTH, 2, 3, n, HY_W), F32),
        scratch_shapes=[pltpu.VMEM((n, LANES), F32)],
        compiler_params=_cp(("arbitrary", "arbitrary", "arbitrary")),
        name="hyena_filter",
    )(feats, tcol, fw["w1"], fw["b1"], fw["w2"], fw["b2"], fw["freq"], fw["w3"], fw["w3"],
      fw["ld"], fw["ld"], dft)


def _filter_weights(w1, b1, w2, b2, w3, freq, log_decay):
    ph = LANES - FILT_HID
    return dict(
        w1=jnp.pad(w1, ((0, 0), (0, LANES - w1.shape[1]), (0, ph))),
        b1=jnp.pad(b1, ((0, 0), (0, ph))).reshape(DEPTH, 1, LANES),
        w2=jnp.pad(w2, ((0, 0), (0, ph), (0, ph))),
        b2=jnp.pad(b2, ((0, 0), (0, ph))).reshape(DEPTH, 1, LANES),
        freq=jnp.pad(freq, ((0, 0), (0, ph))).reshape(DEPTH, 1, LANES),
        w3=jnp.pad(w3, ((0, 0), (0, ph), (0, 0))),
        ld=log_decay.reshape(DEPTH, 1, 4 * HY_W))


def _prep_weights(w_in, pool_w, pool_scale, hy_short_b, mla_q_norm, mla_kv_norm, mla_w_uq, mla_w_ukv,
                  w_out, router_w, norm1_g, norm2_g):
    n_main = N_U + Q_RANK
    kr_cols = w_in[:, :, IN_COLS - ROPE:]
    kr_swap = _rope_swap(kr_cols)
    w_x = jnp.concatenate([w_in[:, :, n_main:IN_COLS - ROPE], kr_cols, kr_cols, kr_swap, kr_swap], axis=2)
    wq = mla_w_uq.reshape(DEPTH, Q_RANK, H, NOPE + ROPE)
    wq_rope = wq[..., NOPE:]
    wq = jnp.concatenate([wq[..., :NOPE].reshape(DEPTH, Q_RANK, -1), wq_rope.reshape(DEPTH, Q_RANK, -1),
                          _rope_swap(wq_rope).reshape(DEPTH, Q_RANK, -1)], axis=2)
    wkv = mla_w_ukv.reshape(DEPTH, KV_RANK, H, NOPE + VD)
    wkv = jnp.concatenate([wkv[..., :NOPE].reshape(DEPTH, KV_RANK, -1), wkv[..., NOPE:].reshape(DEPTH, KV_RANK, -1)],
                          axis=2)
    return dict(
        w_main=w_in.astype(BF16), w_x=w_x.astype(BF16), wq=wq.astype(BF16), wkv=wkv.astype(BF16),
        w_out=w_out.astype(BF16), rw=jnp.pad(router_w, ((0, 0), (0, 0), (0, LANES - E))).astype(BF16),
        pool_w=pool_w.astype(BF16), pool_scale=pool_scale.reshape(DEPTH, 1, POOL_W),
        sb=hy_short_b.reshape(DEPTH, 1, 3 * HY_W),
        qg=mla_q_norm.reshape(DEPTH, 1, Q_RANK), kvg=mla_kv_norm.reshape(DEPTH, 1, KV_RANK),
        g1=norm1_g.reshape(DEPTH, 1, D), g2=norm2_g.reshape(DEPTH, 1, D))


def _layer(x, layer, mod3, wts, consts, kt_c, kt_s, hy_short_w, hy_skip, cache_ckv, krc,
           exp_w_gate, exp_w_up, exp_w_down, final_g):
    up, uv, u1, u2, q, ckv, kr, krr, kv = _inproj(x, mod3, wts, consts, layer)
    y_pool = _pool(up, consts, wts["pool_w"][layer], wts["pool_scale"][layer])
    y_hy = _hyena(uv, u1, u2, hy_short_w[layer], wts["sb"][layer], hy_skip[layer], consts, kt_c, kt_s, layer)
    kvc = _cachekv(cache_ckv, layer, wts["wkv"])
    y_mla = _attention(q, kv, krr, kvc, krc)
    x, h2, aff, afft = _wout(y_pool, y_hy, y_mla, x, mod3, wts, layer)
    rank, rank_t = _rank(aff, afft, consts["tri"])
    xs, gs = _gather(rank, afft, h2)
    ys = _ffn(xs, gs, exp_w_gate, exp_w_up, exp_w_down, layer)
    return _combine(rank_t, ys, x, mod3, consts, final_g), ckv, kr


def kernel(x_prompt, x_sample, cache_ckv, cache_krope, c, c_ctx, ada_w, ada_b, norm1_g, norm2_g, w_in, pool_w, pool_scale, hy_short_w, hy_short_b, hy_ffn_w1, hy_ffn_b1, hy_ffn_w2, hy_ffn_b2, hy_ffn_w3, hy_freq, hy_log_decay, hy_skip, mla_q_norm, mla_kv_norm, mla_w_uq, mla_w_ukv, w_out, router_w, exp_w_gate, exp_w_up, exp_w_down, final_norm_g):
    consts = _constants()
    x = (x_prompt.reshape(T_CTX, D), x_sample.reshape(T - T_CTX, D))
    c16 = jnp.concatenate([jnp.broadcast_to(c_ctx[None], (NG_CTX, D)), c], axis=0)
    mod = _adaln_mod(c16, ada_w, ada_b).reshape(DEPTH, NG * 6, 1, D)

    fw = _filter_weights(hy_ffn_w1, hy_ffn_b1, hy_ffn_w2, hy_ffn_b2, hy_ffn_w3, hy_freq, hy_log_decay)
    kt_c = _hyena_tables(L_CTX, consts["feat_c"], consts["t_c"], consts["dft_c"], fw)
    kt_s = _hyena_tables(L_S, consts["feat_s"], consts["t_s"], consts["dft_s"], fw)

    wts = _prep_weights(w_in, pool_w, pool_scale, hy_short_b, mla_q_norm, mla_kv_norm, mla_w_uq, mla_w_ukv,
                        w_out, router_w, norm1_g, norm2_g)
    krc = jnp.swapaxes(cache_krope, 0, 1).reshape(DEPTH, B_S * L_CTX, ROPE)
    krc = jnp.concatenate([krc, krc], axis=2).astype(BF16)

    ckv_list, kr_list = [], []
    for l in range(DEPTH):
        final_g = final_norm_g.reshape(1, D) if l == DEPTH - 1 else None
        x, ckv, kr = _layer(x, l, mod[l], wts, consts, kt_c, kt_s, hy_short_w, hy_skip, cache_ckv, krc[l],
                            exp_w_gate, exp_w_up, exp_w_down, final_g)
        ckv_list.append(ckv.reshape(B_CTX, L_CTX, KV_RANK))
        kr_list.append(kr[:, :ROPE].reshape(B_CTX, L_CTX, ROPE))

    y_prompt, y_sample = x
    return (y_prompt.reshape(B_CTX, L_CTX, D), y_sample.reshape(B_S, L_S, D),
            jnp.stack(ckv_list, axis=1), jnp.stack(kr_list, axis=1))
```

```python
import functools
import math

import numpy as np
import jax
import jax.numpy as jnp
from jax import lax
from jax.experimental import pallas as pl
from jax.experimental.pallas import tpu as pltpu

F32 = jnp.float32
BF16 = jnp.bfloat16

D = 2048
DEPTH = 2
B_CTX, L_CTX = 32, 256
B_S, L_S = 8, 1024
T_CTX = B_CTX * L_CTX
T = T_CTX + B_S * L_S
GRP = 1024
NG = T // GRP
NG_CTX = T_CTX // GRP
SEQ_PER_GRP = GRP // L_CTX
EPS = 1e-6
GRID_W = 64

POOL_W = 512
POOL_GC = 128
POOL_WINDOWS = (2, 4, 8, 16)
HY_W = 512
POS_BANDS = 8
FILT_HID = 64
H = 8
NOPE = 128
ROPE = 64
VD = 128
Q_RANK = 512
KV_RANK = 256
ROPE_THETA = 10000.0
IN_COLS = 2880
E = 16
FF = 1024
CAP_CTX = 2 * L_CTX // E
CAP_S = 2 * L_S // E
SLOTS = GRP * 2 // E
ROWS_E = NG * SLOTS
LANES = 128
NCH = 512
ATT_SCALE = 1.0 / math.sqrt(NOPE + ROPE)
MB = 1024 * 1024


def _cp(sem, vmem_mb=48):
    return pltpu.CompilerParams(dimension_semantics=sem, vmem_limit_bytes=vmem_mb * MB)


def _rms(x, g):
    return x * lax.rsqrt(jnp.mean(x * x, axis=-1, keepdims=True) + EPS) * g


def _dot(a, b):
    return jnp.dot(a, b, preferred_element_type=F32)


def _dot_nt(a, b):
    return lax.dot_general(a, b, (((1,), (1,)), ((), ())), preferred_element_type=F32)


def _mod_kernel(c_ref, w_ref, b_ref, o_ref):
    c = c_ref[...]
    a = (c * jax.nn.sigmoid(c)).astype(BF16)
    o_ref[...] = _dot(a, w_ref[...].astype(BF16)) + b_ref[...]


def _adaln_mod(c16, ada_w, ada_b):
    tn = 1536
    return pl.pallas_call(
        _mod_kernel,
        grid=(DEPTH, 6 * D // tn),
        in_specs=[
            pl.BlockSpec((NG, D), lambda l, j: (0, 0)),
            pl.BlockSpec((None, D, tn), lambda l, j: (l, 0, j)),
            pl.BlockSpec((None, 1, tn), lambda l, j: (l, 0, j)),
        ],
        out_specs=pl.BlockSpec((None, NG, tn), lambda l, j: (l, 0, j)),
        out_shape=jax.ShapeDtypeStruct((DEPTH, NG, 6 * D), F32),
        compiler_params=_cp(("parallel", "parallel")),
        name="adaln_mod",
    )(c16, ada_w, ada_b.reshape(DEPTH, 1, 6 * D))


def _src_specs(x, tm):
    n_ctx = T_CTX // tm
    n_all = T // tm
    if isinstance(x, tuple):
        return list(x), [
            pl.BlockSpec((tm, D), lambda i: (jnp.minimum(i, n_ctx - 1), 0)),
            pl.BlockSpec((tm, D), lambda i: (jnp.clip(i - n_ctx, 0, n_all - n_ctx - 1), 0)),
        ]
    return [x], [pl.BlockSpec((tm, D), lambda i: (jnp.minimum(i, n_all - 1), 0))]


def _load_src(x_refs, tm, cs=slice(None)):
    if len(x_refs) == 1:
        return x_refs[0][:, cs]
    return jnp.where(pl.program_id(0) < T_CTX // tm, x_refs[0][:, cs], x_refs[1][:, cs])


def _resident(shape, layer):
    nd = len(shape)
    return pl.BlockSpec((None,) + shape, lambda i: (layer,) + (0,) * nd, pipeline_mode=pl.Buffered(1))


def _mod_row(k, per):
    return pl.BlockSpec((None, 1, D), lambda i: (6 * (i // per) + k, 0, 0))


N_U = POOL_W + 3 * HY_W
N_QR = H * ROPE


def _inproj_kernel(*refs, tm):
    (*x_refs, g_ref, shift_ref, scale_ref, wm_ref, wx_ref, qg_ref, wq_ref, csq_ref, kvg_ref, csk_ref, wkv_ref,
     up_ref, uv_ref, u1_ref, u2_ref, q_ref, ckv_ref, kr_ref, krr_ref, kv_ref) = refs

    y = _rms(_load_src(x_refs, tm), g_ref[...])
    h = (y * (1.0 + scale_ref[...]) + shift_ref[...]).astype(BF16)
    for c, u_ref in enumerate((up_ref, uv_ref, u1_ref, u2_ref)):
        u_ref[...] = _dot(h, wm_ref[:, c * HY_W:(c + 1) * HY_W]).astype(u_ref.dtype)

    qn = _rms(_dot(h, wm_ref[:, N_U:]), qg_ref[...]).astype(BF16)
    nq = H * NOPE
    for c in range(nq // NCH):
        cs = slice(c * NCH, (c + 1) * NCH)
        q_ref[:, cs] = (_dot(qn, wq_ref[:, cs]) * ATT_SCALE).astype(q_ref.dtype)
    rot = _dot(qn, wq_ref[:, nq:nq + N_QR]) * csq_ref[0] + _dot(qn, wq_ref[:, nq + N_QR:]) * csq_ref[1]
    q_ref[:, nq:] = (rot * ATT_SCALE).astype(q_ref.dtype)

    kx = _dot(h, wx_ref[...])
    ckv = _rms(kx[:, :KV_RANK], kvg_ref[...])
    kr = kx[:, KV_RANK:KV_RANK + LANES]

    @pl.when(pl.program_id(0) < T_CTX // tm)
    def _():
        ckv_ref[...] = ckv
        kr_ref[...] = kr

    krr_ref[...] = (kr * csk_ref[0] + kx[:, KV_RANK + LANES:] * csk_ref[1]).astype(krr_ref.dtype)
    kv_ref[...] = _dot(ckv.astype(BF16), wkv_ref[...]).astype(kv_ref.dtype)


def _inproj(x, mod3, wts, consts, layer):
    tm = 512
    per = GRP // tm
    xs, x_specs = _src_specs(x, tm)

    def kind(i):
        return jnp.where(i >= T_CTX // tm, 1, 0)

    def rows(width):
        return pl.BlockSpec((tm, width), lambda i: (i, 0))

    def ctx_rows(width):
        return pl.BlockSpec((tm, width), lambda i: (jnp.minimum(i, T_CTX // tm - 1), 0))

    nkv = H * (NOPE + VD)
    return pl.pallas_call(
        functools.partial(_inproj_kernel, tm=tm),
        grid=(T // tm,),
        in_specs=x_specs + [
            _resident((1, D), layer), _mod_row(0, per), _mod_row(1, per),
            _resident((D, N_U + Q_RANK), layer),
            _resident((D, 4 * LANES), layer),
            _resident((1, Q_RANK), layer),
            _resident((Q_RANK, H * (NOPE + 2 * ROPE)), layer),
            pl.BlockSpec((None, 2, tm, N_QR), lambda i: (kind(i), 0, i % per, 0)),
            _resident((1, KV_RANK), layer),
            pl.BlockSpec((None, 2, tm, LANES), lambda i: (kind(i), 0, i % per, 0)),
            _resident((KV_RANK, nkv), layer),
        ],
        out_specs=[rows(HY_W)] * 4 + [rows(H * (NOPE + ROPE)), ctx_rows(KV_RANK), ctx_rows(LANES), rows(LANES),
                                      rows(nkv)],
        out_shape=[jax.ShapeDtypeStruct((T, HY_W), BF16)] * 4 + [
            jax.ShapeDtypeStruct((T, H * (NOPE + ROPE)), BF16),
            jax.ShapeDtypeStruct((T_CTX, KV_RANK), F32),
            jax.ShapeDtypeStruct((T_CTX, LANES), F32),
            jax.ShapeDtypeStruct((T, LANES), BF16),
            jax.ShapeDtypeStruct((T, nkv), BF16),
        ],
        compiler_params=_cp(("arbitrary",), 58),
        name="in_proj",
    )(*xs, wts["g1"], mod3, mod3, wts["w_main"], wts["w_x"], wts["qg"], wts["wq"], consts["cs_q"],
      wts["kvg"], consts["cs_k"], wts["wkv"])


def _pool_kernel(u_ref, ic_ref, is_ref, pw_ref, ps_ref, o_ref):
    g = pl.program_id(0)

    def seq(r0, n, inv_ref):
        row = lax.broadcasted_iota(jnp.int32, (n, POOL_GC), 0)

        def later(x, m):
            return jnp.where(row < n - m, pltpu.roll(x, n - m, 0), 0.0)

        def earlier(x, m):
            return jnp.where(row >= m, pltpu.roll(x, m, 0), 0.0)

        for k, w in enumerate(POOL_WINDOWS):
            cs = slice(k * POOL_GC, (k + 1) * POOL_GC)
            u = u_ref[r0:r0 + n, cs].astype(F32)
            ahead, behind, m = u, earlier(u, 1), 1
            while m < w // 2:
                ahead = ahead + later(ahead, m)
                behind = behind + earlier(behind, m)
                m *= 2
            pooled = (ahead + behind) * inv_ref[:, cs] - u
            y = _dot(pooled.astype(BF16), pw_ref[k]) * ps_ref[:, cs]
            o_ref[r0:r0 + n, cs] = y.astype(o_ref.dtype)

    @pl.when(g < NG_CTX)
    def _():
        for s in range(SEQ_PER_GRP):
            seq(s * L_CTX, L_CTX, ic_ref)

    @pl.when(g >= NG_CTX)
    def _():
        seq(0, L_S, is_ref)


def _pool(proj, consts, pw, ps):
    nw = len(POOL_WINDOWS)
    return pl.pallas_call(
        _pool_kernel,
        grid=(NG,),
        in_specs=[
            pl.BlockSpec((GRP, POOL_W), lambda g: (g, 0)),
            pl.BlockSpec((L_CTX, POOL_W), lambda g: (0, 0)),
            pl.BlockSpec((L_S, POOL_W), lambda g: (0, 0)),
            pl.BlockSpec((nw, POOL_GC, POOL_GC), lambda g: (0, 0, 0)),
            pl.BlockSpec((1, POOL_W), lambda g: (0, 0)),
        ],
        out_specs=pl.BlockSpec((GRP, POOL_W), lambda g: (g, 0)),
        out_shape=jax.ShapeDtypeStruct((T, POOL_W), BF16),
        compiler_params=_cp(("parallel",)),
        name="pool_mix",
    )(proj, consts["inv_c"], consts["inv_s"], pw, ps)


HY_CH = HY_W


def _hyena_kernel(v_ref, x1_ref, x2_ref, swv_ref, sw1_ref, sw2_ref, sbv_ref, sb1_ref, sb2_ref,
                  skip_ref, wc_ref, wtc_ref, ws_ref, wts_ref, ktc_ref, kts_ref, o_ref):
    g = pl.program_id(0)

    def sconv(u_ref, sw_ref, sb_ref, r0, n, cs):
        u = u_ref[r0:r0 + n, cs].astype(F32)
        row = lax.broadcasted_iota(jnp.int32, u.shape, 0)
        prev = jnp.where(row == 0, 0.0, pltpu.roll(u, 1, 0))
        nxt = jnp.where(row == n - 1, 0.0, pltpu.roll(u, n - 1, 0))
        return prev * sw_ref[0:1, cs] + u * sw_ref[1:2, cs] + nxt * sw_ref[2:3, cs] + sb_ref[:, cs]

    def lconv(u, o, n, cs, w_ref, wt_ref, kt_ref):
        spec = _dot(w_ref[...], u.astype(BF16))
        pr, pi = spec[:n], spec[n:]
        ka, kb, ka2 = kt_ref[o, 0, :, cs], kt_ref[o, 1, :, cs], kt_ref[o, 2, :, cs]
        yr = pr * ka - pi * kb
        yi = pr * kb + pi * ka2
        prod = jnp.concatenate([yr, yi], axis=0).astype(BF16)
        return _dot(wt_ref[...], prod) + u * skip_ref[o:o + 1, cs]

    def seq(r0, n, cs, w_ref, wt_ref, kt_ref):
        v = sconv(v_ref, swv_ref, sbv_ref, r0, n, cs)
        x1 = sconv(x1_ref, sw1_ref, sb1_ref, r0, n, cs)
        x2 = sconv(x2_ref, sw2_ref, sb2_ref, r0, n, cs)
        z = x1 * lconv(v, 0, n, cs, w_ref, wt_ref, kt_ref)
        y = x2 * lconv(z, 1, n, cs, w_ref, wt_ref, kt_ref)
        o_ref[r0:r0 + n, cs] = y.astype(o_ref.dtype)

    chunks = [slice(c * HY_CH, (c + 1) * HY_CH) for c in range(HY_W // HY_CH)]

    @pl.when(g < NG_CTX)
    def _():
        for s in range(SEQ_PER_GRP):
            for cs in chunks:
                seq(s * L_CTX, L_CTX, cs, wc_ref, wtc_ref, ktc_ref)

    @pl.when(g >= NG_CTX)
    def _():
        for cs in chunks:
            seq(0, L_S, cs, ws_ref, wts_ref, kts_ref)


def _hyena(uv, u1, u2, sw, sb, skip, consts, kt_c, kt_s, layer):
    rows = pl.BlockSpec((GRP, HY_W), lambda g: (g, 0))

    def swpart(p):
        return pl.BlockSpec((3, HY_W), lambda g: (0, p))

    def sbpart(p):
        return pl.BlockSpec((1, HY_W), lambda g: (0, p))

    def tables(n):
        return pl.BlockSpec((None, 2, 3, n, HY_W), lambda g: (layer, 0, 0, 0, 0), pipeline_mode=pl.Buffered(1))

    return pl.pallas_call(
        _hyena_kernel,
        grid=(NG,),
        in_specs=[
            rows, rows, rows,
            swpart(0), swpart(1), swpart(2),
            sbpart(0), sbpart(1), sbpart(2),
            pl.BlockSpec((2, HY_W), lambda g: (0, 0)),
            pl.BlockSpec((2 * L_CTX, L_CTX), lambda g: (0, 0)),
            pl.BlockSpec((L_CTX, 2 * L_CTX), lambda g: (0, 0)),
            pl.BlockSpec((2 * L_S, L_S), lambda g: (0, 0)),
            pl.BlockSpec((L_S, 2 * L_S), lambda g: (0, 0)),
            tables(L_CTX), tables(L_S),
        ],
        out_specs=pl.BlockSpec((GRP, HY_W), lambda g: (g, 0)),
        out_shape=jax.ShapeDtypeStruct((T, HY_W), BF16),
        compiler_params=_cp(("parallel",), 56),
        name="hyena_mix",
    )(uv, u1, u2, sw, sw, sw, sb, sb, sb, skip,
      consts["dft_c"], consts["dftt_c"], consts["dft_s"], consts["dftt_s"], kt_c, kt_s)


def _cachekv_kernel(x_ref, w_ref, o_ref):
    o_ref[...] = _dot(x_ref[...].astype(BF16), w_ref[...]).astype(o_ref.dtype)


def _cachekv(cache_ckv, layer, wkv):
    return pl.pallas_call(
        _cachekv_kernel,
        grid=(B_S,),
        in_specs=[
            pl.BlockSpec((None, None, L_CTX, KV_RANK), lambda b: (b, layer, 0, 0)),
            pl.BlockSpec((None, KV_RANK, H * (NOPE + VD)), lambda b: (layer, 0, 0)),
        ],
        out_specs=pl.BlockSpec((L_CTX, H * (NOPE + VD)), lambda b: (b, 0)),
        out_shape=jax.ShapeDtypeStruct((B_S * L_CTX, H * (NOPE + VD)), BF16),
        compiler_params=_cp(("parallel",)),
        name="cache_kv",
    )(cache_ckv, wkv)


ATT_TQ = 512
NK_S = L_S + L_CTX


def _attn_kernel(q_ref, kv_ref, krr_ref, kvc_ref, krc_ref, o_ref, kcat_ref):
    g = pl.program_id(0)
    lane = lax.broadcasted_iota(jnp.int32, (1, LANES), 1)
    hk = NOPE + LANES
    vo = H * NOPE

    krr = krr_ref[...]
    for h in range(H):
        kcat_ref[0:GRP, h * hk:h * hk + NOPE] = kv_ref[:, h * NOPE:(h + 1) * NOPE]
        kcat_ref[0:GRP, h * hk + NOPE:(h + 1) * hk] = krr

    def qcat(rows, h):
        qn = q_ref[rows, h * NOPE:(h + 1) * NOPE]
        pair = q_ref[rows, vo + (h // 2) * LANES:vo + (h // 2 + 1) * LANES].astype(F32)
        keep = (lane < ROPE) if h % 2 == 0 else (lane >= ROPE)
        return jnp.concatenate([qn, jnp.where(keep, pair, 0.0).astype(BF16)], axis=1)

    def probs(sc):
        m = jnp.max(sc, axis=-1, keepdims=True)
        p = jnp.exp(sc - m)
        return p.astype(BF16), 1.0 / jnp.sum(p, axis=-1, keepdims=True)

    @pl.when(g < NG_CTX)
    def _():
        def body(s, carry):
            rows = pl.ds(pl.multiple_of(s * L_CTX, L_CTX), L_CTX)
            for h in range(H):
                p, rl = probs(_dot_nt(qcat(rows, h), kcat_ref[rows, h * hk:(h + 1) * hk]))
                o = _dot(p, kv_ref[rows, vo + h * VD:vo + (h + 1) * VD]) * rl
                o_ref[rows, h * VD:(h + 1) * VD] = o.astype(o_ref.dtype)
            return carry

        lax.fori_loop(0, SEQ_PER_GRP, body, 0)

    @pl.when(g >= NG_CTX)
    def _():
        krc = krc_ref[...]
        for h in range(H):
            kcat_ref[GRP:NK_S, h * hk:h * hk + NOPE] = kvc_ref[:, h * NOPE:(h + 1) * NOPE]
            kcat_ref[GRP:NK_S, h * hk + NOPE:(h + 1) * hk] = krc

        def body(t, carry):
            rows = pl.ds(pl.multiple_of(t * ATT_TQ, ATT_TQ), ATT_TQ)
            for h in range(H):
                p, rl = probs(_dot_nt(qcat(rows, h), kcat_ref[:, h * hk:(h + 1) * hk]))
                o = _dot(p[:, :GRP], kv_ref[:, vo + h * VD:vo + (h + 1) * VD])
                o = o + _dot(p[:, GRP:], kvc_ref[:, vo + h * VD:vo + (h + 1) * VD])
                o_ref[rows, h * VD:(h + 1) * VD] = (o * rl).astype(o_ref.dtype)
            return carry

        lax.fori_loop(0, L_S // ATT_TQ, body, 0)


def _attention(q, kv, krr, kvc, krc):
    def cache_blk(g):
        return jnp.maximum(g - NG_CTX, 0)

    return pl.pallas_call(
        _attn_kernel,
        grid=(NG,),
        in_specs=[
            pl.BlockSpec((GRP, H * (NOPE + ROPE)), lambda g: (g, 0)),
            pl.BlockSpec((GRP, H * (NOPE + VD)), lambda g: (g, 0)),
            pl.BlockSpec((GRP, LANES), lambda g: (g, 0)),
            pl.BlockSpec((L_CTX, H * (NOPE + VD)), lambda g: (cache_blk(g), 0)),
            pl.BlockSpec((L_CTX, LANES), lambda g: (cache_blk(g), 0)),
        ],
        out_specs=pl.BlockSpec((GRP, H * VD), lambda g: (g, 0)),
        out_shape=jax.ShapeDtypeStruct((T, H * VD), BF16),
        scratch_shapes=[pltpu.VMEM((NK_S, H * (NOPE + LANES)), BF16)],
        compiler_params=_cp(("parallel",)),
        name="mla_attention",
    )(q, kv, krr, kvc, krc)


def _wout_kernel(yp_ref, yh_ref, ym_ref, w_ref, gate_ref, g2_ref, shift_ref, scale_ref, rw_ref, *refs, tm):
    *x_refs, o_ref, h_ref, aff_ref, afft_ref, prev_ref = refs

    @pl.when(pl.program_id(0) == 0)
    def _():
        prev_ref[...] = jnp.zeros_like(prev_ref)

    hb = (_rms(prev_ref[...], g2_ref[...]) * (1.0 + scale_ref[...]) + shift_ref[...]).astype(BF16)
    h_ref[...] = hb
    logits = _dot(hb, rw_ref[...])
    lane = lax.broadcasted_iota(jnp.int32, logits.shape, 1)
    logits = jnp.where(lane < E, logits, -jnp.inf)
    ex = jnp.exp(logits - jnp.max(logits, axis=-1, keepdims=True))
    aff = ex / jnp.sum(ex, axis=-1, keepdims=True)
    aff_ref[...] = aff
    afft_ref[...] = aff.T[:E]

    y = jnp.concatenate([yp_ref[...], yh_ref[...], ym_ref[...]], axis=1)
    for c in range(D // NCH):
        cs = slice(c * NCH, (c + 1) * NCH)
        v = _load_src(x_refs, tm, cs) + gate_ref[:, cs] * _dot(y, w_ref[:, cs])
        o_ref[:, cs] = v.astype(o_ref.dtype)
        prev_ref[:, cs] = v


def _wout(yp, yh, ym, x, mod3, wts, layer):
    tm = 512
    per = GRP // tm
    n = T // tm
    xs, x_specs = _src_specs(x, tm)

    def cur(i):
        return jnp.minimum(i, n - 1)

    def prev(i):
        return jnp.maximum(i - 1, 0)

    def rows(width, blk):
        return pl.BlockSpec((tm, width), lambda i: (blk(i), 0))

    def mod_row(k, blk):
        return pl.BlockSpec((None, 1, D), lambda i: (6 * (blk(i) // per) + k, 0, 0))

    return pl.pallas_call(
        functools.partial(_wout_kernel, tm=tm),
        grid=(n + 1,),
        in_specs=[
            rows(POOL_W, cur), rows(HY_W, cur), rows(H * VD, cur),
            _resident((D, D), layer),
            mod_row(2, cur),
            _resident((1, D), layer), mod_row(3, prev), mod_row(4, prev),
            _resident((D, LANES), layer),
        ] + x_specs,
        out_specs=[rows(D, cur), rows(D, prev), rows(LANES, prev), pl.BlockSpec((E, tm), lambda i: (0, prev(i)))],
        out_shape=[
            jax.ShapeDtypeStruct((T, D), BF16),
            jax.ShapeDtypeStruct((T, D), BF16),
            jax.ShapeDtypeStruct((T, LANES), F32),
            jax.ShapeDtypeStruct((E, T), F32),
        ],
        scratch_shapes=[pltpu.VMEM((tm, D), F32)],
        compiler_params=_cp(("arbitrary",), 56),
        name="out_proj",
    )(yp, yh, ym, wts["w_out"], mod3, wts["g2"], mod3, mod3, wts["rw"], *xs)


RANK_CH = 256
SEARCH_ROUNDS = 12
SEARCH_WAYS = 8
AFF_MAX = 2.0


def _rank_kernel(aff_ref, afft_ref, tri_ref, rank_ref, rank_t_ref, cnt_ref, cut_ref):
    g = pl.program_id(0)

    def count_ge(a, t):
        return jnp.sum(jnp.where(a >= t, 1.0, 0.0), axis=1, keepdims=True)

    def search(specs, capf):
        acts = [afft_ref[:, r0:r0 + n] for r0, n in specs]
        lo = [jnp.zeros((E, 1), F32) for _ in specs]
        hi = [jnp.full((E, 1), AFF_MAX, F32) for _ in specs]
        for _ in range(SEARCH_ROUNDS):
            for s, a in enumerate(acts):
                step = (hi[s] - lo[s]) * (1.0 / SEARCH_WAYS)
                ts = [lo[s] + step * k for k in range(1, SEARCH_WAYS)]
                ok = [count_ge(a, t) >= capf for t in ts]
                new_lo, new_hi = lo[s], hi[s]
                for t, o in zip(ts, ok):
                    new_lo = jnp.where(o, t, new_lo)
                for t, o in zip(reversed(ts), reversed(ok)):
                    new_hi = jnp.where(o, new_hi, t)
                lo[s], hi[s] = new_lo, new_hi
        open_brackets = jnp.zeros((E, 1), F32)
        for s, a in enumerate(acts):
            top = jnp.max(jnp.where(a >= lo[s], jnp.where(a < hi[s], a, -1.0), -1.0), axis=1, keepdims=True)
            low = jnp.min(jnp.where(a >= lo[s], jnp.where(a < hi[s], a, AFF_MAX), AFF_MAX), axis=1, keepdims=True)
            cut_ref[:, s:s + 1] = top
            open_brackets = open_brackets + jnp.where(top != low, 1.0, 0.0)
        return jnp.sum(open_brackets)

    def exact_cut(s, r0, n, capf):
        for e in range(E):
            row = afft_ref[e:e + 1, r0:r0 + n]
            acc = jnp.zeros((1, n), F32)
            for c in range(n // RANK_CH):
                col = aff_ref[r0 + c * RANK_CH:r0 + (c + 1) * RANK_CH, e:e + 1]
                acc = acc + jnp.sum(jnp.where(col >= row, 1.0, 0.0), axis=0, keepdims=True)
            cnt_ref[e:e + 1, 0:n] = acc
        a = afft_ref[:, r0:r0 + n]
        cut_ref[:, s:s + 1] = jnp.max(jnp.where(cnt_ref[:, 0:n] >= capf, a, -1.0), axis=1, keepdims=True)

    def slots(s, r0, n, capf):
        a = afft_ref[:, r0:r0 + n]
        cut = cut_ref[:, s:s + 1]
        above = a > cut
        tied = a == cut
        above_f = jnp.where(above, 1.0, 0.0)
        n_above = jnp.sum(above_f, axis=1, keepdims=True)
        marks = jnp.concatenate([above_f, jnp.where(tied, 1.0, 0.0)], axis=0).astype(BF16)
        before = _dot(marks, tri_ref[0:n, 0:n])
        tie_slot = n_above + before[E:]
        slot = jnp.where(above, before[:E], jnp.where(tied, jnp.where(tie_slot < capf, tie_slot, n), n))
        rank_ref[:, r0:r0 + n] = slot.astype(jnp.int32)
        wide = jnp.concatenate([slot, jnp.zeros((LANES - E, n), F32)], axis=0)
        rank_t_ref[r0:r0 + n, :] = wide.T.astype(jnp.int32)

    def group(specs, cap):
        capf = float(cap)
        unresolved = search(specs, capf)

        @pl.when(unresolved > 0.0)
        def _():
            for s, (r0, n) in enumerate(specs):
                exact_cut(s, r0, n, capf)

        for s, (r0, n) in enumerate(specs):
            slots(s, r0, n, capf)

    @pl.when(g < NG_CTX)
    def _():
        group([(s * L_CTX, L_CTX) for s in range(SEQ_PER_GRP)], CAP_CTX)

    @pl.when(g >= NG_CTX)
    def _():
        group([(0, L_S)], CAP_S)


def _rank(aff, afft, tri):
    return pl.pallas_call(
        _rank_kernel,
        grid=(NG,),
        in_specs=[
            pl.BlockSpec((GRP, LANES), lambda g: (g, 0)),
            pl.BlockSpec((E, GRP), lambda g: (0, g)),
            pl.BlockSpec((L_S, L_S), lambda g: (0, 0)),
        ],
        out_specs=[pl.BlockSpec((E, GRP), lambda g: (0, g)), pl.BlockSpec((GRP, LANES), lambda g: (g, 0))],
        out_shape=[jax.ShapeDtypeStruct((E, T), jnp.int32), jax.ShapeDtypeStruct((T, LANES), jnp.int32)],
        scratch_shapes=[pltpu.VMEM((E, GRP), F32), pltpu.VMEM((E, LANES), F32)],
        compiler_params=_cp(("parallel",)),
        name="moe_rank",
    )(aff, afft, tri)


def _gather_kernel(rank_ref, afft_ref, h_ref, xs_ref, gs_ref, sel_ref):
    g = pl.program_id(0)

    def seq(r0, n, cap, slot0):
        slot_i = lax.broadcasted_iota(jnp.int32, (cap, n), 0)
        for e in range(E):
            hit = slot_i == rank_ref[e:e + 1, r0:r0 + n]
            sel_ref[e * cap:(e + 1) * cap, 0:n] = jnp.where(hit, 1.0, 0.0).astype(BF16)
            gs_ref[e, slot0:slot0 + cap, :] = jnp.sum(
                jnp.where(hit, afft_ref[e:e + 1, r0:r0 + n], 0.0), axis=1, keepdims=True)
        for c in range(D // NCH):
            cs = slice(c * NCH, (c + 1) * NCH)
            res = _dot(sel_ref[0:E * cap, 0:n], h_ref[r0:r0 + n, cs])
            for e in range(E):
                xs_ref[e, slot0:slot0 + cap, cs] = res[e * cap:(e + 1) * cap].astype(xs_ref.dtype)

    @pl.when(g < NG_CTX)
    def _():
        for s in range(SEQ_PER_GRP):
            seq(s * L_CTX, L_CTX, CAP_CTX, s * CAP_CTX)

    @pl.when(g >= NG_CTX)
    def _():
        seq(0, L_S, CAP_S, 0)


def _gather(rank, afft, h2):
    return pl.pallas_call(
        _gather_kernel,
        grid=(NG,),
        in_specs=[
            pl.BlockSpec((E, GRP), lambda g: (0, g)),
            pl.BlockSpec((E, GRP), lambda g: (0, g)),
            pl.BlockSpec((GRP, D), lambda g: (g, 0)),
        ],
        out_specs=[
            pl.BlockSpec((E, SLOTS, D), lambda g: (0, g, 0)),
            pl.BlockSpec((E, SLOTS, 1), lambda g: (0, g, 0)),
        ],
        out_shape=[
            jax.ShapeDtypeStruct((E, ROWS_E, D), BF16),
            jax.ShapeDtypeStruct((E, ROWS_E, 1), F32),
        ],
        scratch_shapes=[pltpu.VMEM((E * CAP_S, L_S), BF16)],
        compiler_params=_cp(("parallel",)),
        name="moe_gather",
    )(rank, afft, h2)


FFN_TF = 256
FFN_TM = 1024


FFN_NF = FF // FFN_TF


def _ffn_kernel(x_ref, wg_ref, wu_ref, wd_ref, gs_ref, o_ref, acc_ref, cg_ref, cu_ref, cd_ref):
    m = pl.program_id(1)
    f = pl.program_id(2)

    def step(fresh, role):
        if fresh:
            wg, wu, wd = (r[...].astype(BF16) for r in (wg_ref, wu_ref, wd_ref))
            cg_ref[f], cu_ref[f], cd_ref[f] = wg, wu, wd
        else:
            wg, wu, wd = cg_ref[f], cu_ref[f], cd_ref[f]
        x = x_ref[...]
        a = _dot(x, wg)
        b = _dot(x, wu)
        hid = (a * jax.nn.sigmoid(a) * b).astype(BF16)
        for c in range(D // NCH):
            cs = slice(c * NCH, (c + 1) * NCH)
            down = _dot(hid, wd[:, cs])
            if role == "first":
                acc_ref[:, cs] = down
            elif role == "middle":
                acc_ref[:, cs] += down
            else:
                o_ref[:, cs] = ((acc_ref[:, cs] + down) * gs_ref[...]).astype(o_ref.dtype)

    for fresh in (True, False):
        on_half = (m == 0) if fresh else (m != 0)
        pl.when(on_half & (f == 0))(functools.partial(step, fresh, "first"))
        pl.when(on_half & (f > 0) & (f < FFN_NF - 1))(functools.partial(step, fresh, "middle"))
        pl.when(on_half & (f == FFN_NF - 1))(functools.partial(step, fresh, "last"))


def _ffn(xs, gs, w_gate, w_up, w_down, layer):
    def blk(m, f):
        return jnp.where(m == 0, f, FFN_NF - 1)

    return pl.pallas_call(
        _ffn_kernel,
        grid=(E, ROWS_E // FFN_TM, FFN_NF),
        in_specs=[
            pl.BlockSpec((None, FFN_TM, D), lambda e, m, f: (e, m, 0)),
            pl.BlockSpec((None, None, D, FFN_TF), lambda e, m, f: (layer, e, 0, blk(m, f))),
            pl.BlockSpec((None, None, D, FFN_TF), lambda e, m, f: (layer, e, 0, blk(m, f))),
            pl.BlockSpec((None, None, FFN_TF, D), lambda e, m, f: (layer, e, blk(m, f), 0)),
            pl.BlockSpec((None, FFN_TM, 1), lambda e, m, f: (e, m, 0)),
        ],
        out_specs=pl.BlockSpec((None, FFN_TM, D), lambda e, m, f: (e, m, 0)),
        out_shape=jax.ShapeDtypeStruct((E, ROWS_E, D), BF16),
        scratch_shapes=[
            pltpu.VMEM((FFN_TM, D), F32),
            pltpu.VMEM((FFN_NF, D, FFN_TF), BF16),
            pltpu.VMEM((FFN_NF, D, FFN_TF), BF16),
            pltpu.VMEM((FFN_NF, FFN_TF, D), BF16),
        ],
        compiler_params=_cp(("arbitrary", "arbitrary", "arbitrary"), 58),
        name="moe_ffn",
    )(xs, w_gate, w_up, w_down, gs)


def _combine_kernel(rt_ref, ys_ref, x_ref, gate_ref, exc_ref, exs_ref, *refs, last, g0):
    if last:
        fg_ref, o_ref, st_ref = refs
    else:
        o_ref, st_ref = refs
    g = pl.program_id(0) + g0

    def scatter(r0, n, cap, slot0, ex_ref):
        ec = E * cap
        r = jnp.minimum(rt_ref[r0:r0 + n, :], cap).astype(F32).astype(BF16)
        want = (lax.broadcasted_iota(jnp.int32, (1, NCH), 1) & (cap - 1)).astype(F32)
        for c in range(ec // NCH):
            cs = slice(c * NCH, (c + 1) * NCH)
            st_ref[r0:r0 + n, cs] = jnp.where(_dot(r, ex_ref[:, cs]) == want, 1.0, 0.0).astype(BF16)
        for c in range(D // NCH):
            cs = slice(c * NCH, (c + 1) * NCH)
            ys = ys_ref[:, slot0:slot0 + cap, cs].reshape(ec, NCH)
            moe = _dot(st_ref[r0:r0 + n, 0:ec], ys)
            o_ref[r0:r0 + n, cs] = x_ref[r0:r0 + n, cs].astype(F32) + gate_ref[:, cs] * moe

    def finish():
        if last:
            o_ref[...] = _rms(o_ref[...], fg_ref[...])

    @pl.when(g < NG_CTX)
    def _():
        for s in range(SEQ_PER_GRP):
            scatter(s * L_CTX, L_CTX, CAP_CTX, s * CAP_CTX, exc_ref)
        finish()

    @pl.when(g >= NG_CTX)
    def _():
        scatter(0, L_S, CAP_S, 0, exs_ref)
        finish()


def _combine_groups(rank_t, ys, x, mod3, consts, final_g, g0, ng):
    last = final_g is not None
    in_specs = [
        pl.BlockSpec((GRP, LANES), lambda g: (g + g0, 0)),
        pl.BlockSpec((E, SLOTS, D), lambda g: (0, g + g0, 0)),
        pl.BlockSpec((GRP, D), lambda g: (g + g0, 0)),
        pl.BlockSpec((None, 1, D), lambda g: (6 * (g + g0) + 5, 0, 0)),
        pl.BlockSpec((LANES, E * CAP_CTX), lambda g: (0, 0)),
        pl.BlockSpec((LANES, E * CAP_S), lambda g: (0, 0)),
    ]
    args = [rank_t, ys, x, mod3, consts["ex_c"], consts["ex_s"]]
    if last:
        in_specs.append(pl.BlockSpec((1, D), lambda g: (0, 0)))
        args.append(final_g)
    return pl.pallas_call(
        functools.partial(_combine_kernel, last=last, g0=g0),
        grid=(ng,),
        in_specs=in_specs,
        out_specs=pl.BlockSpec((GRP, D), lambda g: ((g if last else g + g0), 0)),
        out_shape=jax.ShapeDtypeStruct(((ng * GRP if last else T), D), F32),
        scratch_shapes=[pltpu.VMEM((GRP, E * CAP_S), BF16)],
        compiler_params=_cp(("parallel",), 56),
        name="moe_combine",
    )(*args)


def _combine(rank_t, ys, x, mod3, consts, final_g=None):
    if final_g is None:
        return _combine_groups(rank_t, ys, x, mod3, consts, None, 0, NG)
    return (_combine_groups(rank_t, ys, x, mod3, consts, final_g, 0, NG_CTX),
            _combine_groups(rank_t, ys, x, mod3, consts, final_g, NG_CTX, NG - NG_CTX))


def _np_constants():
    c = {}
    for tag, n in (("c", L_CTX), ("s", L_S)):
        t = np.arange(n)
        inv = np.zeros((n, POOL_W), np.float32)
        for k, w in enumerate(POOL_WINDOWS):
            lo = np.clip(t - w // 2, 0, n)
            hi = np.clip(t - w // 2 + w, 0, n)
            inv[:, k * POOL_GC:(k + 1) * POOL_GC] = (1.0 / (hi - lo).astype(np.float64))[:, None]
        c["inv_" + tag] = inv
        kk = np.arange(n, dtype=np.float64)[:, None]
        tt = np.arange(n, dtype=np.float64)[None, :]
        ang = np.pi * kk * tt / n
        dft = np.concatenate([np.cos(ang), -np.sin(ang)], axis=0)
        dft[n] = (-1.0) ** np.arange(n)
        c["dft_" + tag] = dft.astype(np.float32)
        c["dftt_" + tag] = np.ascontiguousarray(dft.T).astype(np.float32)
        tl = np.linspace(0.0, 1.0, n, dtype=np.float32)[:, None]
        bands = np.arange(1, POS_BANDS + 1, dtype=np.float32)[None, :]
        feats = np.concatenate([tl, np.sin(2 * np.pi * tl * bands), np.cos(2 * np.pi * tl * bands)], axis=1)
        c["feat_" + tag] = np.pad(feats.astype(np.float32), ((0, 0), (0, LANES - feats.shape[1])))
        c["t_" + tag] = tl
    n_rows = L_S // GRID_W
    row = np.repeat(np.arange(n_rows), GRID_W).astype(np.float32)
    col = np.tile(np.arange(GRID_W), n_rows).astype(np.float32)
    nf = ROPE // 4
    inv_f = (1.0 / ROPE_THETA ** (np.arange(nf, dtype=np.float32) / nf)).astype(np.float32)
    a_row = (row[:, None] * inv_f[None]).astype(np.float32).astype(np.float64)
    a_col = (col[:, None] * inv_f[None]).astype(np.float32).astype(np.float64)
    cos64 = np.concatenate([np.cos(a_row), np.cos(a_row), np.cos(a_col), np.cos(a_col)], axis=1)
    sin64 = np.concatenate([np.sin(a_row), np.sin(a_row), np.sin(a_col), np.sin(a_col)], axis=1)

    def table(reps):
        ident = np.stack([np.ones((L_S, ROPE * reps)), np.zeros((L_S, ROPE * reps))])
        rot = np.stack([np.tile(cos64, (1, reps)), np.tile(sin64, (1, reps))])
        return np.stack([ident, rot]).astype(np.float32)

    c["cs_q"] = table(H)
    c["cs_k"] = table(LANES // ROPE)
    c["tri"] = np.triu(np.ones((L_S, L_S), np.float32), k=1)
    for tag, cap in (("c", CAP_CTX), ("s", CAP_S)):
        ex = np.zeros((LANES, E * cap), np.float32)
        ex[np.arange(E * cap) // cap, np.arange(E * cap)] = 1.0
        c["ex_" + tag] = ex
    return c


def _constants():
    c = {k: jnp.asarray(v) for k, v in _np_constants().items()}
    for k in ("dft_c", "dftt_c", "dft_s", "dftt_s", "ex_c", "ex_s", "tri"):
        c[k] = c[k].astype(BF16)
    return c


def _rope_swap(w):
    q = ROPE // 4
    return jnp.concatenate([-w[..., q:2 * q], w[..., :q], -w[..., 3 * q:], w[..., 2 * q:3 * q]], axis=-1)


def _dot_hi(a, b):
    return jnp.dot(a, b, preferred_element_type=F32, precision=lax.Precision.HIGHEST)


def _filter_kernel(feat_ref, t_ref, w1_ref, b1_ref, w2_ref, b2_ref, fr_ref, w3f_ref, w3b_ref,
                   ldf_ref, ldb_ref, dft_ref, o_ref, z_ref):
    n = feat_ref.shape[0]

    @pl.when((pl.program_id(1) == 0) & (pl.program_id(2) == 0))
    def _():
        fr = fr_ref[...]
        z = jnp.sin(fr * (_dot_hi(feat_ref[...], w1_ref[...]) + b1_ref[...]))
        z_ref[...] = jnp.sin(fr * (_dot_hi(z, w2_ref[...]) + b2_ref[...]))

    z = z_ref[...]
    t = t_ref[...]
    first = lax.broadcasted_iota(jnp.int32, (n, 1), 0) == 0
    hf = _dot_hi(z, w3f_ref[...]) * jnp.exp(-jnp.exp(ldf_ref[...]) * t)
    hb = _dot_hi(z, w3b_ref[...]) * jnp.exp(-jnp.exp(ldb_ref[...]) * t)
    hb = jnp.where(first, 0.0, hb)
    norm = jnp.sum(jnp.abs(hf), axis=0, keepdims=True) + jnp.sum(jnp.abs(hb), axis=0, keepdims=True) + EPS
    both = _dot(dft_ref[...], jnp.concatenate([hf / norm, hb / norm], axis=1).astype(BF16))
    pf, pb = both[:, :hf.shape[1]], both[:, hf.shape[1]:]
    sc = jnp.where(first, 0.5 / n, 1.0 / n)
    ka = (pf[:n] + pb[:n]) * sc
    o_ref[0] = ka
    o_ref[1] = jnp.where(first, 0.0, (pf[n:] - pb[n:]) * sc)
    o_ref[2] = jnp.where(first, (pf[n:] + pb[n:]) * sc, ka)


def _hyena_tables(n, feats, tcol, dft, fw):
    tc = 256
    nc = HY_W // tc

    def lay(shape):
        return pl.BlockSpec((None,) + shape, lambda l, o, c: (l, 0, 0))

    def w3(back):
        return pl.BlockSpec((None, LANES, tc), lambda l, o, c: (l, 0, (2 * o + back) * nc + c))

    def ld(back):
        return pl.BlockSpec((None, 1, tc), lambda l, o, c: (l, 0, (2 * o + back) * nc + c))

    return pl.pallas_call(
        _filter_kernel,
        grid=(DEPTH, 2, nc),
        in_specs=[
            pl.BlockSpec((n, LANES), lambda l, o, c: (0, 0)),
            pl.BlockSpec((n, 1), lambda l, o, c: (0, 0)),
            lay((LANES, LANES)), lay((1, LANES)), lay((LANES, LANES)), lay((1, LANES)), lay((1, LANES)),
            w3(0), w3(1), ld(0), ld(1),
            pl.BlockSpec((2 * n, n), lambda l, o, c: (0, 0)),
        ],
        out_specs=pl.BlockSpec((None, None, 3, n, tc), lambda l, o, c: (l, o, 0, 0, c)),
        out_shape=jax.ShapeDtypeStruct((DEPTH, 2, 3, n, HY_W), F32),
        scratch_shapes=[pltpu.VMEM((n, LANES), F32)],
        compiler_params=_cp(("arbitrary", "arbitrary", "arbitrary")),
        name="hyena_filter",
    )(feats, tcol, fw["w1"], fw["b1"], fw["w2"], fw["b2"], fw["freq"], fw["w3"], fw["w3"],
      fw["ld"], fw["ld"], dft)


def _filter_weights(w1, b1, w2, b2, w3, freq, log_decay):
    ph = LANES - FILT_HID
    return dict(
        w1=jnp.pad(w1, ((0, 0), (0, LANES - w1.shape[1]), (0, ph))),
        b1=jnp.pad(b1, ((0, 0), (0, ph))).reshape(DEPTH, 1, LANES),
        w2=jnp.pad(w2, ((0, 0), (0, ph), (0, ph))),
        b2=jnp.pad(b2, ((0, 0), (0, ph))).reshape(DEPTH, 1, LANES),
        freq=jnp.pad(freq, ((0, 0), (0, ph))).reshape(DEPTH, 1, LANES),
        w3=jnp.pad(w3, ((0, 0), (0, ph), (0, 0))),
        ld=log_decay.reshape(DEPTH, 1, 4 * HY_W))


def _prep_weights(w_in, pool_w, pool_scale, hy_short_b, mla_q_norm, mla_kv_norm, mla_w_uq, mla_w_ukv,
                  w_out, router_w, norm1_g, norm2_g):
    n_main = N_U + Q_RANK
    kr_cols = w_in[:, :, IN_COLS - ROPE:]
    kr_swap = _rope_swap(kr_cols)
    w_x = jnp.concatenate([w_in[:, :, n_main:IN_COLS - ROPE], kr_cols, kr_cols, kr_swap, kr_swap], axis=2)
    wq = mla_w_uq.reshape(DEPTH, Q_RANK, H, NOPE + ROPE)
    wq_rope = wq[..., NOPE:]
    wq = jnp.concatenate([wq[..., :NOPE].reshape(DEPTH, Q_RANK, -1), wq_rope.reshape(DEPTH, Q_RANK, -1),
                          _rope_swap(wq_rope).reshape(DEPTH, Q_RANK, -1)], axis=2)
    wkv = mla_w_ukv.reshape(DEPTH, KV_RANK, H, NOPE + VD)
    wkv = jnp.concatenate([wkv[..., :NOPE].reshape(DEPTH, KV_RANK, -1), wkv[..., NOPE:].reshape(DEPTH, KV_RANK, -1)],
                          axis=2)
    return dict(
        w_main=w_in.astype(BF16), w_x=w_x.astype(BF16), wq=wq.astype(BF16), wkv=wkv.astype(BF16),
        w_out=w_out.astype(BF16), rw=jnp.pad(router_w, ((0, 0), (0, 0), (0, LANES - E))).astype(BF16),
        pool_w=pool_w.astype(BF16), pool_scale=pool_scale.reshape(DEPTH, 1, POOL_W),
        sb=hy_short_b.reshape(DEPTH, 1, 3 * HY_W),
        qg=mla_q_norm.reshape(DEPTH, 1, Q_RANK), kvg=mla_kv_norm.reshape(DEPTH, 1, KV_RANK),
        g1=norm1_g.reshape(DEPTH, 1, D), g2=norm2_g.reshape(DEPTH, 1, D))


def _layer(x, layer, mod3, wts, consts, kt_c, kt_s, hy_short_w, hy_skip, cache_ckv, krc,
           exp_w_gate, exp_w_up, exp_w_down, final_g):
    up, uv, u1, u2, q, ckv, kr, krr, kv = _inproj(x, mod3, wts, consts, layer)
    y_pool = _pool(up, consts, wts["pool_w"][layer], wts["pool_scale"][layer])
    y_hy = _hyena(uv, u1, u2, hy_short_w[layer], wts["sb"][layer], hy_skip[layer], consts, kt_c, kt_s, layer)
    kvc = _cachekv(cache_ckv, layer, wts["wkv"])
    y_mla = _attention(q, kv, krr, kvc, krc)
    x, h2, aff, afft = _wout(y_pool, y_hy, y_mla, x, mod3, wts, layer)
    rank, rank_t = _rank(aff, afft, consts["tri"])
    xs, gs = _gather(rank, afft, h2)
    ys = _ffn(xs, gs, exp_w_gate, exp_w_up, exp_w_down, layer)
    return _combine(rank_t, ys, x, mod3, consts, final_g), ckv, kr


def kernel(x_prompt, x_sample, cache_ckv, cache_krope, c, c_ctx, ada_w, ada_b, norm1_g, norm2_g, w_in, pool_w, pool_scale, hy_short_w, hy_short_b, hy_ffn_w1, hy_ffn_b1, hy_ffn_w2, hy_ffn_b2, hy_ffn_w3, hy_freq, hy_log_decay, hy_skip, mla_q_norm, mla_kv_norm, mla_w_uq, mla_w_ukv, w_out, router_w, exp_w_gate, exp_w_up, exp_w_down, final_norm_g):
    consts = _constants()
    x = (x_prompt.reshape(T_CTX, D), x_sample.reshape(T - T_CTX, D))
    c16 = jnp.concatenate([jnp.broadcast_to(c_ctx[None], (NG_CTX, D)), c], axis=0)
    mod = _adaln_mod(c16, ada_w, ada_b).reshape(DEPTH, NG * 6, 1, D)

    fw = _filter_weights(hy_ffn_w1, hy_ffn_b1, hy_ffn_w2, hy_ffn_b2, hy_ffn_w3, hy_freq, hy_log_decay)
    kt_c = _hyena_tables(L_CTX, consts["feat_c"], consts["t_c"], consts["dft_c"], fw)
    kt_s = _hyena_tables(L_S, consts["feat_s"], consts["t_s"], consts["dft_s"], fw)

    wts = _prep_weights(w_in, pool_w, pool_scale, hy_short_b, mla_q_norm, mla_kv_norm, mla_w_uq, mla_w_ukv,
                        w_out, router_w, norm1_g, norm2_g)
    krc = jnp.swapaxes(cache_krope, 0, 1).reshape(DEPTH, B_S * L_CTX, ROPE)
    krc = jnp.concatenate([krc, krc], axis=2).astype(BF16)

    ckv_list, kr_list = [], []
    for l in range(DEPTH):
        final_g = final_norm_g.reshape(1, D) if l == DEPTH - 1 else None
        x, ckv, kr = _layer(x, l, mod[l], wts, consts, kt_c, kt_s, hy_short_w, hy_skip, cache_ckv, krc[l],
                            exp_w_gate, exp_w_up, exp_w_down, final_g)
        ckv_list.append(ckv.reshape(B_CTX, L_CTX, KV_RANK))
        kr_list.append(kr[:, :ROPE].reshape(B_CTX, L_CTX, ROPE))

    y_prompt, y_sample = x
    return (y_prompt.reshape(B_CTX, L_CTX, D), y_sample.reshape(B_S, L_S, D),
            jnp.stack(ckv_list, axis=1), jnp.stack(kr_list, axis=1))
```
